```python
import math
import jax, jax.numpy as jnp
from jax import lax
import numpy as np

D_MODEL = 1024
BATCH = 4
SEQ = 8192
DEPTH = 2

N_A_LAYERS = DEPTH // 2
N_B_LAYERS = DEPTH - N_A_LAYERS
CONV_WIDTH = 3
FFN_HIDDEN = -(-8 * D_MODEL // (3 * 256)) * 256
N_HEADS = 16
N_KV_GROUPS = 4
HEADS_PER_GROUP = N_HEADS // N_KV_GROUPS
HEAD_DIM = 64
L_CMP = 32
D_CMP = 16
L_SLC = 64
N_SEL = 16
WINDOW = 512
CMP_HIDDEN = 256
Q_BLOCK = 64
N_KV_SETS = 6
RMS_EPS = 1e-5
NEG_INF = -1e30
FORCE_BONUS = 1e4

kernel_name = "yoco_shortconv_nsa_hybrid"


def rms_norm(x, g):
    xf = x.astype(jnp.float32)
    y = xf * lax.rsqrt(jnp.mean(xf * xf, axis=-1, keepdims=True) + RMS_EPS)
    return (y * g.astype(jnp.float32)).astype(x.dtype)


def short_conv_mixer(h, w_in, conv_w, w_out):
    b_gate, c_gate, v = jnp.split(h @ w_in, 3, axis=-1)
    u = lax.conv_general_dilated(
        c_gate * v, conv_w[:, None, :], window_strides=(1,),
        padding=((CONV_WIDTH - 1, 0),), dimension_numbers=("NWC", "WIO", "NWC"),
        feature_group_count=D_MODEL)
    return (b_gate * u) @ w_out


def swiglu(h, w_gu, w_down):
    g, u = jnp.split(h @ w_gu, 2, axis=-1)
    return (jax.nn.silu(g) * u) @ w_down


def compress(kv_raw, pe, w1, w2):
    bsz, seq = kv_raw.shape[0], kv_raw.shape[1]
    n_cmp = (seq - L_CMP) // D_CMP + 1
    idx = jnp.arange(n_cmp)[:, None] * D_CMP + jnp.arange(L_CMP)[None, :]
    blocks = kv_raw[:, idx] + pe[None, None, :, None, :]
    blocks = blocks.transpose(0, 1, 3, 2, 4).reshape(bsz, n_cmp, N_KV_GROUPS, L_CMP * HEAD_DIM)
    return jax.nn.silu(blocks @ w1) @ w2


def shared_kv(h, kv_norm, w_kv, pe_k, w1_k, w2_k, pe_v, w1_v, w2_v):
    bsz, seq, _ = h.shape
    kv = (rms_norm(h, kv_norm) @ w_kv).reshape(bsz, seq, N_KV_SETS, N_KV_GROUPS, HEAD_DIM)
    k_cmp = compress(kv[:, :, 0], pe_k, w1_k, w2_k)
    v_cmp = compress(kv[:, :, 1], pe_v, w1_v, w2_v)
    return (k_cmp, v_cmp, kv[:, :, 2], kv[:, :, 3], kv[:, :, 4], kv[:, :, 5])


def nsa_mixer(h, w_qg, w_o, k_cmp, v_cmp, k_slc, v_slc, k_win, v_win):
    bsz, seq, _ = h.shape
    G, HPG, dh = N_KV_GROUPS, HEADS_PER_GROUP, HEAD_DIM
    qg = h @ w_qg
    q = (qg[..., :N_HEADS * dh] * (dh ** -0.5)).reshape(bsz, seq, G, HPG, dh)
    gates = jax.nn.sigmoid(qg[..., N_HEADS * dh:].astype(jnp.float32)).reshape(bsz, seq, G, HPG, 3)

    slopes = jnp.exp2(-8.0 * jnp.arange(1, N_HEADS + 1, dtype=jnp.float32) / N_HEADS).reshape(G, HPG)

    n_cmp = k_cmp.shape[1]
    cmp_start = jnp.arange(n_cmp) * D_CMP
    cmp_end = cmp_start + L_CMP - 1
    cmp_center = cmp_end.astype(jnp.float32) - (L_CMP - 1) / 2.0
    n_slc = seq // L_SLC
    n_sel = min(N_SEL, n_slc)
    slc_start = jnp.arange(n_slc) * L_SLC
    overlap = ((cmp_start[:, None] < slc_start[None, :] + L_SLC)
               & (cmp_start[:, None] + L_CMP > slc_start[None, :])).astype(jnp.float32)

    k_blk = k_slc.reshape(bsz, n_slc, L_SLC, G, dh).transpose(0, 3, 1, 2, 4)
    v_blk = v_slc.reshape(bsz, n_slc, L_SLC, G, dh).transpose(0, 3, 1, 2, 4)
    pad = ((0, 0), (WINDOW, 0), (0, 0), (0, 0))
    k_win_pad = jnp.pad(k_win, pad)
    v_win_pad = jnp.pad(v_win, pad)
    b_ix = jnp.arange(bsz)[:, None, None, None]
    g_ix = jnp.arange(G)[None, :, None, None]
    j_blk = jnp.arange(n_slc)

    def one_block(qb):
        q0 = qb * Q_BLOCK
        t = q0 + jnp.arange(Q_BLOCK)
        tf = t.astype(jnp.float32)
        qblk = lax.dynamic_slice_in_dim(q, q0, Q_BLOCK, axis=1)
        gblk = lax.dynamic_slice_in_dim(gates, q0, Q_BLOCK, axis=1)

        s = jnp.einsum("bqghd,bngd->bghqn", qblk, k_cmp).astype(jnp.float32)
        vis = cmp_end[None, :] <= t[:, None]
        s = s - slopes[:, :, None, None] * (tf[:, None] - cmp_center[None, :])
        p_cmp = jax.nn.softmax(jnp.where(vis, s, NEG_INF), axis=-1) * vis
        o_cmp = jnp.einsum("bghqn,bngd->bqghd", p_cmp.astype(v_cmp.dtype), v_cmp)

        imp = jnp.einsum("bghqn,nj->bgqj", p_cmp, overlap)
        cur = t // L_SLC
        valid = j_blk[None, :] <= cur[:, None]
        forced = ((j_blk[None, :] == 0) | (j_blk[None, :] == cur[:, None])
                  | (j_blk[None, :] == cur[:, None] - 1))
        imp = jnp.where(valid, imp + FORCE_BONUS * forced, NEG_INF)
        _, sel = lax.top_k(imp, n_sel)
        k_sel = k_blk[b_ix, g_ix, sel]
        v_sel = v_blk[b_ix, g_ix, sel]
        s = jnp.einsum("bqghd,bgqnld->bghqnl", qblk, k_sel).astype(jnp.float32)
        dist = t[None, None, :, None, None] - (sel[..., None] * L_SLC + jnp.arange(L_SLC))
        dist = dist[:, :, None]
        s = s - slopes[:, :, None, None, None] * dist.astype(jnp.float32)
        s = jnp.where(dist >= 0, s, NEG_INF)
        p = jax.nn.softmax(s.reshape(s.shape[:4] + (n_sel * L_SLC,)), axis=-1).reshape(s.shape)
        o_slc = jnp.einsum("bghqnl,bgqnld->bqghd", p.astype(v_sel.dtype), v_sel)

        kw = lax.dynamic_slice_in_dim(k_win_pad, q0, WINDOW + Q_BLOCK, axis=1)
        vw = lax.dynamic_slice_in_dim(v_win_pad, q0, WINDOW + Q_BLOCK, axis=1)
        spos = q0 - WINDOW + jnp.arange(WINDOW + Q_BLOCK)
        d = t[:, None] - spos[None, :]
        m = (d >= 0) & (d < WINDOW) & (spos >= 0)[None, :]
        s = jnp.einsum("bqghd,bkgd->bghqk", qblk, kw).astype(jnp.float32)
        s = s - slopes[:, :, None, None] * d.astype(jnp.float32)
        p = jax.nn.softmax(jnp.where(m, s, NEG_INF), axis=-1)
        o_win = jnp.einsum("bghqk,bkgd->bqghd", p.astype(vw.dtype), vw)

        o = (gblk[..., 0:1] * o_cmp + gblk[..., 1:2] * o_slc + gblk[..., 2:3] * o_win)
        return o.reshape(bsz, Q_BLOCK, N_HEADS * dh).astype(h.dtype)

    out = lax.map(one_block, jnp.arange(seq // Q_BLOCK))
    out = out.transpose(1, 0, 2, 3).reshape(bsz, seq, N_HEADS * dh)
    return out @ w_o


def setup_inputs(seed: int = 0) -> dict:
    key = jax.random.key(seed)
    ks = jax.random.split(key, 24)

    def w(k, shape, fan_in):
        return jax.random.normal(k, shape, jnp.float32) * (fan_in ** -0.5)

    def gain(k, shape):
        return 1.0 + 0.02 * jax.random.normal(k, shape, jnp.float32)

    D, F = D_MODEL, FFN_HIDDEN
    kv_w = N_KV_GROUPS * HEAD_DIM
    qg_w = N_HEADS * HEAD_DIM + 3 * N_HEADS
    return {
        "x": jax.random.normal(ks[0], (BATCH, SEQ, D), jnp.float32),
        "a_norm": gain(ks[1], (N_A_LAYERS, D)),
        "a_w_in": w(ks[2], (N_A_LAYERS, D, 3 * D), D),
        "a_conv": w(ks[3], (N_A_LAYERS, CONV_WIDTH, D), CONV_WIDTH),
        "a_w_out": w(ks[4], (N_A_LAYERS, D, D), D),
        "kv_norm": gain(ks[5], (D,)),
        "w_kv": w(ks[6], (D, N_KV_SETS * kv_w), D),
        "cmp_pe_k": 0.02 * jax.random.normal(ks[7], (L_CMP, HEAD_DIM), jnp.float32),
        "cmp_w1_k": w(ks[8], (L_CMP * HEAD_DIM, CMP_HIDDEN), L_CMP * HEAD_DIM),
        "cmp_w2_k": w(ks[9], (CMP_HIDDEN, HEAD_DIM), CMP_HIDDEN),
        "cmp_pe_v": 0.02 * jax.random.normal(ks[10], (L_CMP, HEAD_DIM), jnp.float32),
        "cmp_w1_v": w(ks[11], (L_CMP * HEAD_DIM, CMP_HIDDEN), L_CMP * HEAD_DIM),
        "cmp_w2_v": w(ks[12], (CMP_HIDDEN, HEAD_DIM), CMP_HIDDEN),
        "b_norm": gain(ks[13], (N_B_LAYERS, D)),
        "b_w_qg": w(ks[14], (N_B_LAYERS, D, qg_w), D),
        "b_w_o": w(ks[15], (N_B_LAYERS, N_HEADS * HEAD_DIM, D), N_HEADS * HEAD_DIM),
        "f_norm": gain(ks[16], (DEPTH, D)),
        "f_w_gu": w(ks[17], (DEPTH, D, 2 * F), D),
        "f_w_down": w(ks[18], (DEPTH, F, D), F),
        "final_norm": gain(ks[19], (D,)),
    }


def reference(x, a_norm, a_w_in, a_conv, a_w_out, kv_norm, w_kv, cmp_pe_k, cmp_w1_k, cmp_w2_k,
              cmp_pe_v, cmp_w1_v, cmp_w2_v, b_norm, b_w_qg, b_w_o, f_norm, f_w_gu, f_w_down,
              final_norm):
    h = x
    shared = None
    for layer in range(DEPTH):
        if layer < N_A_LAYERS:
            i = layer
            h = h + short_conv_mixer(rms_norm(h, a_norm[i]), a_w_in[i], a_conv[i], a_w_out[i])
        else:
            i = layer - N_A_LAYERS
            if shared is None:
                shared = shared_kv(h, kv_norm, w_kv, cmp_pe_k, cmp_w1_k, cmp_w2_k,
                                   cmp_pe_v, cmp_w1_v, cmp_w2_v)
            k_cmp, v_cmp, k_slc, v_slc, k_win, v_win = shared
            h = h + nsa_mixer(rms_norm(h, b_norm[i]), b_w_qg[i], b_w_o[i],
                              k_cmp, v_cmp, k_slc, v_slc, k_win, v_win)
        h = h + swiglu(rms_norm(h, f_norm[layer]), f_w_gu[layer], f_w_down[layer])
    return rms_norm(h, final_norm)
```

```python
import functools

import jax
import jax.numpy as jnp
from jax import lax
from jax.experimental import pallas as pl
from jax.experimental.pallas import tpu as pltpu

F32 = jnp.float32
BF16 = jnp.bfloat16

N_HEADS = 16
N_KV_GROUPS = 4
HEADS_PER_GROUP = N_HEADS // N_KV_GROUPS
HEAD_DIM = 64
L_CMP = 32
D_CMP = 16
L_SLC = 64
N_SEL = 16
WINDOW = 512
N_KV_SETS = 6
CONV_WIDTH = 3
RMS_EPS = 1e-5
NEG_INF = -1e30
FORCE_BONUS = 1e4
GATE_LANES = 128
VMEM_LIMIT_BYTES = 56 * 1024 * 1024


def _rms(x, g):
    return x * lax.rsqrt(jnp.mean(x * x, axis=-1, keepdims=True) + RMS_EPS) * g


def _dot(a, b):
    return jnp.dot(a, b, preferred_element_type=F32)


def _dot_nt(a, b):
    return lax.dot_general(a, b, (((1,), (1,)), ((), ())), preferred_element_type=F32)


def _params(*sem):
    return pltpu.CompilerParams(dimension_semantics=sem, vmem_limit_bytes=VMEM_LIMIT_BYTES)


def _mixer_a_kernel(x_ref, g_ref, win_ref, conv_ref, wout_ref, o_ref, ext_ref, *, tm, d):
    @pl.when(pl.program_id(1) == 0)
    def _():
        ext_ref[0:8, :] = jnp.zeros((8, d), F32)

    x = x_ref[0]
    xb = _rms(x, g_ref[...]).astype(BF16)
    b_gate = _dot(xb, win_ref[:, 0:d])
    cv = _dot(xb, win_ref[:, d:2 * d]) * _dot(xb, win_ref[:, 2 * d:3 * d])
    ext_ref[8:8 + tm, :] = cv
    cv1 = ext_ref[7:7 + tm, :]
    cv2 = ext_ref[6:6 + tm, :]
    u = conv_ref[0:1, :] * cv2 + conv_ref[1:2, :] * cv1 + conv_ref[2:3, :] * cv
    ext_ref[0:8, :] = cv[tm - 8:tm, :]
    o_ref[0] = x + _dot((b_gate * u).astype(BF16), wout_ref[...])


def _mixer_a(x, g, w_in, conv_w, w_out, tm=512):
    b, s, d = x.shape
    assert s % tm == 0
    return pl.pallas_call(
        functools.partial(_mixer_a_kernel, tm=tm, d=d),
        grid=(b, s // tm),
        in_specs=[
            pl.BlockSpec((1, tm, d), lambda i, j: (i, j, 0)),
            pl.BlockSpec((1, d), lambda i, j: (0, 0)),
            pl.BlockSpec((d, 3 * d), lambda i, j: (0, 0)),
            pl.BlockSpec((CONV_WIDTH, d), lambda i, j: (0, 0)),
            pl.BlockSpec((d, d), lambda i, j: (0, 0)),
        ],
        out_specs=pl.BlockSpec((1, tm, d), lambda i, j: (i, j, 0)),
        out_shape=jax.ShapeDtypeStruct((b, s, d), F32),
        scratch_shapes=[pltpu.VMEM((tm + 8, d), F32)],
        compiler_params=_params("arbitrary", "arbitrary"),
        name="mixer_a",
    )(x, g.reshape(1, d), w_in, conv_w, w_out)


def _ffn_kernel(*refs, with_proj, with_final):
    if with_proj:
        h_ref, o_ref_in, wo_ref, g_ref, wg_ref, wu_ref, wd_ref, fg_ref, out_ref, xn_ref, acc_ref, res_ref = refs
    else:
        h_ref, g_ref, wg_ref, wu_ref, wd_ref, fg_ref, out_ref, xn_ref, acc_ref, res_ref = refs
    j = pl.program_id(1)

    @pl.when(j == 0)
    def _():
        h = h_ref[...]
        if with_proj:
            h = h + _dot(o_ref_in[...], wo_ref[...])
        res_ref[...] = h
        xn_ref[...] = _rms(h, g_ref[...]).astype(BF16)
        acc_ref[...] = jnp.zeros_like(acc_ref)

    xb = xn_ref[...]
    gate = _dot(xb, wg_ref[...])
    up = _dot(xb, wu_ref[...])
    act = (gate * jax.nn.sigmoid(gate)) * up
    acc_ref[...] += _dot(act.astype(BF16), wd_ref[...])

    @pl.when(j == pl.num_programs(1) - 1)
    def _():
        y = res_ref[...] + acc_ref[...]
        if with_final:
            y = _rms(y, fg_ref[...])
        out_ref[...] = y


def _ffn(h, g, w_gu, w_down, final_g, o=None, w_o=None, with_final=False, tm=512, tf=1408):
    t, d = h.shape
    f = w_down.shape[0]
    assert t % tm == 0 and f % tf == 0
    nf = f // tf
    with_proj = o is not None
    in_specs = [pl.BlockSpec((tm, d), lambda i, j: (i, 0))]
    args = [h]
    if with_proj:
        in_specs += [pl.BlockSpec((tm, o.shape[1]), lambda i, j: (i, 0)),
                     pl.BlockSpec(w_o.shape, lambda i, j: (0, 0))]
        args += [o, w_o]
    in_specs += [
        pl.BlockSpec((1, d), lambda i, j: (0, 0)),
        pl.BlockSpec((d, tf), lambda i, j: (0, j)),
        pl.BlockSpec((d, tf), lambda i, j: (0, nf + j)),
        pl.BlockSpec((tf, d), lambda i, j: (j, 0)),
        pl.BlockSpec((1, d), lambda i, j: (0, 0)),
    ]
    args += [g.reshape(1, d), w_gu, w_gu, w_down, final_g.reshape(1, d)]
    return pl.pallas_call(
        functools.partial(_ffn_kernel, with_proj=with_proj, with_final=with_final),
        grid=(t // tm, nf),
        in_specs=in_specs,
        out_specs=pl.BlockSpec((tm, d), lambda i, j: (i, 0)),
        out_shape=jax.ShapeDtypeStruct((t, d), F32),
        scratch_shapes=[pltpu.VMEM((tm, d), BF16), pltpu.VMEM((tm, d), F32), pltpu.VMEM((tm, d), F32)],
        compiler_params=_params("arbitrary", "arbitrary"),
        name="ffn_proj" if with_proj else "ffn",
    )(*args)


def _proj_kernel(h_ref, gkv_ref, gq_ref, wkv_ref, wqg_ref, kvc_ref, kva_ref, q_ref, gate_ref):
    x = h_ref[0]
    y = x * lax.rsqrt(jnp.mean(x * x, axis=-1, keepdims=True) + RMS_EPS)
    kv = _dot((y * gkv_ref[...]).astype(BF16), wkv_ref[...])
    for c in range(N_KV_SETS * N_KV_GROUPS):
        st, grp = divmod(c, N_KV_GROUPS)
        blk = kv[:, c * HEAD_DIM:(c + 1) * HEAD_DIM].astype(BF16)
        if st < 2:
            kvc_ref[st, 0, grp] = blk
        else:
            kva_ref[st - 2, 0, grp] = blk
    qg = _dot((y * gq_ref[...]).astype(BF16), wqg_ref[...])
    for hd in range(N_HEADS):
        q_ref[0, hd] = (qg[:, hd * HEAD_DIM:(hd + 1) * HEAD_DIM] * (HEAD_DIM ** -0.5)).astype(BF16)
    nq = N_HEADS * HEAD_DIM
    for grp in range(N_KV_GROUPS):
        gate_ref[0, grp] = jax.nn.sigmoid(qg[:, nq + grp * GATE_LANES: nq + (grp + 1) * GATE_LANES])


def _proj(h, g_kv, g_q, w_kv, w_qg_padded, tm=512):
    b, s, d = h.shape
    assert s % tm == 0
    g, dh = N_KV_GROUPS, HEAD_DIM
    return pl.pallas_call(
        _proj_kernel,
        grid=(b, s // tm),
        in_specs=[
            pl.BlockSpec((1, tm, d), lambda i, j: (i, j, 0)),
            pl.BlockSpec((1, d), lambda i, j: (0, 0)),
            pl.BlockSpec((1, d), lambda i, j: (0, 0)),
            pl.BlockSpec(w_kv.shape, lambda i, j: (0, 0)),
            pl.BlockSpec(w_qg_padded.shape, lambda i, j: (0, 0)),
        ],
        out_specs=[
            pl.BlockSpec((2, 1, g, tm, dh), lambda i, j: (0, i, 0, j, 0)),
            pl.BlockSpec((4, 1, g, tm, dh), lambda i, j: (0, i, 0, j, 0)),
            pl.BlockSpec((1, N_HEADS, tm, dh), lambda i, j: (i, 0, j, 0)),
            pl.BlockSpec((1, g, tm, GATE_LANES), lambda i, j: (i, 0, j, 0)),
        ],
        out_shape=[
            jax.ShapeDtypeStruct((2, b, g, s, dh), BF16),
            jax.ShapeDtypeStruct((4, b, g, s, dh), BF16),
            jax.ShapeDtypeStruct((b, N_HEADS, s, dh), BF16),
            jax.ShapeDtypeStruct((b, g, s, GATE_LANES), F32),
        ],
        compiler_params=_params("arbitrary", "arbitrary"),
        name="proj",
    )(h, g_kv.reshape(1, d), g_q.reshape(1, d), w_kv, w_qg_padded)


def _compress_kernel(r_ref, pe_ref, w1_ref, w2_ref, o_ref, *, m):
    half = D_CMP * HEAD_DIM
    r = r_ref[0, 0]
    top = _dot(r, w1_ref[0, 0:half, :])
    bot = _dot(r, w1_ref[0, half:2 * half, :])
    pe_term = _dot(pe_ref[0], w1_ref[0])[0:1, :]
    hid = top + pltpu.roll(bot, m - 1, axis=0) + pe_term
    hid = hid * jax.nn.sigmoid(hid)
    out = _dot(hid.astype(BF16), w2_ref[0])
    row = lax.broadcasted_iota(jnp.int32, out.shape, 0)
    o_ref[0, 0] = jnp.where(row < m - 1, out, 0.0).astype(BF16)


def _compress(kvc, pe2, w1_2, w2_2):
    _, b, g, s, dh = kvc.shape
    m = s // D_CMP
    r = kvc.reshape(2, b * g, m, D_CMP * dh)
    hid = w1_2.shape[-1]
    return pl.pallas_call(
        functools.partial(_compress_kernel, m=m),
        grid=(2, b * g),
        in_specs=[
            pl.BlockSpec((1, 1, m, D_CMP * dh), lambda i, j: (i, j, 0, 0)),
            pl.BlockSpec((1, 8, L_CMP * dh), lambda i, j: (i, 0, 0)),
            pl.BlockSpec((1, L_CMP * dh, hid), lambda i, j: (i, 0, 0)),
            pl.BlockSpec((1, hid, dh), lambda i, j: (i, 0, 0)),
        ],
        out_specs=pl.BlockSpec((1, 1, m, dh), lambda i, j: (i, j, 0, 0)),
        out_shape=jax.ShapeDtypeStruct((2, b * g, m, dh), BF16),
        compiler_params=_params("arbitrary", "arbitrary"),
        name="compress",
    )(r, pe2, w1_2, w2_2)


def _nsa_cmp_kernel(slopes_ref, q_ref, k_ref, v_ref, gate_ref, oc_ref, sel_ref, *, tq, m):
    grp = pl.program_id(1)
    q0 = pl.program_id(2) * tq
    hpg = HEADS_PER_GROUP
    q = q_ref[0].reshape(hpg * tq, HEAD_DIM)
    s = _dot_nt(q, k_ref[0, 0]).reshape(hpg, tq, m)
    t_i = q0 + lax.broadcasted_iota(jnp.int32, (tq, 1), 0)
    n_i = lax.broadcasted_iota(jnp.int32, (1, m), 1)
    vis = (n_i * D_CMP + (L_CMP - 1)) <= t_i
    rel = t_i.astype(F32) - (n_i.astype(F32) * D_CMP + (L_CMP - 1) / 2.0)
    visf = vis.astype(F32)
    gates = gate_ref[0, 0]
    v = v_ref[0, 0]
    p_sum = jnp.zeros((tq, m), F32)
    outs = []
    for h in range(hpg):
        sh = jnp.where(vis, s[h] - slopes_ref[grp * hpg + h] * rel, NEG_INF)
        e = jnp.exp(sh - jnp.max(sh, axis=-1, keepdims=True))
        p = e * (1.0 / jnp.sum(e, axis=-1, keepdims=True)) * visf
        p_sum = p_sum + p
        outs.append(gates[:, 3 * h:3 * h + 1] * _dot(p.astype(BF16), v))
    oc_ref[0] = jnp.concatenate(outs, axis=-1)

    n_slc = sel_ref.shape[-1]
    nn = lax.broadcasted_iota(jnp.int32, (m, n_slc), 0) * D_CMP
    jj = lax.broadcasted_iota(jnp.int32, (m, n_slc), 1) * L_SLC
    overlap = ((nn < jj + L_SLC) & (nn + L_CMP > jj)).astype(BF16)
    p_hi = p_sum.astype(BF16)
    p_lo = (p_sum - p_hi.astype(F32)).astype(BF16)
    imp = _dot(p_hi, overlap) + _dot(p_lo, overlap)

    lane = lax.broadcasted_iota(jnp.int32, (tq, n_slc), 1)
    cur = jnp.right_shift(t_i, L_SLC.bit_length() - 1)
    valid = lane <= cur
    forced = (lane == 0) | (lane == cur) | (lane == cur - 1)
    val = jnp.where(valid, imp + FORCE_BONUS * forced.astype(F32), NEG_INF)
    lane_f = lane.astype(F32)
    sel = jnp.zeros((tq, n_slc), F32)
    for _ in range(min(N_SEL, n_slc)):
        mx = jnp.max(val, axis=-1, keepdims=True)
        first = jnp.min(jnp.where(val == mx, lane_f, float(n_slc)), axis=-1, keepdims=True)
        hit = lane_f == first
        sel = jnp.where(hit, 1.0, sel)
        val = jnp.where(hit, -3.0e38, val)
    sel_ref[0, 0] = jnp.where(valid, sel, 0.0).astype(BF16)


def _nsa_cmp(slopes, q, kvcmp, gates, tq=128):
    b, nh, s, dh = q.shape
    g, hpg = N_KV_GROUPS, HEADS_PER_GROUP
    m = kvcmp.shape[2]
    n_slc = s // L_SLC
    assert s % tq == 0 and tq % L_SLC == 0
    grid_spec = pltpu.PrefetchScalarGridSpec(
        num_scalar_prefetch=1,
        grid=(b, g, s // tq),
        in_specs=[
            pl.BlockSpec((1, hpg, tq, dh), lambda i, j, k, sl: (i, j, k, 0)),
            pl.BlockSpec((1, 1, m, dh), lambda i, j, k, sl: (0, i * N_KV_GROUPS + j, 0, 0)),
            pl.BlockSpec((1, 1, m, dh), lambda i, j, k, sl: (1, i * N_KV_GROUPS + j, 0, 0)),
            pl.BlockSpec((1, 1, tq, GATE_LANES), lambda i, j, k, sl: (i, j, k, 0)),
        ],
        out_specs=[
            pl.BlockSpec((1, tq, hpg * dh), lambda i, j, k, sl: (i, k, j)),
            pl.BlockSpec((1, 1, tq, n_slc), lambda i, j, k, sl: (i, j, k, 0)),
        ],
    )
    return pl.pallas_call(
        functools.partial(_nsa_cmp_kernel, tq=tq, m=m),
        grid_spec=grid_spec,
        out_shape=[
            jax.ShapeDtypeStruct((b, s, nh * dh), F32),
            jax.ShapeDtypeStruct((b, g, s, n_slc), BF16),
        ],
        compiler_params=_params("arbitrary", "arbitrary", "arbitrary"),
        name="nsa_cmp",
    )(slopes, q, kvcmp, kvcmp, gates)


def _nsa_slc_kernel(slopes_ref, q_ref, ks_ref, vs_ref, kw_ref, vw_ref, sel_ref, gate_ref, oc_ref, o_ref,
                    m_ref, l_ref, acc_ref, *, tq, tk):
    grp = pl.program_id(1)
    qt = pl.program_id(2)
    q0 = qt * tq
    hpg = HEADS_PER_GROUP
    n_slc = sel_ref.shape[-1]
    q = q_ref[0].reshape(hpg * tq, HEAD_DIM)
    t_i = q0 + lax.broadcasted_iota(jnp.int32, (tq, 1), 0)
    sel = sel_ref[0, 0]
    h_i = lax.broadcasted_iota(jnp.int32, (hpg, 1, 1), 0)
    slope3 = jnp.zeros((hpg, 1, 1), F32)
    for h in range(hpg):
        slope3 = jnp.where(h_i == h, slopes_ref[grp * hpg + h], slope3)

    m_ref[...] = jnp.full_like(m_ref, NEG_INF)
    l_ref[...] = jnp.zeros_like(l_ref)
    acc_ref[...] = jnp.zeros_like(acc_ref)

    def tile(ti, causal):
        kstart = pl.multiple_of(ti * tk, tk)
        kt = ks_ref[0, 0, pl.ds(kstart, tk), :]
        vt = vs_ref[0, 0, pl.ds(kstart, tk), :]
        s = _dot_nt(q, kt).reshape(hpg, tq, tk)
        kpos = kstart + lax.broadcasted_iota(jnp.int32, (1, tk), 1)
        rel = (t_i - kpos).astype(F32)
        blk = ti * (tk // L_SLC) + jnp.right_shift(lax.broadcasted_iota(jnp.int32, (n_slc, tk), 1),
                                                   L_SLC.bit_length() - 1)
        expand = (lax.broadcasted_iota(jnp.int32, (n_slc, tk), 0) == blk).astype(BF16)
        mask = _dot(sel, expand) > 0.5
        if causal:
            mask = mask & (rel >= 0.0)
        sb = jnp.where(mask[None], s - slope3 * rel[None], NEG_INF)
        m_old = m_ref[...]
        m_new = jnp.maximum(m_old, jnp.max(sb, axis=-1, keepdims=True))
        alpha = jnp.exp(m_old - m_new)
        p = jnp.where(mask[None], jnp.exp(sb - m_new), 0.0)
        l_ref[...] = alpha * l_ref[...] + jnp.sum(p, axis=-1, keepdims=True)
        pv = _dot(p.reshape(hpg * tq, tk).astype(BF16), vt).reshape(hpg, tq, HEAD_DIM)
        acc_ref[...] = alpha * acc_ref[...] + pv
        m_ref[...] = m_new

    n_full = q0 // tk

    def body(i, carry):
        tile(i, False)
        return carry

    lax.fori_loop(0, n_full, body, 0)
    tile(n_full, True)
    o_slc = acc_ref[...] * (1.0 / l_ref[...])

    wk = WINDOW + tq
    wstart = pl.multiple_of(jnp.maximum(q0 - WINDOW, 0), tq)
    kw = kw_ref[0, 0, pl.ds(wstart, wk), :]
    vw = vw_ref[0, 0, pl.ds(wstart, wk), :]
    s = _dot_nt(q, kw).reshape(hpg, tq, wk)
    dist = t_i - (wstart + lax.broadcasted_iota(jnp.int32, (1, wk), 1))
    wmask = ((dist >= 0) & (dist < WINDOW))[None]
    sb = jnp.where(wmask, s - slope3 * dist.astype(F32)[None], NEG_INF)
    e = jnp.exp(sb - jnp.max(sb, axis=-1, keepdims=True))
    p = e * (1.0 / jnp.sum(e, axis=-1, keepdims=True))
    o_win = _dot(p.reshape(hpg * tq, wk).astype(BF16), vw).reshape(hpg, tq, HEAD_DIM)

    gates = gate_ref[0, 0]
    outs = [gates[:, 3 * h + 1:3 * h + 2] * o_slc[h] + gates[:, 3 * h + 2:3 * h + 3] * o_win[h]
            for h in range(hpg)]
    o_ref[0] = (oc_ref[0] + jnp.concatenate(outs, axis=-1)).astype(BF16)


def _nsa_slc(slopes, q, kva, sel, gates, oc, tq=128, tk=512):
    b, nh, s, dh = q.shape
    g, hpg = N_KV_GROUPS, HEADS_PER_GROUP
    n_slc = s // L_SLC
    assert s % tk == 0 and tk % tq == 0 and WINDOW % tq == 0 and s >= WINDOW + tq
    kv4 = kva.reshape(4, b * g, s, dh)

    def kv_spec(st):
        return pl.BlockSpec((1, 1, s, dh), lambda i, j, k, sl: (st, i * N_KV_GROUPS + j, 0, 0))

    grid_spec = pltpu.PrefetchScalarGridSpec(
        num_scalar_prefetch=1,
        grid=(b, g, s // tq),
        in_specs=[
            pl.BlockSpec((1, hpg, tq, dh), lambda i, j, k, sl: (i, j, k, 0)),
            kv_spec(0), kv_spec(1), kv_spec(2), kv_spec(3),
            pl.BlockSpec((1, 1, tq, n_slc), lambda i, j, k, sl: (i, j, k, 0)),
            pl.BlockSpec((1, 1, tq, GATE_LANES), lambda i, j, k, sl: (i, j, k, 0)),
            pl.BlockSpec((1, tq, hpg * dh), lambda i, j, k, sl: (i, k, j)),
        ],
        out_specs=pl.BlockSpec((1, tq, hpg * dh), lambda i, j, k, sl: (i, k, j)),
        scratch_shapes=[
            pltpu.VMEM((hpg, tq, 1), F32),
            pltpu.VMEM((hpg, tq, 1), F32),
            pltpu.VMEM((hpg, tq, dh), F32),
        ],
    )
    return pl.pallas_call(
        functools.partial(_nsa_slc_kernel, tq=tq, tk=tk),
        grid_spec=grid_spec,
        out_shape=jax.ShapeDtypeStruct((b, s, nh * dh), BF16),
        compiler_params=_params("arbitrary", "arbitrary", "arbitrary"),
        name="nsa_slc",
    )(slopes, q, kv4, kv4, kv4, kv4, sel, gates, oc)


def kernel(x, a_norm, a_w_in, a_conv, a_w_out, kv_norm, w_kv, cmp_pe_k, cmp_w1_k, cmp_w2_k, cmp_pe_v, cmp_w1_v,
           cmp_w2_v, b_norm, b_w_qg, b_w_o, f_norm, f_w_gu, f_w_down, final_norm):
    b, s, d = x.shape
    nq = N_HEADS * HEAD_DIM
    h = _mixer_a(x, a_norm[0], a_w_in[0].astype(BF16), a_conv[0], a_w_out[0].astype(BF16))
    h = _ffn(h.reshape(b * s, d), f_norm[0], f_w_gu[0].astype(BF16), f_w_down[0].astype(BF16), final_norm)
    h = h.reshape(b, s, d)

    gate_w = b_w_qg[0][:, nq:].reshape(d, N_KV_GROUPS, 3 * HEADS_PER_GROUP)
    gate_w = jnp.pad(gate_w, ((0, 0), (0, 0), (0, GATE_LANES - 3 * HEADS_PER_GROUP)))
    w_qg = jnp.concatenate([b_w_qg[0][:, :nq], gate_w.reshape(d, N_KV_GROUPS * GATE_LANES)], axis=1)
    kvc, kva, q, gates = _proj(h, kv_norm, b_norm[0], w_kv.astype(BF16), w_qg.astype(BF16))

    pe2 = jnp.stack([cmp_pe_k, cmp_pe_v]).reshape(2, 1, L_CMP * HEAD_DIM)
    pe2 = jnp.broadcast_to(pe2, (2, 8, L_CMP * HEAD_DIM)).astype(BF16)
    kvcmp = _compress(kvc, pe2, jnp.stack([cmp_w1_k, cmp_w1_v]).astype(BF16),
                      jnp.stack([cmp_w2_k, cmp_w2_v]).astype(BF16))

    slopes = jnp.exp2(-8.0 * jnp.arange(1, N_HEADS + 1, dtype=F32) / N_HEADS)
    oc, sel = _nsa_cmp(slopes, q, kvcmp, gates)
    o = _nsa_slc(slopes, q, kva, sel, gates, oc)

    out = _ffn(h.reshape(b * s, d), f_norm[1], f_w_gu[1].astype(BF16), f_w_down[1].astype(BF16), final_norm,
               o=o.reshape(b * s, nq), w_o=b_w_o[0].astype(BF16), with_final=True)
    return out.reshape(b, s, d)
```

```python
import functools
import math

import jax
import jax.numpy as jnp
from jax import lax
from jax.experimental import pallas as pl
from jax.experimental.pallas import tpu as pltpu

F32 = jnp.float32
BF16 = jnp.bfloat16

N_HEADS = 16
N_KV_GROUPS = 4
HEADS_PER_GROUP = N_HEADS // N_KV_GROUPS
HEAD_DIM = 64
L_CMP = 32
D_CMP = 16
L_SLC = 64
N_SEL = 16
N_FORCED = 3
WINDOW = 512
N_KV_SETS = 6
CONV_WIDTH = 3
RMS_EPS = 1e-5
NEG_INF = -1e30
LOG2E = math.log2(math.e)
LANES = 128
MAX_SLC_BLOCKS = LANES
POS_LANE = HEAD_DIM
N_SPLIT = 3
VMEM_LIMIT_BYTES = 56 * 1024 * 1024


def _rms(x, g):
    return x * lax.rsqrt(jnp.mean(x * x, axis=-1, keepdims=True) + RMS_EPS) * g


def _dot(a, b):
    return jnp.dot(a, b, preferred_element_type=F32)


def _dot_nt(a, b):
    return lax.dot_general(a, b, (((1,), (1,)), ((), ())), preferred_element_type=F32)


def _params(*sem):
    return pltpu.CompilerParams(dimension_semantics=sem, vmem_limit_bytes=VMEM_LIMIT_BYTES)


def _pos_columns(coarse, fine, lane):
    is_c = (lane >= POS_LANE) & (lane < POS_LANE + N_SPLIT)
    is_f = (lane >= POS_LANE + N_SPLIT) & (lane < POS_LANE + 2 * N_SPLIT)
    return jnp.where(is_c, coarse, jnp.where(is_f, fine, 0.0))


def _mixer_a_kernel(x_ref, g_ref, win_ref, conv_ref, wout_ref, o_ref, ext_ref, *, tm, d):
    @pl.when(pl.program_id(1) == 0)
    def _():
        ext_ref[0:8, :] = jnp.zeros((8, d), F32)

    x = x_ref[0]
    xb = _rms(x, g_ref[...]).astype(BF16)
    b_gate = _dot(xb, win_ref[:, 0:d])
    cv = _dot(xb, win_ref[:, d:2 * d]) * _dot(xb, win_ref[:, 2 * d:3 * d])
    ext_ref[8:8 + tm, :] = cv
    cv1 = ext_ref[7:7 + tm, :]
    cv2 = ext_ref[6:6 + tm, :]
    u = conv_ref[0:1, :] * cv2 + conv_ref[1:2, :] * cv1 + conv_ref[2:3, :] * cv
    ext_ref[0:8, :] = cv[tm - 8:tm, :]
    o_ref[0] = x + _dot((b_gate * u).astype(BF16), wout_ref[...])


def _mixer_a(x, g, w_in, conv_w, w_out, tm=512):
    b, s, d = x.shape
    assert s % tm == 0
    return pl.pallas_call(
        functools.partial(_mixer_a_kernel, tm=tm, d=d),
        grid=(b, s // tm),
        in_specs=[
            pl.BlockSpec((1, tm, d), lambda i, j: (i, j, 0)),
            pl.BlockSpec((1, d), lambda i, j: (0, 0)),
            pl.BlockSpec((d, 3 * d), lambda i, j: (0, 0)),
            pl.BlockSpec((CONV_WIDTH, d), lambda i, j: (0, 0)),
            pl.BlockSpec((d, d), lambda i, j: (0, 0)),
        ],
        out_specs=pl.BlockSpec((1, tm, d), lambda i, j: (i, j, 0)),
        out_shape=jax.ShapeDtypeStruct((b, s, d), F32),
        scratch_shapes=[pltpu.VMEM((tm + 8, d), F32)],
        compiler_params=_params("arbitrary", "arbitrary"),
        name="mixer_a",
    )(x, g.reshape(1, d), w_in, conv_w, w_out)


def _ffn_kernel(*refs, with_proj, with_final):
    if with_proj:
        h_ref, o_ref_in, wo_ref, g_ref, wg_ref, wu_ref, wd_ref, fg_ref, out_ref, xn_ref, acc_ref, res_ref = refs
    else:
        h_ref, g_ref, wg_ref, wu_ref, wd_ref, fg_ref, out_ref, xn_ref, acc_ref, res_ref = refs
    j = pl.program_id(1)

    @pl.when(j == 0)
    def _():
        h = h_ref[...]
        if with_proj:
            h = h + _dot(o_ref_in[...], wo_ref[...])
        res_ref[...] = h
        xn_ref[...] = _rms(h, g_ref[...]).astype(BF16)
        acc_ref[...] = jnp.zeros_like(acc_ref)

    xb = xn_ref[...]
    gate = _dot(xb, wg_ref[...])
    up = _dot(xb, wu_ref[...])
    act = (gate * jax.nn.sigmoid(gate)) * up
    acc_ref[...] += _dot(act.astype(BF16), wd_ref[...])

    @pl.when(j == pl.num_programs(1) - 1)
    def _():
        y = res_ref[...] + acc_ref[...]
        if with_final:
            y = _rms(y, fg_ref[...])
        out_ref[...] = y


def _ffn(h, g, w_gu, w_down, final_g, o=None, w_o=None, with_final=False, tm=512, tf=1408):
    t, d = h.shape
    f = w_down.shape[0]
    assert t % tm == 0 and f % tf == 0
    nf = f // tf
    with_proj = o is not None
    in_specs = [pl.BlockSpec((tm, d), lambda i, j: (i, 0))]
    args = [h]
    if with_proj:
        in_specs += [pl.BlockSpec((tm, o.shape[1]), lambda i, j: (i, 0)),
                     pl.BlockSpec(w_o.shape, lambda i, j: (0, 0))]
        args += [o, w_o]
    in_specs += [
        pl.BlockSpec((1, d), lambda i, j: (0, 0)),
        pl.BlockSpec((d, tf), lambda i, j: (0, j)),
        pl.BlockSpec((d, tf), lambda i, j: (0, nf + j)),
        pl.BlockSpec((tf, d), lambda i, j: (j, 0)),
        pl.BlockSpec((1, d), lambda i, j: (0, 0)),
    ]
    args += [g.reshape(1, d), w_gu, w_gu, w_down, final_g.reshape(1, d)]
    return pl.pallas_call(
        functools.partial(_ffn_kernel, with_proj=with_proj, with_final=with_final),
        grid=(t // tm, nf),
        in_specs=in_specs,
        out_specs=pl.BlockSpec((tm, d), lambda i, j: (i, 0)),
        out_shape=jax.ShapeDtypeStruct((t, d), F32),
        scratch_shapes=[pltpu.VMEM((tm, d), BF16), pltpu.VMEM((tm, d), F32), pltpu.VMEM((tm, d), F32)],
        compiler_params=_params("arbitrary", "arbitrary"),
        name="ffn_proj" if with_proj else "ffn",
    )(*args)


_KC0, _VC0, _KS0, _VS0, _KW0, _VW0 = 0, 256, 512, 1024, 1280, 1792
_KV_COLS = 2048


def _proj_kernel(h_ref, gkv_ref, gq_ref, wkv_ref, wqg_ref, qc_ref,
                 kvc_ref, kas_ref, vts_ref, kaw_ref, vtw_ref, q_ref, gate_ref, *, tm):
    g, dh = N_KV_GROUPS, HEAD_DIM
    s0 = pl.program_id(1) * tm
    x = h_ref[0]
    y = x * lax.rsqrt(jnp.mean(x * x, axis=-1, keepdims=True) + RMS_EPS)
    kv = _dot((y * gkv_ref[...]).astype(BF16), wkv_ref[...])
    for grp in range(g):
        kvc_ref[0, 0, grp] = kv[:, _KC0 + grp * dh:_KC0 + (grp + 1) * dh].astype(BF16)
        kvc_ref[1, 0, grp] = kv[:, _VC0 + grp * dh:_VC0 + (grp + 1) * dh].astype(BF16)

    kpos = s0 + lax.broadcasted_iota(jnp.int32, (tm, LANES), 0)
    lane = lax.broadcasted_iota(jnp.int32, (tm, LANES), 1)
    kblk = jnp.right_shift(kpos, L_SLC.bit_length() - 1)
    posc = _pos_columns(kblk.astype(F32), jnp.bitwise_and(kpos, L_SLC - 1).astype(F32), lane)
    onehot = (lane == kblk).astype(BF16)
    for grp in range(g):
        kas_ref[0, grp, :, 0:LANES] = (kv[:, _KS0 + grp * LANES:_KS0 + (grp + 1) * LANES] + posc).astype(BF16)
        kas_ref[0, grp, :, LANES:2 * LANES] = onehot
        kaw_ref[0, grp] = (kv[:, _KW0 + grp * LANES:_KW0 + (grp + 1) * LANES] + posc).astype(BF16)
    for pair in range(g // 2):
        vt = kv[:, _VS0 + pair * LANES:_VS0 + (pair + 1) * LANES].T
        vts_ref[0, 2 * pair] = vt[0:dh].astype(BF16)
        vts_ref[0, 2 * pair + 1] = vt[dh:2 * dh].astype(BF16)
        vt = kv[:, _VW0 + pair * LANES:_VW0 + (pair + 1) * LANES].T
        vtw_ref[0, 2 * pair] = vt[0:dh].astype(BF16)
        vtw_ref[0, 2 * pair + 1] = vt[dh:2 * dh].astype(BF16)

    qg = _dot((y * gq_ref[...]).astype(BF16), wqg_ref[...])
    for hd in range(N_HEADS):
        qh = qg[:, hd * LANES:(hd + 1) * LANES] * (LOG2E * dh ** -0.5) + qc_ref[hd:hd + 1, :]
        q_ref[0, hd] = qh.astype(BF16)
    nq = N_HEADS * LANES
    for grp in range(g):
        gate_ref[0, grp] = jax.nn.sigmoid(qg[:, nq + grp * LANES: nq + (grp + 1) * LANES])


def _proj(h, g_kv, g_q, w_kv, w_qg, qconst, tm=512):
    b, s, d = h.shape
    assert s % tm == 0 and s // L_SLC <= MAX_SLC_BLOCKS
    g, dh = N_KV_GROUPS, HEAD_DIM
    return pl.pallas_call(
        functools.partial(_proj_kernel, tm=tm),
        grid=(b, s // tm),
        in_specs=[
            pl.BlockSpec((1, tm, d), lambda i, j: (i, j, 0)),
            pl.BlockSpec((1, d), lambda i, j: (0, 0)),
            pl.BlockSpec((1, d), lambda i, j: (0, 0)),
            pl.BlockSpec(w_kv.shape, lambda i, j: (0, 0)),
            pl.BlockSpec(w_qg.shape, lambda i, j: (0, 0)),
            pl.BlockSpec(qconst.shape, lambda i, j: (0, 0)),
        ],
        out_specs=[
            pl.BlockSpec((2, 1, g, tm, dh), lambda i, j: (0, i, 0, j, 0)),
            pl.BlockSpec((1, g, tm, 2 * LANES), lambda i, j: (i, 0, j, 0)),
            pl.BlockSpec((1, g, dh, tm), lambda i, j: (i, 0, 0, j)),
            pl.BlockSpec((1, g, tm, LANES), lambda i, j: (i, 0, j, 0)),
            pl.BlockSpec((1, g, dh, tm), lambda i, j: (i, 0, 0, j)),
            pl.BlockSpec((1, N_HEADS, tm, LANES), lambda i, j: (i, 0, j, 0)),
            pl.BlockSpec((1, g, tm, LANES), lambda i, j: (i, 0, j, 0)),
        ],
        out_shape=[
            jax.ShapeDtypeStruct((2, b, g, s, dh), BF16),
            jax.ShapeDtypeStruct((b, g, s, 2 * LANES), BF16),
            jax.ShapeDtypeStruct((b, g, dh, s), BF16),
            jax.ShapeDtypeStruct((b, g, s, LANES), BF16),
            jax.ShapeDtypeStruct((b, g, dh, s), BF16),
            jax.ShapeDtypeStruct((b, N_HEADS, s, LANES), BF16),
            jax.ShapeDtypeStruct((b, g, s, LANES), F32),
        ],
        compiler_params=_params("arbitrary", "arbitrary"),
        name="proj",
    )(h, g_kv.reshape(1, d), g_q.reshape(1, d), w_kv, w_qg, qconst)


def _compress_kernel(r_ref, pe_ref, w1_ref, w2_ref, kca_ref, vct_ref, *, m):
    half = D_CMP * HEAD_DIM
    row = lax.broadcasted_iota(jnp.int32, (m, LANES), 0)
    lane = lax.broadcasted_iota(jnp.int32, (m, LANES), 1)

    def mlp(i):
        r = r_ref[i, 0]
        top = _dot(r, w1_ref[i, 0:half, :])
        bot = _dot(r, w1_ref[i, half:2 * half, :])
        pe_term = _dot(pe_ref[i], w1_ref[i])[0:1, :]
        hid = top + pltpu.roll(bot, m - 1, axis=0) + pe_term
        hid = hid * jax.nn.sigmoid(hid)
        out = _dot(hid.astype(BF16), w2_ref[i])
        return jnp.where(row < m - 1, out, 0.0)

    coarse = jnp.right_shift(row, 2).astype(F32)
    fine = jnp.bitwise_and(row, 3).astype(F32) * D_CMP + (L_CMP - 1) / 2.0
    kca_ref[0] = (mlp(0) + _pos_columns(coarse, fine, lane)).astype(BF16)
    vct_ref[0] = mlp(1).T[0:HEAD_DIM].astype(BF16)


def _compress(kvc, pe2, w1_2, w2_2):
    _, b, g, s, dh = kvc.shape
    m = s // D_CMP
    r = kvc.reshape(2, b * g, m, D_CMP * dh)
    hid = w1_2.shape[-1]
    return pl.pallas_call(
        functools.partial(_compress_kernel, m=m),
        grid=(b * g,),
        in_specs=[
            pl.BlockSpec((2, 1, m, D_CMP * dh), lambda j: (0, j, 0, 0)),
            pl.BlockSpec((2, 8, L_CMP * dh), lambda j: (0, 0, 0)),
            pl.BlockSpec((2, L_CMP * dh, hid), lambda j: (0, 0, 0)),
            pl.BlockSpec((2, hid, LANES), lambda j: (0, 0, 0)),
        ],
        out_specs=[
            pl.BlockSpec((1, m, LANES), lambda j: (j, 0, 0)),
            pl.BlockSpec((1, dh, m), lambda j: (j, 0, 0)),
        ],
        out_shape=[
            jax.ShapeDtypeStruct((b * g, m, LANES), BF16),
            jax.ShapeDtypeStruct((b * g, dh, m), BF16),
        ],
        compiler_params=_params("arbitrary"),
        name="compress",
    )(r, pe2, w1_2, w2_2)


def _head_stack(ot, gt, branch, tq):
    return jnp.concatenate(
        [ot[:, h * tq:(h + 1) * tq] * gt[3 * h + branch:3 * h + branch + 1, :] for h in range(HEADS_PER_GROUP)],
        axis=0)


def _nsa_cmp_kernel(q_ref, kca_ref, vct_ref, gate_ref, oc_ref, negm_ref, *, tq, m):
    q0 = pl.program_id(2) * tq
    hpg = HEADS_PER_GROUP
    ncol = hpg * tq
    q = q_ref[0].reshape(ncol, LANES)
    st = _dot_nt(kca_ref[0], q)
    tcol = q0 + jnp.bitwise_and(lax.broadcasted_iota(jnp.int32, (1, ncol), 1), tq - 1)
    nrow = lax.broadcasted_iota(jnp.int32, (m, 1), 0)
    vis = (nrow * D_CMP + (L_CMP - 1)) <= tcol
    st = jnp.where(vis, st, NEG_INF)
    e = jnp.exp2(st - jnp.max(st, axis=0, keepdims=True))
    p = e * (1.0 / jnp.sum(e, axis=0, keepdims=True)) * vis.astype(F32)
    ot = _dot(vct_ref[0], p.astype(BF16))
    gt = gate_ref[0, 0].T
    oc_ref[0] = _head_stack(ot, gt, 0, tq).T

    pt = p[:, 0:tq]
    for h in range(1, hpg):
        pt = pt + p[:, h * tq:(h + 1) * tq]
    jj = lax.broadcasted_iota(jnp.int32, (MAX_SLC_BLOCKS, m), 0) * L_SLC
    nn = lax.broadcasted_iota(jnp.int32, (MAX_SLC_BLOCKS, m), 1) * D_CMP
    overlap_t = ((nn < jj + L_SLC) & (nn + L_CMP > jj)).astype(BF16)
    p_hi = pt.astype(BF16)
    p_lo = (pt - p_hi.astype(F32)).astype(BF16)
    imp = _dot(overlap_t, p_hi) + _dot(overlap_t, p_lo)

    jrow = lax.broadcasted_iota(jnp.int32, (MAX_SLC_BLOCKS, tq), 0)
    cur = jnp.right_shift(q0 + lax.broadcasted_iota(jnp.int32, (1, tq), 1), L_SLC.bit_length() - 1)
    valid = jrow <= cur
    forced = (jrow == 0) | (jrow == cur) | (jrow == cur - 1)
    val = jnp.where(valid & jnp.logical_not(forced), imp, NEG_INF)
    jrow_f = jrow.astype(F32)
    sel = forced
    for _ in range(N_SEL - N_FORCED):
        mx = jnp.max(val, axis=0, keepdims=True)
        first = jnp.min(jnp.where(val == mx, jrow_f, float(MAX_SLC_BLOCKS)), axis=0, keepdims=True)
        hit = jrow_f == first
        sel = sel | hit
        val = jnp.where(hit, -3.0e38, val)
    negm = jnp.where(sel & valid, 0.0, NEG_INF)
    negm_ref[0, 0] = negm.T.astype(BF16)


def _nsa_cmp(q, kca, vct, gates, tq=128):
    b, nh, s, _ = q.shape
    g, hpg, dh = N_KV_GROUPS, HEADS_PER_GROUP, HEAD_DIM
    m = kca.shape[1]
    assert s % tq == 0 and tq % L_SLC == 0 and tq & (tq - 1) == 0
    return pl.pallas_call(
        functools.partial(_nsa_cmp_kernel, tq=tq, m=m),
        grid=(b, g, s // tq),
        in_specs=[
            pl.BlockSpec((1, hpg, tq, LANES), lambda i, j, k: (i, j, k, 0)),
            pl.BlockSpec((1, m, LANES), lambda i, j, k: (i * N_KV_GROUPS + j, 0, 0)),
            pl.BlockSpec((1, dh, m), lambda i, j, k: (i * N_KV_GROUPS + j, 0, 0)),
            pl.BlockSpec((1, 1, tq, LANES), lambda i, j, k: (i, j, k, 0)),
        ],
        out_specs=[
            pl.BlockSpec((1, tq, hpg * dh), lambda i, j, k: (i, k, j)),
            pl.BlockSpec((1, 1, tq, MAX_SLC_BLOCKS), lambda i, j, k: (i, j, k, 0)),
        ],
        out_shape=[
            jax.ShapeDtypeStruct((b, s, nh * dh), F32),
            jax.ShapeDtypeStruct((b, g, s, MAX_SLC_BLOCKS), BF16),
        ],
        compiler_params=_params("arbitrary", "arbitrary", "arbitrary"),
        name="nsa_cmp",
    )(q, kca, vct, gates)


def _nsa_slc_kernel(q_ref, kas_ref, vts_ref, kaw_ref, vtw_ref, negm_ref, gate_ref, oc_ref, o_ref,
                    m_ref, l_ref, acc_ref, *, tq, tk, unroll):
    q0 = pl.program_id(2) * tq
    hpg = HEADS_PER_GROUP
    ncol = hpg * tq
    q = q_ref[0].reshape(ncol, LANES)
    negm = negm_ref[0, 0]
    qa = jnp.concatenate([q, jnp.concatenate([negm] * hpg, axis=0)], axis=1)
    tcol = q0 + jnp.bitwise_and(lax.broadcasted_iota(jnp.int32, (1, ncol), 1), tq - 1)

    m_ref[...] = jnp.full_like(m_ref, NEG_INF)
    l_ref[...] = jnp.zeros_like(l_ref)
    acc_ref[...] = jnp.zeros_like(acc_ref)

    last_start = kas_ref.shape[2] - tk

    def group(t0, causal):
        starts, scores = [], []
        for u in range(unroll):
            ti = t0 + u
            kstart = pl.multiple_of(jnp.minimum(ti * tk, last_start) if causal else ti * tk, tk)
            st = _dot_nt(kas_ref[0, 0, pl.ds(kstart, tk), :], qa)
            if causal:
                kpos = ti * tk + lax.broadcasted_iota(jnp.int32, (tk, 1), 0)
                st = jnp.where(kpos <= tcol, st, NEG_INF)
            starts.append(kstart)
            scores.append(st)
        for kstart, st in zip(starts, scores):
            m_old = m_ref[...]
            m_new = jnp.maximum(m_old, jnp.max(st, axis=0, keepdims=True))
            alpha = jnp.exp2(m_old - m_new)
            p = jnp.exp2(st - m_new)
            l_ref[...] = alpha * l_ref[...] + jnp.sum(p, axis=0, keepdims=True)
            acc_ref[...] = alpha * acc_ref[...] + _dot(vts_ref[0, 0, :, pl.ds(kstart, tk)], p.astype(BF16))
            m_ref[...] = m_new

    n_plain = (q0 // tk) // unroll

    def body(i, carry):
        group(i * unroll, False)
        return carry

    lax.fori_loop(0, n_plain, body, 0)
    group(n_plain * unroll, True)
    o_slc_t = acc_ref[...] * (1.0 / l_ref[...])

    wk = WINDOW + tq
    wstart = pl.multiple_of(jnp.maximum(q0 - WINDOW, 0), tq)
    st = _dot_nt(kaw_ref[0, 0, pl.ds(wstart, wk), :], q)
    dist = tcol - (wstart + lax.broadcasted_iota(jnp.int32, (wk, 1), 0))
    st = jnp.where((dist >= 0) & (dist < WINDOW), st, NEG_INF)
    e = jnp.exp2(st - jnp.max(st, axis=0, keepdims=True))
    p = e * (1.0 / jnp.sum(e, axis=0, keepdims=True))
    o_win_t = _dot(vtw_ref[0, 0, :, pl.ds(wstart, wk)], p.astype(BF16))

    gt = gate_ref[0, 0].T
    comb = _head_stack(o_slc_t, gt, 1, tq) + _head_stack(o_win_t, gt, 2, tq)
    o_ref[0] = (oc_ref[0] + comb.T).astype(BF16)


def _nsa_slc(q, kas, vts, kaw, vtw, negm, gates, oc, tq=128, tk=512, unroll=4):
    b, nh, s, _ = q.shape
    g, hpg, dh = N_KV_GROUPS, HEADS_PER_GROUP, HEAD_DIM
    assert s % tk == 0 and tk % tq == 0 and WINDOW % tq == 0 and s >= WINDOW + tq and tq & (tq - 1) == 0
    return pl.pallas_call(
        functools.partial(_nsa_slc_kernel, tq=tq, tk=tk, unroll=unroll),
        grid=(b, g, s // tq),
        in_specs=[
            pl.BlockSpec((1, hpg, tq, LANES), lambda i, j, k: (i, j, k, 0)),
            pl.BlockSpec((1, 1, s, 2 * LANES), lambda i, j, k: (i, j, 0, 0)),
            pl.BlockSpec((1, 1, dh, s), lambda i, j, k: (i, j, 0, 0)),
            pl.BlockSpec((1, 1, s, LANES), lambda i, j, k: (i, j, 0, 0)),
            pl.BlockSpec((1, 1, dh, s), lambda i, j, k: (i, j, 0, 0)),
            pl.BlockSpec((1, 1, tq, MAX_SLC_BLOCKS), lambda i, j, k: (i, j, k, 0)),
            pl.BlockSpec((1, 1, tq, LANES), lambda i, j, k: (i, j, k, 0)),
            pl.BlockSpec((1, tq, hpg * dh), lambda i, j, k: (i, k, j)),
        ],
        out_specs=pl.BlockSpec((1, tq, hpg * dh), lambda i, j, k: (i, k, j)),
        out_shape=jax.ShapeDtypeStruct((b, s, nh * dh), BF16),
        scratch_shapes=[
            pltpu.VMEM((1, hpg * tq), F32),
            pltpu.VMEM((1, hpg * tq), F32),
            pltpu.VMEM((dh, hpg * tq), F32),
        ],
        compiler_params=_params("arbitrary", "arbitrary", "arbitrary"),
        name="nsa_slc",
    )(q, kas, vts, kaw, vtw, negm, gates, oc)


def _split_bf16(x, n):
    pieces = []
    for _ in range(n):
        p = x.astype(BF16).astype(F32)
        pieces.append(p)
        x = x - p
    return pieces


def _q_constants():
    slopes = jnp.exp2(-8.0 * jnp.arange(1, N_HEADS + 1, dtype=F32) / N_HEADS) * LOG2E
    pieces = jnp.stack(_split_bf16(slopes, N_SPLIT), axis=1)
    qc = jnp.zeros((N_HEADS, LANES), F32)
    qc = qc.at[:, POS_LANE:POS_LANE + N_SPLIT].set(pieces * L_SLC)
    qc = qc.at[:, POS_LANE + N_SPLIT:POS_LANE + 2 * N_SPLIT].set(pieces)
    return qc


def _kv_weight(w_kv):
    d = w_kv.shape[0]
    w = w_kv.reshape(d, N_KV_SETS, N_KV_GROUPS, HEAD_DIM)

    def flat(x):
        return x.reshape(d, N_KV_GROUPS * HEAD_DIM)

    def padded(x):
        return jnp.pad(x, ((0, 0), (0, 0), (0, LANES - HEAD_DIM))).reshape(d, N_KV_GROUPS * LANES)

    out = jnp.concatenate([flat(w[:, 0]), flat(w[:, 1]), padded(w[:, 2]), flat(w[:, 3]), padded(w[:, 4]),
                           flat(w[:, 5])], axis=1)
    assert out.shape[1] == _KV_COLS
    return out.astype(BF16)


def _qg_weight(w_qg):
    d = w_qg.shape[0]
    nq = N_HEADS * HEAD_DIM
    wq = jnp.pad(w_qg[:, :nq].reshape(d, N_HEADS, HEAD_DIM), ((0, 0), (0, 0), (0, LANES - HEAD_DIM)))
    wg = jnp.pad(w_qg[:, nq:].reshape(d, N_KV_GROUPS, 3 * HEADS_PER_GROUP),
                 ((0, 0), (0, 0), (0, LANES - 3 * HEADS_PER_GROUP)))
    return jnp.concatenate([wq.reshape(d, N_HEADS * LANES), wg.reshape(d, N_KV_GROUPS * LANES)], axis=1).astype(BF16)


def kernel(x, a_norm, a_w_in, a_conv, a_w_out, kv_norm, w_kv, cmp_pe_k, cmp_w1_k, cmp_w2_k, cmp_pe_v, cmp_w1_v,
           cmp_w2_v, b_norm, b_w_qg, b_w_o, f_norm, f_w_gu, f_w_down, final_norm):
    b, s, d = x.shape
    nq = N_HEADS * HEAD_DIM
    h = _mixer_a(x, a_norm[0], a_w_in[0].astype(BF16), a_conv[0], a_w_out[0].astype(BF16))
    h = _ffn(h.reshape(b * s, d), f_norm[0], f_w_gu[0].astype(BF16), f_w_down[0].astype(BF16), final_norm)
    h = h.reshape(b, s, d)

    kvc, kas, vts, kaw, vtw, q, gates = _proj(h, kv_norm, b_norm[0], _kv_weight(w_kv), _qg_weight(b_w_qg[0]),
                                              _q_constants())
    pe2 = jnp.stack([cmp_pe_k, cmp_pe_v]).reshape(2, 1, L_CMP * HEAD_DIM)
    pe2 = jnp.broadcast_to(pe2, (2, 8, L_CMP * HEAD_DIM)).astype(BF16)
    w2 = jnp.pad(jnp.stack([cmp_w2_k, cmp_w2_v]), ((0, 0), (0, 0), (0, LANES - HEAD_DIM))).astype(BF16)
    kca, vct = _compress(kvc, pe2, jnp.stack([cmp_w1_k, cmp_w1_v]).astype(BF16), w2)

    oc, negm = _nsa_cmp(q, kca, vct, gates)
    o = _nsa_slc(q, kas, vts, kaw, vtw, negm, gates, oc)

    out = _ffn(h.reshape(b * s, d), f_norm[1], f_w_gu[1].astype(BF16), f_w_down[1].astype(BF16), final_norm,
               o=o.reshape(b * s, nq), w_o=b_w_o[0].astype(BF16), with_final=True)
    return out.reshape(b, s, d)
```

```python
import functools
import math

import jax
import jax.numpy as jnp
from jax import lax
from jax.experimental import pallas as pl
from jax.experimental.pallas import tpu as pltpu

F32 = jnp.float32
BF16 = jnp.bfloat16

N_HEADS = 16
N_KV_GROUPS = 4
HEADS_PER_GROUP = N_HEADS // N_KV_GROUPS
HEAD_DIM = 64
L_CMP = 32
D_CMP = 16
L_SLC = 64
N_SEL = 16
N_FORCED = 3
WINDOW = 512
N_KV_SETS = 6
CONV_WIDTH = 3
RMS_EPS = 1e-5
NEG_INF = -1e30
LOG2E = math.log2(math.e)
LANES = 128
MAX_SLC_BLOCKS = LANES
SUPER_KEYS = LANES
SUPER_SHIFT = (SUPER_KEYS // L_SLC).bit_length() - 1
MAX_SUPER = MAX_SLC_BLOCKS * L_SLC // SUPER_KEYS
CHUNK_SUPERS = 4
FAR_POS = 1 << 30
POS_LANE = HEAD_DIM
N_SPLIT = 3
VMEM_LIMIT_BYTES = 56 * 1024 * 1024


def _rms(x, g):
    return x * lax.rsqrt(jnp.mean(x * x, axis=-1, keepdims=True) + RMS_EPS) * g


def _dot(a, b):
    return jnp.dot(a, b, preferred_element_type=F32)


def _dot_nt(a, b):
    return lax.dot_general(a, b, (((1,), (1,)), ((), ())), preferred_element_type=F32)


def _params(*sem):
    return pltpu.CompilerParams(dimension_semantics=sem, vmem_limit_bytes=VMEM_LIMIT_BYTES)


def _pos_columns(coarse, fine, lane):
    is_c = (lane >= POS_LANE) & (lane < POS_LANE + N_SPLIT)
    is_f = (lane >= POS_LANE + N_SPLIT) & (lane < POS_LANE + 2 * N_SPLIT)
    return jnp.where(is_c, coarse, jnp.where(is_f, fine, 0.0))


def _mixer_a_kernel(x_ref, g_ref, win_ref, conv_ref, wout_ref, o_ref, ext_ref, *, tm, d):
    @pl.when(pl.program_id(1) == 0)
    def _():
        ext_ref[0:8, :] = jnp.zeros((8, d), F32)

    x = x_ref[0]
    xb = _rms(x, g_ref[...]).astype(BF16)
    b_gate = _dot(xb, win_ref[:, 0:d])
    cv = _dot(xb, win_ref[:, d:2 * d]) * _dot(xb, win_ref[:, 2 * d:3 * d])
    ext_ref[8:8 + tm, :] = cv
    cv1 = ext_ref[7:7 + tm, :]
    cv2 = ext_ref[6:6 + tm, :]
    u = conv_ref[0:1, :] * cv2 + conv_ref[1:2, :] * cv1 + conv_ref[2:3, :] * cv
    ext_ref[0:8, :] = cv[tm - 8:tm, :]
    o_ref[0] = x + _dot((b_gate * u).astype(BF16), wout_ref[...])


def _mixer_a(x, g, w_in, conv_w, w_out, tm=512):
    b, s, d = x.shape
    assert s % tm == 0
    return pl.pallas_call(
        functools.partial(_mixer_a_kernel, tm=tm, d=d),
        grid=(b, s // tm),
        in_specs=[
            pl.BlockSpec((1, tm, d), lambda i, j: (i, j, 0)),
            pl.BlockSpec((1, d), lambda i, j: (0, 0)),
            pl.BlockSpec((d, 3 * d), lambda i, j: (0, 0)),
            pl.BlockSpec((CONV_WIDTH, d), lambda i, j: (0, 0)),
            pl.BlockSpec((d, d), lambda i, j: (0, 0)),
        ],
        out_specs=pl.BlockSpec((1, tm, d), lambda i, j: (i, j, 0)),
        out_shape=jax.ShapeDtypeStruct((b, s, d), F32),
        scratch_shapes=[pltpu.VMEM((tm + 8, d), F32)],
        compiler_params=_params("arbitrary", "arbitrary"),
        name="mixer_a",
    )(x, g.reshape(1, d), w_in, conv_w, w_out)


def _ffn_kernel(*refs, with_proj, with_final):
    if with_proj:
        h_ref, o_ref_in, wo_ref, g_ref, wg_ref, wu_ref, wd_ref, fg_ref, out_ref, xn_ref, acc_ref, res_ref = refs
    else:
        h_ref, g_ref, wg_ref, wu_ref, wd_ref, fg_ref, out_ref, xn_ref, acc_ref, res_ref = refs
    j = pl.program_id(1)

    @pl.when(j == 0)
    def _():
        h = h_ref[...]
        if with_proj:
            h = h + _dot(o_ref_in[...], wo_ref[...])
        res_ref[...] = h
        xn_ref[...] = _rms(h, g_ref[...]).astype(BF16)
        acc_ref[...] = jnp.zeros_like(acc_ref)

    xb = xn_ref[...]
    gate = _dot(xb, wg_ref[...])
    up = _dot(xb, wu_ref[...])
    act = (gate * jax.nn.sigmoid(gate)) * up
    acc_ref[...] += _dot(act.astype(BF16), wd_ref[...])

    @pl.when(j == pl.num_programs(1) - 1)
    def _():
        y = res_ref[...] + acc_ref[...]
        if with_final:
            y = _rms(y, fg_ref[...])
        out_ref[...] = y


def _ffn(h, g, w_gu, w_down, final_g, o=None, w_o=None, with_final=False, tm=512, tf=1408):
    t, d = h.shape
    f = w_down.shape[0]
    assert t % tm == 0 and f % tf == 0
    nf = f // tf
    with_proj = o is not None
    in_specs = [pl.BlockSpec((tm, d), lambda i, j: (i, 0))]
    args = [h]
    if with_proj:
        in_specs += [pl.BlockSpec((tm, o.shape[1]), lambda i, j: (i, 0)),
                     pl.BlockSpec(w_o.shape, lambda i, j: (0, 0))]
        args += [o, w_o]
    in_specs += [
        pl.BlockSpec((1, d), lambda i, j: (0, 0)),
        pl.BlockSpec((d, tf), lambda i, j: (0, j)),
        pl.BlockSpec((d, tf), lambda i, j: (0, nf + j)),
        pl.BlockSpec((tf, d), lambda i, j: (j, 0)),
        pl.BlockSpec((1, d), lambda i, j: (0, 0)),
    ]
    args += [g.reshape(1, d), w_gu, w_gu, w_down, final_g.reshape(1, d)]
    return pl.pallas_call(
        functools.partial(_ffn_kernel, with_proj=with_proj, with_final=with_final),
        grid=(t // tm, nf),
        in_specs=in_specs,
        out_specs=pl.BlockSpec((tm, d), lambda i, j: (i, 0)),
        out_shape=jax.ShapeDtypeStruct((t, d), F32),
        scratch_shapes=[pltpu.VMEM((tm, d), BF16), pltpu.VMEM((tm, d), F32), pltpu.VMEM((tm, d), F32)],
        compiler_params=_params("arbitrary", "arbitrary"),
        name="ffn_proj" if with_proj else "ffn",
    )(*args)


_KC0, _VC0, _KS0, _VS0, _KW0, _VW0 = 0, 256, 512, 1024, 1280, 1792
_KV_COLS = 2048


def _proj_kernel(h_ref, gkv_ref, gq_ref, wkv_ref, wqg_ref, qc_ref,
                 kvc_ref, kas_ref, vts_ref, kaw_ref, vtw_ref, q_ref, gate_ref, *, tm):
    g, dh = N_KV_GROUPS, HEAD_DIM
    s0 = pl.program_id(1) * tm
    x = h_ref[0]
    y = x * lax.rsqrt(jnp.mean(x * x, axis=-1, keepdims=True) + RMS_EPS)
    kv = _dot((y * gkv_ref[...]).astype(BF16), wkv_ref[...])
    for grp in range(g):
        kvc_ref[0, 0, grp] = kv[:, _KC0 + grp * dh:_KC0 + (grp + 1) * dh].astype(BF16)
        kvc_ref[1, 0, grp] = kv[:, _VC0 + grp * dh:_VC0 + (grp + 1) * dh].astype(BF16)

    kpos = s0 + lax.broadcasted_iota(jnp.int32, (tm, LANES), 0)
    lane = lax.broadcasted_iota(jnp.int32, (tm, LANES), 1)
    kblk = jnp.right_shift(kpos, L_SLC.bit_length() - 1)
    posc = _pos_columns(kblk.astype(F32), jnp.bitwise_and(kpos, L_SLC - 1).astype(F32), lane)
    onehot = (lane == kblk).astype(BF16)
    for grp in range(g):
        kas_ref[0, grp, :, 0:LANES] = (kv[:, _KS0 + grp * LANES:_KS0 + (grp + 1) * LANES] + posc).astype(BF16)
        kas_ref[0, grp, :, LANES:2 * LANES] = onehot
        kaw_ref[0, grp] = (kv[:, _KW0 + grp * LANES:_KW0 + (grp + 1) * LANES] + posc).astype(BF16)
    for pair in range(g // 2):
        vt = kv[:, _VS0 + pair * LANES:_VS0 + (pair + 1) * LANES].T
        vts_ref[0, 2 * pair] = vt[0:dh].astype(BF16)
        vts_ref[0, 2 * pair + 1] = vt[dh:2 * dh].astype(BF16)
        vt = kv[:, _VW0 + pair * LANES:_VW0 + (pair + 1) * LANES].T
        vtw_ref[0, 2 * pair] = vt[0:dh].astype(BF16)
        vtw_ref[0, 2 * pair + 1] = vt[dh:2 * dh].astype(BF16)

    qg = _dot((y * gq_ref[...]).astype(BF16), wqg_ref[...])
    for hd in range(N_HEADS):
        qh = qg[:, hd * LANES:(hd + 1) * LANES] * (LOG2E * dh ** -0.5) + qc_ref[hd:hd + 1, :]
        q_ref[0, hd] = qh.astype(BF16)
    nq = N_HEADS * LANES
    for grp in range(g):
        gate_ref[0, grp] = jax.nn.sigmoid(qg[:, nq + grp * LANES: nq + (grp + 1) * LANES])


def _proj(h, g_kv, g_q, w_kv, w_qg, qconst, tm=512):
    b, s, d = h.shape
    assert s % tm == 0 and s // L_SLC <= MAX_SLC_BLOCKS
    g, dh = N_KV_GROUPS, HEAD_DIM
    return pl.pallas_call(
        functools.partial(_proj_kernel, tm=tm),
        grid=(b, s // tm),
        in_specs=[
            pl.BlockSpec((1, tm, d), lambda i, j: (i, j, 0)),
            pl.BlockSpec((1, d), lambda i, j: (0, 0)),
            pl.BlockSpec((1, d), lambda i, j: (0, 0)),
            pl.BlockSpec(w_kv.shape, lambda i, j: (0, 0)),
            pl.BlockSpec(w_qg.shape, lambda i, j: (0, 0)),
            pl.BlockSpec(qconst.shape, lambda i, j: (0, 0)),
        ],
        out_specs=[
            pl.BlockSpec((2, 1, g, tm, dh), lambda i, j: (0, i, 0, j, 0)),
            pl.BlockSpec((1, g, tm, 2 * LANES), lambda i, j: (i, 0, j, 0)),
            pl.BlockSpec((1, g, dh, tm), lambda i, j: (i, 0, 0, j)),
            pl.BlockSpec((1, g, tm, LANES), lambda i, j: (i, 0, j, 0)),
            pl.BlockSpec((1, g, dh, tm), lambda i, j: (i, 0, 0, j)),
            pl.BlockSpec((1, N_HEADS, tm, LANES), lambda i, j: (i, 0, j, 0)),
            pl.BlockSpec((1, g, tm, LANES), lambda i, j: (i, 0, j, 0)),
        ],
        out_shape=[
            jax.ShapeDtypeStruct((2, b, g, s, dh), BF16),
            jax.ShapeDtypeStruct((b, g, s, 2 * LANES), BF16),
            jax.ShapeDtypeStruct((b, g, dh, s), BF16),
            jax.ShapeDtypeStruct((b, g, s, LANES), BF16),
            jax.ShapeDtypeStruct((b, g, dh, s), BF16),
            jax.ShapeDtypeStruct((b, N_HEADS, s, LANES), BF16),
            jax.ShapeDtypeStruct((b, g, s, LANES), F32),
        ],
        compiler_params=_params("arbitrary", "arbitrary"),
        name="proj",
    )(h, g_kv.reshape(1, d), g_q.reshape(1, d), w_kv, w_qg, qconst)


def _compress_kernel(r_ref, pe_ref, w1_ref, w2_ref, kca_ref, vct_ref, *, m):
    half = D_CMP * HEAD_DIM
    row = lax.broadcasted_iota(jnp.int32, (m, LANES), 0)
    lane = lax.broadcasted_iota(jnp.int32, (m, LANES), 1)

    def mlp(i):
        r = r_ref[i, 0]
        top = _dot(r, w1_ref[i, 0:half, :])
        bot = _dot(r, w1_ref[i, half:2 * half, :])
        pe_term = _dot(pe_ref[i], w1_ref[i])[0:1, :]
        hid = top + pltpu.roll(bot, m - 1, axis=0) + pe_term
        hid = hid * jax.nn.sigmoid(hid)
        out = _dot(hid.astype(BF16), w2_ref[i])
        return jnp.where(row < m - 1, out, 0.0)

    coarse = jnp.right_shift(row, 2).astype(F32)
    fine = jnp.bitwise_and(row, 3).astype(F32) * D_CMP + (L_CMP - 1) / 2.0
    kca_ref[0] = (mlp(0) + _pos_columns(coarse, fine, lane)).astype(BF16)
    vct_ref[0] = mlp(1).T[0:HEAD_DIM].astype(BF16)


def _compress(kvc, pe2, w1_2, w2_2):
    _, b, g, s, dh = kvc.shape
    m = s // D_CMP
    r = kvc.reshape(2, b * g, m, D_CMP * dh)
    hid = w1_2.shape[-1]
    return pl.pallas_call(
        functools.partial(_compress_kernel, m=m),
        grid=(b * g,),
        in_specs=[
            pl.BlockSpec((2, 1, m, D_CMP * dh), lambda j: (0, j, 0, 0)),
            pl.BlockSpec((2, 8, L_CMP * dh), lambda j: (0, 0, 0)),
            pl.BlockSpec((2, L_CMP * dh, hid), lambda j: (0, 0, 0)),
            pl.BlockSpec((2, hid, LANES), lambda j: (0, 0, 0)),
        ],
        out_specs=[
            pl.BlockSpec((1, m, LANES), lambda j: (j, 0, 0)),
            pl.BlockSpec((1, dh, m), lambda j: (j, 0, 0)),
        ],
        out_shape=[
            jax.ShapeDtypeStruct((b * g, m, LANES), BF16),
            jax.ShapeDtypeStruct((b * g, dh, m), BF16),
        ],
        compiler_params=_params("arbitrary"),
        name="compress",
    )(r, pe2, w1_2, w2_2)


def _head_stack(ot, gt, branch, tq):
    return jnp.concatenate(
        [ot[:, h * tq:(h + 1) * tq] * gt[3 * h + branch:3 * h + branch + 1, :] for h in range(HEADS_PER_GROUP)],
        axis=0)


def _nsa_cmp_kernel(q_ref, kca_ref, vct_ref, gate_ref, oc_ref, negm_ref, lst_ref, *, tq, m):
    q0 = pl.program_id(2) * tq
    hpg = HEADS_PER_GROUP
    ncol = hpg * tq
    q = q_ref[0].reshape(ncol, LANES)
    st = _dot_nt(kca_ref[0], q)
    tcol = q0 + jnp.bitwise_and(lax.broadcasted_iota(jnp.int32, (1, ncol), 1), tq - 1)
    nrow = lax.broadcasted_iota(jnp.int32, (m, 1), 0)
    vis = (nrow * D_CMP + (L_CMP - 1)) <= tcol
    st = jnp.where(vis, st, NEG_INF)
    e = jnp.exp2(st - jnp.max(st, axis=0, keepdims=True))
    p = e * (1.0 / jnp.sum(e, axis=0, keepdims=True)) * vis.astype(F32)
    ot = _dot(vct_ref[0], p.astype(BF16))
    gt = gate_ref[0, 0].T
    oc_ref[0] = _head_stack(ot, gt, 0, tq).T

    pt = p[:, 0:tq]
    for h in range(1, hpg):
        pt = pt + p[:, h * tq:(h + 1) * tq]
    jj = lax.broadcasted_iota(jnp.int32, (MAX_SLC_BLOCKS, m), 0) * L_SLC
    nn = lax.broadcasted_iota(jnp.int32, (MAX_SLC_BLOCKS, m), 1) * D_CMP
    overlap_t = ((nn < jj + L_SLC) & (nn + L_CMP > jj)).astype(BF16)
    p_hi = pt.astype(BF16)
    p_lo = (pt - p_hi.astype(F32)).astype(BF16)
    imp = _dot(overlap_t, p_hi) + _dot(overlap_t, p_lo)

    jrow = lax.broadcasted_iota(jnp.int32, (MAX_SLC_BLOCKS, tq), 0)
    cur = jnp.right_shift(q0 + lax.broadcasted_iota(jnp.int32, (1, tq), 1), L_SLC.bit_length() - 1)
    valid = jrow <= cur
    forced = (jrow == 0) | (jrow == cur) | (jrow == cur - 1)
    val = jnp.where(valid & jnp.logical_not(forced), imp, NEG_INF)
    jrow_f = jrow.astype(F32)
    sel = forced
    for _ in range(N_SEL - N_FORCED):
        mx = jnp.max(val, axis=0, keepdims=True)
        first = jnp.min(jnp.where(val == mx, jrow_f, float(MAX_SLC_BLOCKS)), axis=0, keepdims=True)
        hit = jrow_f == first
        sel = sel | hit
        val = jnp.where(hit, -3.0e38, val)
    sel_t = jnp.where(sel & valid, 1.0, 0.0).T
    negm_ref[0, 0] = jnp.where(sel_t > 0.5, 0.0, NEG_INF).astype(BF16)

    used = jnp.broadcast_to(jnp.max(sel_t, axis=0, keepdims=True), (8, MAX_SLC_BLOCKS)).astype(BF16)
    a_i = lax.broadcasted_iota(jnp.int32, (MAX_SLC_BLOCKS, MAX_SLC_BLOCKS), 0)
    b_i = lax.broadcasted_iota(jnp.int32, (MAX_SLC_BLOCKS, MAX_SLC_BLOCKS), 1)
    pair = (jnp.right_shift(a_i, SUPER_SHIFT) == b_i).astype(BF16)
    sbf = (_dot(used, pair) > 0.5).astype(BF16)
    pos = _dot(sbf, (a_i < b_i).astype(BF16))
    slot_t = ((pos[0:1, :] == a_i.astype(F32)) & (sbf[0:1, :] > 0.5)).astype(BF16)
    ids = lax.broadcasted_iota(jnp.int32, (8, MAX_SLC_BLOCKS), 1).astype(BF16)
    lst = _dot_nt(ids, slot_t)
    cnt = _dot(sbf, jnp.ones((MAX_SLC_BLOCKS, MAX_SLC_BLOCKS), BF16))
    row8 = lax.broadcasted_iota(jnp.int32, (8, MAX_SLC_BLOCKS), 0)
    lst_ref[0] = jnp.where(row8 == 1, cnt, lst).astype(jnp.int32)


def _nsa_cmp(q, kca, vct, gates, tq=128):
    b, nh, s, _ = q.shape
    g, hpg, dh = N_KV_GROUPS, HEADS_PER_GROUP, HEAD_DIM
    m = kca.shape[1]
    assert s % tq == 0 and tq % L_SLC == 0 and tq & (tq - 1) == 0
    nqt = s // tq
    return pl.pallas_call(
        functools.partial(_nsa_cmp_kernel, tq=tq, m=m),
        grid=(b, g, s // tq),
        in_specs=[
            pl.BlockSpec((1, hpg, tq, LANES), lambda i, j, k: (i, j, k, 0)),
            pl.BlockSpec((1, m, LANES), lambda i, j, k: (i * N_KV_GROUPS + j, 0, 0)),
            pl.BlockSpec((1, dh, m), lambda i, j, k: (i * N_KV_GROUPS + j, 0, 0)),
            pl.BlockSpec((1, 1, tq, LANES), lambda i, j, k: (i, j, k, 0)),
        ],
        out_specs=[
            pl.BlockSpec((1, tq, hpg * dh), lambda i, j, k: (i, k, j)),
            pl.BlockSpec((1, 1, tq, MAX_SLC_BLOCKS), lambda i, j, k: (i, j, k, 0)),
            pl.BlockSpec((1, 8, MAX_SLC_BLOCKS), lambda i, j, k: ((i * N_KV_GROUPS + j) * nqt + k, 0, 0)),
        ],
        out_shape=[
            jax.ShapeDtypeStruct((b, s, nh * dh), F32),
            jax.ShapeDtypeStruct((b, g, s, MAX_SLC_BLOCKS), BF16),
            jax.ShapeDtypeStruct((b * g * nqt, 8, MAX_SLC_BLOCKS), jnp.int32),
        ],
        compiler_params=_params("arbitrary", "arbitrary", "arbitrary"),
        name="nsa_cmp",
    )(q, kca, vct, gates)


def _nsa_slc_kernel(lst_ref, cnt_ref, q_ref, kas_ref, vts_ref, kaw_ref, vtw_ref, negm_ref, gate_ref, oc_ref, o_ref,
                    m_ref, l_ref, acc_ref, *, tq):
    q0 = pl.program_id(2) * tq
    hpg = HEADS_PER_GROUP
    ncol = hpg * tq
    tk = CHUNK_SUPERS * SUPER_KEYS
    q = q_ref[0].reshape(ncol, LANES)
    negm = negm_ref[0, 0]
    qa = jnp.concatenate([q, jnp.concatenate([negm] * hpg, axis=0)], axis=1)
    tcol = q0 + jnp.bitwise_and(lax.broadcasted_iota(jnp.int32, (1, ncol), 1), tq - 1)

    m_ref[...] = jnp.full_like(m_ref, NEG_INF)
    l_ref[...] = jnp.zeros_like(l_ref)
    acc_ref[...] = jnp.zeros_like(acc_ref)

    step = (pl.program_id(0) * pl.num_programs(1) + pl.program_id(1)) * pl.num_programs(2) + pl.program_id(2)
    n_used = cnt_ref[step]
    base = step * MAX_SUPER
    n_chunks = jnp.right_shift(n_used + CHUNK_SUPERS - 1, CHUNK_SUPERS.bit_length() - 1)

    def gather(chunk, partial):
        ks, vs, kbase = [], [], []
        for j in range(CHUNK_SUPERS):
            slot = chunk * CHUNK_SUPERS + j
            if partial:
                ok = (slot >= 0) & (slot < n_used)
                sb = jnp.where(ok, lst_ref[base + jnp.clip(slot, 0, MAX_SUPER - 1)], 0)
                kbase.append(jnp.where(ok, sb * SUPER_KEYS, FAR_POS))
            else:
                sb = lst_ref[base + slot]
            start = pl.multiple_of(sb * SUPER_KEYS, SUPER_KEYS)
            ks.append(kas_ref[0, 0, pl.ds(start, SUPER_KEYS), :])
            vs.append(vts_ref[0, 0, :, pl.ds(start, SUPER_KEYS)])
        return jnp.concatenate(ks, axis=0), jnp.concatenate(vs, axis=1), kbase

    def scores(kt, kbase):
        st = _dot_nt(kt, qa)
        if kbase:
            r = lax.broadcasted_iota(jnp.int32, (tk, 1), 0)
            rs = jnp.right_shift(r, SUPER_KEYS.bit_length() - 1)
            kpos = jnp.full((tk, 1), kbase[0], jnp.int32)
            for j in range(1, CHUNK_SUPERS):
                kpos = jnp.where(rs == j, kbase[j], kpos)
            kpos = kpos + jnp.bitwise_and(r, SUPER_KEYS - 1)
            st = jnp.where(kpos <= tcol, st, NEG_INF)
        return st

    def update(st, vt):
        m_old = m_ref[...]
        m_new = jnp.maximum(m_old, jnp.max(st, axis=0, keepdims=True))
        alpha = jnp.exp2(m_old - m_new)
        p = jnp.exp2(st - m_new)
        l_ref[...] = alpha * l_ref[...] + jnp.sum(p, axis=0, keepdims=True)
        acc_ref[...] = alpha * acc_ref[...] + _dot(vt, p.astype(BF16))
        m_ref[...] = m_new

    def body(c, carry):
        kt, vt, _ = gather(c, False)
        update(scores(kt, None), vt)
        return carry

    lax.fori_loop(0, jnp.maximum(n_chunks - 2, 0), body, 0)

    kt1, vt1, kb1 = gather(n_chunks - 1, True)
    kt2, vt2, kb2 = gather(n_chunks - 2, True)
    wk = WINDOW + tq
    wstart = pl.multiple_of(jnp.maximum(q0 - WINDOW, 0), tq)
    st1 = scores(kt1, kb1)
    st2 = scores(kt2, kb2)
    stw = _dot_nt(kaw_ref[0, 0, pl.ds(wstart, wk), :], q)
    update(st1, vt1)
    update(st2, vt2)
    o_slc_t = acc_ref[...] * (1.0 / l_ref[...])

    dist = tcol - (wstart + lax.broadcasted_iota(jnp.int32, (wk, 1), 0))
    stw = jnp.where((dist >= 0) & (dist < WINDOW), stw, NEG_INF)
    e = jnp.exp2(stw - jnp.max(stw, axis=0, keepdims=True))
    p = e * (1.0 / jnp.sum(e, axis=0, keepdims=True))
    o_win_t = _dot(vtw_ref[0, 0, :, pl.ds(wstart, wk)], p.astype(BF16))

    gt = gate_ref[0, 0].T
    comb = _head_stack(o_slc_t, gt, 1, tq) + _head_stack(o_win_t, gt, 2, tq)
    o_ref[0] = (oc_ref[0] + comb.T).astype(BF16)


def _nsa_slc(lists, counts, q, kas, vts, kaw, vtw, negm, gates, oc, tq=128):
    b, nh, s, _ = q.shape
    g, hpg, dh = N_KV_GROUPS, HEADS_PER_GROUP, HEAD_DIM
    assert tq == SUPER_KEYS and WINDOW % tq == 0 and s >= WINDOW + tq and s % (CHUNK_SUPERS * SUPER_KEYS) == 0
    grid_spec = pltpu.PrefetchScalarGridSpec(
        num_scalar_prefetch=2,
        grid=(b, g, s // tq),
        in_specs=[
            pl.BlockSpec((1, hpg, tq, LANES), lambda i, j, k, *_: (i, j, k, 0)),
            pl.BlockSpec((1, 1, s, 2 * LANES), lambda i, j, k, *_: (i, j, 0, 0)),
            pl.BlockSpec((1, 1, dh, s), lambda i, j, k, *_: (i, j, 0, 0)),
            pl.BlockSpec((1, 1, s, LANES), lambda i, j, k, *_: (i, j, 0, 0)),
            pl.BlockSpec((1, 1, dh, s), lambda i, j, k, *_: (i, j, 0, 0)),
            pl.BlockSpec((1, 1, tq, MAX_SLC_BLOCKS), lambda i, j, k, *_: (i, j, k, 0)),
            pl.BlockSpec((1, 1, tq, LANES), lambda i, j, k, *_: (i, j, k, 0)),
            pl.BlockSpec((1, tq, hpg * dh), lambda i, j, k, *_: (i, k, j)),
        ],
        out_specs=pl.BlockSpec((1, tq, hpg * dh), lambda i, j, k, *_: (i, k, j)),
        scratch_shapes=[
            pltpu.VMEM((1, hpg * tq), F32),
            pltpu.VMEM((1, hpg * tq), F32),
            pltpu.VMEM((dh, hpg * tq), F32),
        ],
    )
    return pl.pallas_call(
        functools.partial(_nsa_slc_kernel, tq=tq),
        grid_spec=grid_spec,
        out_shape=jax.ShapeDtypeStruct((b, s, nh * dh), BF16),
        compiler_params=_params("arbitrary", "arbitrary", "arbitrary"),
        name="nsa_slc",
    )(lists, counts, q, kas, vts, kaw, vtw, negm, gates, oc)


def _split_bf16(x, n):
    pieces = []
    for _ in range(n):
        p = x.astype(BF16).astype(F32)
        pieces.append(p)
        x = x - p
    return pieces


def _q_constants():
    slopes = jnp.exp2(-8.0 * jnp.arange(1, N_HEADS + 1, dtype=F32) / N_HEADS) * LOG2E
    pieces = jnp.stack(_split_bf16(slopes, N_SPLIT), axis=1)
    qc = jnp.zeros((N_HEADS, LANES), F32)
    qc = qc.at[:, POS_LANE:POS_LANE + N_SPLIT].set(pieces * L_SLC)
    qc = qc.at[:, POS_LANE + N_SPLIT:POS_LANE + 2 * N_SPLIT].set(pieces)
    return qc


def _kv_weight(w_kv):
    d = w_kv.shape[0]
    w = w_kv.reshape(d, N_KV_SETS, N_KV_GROUPS, HEAD_DIM)

    def flat(x):
        return x.reshape(d, N_KV_GROUPS * HEAD_DIM)

    def padded(x):
        return jnp.pad(x, ((0, 0), (0, 0), (0, LANES - HEAD_DIM))).reshape(d, N_KV_GROUPS * LANES)

    out = jnp.concatenate([flat(w[:, 0]), flat(w[:, 1]), padded(w[:, 2]), flat(w[:, 3]), padded(w[:, 4]),
                           flat(w[:, 5])], axis=1)
    assert out.shape[1] == _KV_COLS
    return out.astype(BF16)


def _qg_weight(w_qg):
    d = w_qg.shape[0]
    nq = N_HEADS * HEAD_DIM
    wq = jnp.pad(w_qg[:, :nq].reshape(d, N_HEADS, HEAD_DIM), ((0, 0), (0, 0), (0, LANES - HEAD_DIM)))
    wg = jnp.pad(w_qg[:, nq:].reshape(d, N_KV_GROUPS, 3 * HEADS_PER_GROUP),
                 ((0, 0), (0, 0), (0, LANES - 3 * HEADS_PER_GROUP)))
    return jnp.concatenate([wq.reshape(d, N_HEADS * LANES), wg.reshape(d, N_KV_GROUPS * LANES)], axis=1).astype(BF16)


def kernel(x, a_norm, a_w_in, a_conv, a_w_out, kv_norm, w_kv, cmp_pe_k, cmp_w1_k, cmp_w2_k, cmp_pe_v, cmp_w1_v,
           cmp_w2_v, b_norm, b_w_qg, b_w_o, f_norm, f_w_gu, f_w_down, final_norm):
    b, s, d = x.shape
    nq = N_HEADS * HEAD_DIM
    h = _mixer_a(x, a_norm[0], a_w_in[0].astype(BF16), a_conv[0], a_w_out[0].astype(BF16))
    h = _ffn(h.reshape(b * s, d), f_norm[0], f_w_gu[0].astype(BF16), f_w_down[0].astype(BF16), final_norm)
    h = h.reshape(b, s, d)

    kvc, kas, vts, kaw, vtw, q, gates = _proj(h, kv_norm, b_norm[0], _kv_weight(w_kv), _qg_weight(b_w_qg[0]),
                                              _q_constants())
    pe2 = jnp.stack([cmp_pe_k, cmp_pe_v]).reshape(2, 1, L_CMP * HEAD_DIM)
    pe2 = jnp.broadcast_to(pe2, (2, 8, L_CMP * HEAD_DIM)).astype(BF16)
    w2 = jnp.pad(jnp.stack([cmp_w2_k, cmp_w2_v]), ((0, 0), (0, 0), (0, LANES - HEAD_DIM))).astype(BF16)
    kca, vct = _compress(kvc, pe2, jnp.stack([cmp_w1_k, cmp_w1_v]).astype(BF16), w2)

    oc, negm, used = _nsa_cmp(q, kca, vct, gates)
    o = _nsa_slc(used[:, 0, :MAX_SUPER].reshape(-1), used[:, 1, 0], q, kas, vts, kaw, vtw, negm, gates, oc)

    out = _ffn(h.reshape(b * s, d), f_norm[1], f_w_gu[1].astype(BF16), f_w_down[1].astype(BF16), final_norm,
               o=o.reshape(b * s, nq), w_o=b_w_o[0].astype(BF16), with_final=True)
    return out.reshape(b, s, d)
```

```python
import functools
import math

import jax
import jax.numpy as jnp
from jax import lax
from jax.experimental import pallas as pl
from jax.experimental.pallas import tpu as pltpu

F32 = jnp.float32
BF16 = jnp.bfloat16

N_HEADS = 16
N_KV_GROUPS = 4
HEADS_PER_GROUP = N_HEADS // N_KV_GROUPS
HEAD_DIM = 64
L_CMP = 32
D_CMP = 16
L_SLC = 64
N_SEL = 16
N_FORCED = 3
WINDOW = 512
N_KV_SETS = 6
CONV_WIDTH = 3
RMS_EPS = 1e-5
NEG_INF = -1e30
LOG2E = math.log2(math.e)
LANES = 128
MAX_SLC_BLOCKS = LANES
SUPER_KEYS = LANES
SUPER_SHIFT = (SUPER_KEYS // L_SLC).bit_length() - 1
MAX_SUPER = MAX_SLC_BLOCKS * L_SLC // SUPER_KEYS
CHUNK_SUPERS = 4
FINAL_CHUNKS = 3
CMP_ROW_TILE = 128
FAR_POS = 1 << 30
POS_LANE = HEAD_DIM
N_SPLIT = 3
VMEM_LIMIT_BYTES = 56 * 1024 * 1024


def _rms(x, g):
    return x * lax.rsqrt(jnp.mean(x * x, axis=-1, keepdims=True) + RMS_EPS) * g


def _dot(a, b):
    return jnp.dot(a, b, preferred_element_type=F32)


def _dot_nt(a, b):
    return lax.dot_general(a, b, (((1,), (1,)), ((), ())), preferred_element_type=F32)


def _params(*sem):
    return pltpu.CompilerParams(dimension_semantics=sem, vmem_limit_bytes=VMEM_LIMIT_BYTES)


def _pos_columns(coarse, fine, lane):
    is_c = (lane >= POS_LANE) & (lane < POS_LANE + N_SPLIT)
    is_f = (lane >= POS_LANE + N_SPLIT) & (lane < POS_LANE + 2 * N_SPLIT)
    return jnp.where(is_c, coarse, jnp.where(is_f, fine, 0.0))


def _mixer_a_kernel(x_ref, g_ref, win_ref, conv_ref, wout_ref, o_ref, ext_ref, *, tm, d):
    @pl.when(pl.program_id(1) == 0)
    def _():
        ext_ref[0:8, :] = jnp.zeros((8, d), F32)

    x = x_ref[0]
    xb = _rms(x, g_ref[...]).astype(BF16)
    b_gate = _dot(xb, win_ref[:, 0:d])
    cv = _dot(xb, win_ref[:, d:2 * d]) * _dot(xb, win_ref[:, 2 * d:3 * d])
    ext_ref[8:8 + tm, :] = cv
    cv1 = ext_ref[7:7 + tm, :]
    cv2 = ext_ref[6:6 + tm, :]
    u = conv_ref[0:1, :] * cv2 + conv_ref[1:2, :] * cv1 + conv_ref[2:3, :] * cv
    ext_ref[0:8, :] = cv[tm - 8:tm, :]
    o_ref[0] = x + _dot((b_gate * u).astype(BF16), wout_ref[...])


def _mixer_a(x, g, w_in, conv_w, w_out, tm=512):
    b, s, d = x.shape
    assert s % tm == 0
    return pl.pallas_call(
        functools.partial(_mixer_a_kernel, tm=tm, d=d),
        grid=(b, s // tm),
        in_specs=[
            pl.BlockSpec((1, tm, d), lambda i, j: (i, j, 0)),
            pl.BlockSpec((1, d), lambda i, j: (0, 0)),
            pl.BlockSpec((d, 3 * d), lambda i, j: (0, 0)),
            pl.BlockSpec((CONV_WIDTH, d), lambda i, j: (0, 0)),
            pl.BlockSpec((d, d), lambda i, j: (0, 0)),
        ],
        out_specs=pl.BlockSpec((1, tm, d), lambda i, j: (i, j, 0)),
        out_shape=jax.ShapeDtypeStruct((b, s, d), F32),
        scratch_shapes=[pltpu.VMEM((tm + 8, d), F32)],
        compiler_params=_params("arbitrary", "arbitrary"),
        name="mixer_a",
    )(x, g.reshape(1, d), w_in, conv_w, w_out)


def _ffn_kernel(*refs, with_proj, with_final):
    if with_proj:
        h_ref, o_ref_in, wo_ref, g_ref, wg_ref, wu_ref, wd_ref, fg_ref, out_ref, xn_ref, acc_ref, res_ref = refs
    else:
        h_ref, g_ref, wg_ref, wu_ref, wd_ref, fg_ref, out_ref, xn_ref, acc_ref, res_ref = refs
    j = pl.program_id(1)

    @pl.when(j == 0)
    def _():
        h = h_ref[...]
        if with_proj:
            h = h + _dot(o_ref_in[...], wo_ref[...])
        res_ref[...] = h
        xn_ref[...] = _rms(h, g_ref[...]).astype(BF16)
        acc_ref[...] = jnp.zeros_like(acc_ref)

    xb = xn_ref[...]
    gate = _dot(xb, wg_ref[...])
    up = _dot(xb, wu_ref[...])
    act = (gate * jax.nn.sigmoid(gate)) * up
    acc_ref[...] += _dot(act.astype(BF16), wd_ref[...])

    @pl.when(j == pl.num_programs(1) - 1)
    def _():
        y = res_ref[...] + acc_ref[...]
        if with_final:
            y = _rms(y, fg_ref[...])
        out_ref[...] = y


def _ffn(h, g, w_gu, w_down, final_g, o=None, w_o=None, with_final=False, tm=512, tf=1408):
    t, d = h.shape
    f = w_down.shape[0]
    assert t % tm == 0 and f % tf == 0
    nf = f // tf
    with_proj = o is not None
    in_specs = [pl.BlockSpec((tm, d), lambda i, j: (i, 0))]
    args = [h]
    if with_proj:
        in_specs += [pl.BlockSpec((tm, o.shape[1]), lambda i, j: (i, 0)),
                     pl.BlockSpec(w_o.shape, lambda i, j: (0, 0))]
        args += [o, w_o]
    in_specs += [
        pl.BlockSpec((1, d), lambda i, j: (0, 0)),
        pl.BlockSpec((d, tf), lambda i, j: (0, j)),
        pl.BlockSpec((d, tf), lambda i, j: (0, nf + j)),
        pl.BlockSpec((tf, d), lambda i, j: (j, 0)),
        pl.BlockSpec((1, d), lambda i, j: (0, 0)),
    ]
    args += [g.reshape(1, d), w_gu, w_gu, w_down, final_g.reshape(1, d)]
    return pl.pallas_call(
        functools.partial(_ffn_kernel, with_proj=with_proj, with_final=with_final),
        grid=(t // tm, nf),
        in_specs=in_specs,
        out_specs=pl.BlockSpec((tm, d), lambda i, j: (i, 0)),
        out_shape=jax.ShapeDtypeStruct((t, d), F32),
        scratch_shapes=[pltpu.VMEM((tm, d), BF16), pltpu.VMEM((tm, d), F32), pltpu.VMEM((tm, d), F32)],
        compiler_params=_params("arbitrary", "arbitrary"),
        name="ffn_proj" if with_proj else "ffn",
    )(*args)


_KC0, _VC0, _KS0, _VS0, _KW0, _VW0 = 0, 256, 512, 1024, 1280, 1792
_KV_COLS = 2048


def _proj_kernel(h_ref, gkv_ref, gq_ref, wkv_ref, wqg_ref, qc_ref,
                 kvc_ref, kas_ref, vts_ref, kaw_ref, vtw_ref, q_ref, gate_ref, *, tm):
    g, dh = N_KV_GROUPS, HEAD_DIM
    s0 = pl.program_id(1) * tm
    x = h_ref[0]
    y = x * lax.rsqrt(jnp.mean(x * x, axis=-1, keepdims=True) + RMS_EPS)
    kv = _dot((y * gkv_ref[...]).astype(BF16), wkv_ref[...])
    for grp in range(g):
        kvc_ref[0, 0, grp] = kv[:, _KC0 + grp * dh:_KC0 + (grp + 1) * dh].astype(BF16)
        kvc_ref[1, 0, grp] = kv[:, _VC0 + grp * dh:_VC0 + (grp + 1) * dh].astype(BF16)

    kpos = s0 + lax.broadcasted_iota(jnp.int32, (tm, LANES), 0)
    lane = lax.broadcasted_iota(jnp.int32, (tm, LANES), 1)
    kblk = jnp.right_shift(kpos, L_SLC.bit_length() - 1)
    posc = _pos_columns(kblk.astype(F32), jnp.bitwise_and(kpos, L_SLC - 1).astype(F32), lane)
    onehot = (lane == kblk).astype(BF16)
    for grp in range(g):
        kas_ref[0, grp, :, 0:LANES] = (kv[:, _KS0 + grp * LANES:_KS0 + (grp + 1) * LANES] + posc).astype(BF16)
        kas_ref[0, grp, :, LANES:2 * LANES] = onehot
        kaw_ref[0, grp] = (kv[:, _KW0 + grp * LANES:_KW0 + (grp + 1) * LANES] + posc).astype(BF16)
    for pair in range(g // 2):
        vt = kv[:, _VS0 + pair * LANES:_VS0 + (pair + 1) * LANES].T
        vts_ref[0, 2 * pair] = vt[0:dh].astype(BF16)
        vts_ref[0, 2 * pair + 1] = vt[dh:2 * dh].astype(BF16)
        vt = kv[:, _VW0 + pair * LANES:_VW0 + (pair + 1) * LANES].T
        vtw_ref[0, 2 * pair] = vt[0:dh].astype(BF16)
        vtw_ref[0, 2 * pair + 1] = vt[dh:2 * dh].astype(BF16)

    qg = _dot((y * gq_ref[...]).astype(BF16), wqg_ref[...])
    for hd in range(N_HEADS):
        qh = qg[:, hd * LANES:(hd + 1) * LANES] * (LOG2E * dh ** -0.5) + qc_ref[hd:hd + 1, :]
        q_ref[0, hd] = qh.astype(BF16)
    nq = N_HEADS * LANES
    for grp in range(g):
        gate_ref[0, grp] = jax.nn.sigmoid(qg[:, nq + grp * LANES: nq + (grp + 1) * LANES])


def _proj(h, g_kv, g_q, w_kv, w_qg, qconst, tm=512):
    b, s, d = h.shape
    assert s % tm == 0 and s // L_SLC <= MAX_SLC_BLOCKS
    g, dh = N_KV_GROUPS, HEAD_DIM
    return pl.pallas_call(
        functools.partial(_proj_kernel, tm=tm),
        grid=(b, s // tm),
        in_specs=[
            pl.BlockSpec((1, tm, d), lambda i, j: (i, j, 0)),
            pl.BlockSpec((1, d), lambda i, j: (0, 0)),
            pl.BlockSpec((1, d), lambda i, j: (0, 0)),
            pl.BlockSpec(w_kv.shape, lambda i, j: (0, 0)),
            pl.BlockSpec(w_qg.shape, lambda i, j: (0, 0)),
            pl.BlockSpec(qconst.shape, lambda i, j: (0, 0)),
        ],
        out_specs=[
            pl.BlockSpec((2, 1, g, tm, dh), lambda i, j: (0, i, 0, j, 0)),
            pl.BlockSpec((1, g, tm, 2 * LANES), lambda i, j: (i, 0, j, 0)),
            pl.BlockSpec((1, g, dh, tm), lambda i, j: (i, 0, 0, j)),
            pl.BlockSpec((1, g, tm, LANES), lambda i, j: (i, 0, j, 0)),
            pl.BlockSpec((1, g, dh, tm), lambda i, j: (i, 0, 0, j)),
            pl.BlockSpec((1, N_HEADS, tm, LANES), lambda i, j: (i, 0, j, 0)),
            pl.BlockSpec((1, g, tm, LANES), lambda i, j: (i, 0, j, 0)),
        ],
        out_shape=[
            jax.ShapeDtypeStruct((2, b, g, s, dh), BF16),
            jax.ShapeDtypeStruct((b, g, s, 2 * LANES), BF16),
            jax.ShapeDtypeStruct((b, g, dh, s), BF16),
            jax.ShapeDtypeStruct((b, g, s, LANES), BF16),
            jax.ShapeDtypeStruct((b, g, dh, s), BF16),
            jax.ShapeDtypeStruct((b, N_HEADS, s, LANES), BF16),
            jax.ShapeDtypeStruct((b, g, s, LANES), F32),
        ],
        compiler_params=_params("arbitrary", "arbitrary"),
        name="proj",
    )(h, g_kv.reshape(1, d), g_q.reshape(1, d), w_kv, w_qg, qconst)


def _compress_kernel(r_ref, pe_ref, w1_ref, w2_ref, kca_ref, vct_ref, *, m):
    half = D_CMP * HEAD_DIM
    row = lax.broadcasted_iota(jnp.int32, (m, LANES), 0)
    lane = lax.broadcasted_iota(jnp.int32, (m, LANES), 1)

    def mlp(i):
        r = r_ref[i, 0]
        top = _dot(r, w1_ref[i, 0:half, :])
        bot = _dot(r, w1_ref[i, half:2 * half, :])
        pe_term = _dot(pe_ref[i], w1_ref[i])[0:1, :]
        hid = top + pltpu.roll(bot, m - 1, axis=0) + pe_term
        hid = hid * jax.nn.sigmoid(hid)
        out = _dot(hid.astype(BF16), w2_ref[i])
        return jnp.where(row < m - 1, out, 0.0)

    coarse = jnp.right_shift(row, 2).astype(F32)
    fine = jnp.bitwise_and(row, 3).astype(F32) * D_CMP + (L_CMP - 1) / 2.0
    kca_ref[0] = (mlp(0) + _pos_columns(coarse, fine, lane)).astype(BF16)
    vct_ref[0] = mlp(1).T[0:HEAD_DIM].astype(BF16)


def _compress(kvc, pe2, w1_2, w2_2):
    _, b, g, s, dh = kvc.shape
    m = s // D_CMP
    r = kvc.reshape(2, b * g, m, D_CMP * dh)
    hid = w1_2.shape[-1]
    return pl.pallas_call(
        functools.partial(_compress_kernel, m=m),
        grid=(b * g,),
        in_specs=[
            pl.BlockSpec((2, 1, m, D_CMP * dh), lambda j: (0, j, 0, 0)),
            pl.BlockSpec((2, 8, L_CMP * dh), lambda j: (0, 0, 0)),
            pl.BlockSpec((2, L_CMP * dh, hid), lambda j: (0, 0, 0)),
            pl.BlockSpec((2, hid, LANES), lambda j: (0, 0, 0)),
        ],
        out_specs=[
            pl.BlockSpec((1, m, LANES), lambda j: (j, 0, 0)),
            pl.BlockSpec((1, dh, m), lambda j: (j, 0, 0)),
        ],
        out_shape=[
            jax.ShapeDtypeStruct((b * g, m, LANES), BF16),
            jax.ShapeDtypeStruct((b * g, dh, m), BF16),
        ],
        compiler_params=_params("arbitrary"),
        name="compress",
    )(r, pe2, w1_2, w2_2)


def _head_stack(ot, gt, branch, tq):
    return jnp.concatenate(
        [ot[:, h * tq:(h + 1) * tq] * gt[3 * h + branch:3 * h + branch + 1, :] for h in range(HEADS_PER_GROUP)],
        axis=0)


def _nsa_cmp_kernel(q_ref, kca_ref, vct_ref, gate_ref, oc_ref, negm_ref, used_ref, *, tq, m):
    q0 = pl.program_id(2) * tq
    hpg = HEADS_PER_GROUP
    ncol = hpg * tq
    q = q_ref[0].reshape(ncol, LANES)
    tcol = q0 + jnp.bitwise_and(lax.broadcasted_iota(jnp.int32, (1, ncol), 1), tq - 1)

    def attend(rows):
        st = _dot_nt(kca_ref[0, 0:rows, :], q)
        nrow = lax.broadcasted_iota(jnp.int32, (rows, 1), 0)
        vis = (nrow * D_CMP + (L_CMP - 1)) <= tcol
        st = jnp.where(vis, st, NEG_INF)
        e = jnp.exp2(st - jnp.max(st, axis=0, keepdims=True))
        rcp = jnp.where(tcol >= L_CMP - 1, 1.0 / jnp.sum(e, axis=0, keepdims=True), 0.0)
        p = e * rcp
        ot = _dot(vct_ref[0, :, 0:rows], p.astype(BF16))
        pt = p[:, 0:tq]
        for h in range(1, hpg):
            pt = pt + p[:, h * tq:(h + 1) * tq]
        jj = lax.broadcasted_iota(jnp.int32, (MAX_SLC_BLOCKS, rows), 0) * L_SLC
        nn = lax.broadcasted_iota(jnp.int32, (MAX_SLC_BLOCKS, rows), 1) * D_CMP
        overlap_t = ((nn < jj + L_SLC) & (nn + L_CMP > jj)).astype(BF16)
        p_hi = pt.astype(BF16)
        p_lo = (pt - p_hi.astype(F32)).astype(BF16)
        return ot, _dot(overlap_t, p_hi) + _dot(overlap_t, p_lo)

    row_tile = min(m, CMP_ROW_TILE)
    n_variants = m // row_tile
    if n_variants == 1:
        ot, imp = attend(m)
    else:
        needed = pl.program_id(2) * (tq // D_CMP) + (tq // D_CMP - 1)
        variant = jnp.minimum(jnp.right_shift(needed - 1, row_tile.bit_length() - 1), n_variants - 1)
        ot, imp = lax.switch(variant, [functools.partial(attend, (v + 1) * row_tile) for v in range(n_variants)])
    gt = gate_ref[0, 0].T
    oc_ref[0] = _head_stack(ot, gt, 0, tq).T

    jrow = lax.broadcasted_iota(jnp.int32, (MAX_SLC_BLOCKS, tq), 0)
    cur = jnp.right_shift(q0 + lax.broadcasted_iota(jnp.int32, (1, tq), 1), L_SLC.bit_length() - 1)
    valid = jrow <= cur
    forced = (jrow == 0) | (jrow == cur) | (jrow == cur - 1)
    val = jnp.where(valid & jnp.logical_not(forced), imp, NEG_INF)
    jrow_f = jrow.astype(F32)
    sel = jnp.where(forced, 1.0, 0.0)
    for _ in range(N_SEL - N_FORCED):
        mx = jnp.max(val, axis=0, keepdims=True)
        first = jnp.min(jnp.where(val == mx, jrow_f, float(MAX_SLC_BLOCKS)), axis=0, keepdims=True)
        hit = jrow_f == first
        sel = jnp.where(hit, 1.0, sel)
        val = jnp.where(hit, -3.0e38, val)
    sel_t = jnp.where(valid, sel, 0.0).T
    negm_ref[0, 0] = jnp.where(sel_t > 0.5, 0.0, NEG_INF).astype(BF16)
    for r in range(tq // SUPER_KEYS):
        used_ref[r] = jnp.max(sel_t[r * SUPER_KEYS:(r + 1) * SUPER_KEYS], axis=0, keepdims=True)


def _nsa_cmp(q, kca, vct, gates, tq=256):
    b, nh, s, _ = q.shape
    g, hpg, dh = N_KV_GROUPS, HEADS_PER_GROUP, HEAD_DIM
    m = kca.shape[1]
    assert s % tq == 0 and tq % SUPER_KEYS == 0 and tq & (tq - 1) == 0
    nqt = s // tq
    sub = tq // SUPER_KEYS
    return pl.pallas_call(
        functools.partial(_nsa_cmp_kernel, tq=tq, m=m),
        grid=(b, g, s // tq),
        in_specs=[
            pl.BlockSpec((1, hpg, tq, LANES), lambda i, j, k: (i, j, k, 0)),
            pl.BlockSpec((1, m, LANES), lambda i, j, k: (i * N_KV_GROUPS + j, 0, 0)),
            pl.BlockSpec((1, dh, m), lambda i, j, k: (i * N_KV_GROUPS + j, 0, 0)),
            pl.BlockSpec((1, 1, tq, LANES), lambda i, j, k: (i, j, k, 0)),
        ],
        out_specs=[
            pl.BlockSpec((1, tq, hpg * dh), lambda i, j, k: (i, k, j)),
            pl.BlockSpec((1, 1, tq, MAX_SLC_BLOCKS), lambda i, j, k: (i, j, k, 0)),
            pl.BlockSpec((sub, 1, MAX_SLC_BLOCKS), lambda i, j, k: ((i * N_KV_GROUPS + j) * nqt + k, 0, 0)),
        ],
        out_shape=[
            jax.ShapeDtypeStruct((b, s, nh * dh), F32),
            jax.ShapeDtypeStruct((b, g, s, MAX_SLC_BLOCKS), BF16),
            jax.ShapeDtypeStruct((b * g * nqt * sub, 1, MAX_SLC_BLOCKS), F32),
        ],
        compiler_params=_params("arbitrary", "arbitrary", "arbitrary"),
        name="nsa_cmp",
    )(q, kca, vct, gates)


def _super_lists_kernel(used_ref, lst_ref, cnt_ref, *, tb):
    ut = used_ref[...].T.astype(BF16)
    a_i = lax.broadcasted_iota(jnp.int32, (MAX_SLC_BLOCKS, MAX_SLC_BLOCKS), 0)
    b_i = lax.broadcasted_iota(jnp.int32, (MAX_SLC_BLOCKS, MAX_SLC_BLOCKS), 1)
    pair_t = (a_i == jnp.right_shift(b_i, SUPER_SHIFT)).astype(BF16)
    sbf = jnp.where(_dot(pair_t, ut) > 0.5, 1.0, 0.0).astype(BF16)
    incl = _dot((b_i <= a_i).astype(BF16), sbf)[0:MAX_SUPER]
    cnt_ref[...] = jnp.broadcast_to(incl[MAX_SUPER - 1:MAX_SUPER, :], (8, tb)).astype(jnp.int32)
    for i in range(MAX_SUPER):
        lst_ref[i:i + 1, :] = jnp.sum(jnp.where(incl <= float(i), 1.0, 0.0), axis=0, keepdims=True).astype(jnp.int32)


def _super_lists(used):
    t = used.shape[0]
    tb = min(t, LANES)
    assert t % tb == 0
    return pl.pallas_call(
        functools.partial(_super_lists_kernel, tb=tb),
        grid=(t // tb,),
        in_specs=[pl.BlockSpec((tb, MAX_SLC_BLOCKS), lambda i: (i, 0))],
        out_specs=[
            pl.BlockSpec((MAX_SUPER, tb), lambda i: (0, i)),
            pl.BlockSpec((8, tb), lambda i: (0, i)),
        ],
        out_shape=[
            jax.ShapeDtypeStruct((MAX_SUPER, t), jnp.int32),
            jax.ShapeDtypeStruct((8, t), jnp.int32),
        ],
        compiler_params=_params("arbitrary"),
        name="super_lists",
    )(used)


def _nsa_slc_kernel(lst_ref, cnt_ref, q_ref, kas_ref, vts_ref, kaw_ref, vtw_ref, negm_ref, gate_ref, oc_ref, o_ref,
                    m_ref, l_ref, acc_ref, *, tq):
    q0 = pl.program_id(2) * tq
    hpg = HEADS_PER_GROUP
    ncol = hpg * tq
    tk = CHUNK_SUPERS * SUPER_KEYS
    q = q_ref[0].reshape(ncol, LANES)
    negm = negm_ref[0, 0]
    qa = jnp.concatenate([q, jnp.concatenate([negm] * hpg, axis=0)], axis=1)
    tcol = q0 + jnp.bitwise_and(lax.broadcasted_iota(jnp.int32, (1, ncol), 1), tq - 1)

    m_ref[...] = jnp.full_like(m_ref, NEG_INF)
    l_ref[...] = jnp.zeros_like(l_ref)
    acc_ref[...] = jnp.zeros_like(acc_ref)

    n_steps = pl.num_programs(0) * pl.num_programs(1) * pl.num_programs(2)
    step = (pl.program_id(0) * pl.num_programs(1) + pl.program_id(1)) * pl.num_programs(2) + pl.program_id(2)
    n_used = cnt_ref[step]
    n_chunks = jnp.right_shift(n_used + CHUNK_SUPERS - 1, CHUNK_SUPERS.bit_length() - 1)

    def gather(chunk, partial):
        ks, vs, kbase = [], [], []
        for j in range(CHUNK_SUPERS):
            slot = chunk * CHUNK_SUPERS + j
            if partial:
                ok = (slot >= 0) & (slot < n_used)
                sb = jnp.where(ok, lst_ref[jnp.clip(slot, 0, MAX_SUPER - 1) * n_steps + step], 0)
                kbase.append(jnp.where(ok, sb * SUPER_KEYS, FAR_POS))
            else:
                sb = lst_ref[slot * n_steps + step]
            start = pl.multiple_of(sb * SUPER_KEYS, SUPER_KEYS)
            ks.append(kas_ref[0, 0, pl.ds(start, SUPER_KEYS), :])
            vs.append(vts_ref[0, 0, :, pl.ds(start, SUPER_KEYS)])
        return jnp.concatenate(ks, axis=0), jnp.concatenate(vs, axis=1), kbase

    def scores(kt, kbase):
        st = _dot_nt(kt, qa)
        if kbase:
            r = lax.broadcasted_iota(jnp.int32, (tk, 1), 0)
            rs = jnp.right_shift(r, SUPER_KEYS.bit_length() - 1)
            kpos = jnp.full((tk, 1), kbase[0], jnp.int32)
            for j in range(1, CHUNK_SUPERS):
                kpos = jnp.where(rs == j, kbase[j], kpos)
            kpos = kpos + jnp.bitwise_and(r, SUPER_KEYS - 1)
            st = jnp.where(kpos <= tcol, st, NEG_INF)
        return st

    def update(st, vt):
        m_old = m_ref[...]
        m_new = jnp.maximum(m_old, jnp.max(st, axis=0, keepdims=True))
        alpha = jnp.exp2(m_old - m_new)
        p = jnp.exp2(st - m_new)
        l_ref[...] = alpha * l_ref[...] + jnp.sum(p, axis=0, keepdims=True)
        acc_ref[...] = alpha * acc_ref[...] + _dot(vt, p.astype(BF16))
        m_ref[...] = m_new

    def body(c, carry):
        kt, vt, _ = gather(c, False)
        update(scores(kt, None), vt)
        return carry

    lax.fori_loop(0, jnp.maximum(n_chunks - FINAL_CHUNKS, 0), body, 0)

    finals = []
    for back in range(1, FINAL_CHUNKS + 1):
        kt, vt, kbase = gather(n_chunks - back, True)
        if back == 1:
            st = scores(kt, kbase)
        else:
            st = scores(kt, None) + jnp.where(n_chunks >= back, 0.0, NEG_INF)
        finals.append((st, vt))
    wk = WINDOW + tq
    wstart = pl.multiple_of(jnp.maximum(q0 - WINDOW, 0), tq)
    stw = _dot_nt(kaw_ref[0, 0, pl.ds(wstart, wk), :], q)
    for st, vt in finals:
        update(st, vt)
    o_slc_t = acc_ref[...] * (1.0 / l_ref[...])

    dist = tcol - (wstart + lax.broadcasted_iota(jnp.int32, (wk, 1), 0))
    in_window = lax.bitcast_convert_type(dist, jnp.uint32) < jnp.uint32(WINDOW)
    stw = jnp.where(in_window, stw, NEG_INF)
    e = jnp.exp2(stw - jnp.max(stw, axis=0, keepdims=True))
    o_win_t = _dot(vtw_ref[0, 0, :, pl.ds(wstart, wk)], e.astype(BF16))
    o_win_t = o_win_t * (1.0 / jnp.sum(e, axis=0, keepdims=True))

    gt = gate_ref[0, 0].T
    comb = _head_stack(o_slc_t, gt, 1, tq) + _head_stack(o_win_t, gt, 2, tq)
    o_ref[0] = (oc_ref[0] + comb.T).astype(BF16)


def _nsa_slc(lists, counts, q, kas, vts, kaw, vtw, negm, gates, oc, tq=128):
    b, nh, s, _ = q.shape
    g, hpg, dh = N_KV_GROUPS, HEADS_PER_GROUP, HEAD_DIM
    assert tq == SUPER_KEYS and WINDOW % tq == 0 and s >= WINDOW + tq and s % (CHUNK_SUPERS * SUPER_KEYS) == 0
    grid_spec = pltpu.PrefetchScalarGridSpec(
        num_scalar_prefetch=2,
        grid=(b, g, s // tq),
        in_specs=[
            pl.BlockSpec((1, hpg, tq, LANES), lambda i, j, k, *_: (i, j, k, 0)),
            pl.BlockSpec((1, 1, s, 2 * LANES), lambda i, j, k, *_: (i, j, 0, 0)),
            pl.BlockSpec((1, 1, dh, s), lambda i, j, k, *_: (i, j, 0, 0)),
            pl.BlockSpec((1, 1, s, LANES), lambda i, j, k, *_: (i, j, 0, 0)),
            pl.BlockSpec((1, 1, dh, s), lambda i, j, k, *_: (i, j, 0, 0)),
            pl.BlockSpec((1, 1, tq, MAX_SLC_BLOCKS), lambda i, j, k, *_: (i, j, k, 0)),
            pl.BlockSpec((1, 1, tq, LANES), lambda i, j, k, *_: (i, j, k, 0)),
            pl.BlockSpec((1, tq, hpg * dh), lambda i, j, k, *_: (i, k, j)),
        ],
        out_specs=pl.BlockSpec((1, tq, hpg * dh), lambda i, j, k, *_: (i, k, j)),
        scratch_shapes=[
            pltpu.VMEM((1, hpg * tq), F32),
            pltpu.VMEM((1, hpg * tq), F32),
            pltpu.VMEM((dh, hpg * tq), F32),
        ],
    )
    return pl.pallas_call(
        functools.partial(_nsa_slc_kernel, tq=tq),
        grid_spec=grid_spec,
        out_shape=jax.ShapeDtypeStruct((b, s, nh * dh), BF16),
        compiler_params=_params("arbitrary", "arbitrary", "arbitrary"),
        name="nsa_slc",
    )(lists, counts, q, kas, vts, kaw, vtw, negm, gates, oc)


def _split_bf16(x, n):
    pieces = []
    for _ in range(n):
        p = x.astype(BF16).astype(F32)
        pieces.append(p)
        x = x - p
    return pieces


def _q_constants():
    slopes = jnp.exp2(-8.0 * jnp.arange(1, N_HEADS + 1, dtype=F32) / N_HEADS) * LOG2E
    pieces = jnp.stack(_split_bf16(slopes, N_SPLIT), axis=1)
    qc = jnp.zeros((N_HEADS, LANES), F32)
    qc = qc.at[:, POS_LANE:POS_LANE + N_SPLIT].set(pieces * L_SLC)
    qc = qc.at[:, POS_LANE + N_SPLIT:POS_LANE + 2 * N_SPLIT].set(pieces)
    return qc


def _kv_weight(w_kv):
    d = w_kv.shape[0]
    w = w_kv.reshape(d, N_KV_SETS, N_KV_GROUPS, HEAD_DIM)

    def flat(x):
        return x.reshape(d, N_KV_GROUPS * HEAD_DIM)

    def padded(x):
        return jnp.pad(x, ((0, 0), (0, 0), (0, LANES - HEAD_DIM))).reshape(d, N_KV_GROUPS * LANES)

    out = jnp.concatenate([flat(w[:, 0]), flat(w[:, 1]), padded(w[:, 2]), flat(w[:, 3]), padded(w[:, 4]),
                           flat(w[:, 5])], axis=1)
    assert out.shape[1] == _KV_COLS
    return out.astype(BF16)


def _qg_weight(w_qg):
    d = w_qg.shape[0]
    nq = N_HEADS * HEAD_DIM
    wq = jnp.pad(w_qg[:, :nq].reshape(d, N_HEADS, HEAD_DIM), ((0, 0), (0, 0), (0, LANES - HEAD_DIM)))
    wg = jnp.pad(w_qg[:, nq:].reshape(d, N_KV_GROUPS, 3 * HEADS_PER_GROUP),
                 ((0, 0), (0, 0), (0, LANES - 3 * HEADS_PER_GROUP)))
    return jnp.concatenate([wq.reshape(d, N_HEADS * LANES), wg.reshape(d, N_KV_GROUPS * LANES)], axis=1).astype(BF16)


def kernel(x, a_norm, a_w_in, a_conv, a_w_out, kv_norm, w_kv, cmp_pe_k, cmp_w1_k, cmp_w2_k, cmp_pe_v, cmp_w1_v,
           cmp_w2_v, b_norm, b_w_qg, b_w_o, f_norm, f_w_gu, f_w_down, final_norm):
    b, s, d = x.shape
    nq = N_HEADS * HEAD_DIM
    h = _mixer_a(x, a_norm[0], a_w_in[0].astype(BF16), a_conv[0], a_w_out[0].astype(BF16))
    h = _ffn(h.reshape(b * s, d), f_norm[0], f_w_gu[0].astype(BF16), f_w_down[0].astype(BF16), final_norm)
    h = h.reshape(b, s, d)

    kvc, kas, vts, kaw, vtw, q, gates = _proj(h, kv_norm, b_norm[0], _kv_weight(w_kv), _qg_weight(b_w_qg[0]),
                                              _q_constants())
    pe2 = jnp.stack([cmp_pe_k, cmp_pe_v]).reshape(2, 1, L_CMP * HEAD_DIM)
    pe2 = jnp.broadcast_to(pe2, (2, 8, L_CMP * HEAD_DIM)).astype(BF16)
    w2 = jnp.pad(jnp.stack([cmp_w2_k, cmp_w2_v]), ((0, 0), (0, 0), (0, LANES - HEAD_DIM))).astype(BF16)
    kca, vct = _compress(kvc, pe2, jnp.stack([cmp_w1_k, cmp_w1_v]).astype(BF16), w2)

    oc, negm, used = _nsa_cmp(q, kca, vct, gates)
    lists, counts = _super_lists(used.reshape(used.shape[0], MAX_SLC_BLOCKS))
    o = _nsa_slc(lists.reshape(-1), counts[0], q, kas, vts, kaw, vtw, negm, gates, oc)

    out = _ffn(h.reshape(b * s, d), f_norm[1], f_w_gu[1].astype(BF16), f_w_down[1].astype(BF16), final_norm,
               o=o.reshape(b * s, nq), w_o=b_w_o[0].astype(BF16), with_final=True)
    return out.reshape(b, s, d)
```

```python
import functools
import math

import jax
import jax.numpy as jnp
from jax import lax
from jax.experimental import pallas as pl
from jax.experimental.pallas import tpu as pltpu

F32 = jnp.float32
BF16 = jnp.bfloat16

N_HEADS = 16
N_KV_GROUPS = 4
HEADS_PER_GROUP = N_HEADS // N_KV_GROUPS
HEAD_DIM = 64
L_CMP = 32
D_CMP = 16
L_SLC = 64
N_SEL = 16
N_FORCED = 3
WINDOW = 512
N_KV_SETS = 6
CONV_WIDTH = 3
RMS_EPS = 1e-5
NEG_INF = -1e30
LOG2E = math.log2(math.e)
LANES = 128
MAX_SLC_BLOCKS = LANES
SUPER_KEYS = LANES
SUPER_SHIFT = (SUPER_KEYS // L_SLC).bit_length() - 1
MAX_SUPER = MAX_SLC_BLOCKS * L_SLC // SUPER_KEYS
CHUNK_SUPERS = 4
FINAL_CHUNKS = 3
CMP_ROW_TILE = 128
VT_ROWS = HEAD_DIM + 16
FAR_POS = 1 << 30
POS_LANE = HEAD_DIM
N_SPLIT = 3
VMEM_LIMIT_BYTES = 56 * 1024 * 1024


def _rms(x, g):
    return x * lax.rsqrt(jnp.mean(x * x, axis=-1, keepdims=True) + RMS_EPS) * g


def _dot(a, b):
    return jnp.dot(a, b, preferred_element_type=F32)


def _dot_nt(a, b):
    return lax.dot_general(a, b, (((1,), (1,)), ((), ())), preferred_element_type=F32)


def _params(*sem):
    return pltpu.CompilerParams(dimension_semantics=sem, vmem_limit_bytes=VMEM_LIMIT_BYTES)


def _pos_columns(coarse, fine, lane):
    is_c = (lane >= POS_LANE) & (lane < POS_LANE + N_SPLIT)
    is_f = (lane >= POS_LANE + N_SPLIT) & (lane < POS_LANE + 2 * N_SPLIT)
    return jnp.where(is_c, coarse, jnp.where(is_f, fine, 0.0))


def _mixer_a_kernel(x_ref, g_ref, win_ref, conv_ref, wout_ref, o_ref, ext_ref, *, tm, d):
    @pl.when(pl.program_id(1) == 0)
    def _():
        ext_ref[0:8, :] = jnp.zeros((8, d), F32)

    x = x_ref[0]
    xb = _rms(x, g_ref[...]).astype(BF16)
    b_gate = _dot(xb, win_ref[:, 0:d])
    cv = _dot(xb, win_ref[:, d:2 * d]) * _dot(xb, win_ref[:, 2 * d:3 * d])
    ext_ref[8:8 + tm, :] = cv
    cv1 = ext_ref[7:7 + tm, :]
    cv2 = ext_ref[6:6 + tm, :]
    u = conv_ref[0:1, :] * cv2 + conv_ref[1:2, :] * cv1 + conv_ref[2:3, :] * cv
    ext_ref[0:8, :] = cv[tm - 8:tm, :]
    o_ref[0] = x + _dot((b_gate * u).astype(BF16), wout_ref[...])


def _mixer_a(x, g, w_in, conv_w, w_out, tm=512):
    b, s, d = x.shape
    assert s % tm == 0
    return pl.pallas_call(
        functools.partial(_mixer_a_kernel, tm=tm, d=d),
        grid=(b, s // tm),
        in_specs=[
            pl.BlockSpec((1, tm, d), lambda i, j: (i, j, 0)),
            pl.BlockSpec((1, d), lambda i, j: (0, 0)),
            pl.BlockSpec((d, 3 * d), lambda i, j: (0, 0)),
            pl.BlockSpec((CONV_WIDTH, d), lambda i, j: (0, 0)),
            pl.BlockSpec((d, d), lambda i, j: (0, 0)),
        ],
        out_specs=pl.BlockSpec((1, tm, d), lambda i, j: (i, j, 0)),
        out_shape=jax.ShapeDtypeStruct((b, s, d), F32),
        scratch_shapes=[pltpu.VMEM((tm + 8, d), F32)],
        compiler_params=_params("arbitrary", "arbitrary"),
        name="mixer_a",
    )(x, g.reshape(1, d), w_in, conv_w, w_out)


def _ffn_kernel(*refs, with_proj, with_final):
    if with_proj:
        h_ref, o_ref_in, wo_ref, g_ref, wg_ref, wu_ref, wd_ref, fg_ref, out_ref, xn_ref, acc_ref, res_ref = refs
    else:
        h_ref, g_ref, wg_ref, wu_ref, wd_ref, fg_ref, out_ref, xn_ref, acc_ref, res_ref = refs
    j = pl.program_id(1)

    @pl.when(j == 0)
    def _():
        h = h_ref[...]
        if with_proj:
            h = h + _dot(o_ref_in[...], wo_ref[...])
        res_ref[...] = h
        xn_ref[...] = _rms(h, g_ref[...]).astype(BF16)
        acc_ref[...] = jnp.zeros_like(acc_ref)

    xb = xn_ref[...]
    gate = _dot(xb, wg_ref[...])
    up = _dot(xb, wu_ref[...])
    act = (gate * jax.nn.sigmoid(gate)) * up
    acc_ref[...] += _dot(act.astype(BF16), wd_ref[...])

    @pl.when(j == pl.num_programs(1) - 1)
    def _():
        y = res_ref[...] + acc_ref[...]
        if with_final:
            y = _rms(y, fg_ref[...])
        out_ref[...] = y


def _ffn(h, g, w_gu, w_down, final_g, o=None, w_o=None, with_final=False, tm=512, tf=1408):
    t, d = h.shape
    f = w_down.shape[0]
    assert t % tm == 0 and f % tf == 0
    nf = f // tf
    with_proj = o is not None
    in_specs = [pl.BlockSpec((tm, d), lambda i, j: (i, 0))]
    args = [h]
    if with_proj:
        in_specs += [pl.BlockSpec((tm, o.shape[1]), lambda i, j: (i, 0)),
                     pl.BlockSpec(w_o.shape, lambda i, j: (0, 0))]
        args += [o, w_o]
    in_specs += [
        pl.BlockSpec((1, d), lambda i, j: (0, 0)),
        pl.BlockSpec((d, tf), lambda i, j: (0, j)),
        pl.BlockSpec((d, tf), lambda i, j: (0, nf + j)),
        pl.BlockSpec((tf, d), lambda i, j: (j, 0)),
        pl.BlockSpec((1, d), lambda i, j: (0, 0)),
    ]
    args += [g.reshape(1, d), w_gu, w_gu, w_down, final_g.reshape(1, d)]
    return pl.pallas_call(
        functools.partial(_ffn_kernel, with_proj=with_proj, with_final=with_final),
        grid=(t // tm, nf),
        in_specs=in_specs,
        out_specs=pl.BlockSpec((tm, d), lambda i, j: (i, 0)),
        out_shape=jax.ShapeDtypeStruct((t, d), F32),
        scratch_shapes=[pltpu.VMEM((tm, d), BF16), pltpu.VMEM((tm, d), F32), pltpu.VMEM((tm, d), F32)],
        compiler_params=_params("arbitrary", "arbitrary"),
        name="ffn_proj" if with_proj else "ffn",
    )(*args)


_KC0, _VC0, _KS0, _VS0, _KW0, _VW0 = 0, 256, 512, 1024, 1280, 1792
_KV_COLS = 2048


def _proj_kernel(h_ref, gkv_ref, gq_ref, wkv_ref, wqg_ref, qc_ref,
                 kvc_ref, kas_ref, vts_ref, kaw_ref, vtw_ref, q_ref, gate_ref, *, tm):
    g, dh = N_KV_GROUPS, HEAD_DIM
    s0 = pl.program_id(1) * tm
    x = h_ref[0]
    y = x * lax.rsqrt(jnp.mean(x * x, axis=-1, keepdims=True) + RMS_EPS)
    kv = _dot((y * gkv_ref[...]).astype(BF16), wkv_ref[...])
    for grp in range(g):
        kvc_ref[0, 0, grp] = kv[:, _KC0 + grp * dh:_KC0 + (grp + 1) * dh].astype(BF16)
        kvc_ref[1, 0, grp] = kv[:, _VC0 + grp * dh:_VC0 + (grp + 1) * dh].astype(BF16)

    kpos = s0 + lax.broadcasted_iota(jnp.int32, (tm, LANES), 0)
    lane = lax.broadcasted_iota(jnp.int32, (tm, LANES), 1)
    kblk = jnp.right_shift(kpos, L_SLC.bit_length() - 1)
    posc = _pos_columns(kblk.astype(F32), jnp.bitwise_and(kpos, L_SLC - 1).astype(F32), lane)
    onehot = (lane == kblk).astype(BF16)
    for grp in range(g):
        kas_ref[0, grp, :, 0:LANES] = (kv[:, _KS0 + grp * LANES:_KS0 + (grp + 1) * LANES] + posc).astype(BF16)
        kas_ref[0, grp, :, LANES:2 * LANES] = onehot
        kaw_ref[0, grp] = (kv[:, _KW0 + grp * LANES:_KW0 + (grp + 1) * LANES] + posc).astype(BF16)
    ones_rows = (lax.broadcasted_iota(jnp.int32, (VT_ROWS - dh, tm), 0) == 0).astype(BF16)
    for pair in range(g // 2):
        for v0, vt_ref in ((_VS0, vts_ref), (_VW0, vtw_ref)):
            vt = kv[:, v0 + pair * LANES:v0 + (pair + 1) * LANES].T
            for half in range(2):
                vt_ref[0, 2 * pair + half, 0:dh] = vt[half * dh:(half + 1) * dh].astype(BF16)
                vt_ref[0, 2 * pair + half, dh:VT_ROWS] = ones_rows

    qg = _dot((y * gq_ref[...]).astype(BF16), wqg_ref[...])
    for hd in range(N_HEADS):
        qh = qg[:, hd * LANES:(hd + 1) * LANES] * (LOG2E * dh ** -0.5) + qc_ref[hd:hd + 1, :]
        q_ref[0, hd] = qh.astype(BF16)
    nq = N_HEADS * LANES
    for grp in range(g):
        gate_ref[0, grp] = jax.nn.sigmoid(qg[:, nq + grp * LANES: nq + (grp + 1) * LANES])


def _proj(h, g_kv, g_q, w_kv, w_qg, qconst, tm=512):
    b, s, d = h.shape
    assert s % tm == 0 and s // L_SLC <= MAX_SLC_BLOCKS
    g, dh = N_KV_GROUPS, HEAD_DIM
    return pl.pallas_call(
        functools.partial(_proj_kernel, tm=tm),
        grid=(b, s // tm),
        in_specs=[
            pl.BlockSpec((1, tm, d), lambda i, j: (i, j, 0)),
            pl.BlockSpec((1, d), lambda i, j: (0, 0)),
            pl.BlockSpec((1, d), lambda i, j: (0, 0)),
            pl.BlockSpec(w_kv.shape, lambda i, j: (0, 0)),
            pl.BlockSpec(w_qg.shape, lambda i, j: (0, 0)),
            pl.BlockSpec(qconst.shape, lambda i, j: (0, 0)),
        ],
        out_specs=[
            pl.BlockSpec((2, 1, g, tm, dh), lambda i, j: (0, i, 0, j, 0)),
            pl.BlockSpec((1, g, tm, 2 * LANES), lambda i, j: (i, 0, j, 0)),
            pl.BlockSpec((1, g, VT_ROWS, tm), lambda i, j: (i, 0, 0, j)),
            pl.BlockSpec((1, g, tm, LANES), lambda i, j: (i, 0, j, 0)),
            pl.BlockSpec((1, g, VT_ROWS, tm), lambda i, j: (i, 0, 0, j)),
            pl.BlockSpec((1, N_HEADS, tm, LANES), lambda i, j: (i, 0, j, 0)),
            pl.BlockSpec((1, g, tm, LANES), lambda i, j: (i, 0, j, 0)),
        ],
        out_shape=[
            jax.ShapeDtypeStruct((2, b, g, s, dh), BF16),
            jax.ShapeDtypeStruct((b, g, s, 2 * LANES), BF16),
            jax.ShapeDtypeStruct((b, g, VT_ROWS, s), BF16),
            jax.ShapeDtypeStruct((b, g, s, LANES), BF16),
            jax.ShapeDtypeStruct((b, g, VT_ROWS, s), BF16),
            jax.ShapeDtypeStruct((b, N_HEADS, s, LANES), BF16),
            jax.ShapeDtypeStruct((b, g, s, LANES), F32),
        ],
        compiler_params=_params("arbitrary", "arbitrary"),
        name="proj",
    )(h, g_kv.reshape(1, d), g_q.reshape(1, d), w_kv, w_qg, qconst)


def _compress_kernel(r_ref, pe_ref, w1_ref, w2_ref, kca_ref, vct_ref, *, m):
    half = D_CMP * HEAD_DIM
    row = lax.broadcasted_iota(jnp.int32, (m, LANES), 0)
    lane = lax.broadcasted_iota(jnp.int32, (m, LANES), 1)

    def mlp(i):
        r = r_ref[i, 0]
        top = _dot(r, w1_ref[i, 0:half, :])
        bot = _dot(r, w1_ref[i, half:2 * half, :])
        pe_term = _dot(pe_ref[i], w1_ref[i])[0:1, :]
        hid = top + pltpu.roll(bot, m - 1, axis=0) + pe_term
        hid = hid * jax.nn.sigmoid(hid)
        out = _dot(hid.astype(BF16), w2_ref[i])
        return jnp.where(row < m - 1, out, 0.0)

    coarse = jnp.right_shift(row, 2).astype(F32)
    fine = jnp.bitwise_and(row, 3).astype(F32) * D_CMP + (L_CMP - 1) / 2.0
    kca_ref[0] = (mlp(0) + _pos_columns(coarse, fine, lane)).astype(BF16)
    vct_ref[0] = mlp(1).T[0:HEAD_DIM].astype(BF16)


def _compress(kvc, pe2, w1_2, w2_2):
    _, b, g, s, dh = kvc.shape
    m = s // D_CMP
    r = kvc.reshape(2, b * g, m, D_CMP * dh)
    hid = w1_2.shape[-1]
    return pl.pallas_call(
        functools.partial(_compress_kernel, m=m),
        grid=(b * g,),
        in_specs=[
            pl.BlockSpec((2, 1, m, D_CMP * dh), lambda j: (0, j, 0, 0)),
            pl.BlockSpec((2, 8, L_CMP * dh), lambda j: (0, 0, 0)),
            pl.BlockSpec((2, L_CMP * dh, hid), lambda j: (0, 0, 0)),
            pl.BlockSpec((2, hid, LANES), lambda j: (0, 0, 0)),
        ],
        out_specs=[
            pl.BlockSpec((1, m, LANES), lambda j: (j, 0, 0)),
            pl.BlockSpec((1, dh, m), lambda j: (j, 0, 0)),
        ],
        out_shape=[
            jax.ShapeDtypeStruct((b * g, m, LANES), BF16),
            jax.ShapeDtypeStruct((b * g, dh, m), BF16),
        ],
        compiler_params=_params("arbitrary"),
        name="compress",
    )(r, pe2, w1_2, w2_2)


def _head_stack(ot, gt, branch, tq):
    return jnp.concatenate(
        [ot[:, h * tq:(h + 1) * tq] * gt[3 * h + branch:3 * h + branch + 1, :] for h in range(HEADS_PER_GROUP)],
        axis=0)


def _nsa_cmp_kernel(q_ref, kca_ref, vct_ref, gate_ref, oc_ref, negm_ref, used_ref, *, tq, m):
    q0 = pl.program_id(2) * tq
    hpg = HEADS_PER_GROUP
    ncol = hpg * tq
    q = q_ref[0].reshape(ncol, LANES)
    tcol = q0 + jnp.bitwise_and(lax.broadcasted_iota(jnp.int32, (1, ncol), 1), tq - 1)

    def attend(rows):
        st = _dot_nt(kca_ref[0, 0:rows, :], q)
        nrow = lax.broadcasted_iota(jnp.int32, (rows, 1), 0)
        vis = (nrow * D_CMP + (L_CMP - 1)) <= tcol
        st = jnp.where(vis, st, NEG_INF)
        e = jnp.exp2(st - jnp.max(st, axis=0, keepdims=True))
        rcp = jnp.where(tcol >= L_CMP - 1, 1.0 / jnp.sum(e, axis=0, keepdims=True), 0.0)
        p = e * rcp
        ot = _dot(vct_ref[0, :, 0:rows], p.astype(BF16))
        pt = p[:, 0:tq]
        for h in range(1, hpg):
            pt = pt + p[:, h * tq:(h + 1) * tq]
        jj = lax.broadcasted_iota(jnp.int32, (MAX_SLC_BLOCKS, rows), 0) * L_SLC
        nn = lax.broadcasted_iota(jnp.int32, (MAX_SLC_BLOCKS, rows), 1) * D_CMP
        overlap_t = ((nn < jj + L_SLC) & (nn + L_CMP > jj)).astype(BF16)
        p_hi = pt.astype(BF16)
        p_lo = (pt - p_hi.astype(F32)).astype(BF16)
        return ot, _dot(overlap_t, p_hi) + _dot(overlap_t, p_lo)

    row_tile = min(m, CMP_ROW_TILE)
    n_variants = m // row_tile
    if n_variants == 1:
        ot, imp = attend(m)
    else:
        needed = pl.program_id(2) * (tq // D_CMP) + (tq // D_CMP - 1)
        variant = jnp.minimum(jnp.right_shift(needed - 1, row_tile.bit_length() - 1), n_variants - 1)
        ot, imp = lax.switch(variant, [functools.partial(attend, (v + 1) * row_tile) for v in range(n_variants)])
    gt = gate_ref[0, 0].T
    oc_ref[0] = _head_stack(ot, gt, 0, tq).T

    jrow = lax.broadcasted_iota(jnp.int32, (MAX_SLC_BLOCKS, tq), 0)
    cur = jnp.right_shift(q0 + lax.broadcasted_iota(jnp.int32, (1, tq), 1), L_SLC.bit_length() - 1)
    valid = jrow <= cur
    forced = (jrow == 0) | (jrow == cur) | (jrow == cur - 1)
    val = jnp.where(valid & jnp.logical_not(forced), imp, NEG_INF)
    jrow_f = jrow.astype(F32)
    sel = jnp.where(forced, 1.0, 0.0)
    for _ in range(N_SEL - N_FORCED):
        mx = jnp.max(val, axis=0, keepdims=True)
        first = jnp.min(jnp.where(val == mx, jrow_f, float(MAX_SLC_BLOCKS)), axis=0, keepdims=True)
        hit = jrow_f == first
        sel = jnp.where(hit, 1.0, sel)
        val = jnp.where(hit, -3.0e38, val)
    sel_t = jnp.where(valid, sel, 0.0).T
    negm_ref[0, 0] = jnp.where(sel_t > 0.5, 0.0, NEG_INF).astype(BF16)
    for r in range(tq // SUPER_KEYS):
        used_ref[r] = jnp.max(sel_t[r * SUPER_KEYS:(r + 1) * SUPER_KEYS], axis=0, keepdims=True)


def _nsa_cmp(q, kca, vct, gates, tq=256):
    b, nh, s, _ = q.shape
    g, hpg, dh = N_KV_GROUPS, HEADS_PER_GROUP, HEAD_DIM
    m = kca.shape[1]
    assert s % tq == 0 and tq % SUPER_KEYS == 0 and tq & (tq - 1) == 0
    nqt = s // tq
    sub = tq // SUPER_KEYS
    return pl.pallas_call(
        functools.partial(_nsa_cmp_kernel, tq=tq, m=m),
        grid=(b, g, s // tq),
        in_specs=[
            pl.BlockSpec((1, hpg, tq, LANES), lambda i, j, k: (i, j, k, 0)),
            pl.BlockSpec((1, m, LANES), lambda i, j, k: (i * N_KV_GROUPS + j, 0, 0)),
            pl.BlockSpec((1, dh, m), lambda i, j, k: (i * N_KV_GROUPS + j, 0, 0)),
            pl.BlockSpec((1, 1, tq, LANES), lambda i, j, k: (i, j, k, 0)),
        ],
        out_specs=[
            pl.BlockSpec((1, tq, hpg * dh), lambda i, j, k: (i, k, j)),
            pl.BlockSpec((1, 1, tq, MAX_SLC_BLOCKS), lambda i, j, k: (i, j, k, 0)),
            pl.BlockSpec((sub, 1, MAX_SLC_BLOCKS), lambda i, j, k: ((i * N_KV_GROUPS + j) * nqt + k, 0, 0)),
        ],
        out_shape=[
            jax.ShapeDtypeStruct((b, s, nh * dh), F32),
            jax.ShapeDtypeStruct((b, g, s, MAX_SLC_BLOCKS), BF16),
            jax.ShapeDtypeStruct((b * g * nqt * sub, 1, MAX_SLC_BLOCKS), F32),
        ],
        compiler_params=_params("arbitrary", "arbitrary", "arbitrary"),
        name="nsa_cmp",
    )(q, kca, vct, gates)


def _super_lists_kernel(used_ref, lst_ref, cnt_ref, *, tb):
    ut = used_ref[...].T.astype(BF16)
    a_i = lax.broadcasted_iota(jnp.int32, (MAX_SLC_BLOCKS, MAX_SLC_BLOCKS), 0)
    b_i = lax.broadcasted_iota(jnp.int32, (MAX_SLC_BLOCKS, MAX_SLC_BLOCKS), 1)
    pair_t = (a_i == jnp.right_shift(b_i, SUPER_SHIFT)).astype(BF16)
    sbf = jnp.where(_dot(pair_t, ut) > 0.5, 1.0, 0.0).astype(BF16)
    incl = _dot((b_i <= a_i).astype(BF16), sbf)[0:MAX_SUPER]
    cnt_ref[...] = jnp.broadcast_to(incl[MAX_SUPER - 1:MAX_SUPER, :], (8, tb)).astype(jnp.int32)
    for i in range(MAX_SUPER):
        lst_ref[i:i + 1, :] = jnp.sum(jnp.where(incl <= float(i), 1.0, 0.0), axis=0, keepdims=True).astype(jnp.int32)


def _super_lists(used):
    t = used.shape[0]
    tb = min(t, LANES)
    assert t % tb == 0
    return pl.pallas_call(
        functools.partial(_super_lists_kernel, tb=tb),
        grid=(t // tb,),
        in_specs=[pl.BlockSpec((tb, MAX_SLC_BLOCKS), lambda i: (i, 0))],
        out_specs=[
            pl.BlockSpec((MAX_SUPER, tb), lambda i: (0, i)),
            pl.BlockSpec((8, tb), lambda i: (0, i)),
        ],
        out_shape=[
            jax.ShapeDtypeStruct((MAX_SUPER, t), jnp.int32),
            jax.ShapeDtypeStruct((8, t), jnp.int32),
        ],
        compiler_params=_params("arbitrary"),
        name="super_lists",
    )(used)


def _nsa_slc_kernel(lst_ref, cnt_ref, q_ref, kas_ref, vts_ref, kaw_ref, vtw_ref, negm_ref, gate_ref, oc_ref, o_ref,
                    m_ref, acc_ref, *, tq):
    q0 = pl.program_id(2) * tq
    hpg = HEADS_PER_GROUP
    ncol = hpg * tq
    tk = CHUNK_SUPERS * SUPER_KEYS
    q = q_ref[0].reshape(ncol, LANES)
    negm = negm_ref[0, 0]
    qa = jnp.concatenate([q, jnp.concatenate([negm] * hpg, axis=0)], axis=1)
    tcol = q0 + jnp.bitwise_and(lax.broadcasted_iota(jnp.int32, (1, ncol), 1), tq - 1)

    m_ref[...] = jnp.full_like(m_ref, NEG_INF)
    acc_ref[...] = jnp.zeros_like(acc_ref)

    n_steps = pl.num_programs(0) * pl.num_programs(1) * pl.num_programs(2)
    step = (pl.program_id(0) * pl.num_programs(1) + pl.program_id(1)) * pl.num_programs(2) + pl.program_id(2)
    n_used = cnt_ref[step]
    n_chunks = jnp.right_shift(n_used + CHUNK_SUPERS - 1, CHUNK_SUPERS.bit_length() - 1)

    def gather(chunk, partial):
        ks, vs, kbase = [], [], []
        for j in range(CHUNK_SUPERS):
            slot = chunk * CHUNK_SUPERS + j
            if partial:
                ok = (slot >= 0) & (slot < n_used)
                sb = jnp.where(ok, lst_ref[jnp.clip(slot, 0, MAX_SUPER - 1) * n_steps + step], 0)
                kbase.append(jnp.where(ok, sb * SUPER_KEYS, FAR_POS))
            else:
                sb = lst_ref[slot * n_steps + step]
            start = pl.multiple_of(sb * SUPER_KEYS, SUPER_KEYS)
            ks.append(kas_ref[0, 0, pl.ds(start, SUPER_KEYS), :])
            vs.append(vts_ref[0, 0, :, pl.ds(start, SUPER_KEYS)])
        return jnp.concatenate(ks, axis=0), jnp.concatenate(vs, axis=1), kbase

    def scores(kt, kbase):
        st = _dot_nt(kt, qa)
        if kbase:
            r = lax.broadcasted_iota(jnp.int32, (tk, 1), 0)
            rs = jnp.right_shift(r, SUPER_KEYS.bit_length() - 1)
            kpos = jnp.full((tk, 1), kbase[0], jnp.int32)
            for j in range(1, CHUNK_SUPERS):
                kpos = jnp.where(rs == j, kbase[j], kpos)
            kpos = kpos + jnp.bitwise_and(r, SUPER_KEYS - 1)
            st = jnp.where(kpos <= tcol, st, NEG_INF)
        return st

    def update(st, vt, present=None):
        m_old = m_ref[...]
        m_new = jnp.maximum(m_old, jnp.max(st, axis=0, keepdims=True))
        if present is not None:
            m_new = jnp.where(present, m_new, m_old)
        p = jnp.exp2(st - m_new)
        acc_new = jnp.exp2(m_old - m_new) * acc_ref[...] + _dot(vt, p.astype(BF16))
        acc_ref[...] = acc_new if present is None else jnp.where(present, acc_new, acc_ref[...])
        m_ref[...] = m_new

    def body(c, carry):
        kt, vt, _ = gather(c, False)
        update(scores(kt, None), vt)
        return carry

    lax.fori_loop(0, jnp.maximum(n_chunks - FINAL_CHUNKS, 0), body, 0)

    finals = []
    for back in range(1, FINAL_CHUNKS + 1):
        kt, vt, kbase = gather(n_chunks - back, True)
        finals.append((scores(kt, kbase if back == 1 else None), vt, None if back == 1 else n_chunks >= back))

    n_win = WINDOW // SUPER_KEYS + 1
    qt = pl.program_id(2)
    row = lax.broadcasted_iota(jnp.int32, (SUPER_KEYS, 1), 0)
    win_scores, win_v = [], []
    for r in range(n_win):
        sb = qt - (n_win - 1) + r
        start = pl.multiple_of(jnp.maximum(sb, 0) * SUPER_KEYS, SUPER_KEYS)
        sw = _dot_nt(kaw_ref[0, 0, pl.ds(start, SUPER_KEYS), :], q)
        if r == n_win - 1:
            sw = jnp.where(sb * SUPER_KEYS + row <= tcol, sw, NEG_INF)
        elif r == 0:
            sw = jnp.where(jnp.where(sb >= 0, sb * SUPER_KEYS + WINDOW, -FAR_POS) + row > tcol, sw, NEG_INF)
        else:
            sw = sw + jnp.where(sb >= 0, 0.0, NEG_INF)
        win_scores.append(sw)
        win_v.append(vtw_ref[0, 0, :, pl.ds(start, SUPER_KEYS)])

    for st, vt, present in finals:
        update(st, vt, present)
    acc = acc_ref[...]
    o_slc_t = acc[0:HEAD_DIM] * (1.0 / acc[HEAD_DIM:HEAD_DIM + 1])

    stw = jnp.concatenate(win_scores, axis=0)
    e = jnp.exp2(stw - jnp.max(stw, axis=0, keepdims=True))
    acc_w = _dot(jnp.concatenate(win_v, axis=1), e.astype(BF16))
    o_win_t = acc_w[0:HEAD_DIM] * (1.0 / acc_w[HEAD_DIM:HEAD_DIM + 1])

    gt = gate_ref[0, 0].T
    comb = _head_stack(o_slc_t, gt, 1, tq) + _head_stack(o_win_t, gt, 2, tq)
    o_ref[0] = (oc_ref[0] + comb.T).astype(BF16)


def _nsa_slc(lists, counts, q, kas, vts, kaw, vtw, negm, gates, oc, tq=128):
    b, nh, s, _ = q.shape
    g, hpg, dh = N_KV_GROUPS, HEADS_PER_GROUP, HEAD_DIM
    assert tq == SUPER_KEYS and WINDOW % tq == 0 and s >= WINDOW + tq and s % (CHUNK_SUPERS * SUPER_KEYS) == 0
    grid_spec = pltpu.PrefetchScalarGridSpec(
        num_scalar_prefetch=2,
        grid=(b, g, s // tq),
        in_specs=[
            pl.BlockSpec((1, hpg, tq, LANES), lambda i, j, k, *_: (i, j, k, 0)),
            pl.BlockSpec((1, 1, s, 2 * LANES), lambda i, j, k, *_: (i, j, 0, 0)),
            pl.BlockSpec((1, 1, VT_ROWS, s), lambda i, j, k, *_: (i, j, 0, 0)),
            pl.BlockSpec((1, 1, s, LANES), lambda i, j, k, *_: (i, j, 0, 0)),
            pl.BlockSpec((1, 1, VT_ROWS, s), lambda i, j, k, *_: (i, j, 0, 0)),
            pl.BlockSpec((1, 1, tq, MAX_SLC_BLOCKS), lambda i, j, k, *_: (i, j, k, 0)),
            pl.BlockSpec((1, 1, tq, LANES), lambda i, j, k, *_: (i, j, k, 0)),
            pl.BlockSpec((1, tq, hpg * dh), lambda i, j, k, *_: (i, k, j)),
        ],
        out_specs=pl.BlockSpec((1, tq, hpg * dh), lambda i, j, k, *_: (i, k, j)),
        scratch_shapes=[
            pltpu.VMEM((1, hpg * tq), F32),
            pltpu.VMEM((VT_ROWS, hpg * tq), F32),
        ],
    )
    return pl.pallas_call(
        functools.partial(_nsa_slc_kernel, tq=tq),
        grid_spec=grid_spec,
        out_shape=jax.ShapeDtypeStruct((b, s, nh * dh), BF16),
        compiler_params=_params("arbitrary", "arbitrary", "arbitrary"),
        name="nsa_slc",
    )(lists, counts, q, kas, vts, kaw, vtw, negm, gates, oc)


def _split_bf16(x, n):
    pieces = []
    for _ in range(n):
        p = x.astype(BF16).astype(F32)
        pieces.append(p)
        x = x - p
    return pieces


def _q_constants():
    slopes = jnp.exp2(-8.0 * jnp.arange(1, N_HEADS + 1, dtype=F32) / N_HEADS) * LOG2E
    pieces = jnp.stack(_split_bf16(slopes, N_SPLIT), axis=1)
    qc = jnp.zeros((N_HEADS, LANES), F32)
    qc = qc.at[:, POS_LANE:POS_LANE + N_SPLIT].set(pieces * L_SLC)
    qc = qc.at[:, POS_LANE + N_SPLIT:POS_LANE + 2 * N_SPLIT].set(pieces)
    return qc


def _kv_weight(w_kv):
    d = w_kv.shape[0]
    w = w_kv.reshape(d, N_KV_SETS, N_KV_GROUPS, HEAD_DIM)

    def flat(x):
        return x.reshape(d, N_KV_GROUPS * HEAD_DIM)

    def padded(x):
        return jnp.pad(x, ((0, 0), (0, 0), (0, LANES - HEAD_DIM))).reshape(d, N_KV_GROUPS * LANES)

    out = jnp.concatenate([flat(w[:, 0]), flat(w[:, 1]), padded(w[:, 2]), flat(w[:, 3]), padded(w[:, 4]),
                           flat(w[:, 5])], axis=1)
    assert out.shape[1] == _KV_COLS
    return out.astype(BF16)


def _qg_weight(w_qg):
    d = w_qg.shape[0]
    nq = N_HEADS * HEAD_DIM
    wq = jnp.pad(w_qg[:, :nq].reshape(d, N_HEADS, HEAD_DIM), ((0, 0), (0, 0), (0, LANES - HEAD_DIM)))
    wg = jnp.pad(w_qg[:, nq:].reshape(d, N_KV_GROUPS, 3 * HEADS_PER_GROUP),
                 ((0, 0), (0, 0), (0, LANES - 3 * HEADS_PER_GROUP)))
    return jnp.concatenate([wq.reshape(d, N_HEADS * LANES), wg.reshape(d, N_KV_GROUPS * LANES)], axis=1).astype(BF16)


def kernel(x, a_norm, a_w_in, a_conv, a_w_out, kv_norm, w_kv, cmp_pe_k, cmp_w1_k, cmp_w2_k, cmp_pe_v, cmp_w1_v,
           cmp_w2_v, b_norm, b_w_qg, b_w_o, f_norm, f_w_gu, f_w_down, final_norm):
    b, s, d = x.shape
    nq = N_HEADS * HEAD_DIM
    h = _mixer_a(x, a_norm[0], a_w_in[0].astype(BF16), a_conv[0], a_w_out[0].astype(BF16))
    h = _ffn(h.reshape(b * s, d), f_norm[0], f_w_gu[0].astype(BF16), f_w_down[0].astype(BF16), final_norm)
    h = h.reshape(b, s, d)

    kvc, kas, vts, kaw, vtw, q, gates = _proj(h, kv_norm, b_norm[0], _kv_weight(w_kv), _qg_weight(b_w_qg[0]),
                                              _q_constants())
    pe2 = jnp.stack([cmp_pe_k, cmp_pe_v]).reshape(2, 1, L_CMP * HEAD_DIM)
    pe2 = jnp.broadcast_to(pe2, (2, 8, L_CMP * HEAD_DIM)).astype(BF16)
    w2 = jnp.pad(jnp.stack([cmp_w2_k, cmp_w2_v]), ((0, 0), (0, 0), (0, LANES - HEAD_DIM))).astype(BF16)
    kca, vct = _compress(kvc, pe2, jnp.stack([cmp_w1_k, cmp_w1_v]).astype(BF16), w2)

    oc, negm, used = _nsa_cmp(q, kca, vct, gates)
    lists, counts = _super_lists(used.reshape(used.shape[0], MAX_SLC_BLOCKS))
    o = _nsa_slc(lists.reshape(-1), counts[0], q, kas, vts, kaw, vtw, negm, gates, oc)

    out = _ffn(h.reshape(b * s, d), f_norm[1], f_w_gu[1].astype(BF16), f_w_down[1].astype(BF16), final_norm,
               o=o.reshape(b * s, nq), w_o=b_w_o[0].astype(BF16), with_final=True)
    return out.reshape(b, s, d)
```

```python
import functools
import math

import jax
import jax.numpy as jnp
from jax import lax
from jax.experimental import pallas as pl
from jax.experimental.pallas import tpu as pltpu

F32 = jnp.float32
BF16 = jnp.bfloat16

N_HEADS = 16
N_KV_GROUPS = 4
HEADS_PER_GROUP = N_HEADS // N_KV_GROUPS
HEAD_DIM = 64
L_CMP = 32
D_CMP = 16
L_SLC = 64
N_SEL = 16
N_FORCED = 3
WINDOW = 512
N_KV_SETS = 6
CONV_WIDTH = 3
RMS_EPS = 1e-5
NEG_INF = -1e30
LOG2E = math.log2(math.e)
LANES = 128
MAX_SLC_BLOCKS = LANES
SUPER_KEYS = LANES
SUPER_SHIFT = (SUPER_KEYS // L_SLC).bit_length() - 1
MAX_SUPER = MAX_SLC_BLOCKS * L_SLC // SUPER_KEYS
CHUNK_SUPERS = 4
FINAL_CHUNKS = 3
CMP_ROW_TILE = 128
VT_ROWS = HEAD_DIM + 16
FAR_POS = 1 << 30
POS_LANE = HEAD_DIM
N_SPLIT = 3
VMEM_LIMIT_BYTES = 56 * 1024 * 1024


def _rms(x, g):
    return x * lax.rsqrt(jnp.mean(x * x, axis=-1, keepdims=True) + RMS_EPS) * g


def _dot(a, b):
    return jnp.dot(a, b, preferred_element_type=F32)


def _dot_nt(a, b):
    return lax.dot_general(a, b, (((1,), (1,)), ((), ())), preferred_element_type=F32)


def _params(*sem, flags=None):
    return pltpu.CompilerParams(dimension_semantics=sem, vmem_limit_bytes=VMEM_LIMIT_BYTES, flags=flags)


def _pos_columns(coarse, fine, lane):
    is_c = (lane >= POS_LANE) & (lane < POS_LANE + N_SPLIT)
    is_f = (lane >= POS_LANE + N_SPLIT) & (lane < POS_LANE + 2 * N_SPLIT)
    return jnp.where(is_c, coarse, jnp.where(is_f, fine, 0.0))


def _mixer_a_kernel(x_ref, g_ref, win_ref, conv_ref, wout_ref, o_ref, ext_ref, *, tm, d):
    @pl.when(pl.program_id(1) == 0)
    def _():
        ext_ref[0:8, :] = jnp.zeros((8, d), F32)

    x = x_ref[0]
    xb = _rms(x, g_ref[...]).astype(BF16)
    b_gate = _dot(xb, win_ref[:, 0:d])
    cv = _dot(xb, win_ref[:, d:2 * d]) * _dot(xb, win_ref[:, 2 * d:3 * d])
    ext_ref[8:8 + tm, :] = cv
    cv1 = ext_ref[7:7 + tm, :]
    cv2 = ext_ref[6:6 + tm, :]
    u = conv_ref[0:1, :] * cv2 + conv_ref[1:2, :] * cv1 + conv_ref[2:3, :] * cv
    ext_ref[0:8, :] = cv[tm - 8:tm, :]
    o_ref[0] = x + _dot((b_gate * u).astype(BF16), wout_ref[...])


def _mixer_a(x, g, w_in, conv_w, w_out, tm=512):
    b, s, d = x.shape
    assert s % tm == 0
    return pl.pallas_call(
        functools.partial(_mixer_a_kernel, tm=tm, d=d),
        grid=(b, s // tm),
        in_specs=[
            pl.BlockSpec((1, tm, d), lambda i, j: (i, j, 0)),
            pl.BlockSpec((1, d), lambda i, j: (0, 0)),
            pl.BlockSpec((d, 3 * d), lambda i, j: (0, 0)),
            pl.BlockSpec((CONV_WIDTH, d), lambda i, j: (0, 0)),
            pl.BlockSpec((d, d), lambda i, j: (0, 0)),
        ],
        out_specs=pl.BlockSpec((1, tm, d), lambda i, j: (i, j, 0)),
        out_shape=jax.ShapeDtypeStruct((b, s, d), F32),
        scratch_shapes=[pltpu.VMEM((tm + 8, d), F32)],
        compiler_params=_params("arbitrary", "arbitrary"),
        name="mixer_a",
    )(x, g.reshape(1, d), w_in, conv_w, w_out)


def _ffn_kernel(*refs, with_proj, with_final, f, tf):
    if with_proj:
        h_ref, o_ref_in, wo_ref, g_ref, wgu_ref, wd_ref, fg_ref, out_ref = refs
    else:
        h_ref, g_ref, wgu_ref, wd_ref, fg_ref, out_ref = refs
    h = h_ref[...]
    if with_proj:
        h = h + _dot(o_ref_in[...], wo_ref[...])
    xb = _rms(h, g_ref[...]).astype(BF16)
    y = h
    for c in range(f // tf):
        gate = _dot(xb, wgu_ref[:, c * tf:(c + 1) * tf])
        up = _dot(xb, wgu_ref[:, f + c * tf:f + (c + 1) * tf])
        act = (gate * jax.nn.sigmoid(gate)) * up
        y = y + _dot(act.astype(BF16), wd_ref[c * tf:(c + 1) * tf, :])
    if with_final:
        y = _rms(y, fg_ref[...])
    out_ref[...] = y


def _resident(shape):
    return pl.BlockSpec(shape, lambda i: (0,) * len(shape), pipeline_mode=pl.Buffered(1))


def _ffn(h, g, w_gu, w_down, final_g, o=None, w_o=None, with_final=False, tm=512, tf=1408):
    t, d = h.shape
    f = w_down.shape[0]
    assert t % tm == 0 and f % tf == 0
    with_proj = o is not None
    in_specs = [pl.BlockSpec((tm, d), lambda i: (i, 0))]
    args = [h]
    if with_proj:
        in_specs += [pl.BlockSpec((tm, o.shape[1]), lambda i: (i, 0)), _resident(w_o.shape)]
        args += [o, w_o]
    in_specs += [_resident((1, d)), _resident(w_gu.shape), _resident(w_down.shape), _resident((1, d))]
    args += [g.reshape(1, d), w_gu, w_down, final_g.reshape(1, d)]
    return pl.pallas_call(
        functools.partial(_ffn_kernel, with_proj=with_proj, with_final=with_final, f=f, tf=tf),
        grid=(t // tm,),
        in_specs=in_specs,
        out_specs=pl.BlockSpec((tm, d), lambda i: (i, 0)),
        out_shape=jax.ShapeDtypeStruct((t, d), F32),
        compiler_params=_params("arbitrary"),
        name="ffn_proj" if with_proj else "ffn",
    )(*args)


def _halves(x):
    return x, pltpu.roll(x, HEAD_DIM, axis=1)


def _proj_kernel(h_ref, gkv_ref, gq_ref, wkv_ref, wqg_ref, qc_ref,
                 kvc_ref, kas_ref, vts_ref, kaw_ref, vtw_ref, q_ref, gate_ref, *, tm):
    g, dh = N_KV_GROUPS, HEAD_DIM
    gw = g * dh
    s0 = pl.program_id(1) * tm
    x = h_ref[0]
    y = x * lax.rsqrt(jnp.mean(x * x, axis=-1, keepdims=True) + RMS_EPS)
    kv = _dot((y * gkv_ref[...]).astype(BF16), wkv_ref[...])
    for st in range(2):
        for grp in range(g):
            kvc_ref[st, 0, grp] = kv[:, st * gw + grp * dh:st * gw + (grp + 1) * dh].astype(BF16)

    kpos = s0 + lax.broadcasted_iota(jnp.int32, (tm, LANES), 0)
    lane = lax.broadcasted_iota(jnp.int32, (tm, LANES), 1)
    low = lane < dh
    kblk = jnp.right_shift(kpos, L_SLC.bit_length() - 1)
    posc = _pos_columns(kblk.astype(F32), jnp.bitwise_and(kpos, L_SLC - 1).astype(F32), lane)
    onehot = (lane == kblk).astype(BF16)
    ones_rows = (lax.broadcasted_iota(jnp.int32, (VT_ROWS - dh, tm), 0) == 0).astype(BF16)
    for pair in range(g // 2):
        for st, k_ref, vt_ref in ((2, kas_ref, vts_ref), (4, kaw_ref, vtw_ref)):
            k_pair = kv[:, st * gw + pair * LANES:st * gw + (pair + 1) * LANES]
            vt = kv[:, (st + 1) * gw + pair * LANES:(st + 1) * gw + (pair + 1) * LANES].T
            for half, kh in enumerate(_halves(k_pair)):
                grp = 2 * pair + half
                k_ref[0, grp, :, 0:LANES] = jnp.where(low, kh, posc).astype(BF16)
                if st == 2:
                    k_ref[0, grp, :, LANES:2 * LANES] = onehot
                vt_ref[0, grp, 0:dh] = vt[half * dh:(half + 1) * dh].astype(BF16)
                vt_ref[0, grp, dh:VT_ROWS] = ones_rows

    qg = _dot((y * gq_ref[...]).astype(BF16), wqg_ref[...])
    for pair in range(N_HEADS // 2):
        q_pair = qg[:, pair * LANES:(pair + 1) * LANES] * (LOG2E * dh ** -0.5)
        for half, qh in enumerate(_halves(q_pair)):
            hd = 2 * pair + half
            q_ref[0, hd] = jnp.where(low, qh, qc_ref[hd:hd + 1, :]).astype(BF16)
    nq = N_HEADS * dh
    for grp in range(g):
        gate_ref[0, grp] = jax.nn.sigmoid(qg[:, nq + grp * LANES: nq + (grp + 1) * LANES])


def _proj(h, g_kv, g_q, w_kv, w_qg, qconst, tm=512):
    b, s, d = h.shape
    assert s % tm == 0 and s // L_SLC <= MAX_SLC_BLOCKS
    g, dh = N_KV_GROUPS, HEAD_DIM
    return pl.pallas_call(
        functools.partial(_proj_kernel, tm=tm),
        grid=(b, s // tm),
        in_specs=[
            pl.BlockSpec((1, tm, d), lambda i, j: (i, j, 0)),
            pl.BlockSpec((1, d), lambda i, j: (0, 0)),
            pl.BlockSpec((1, d), lambda i, j: (0, 0)),
            pl.BlockSpec(w_kv.shape, lambda i, j: (0, 0)),
            pl.BlockSpec(w_qg.shape, lambda i, j: (0, 0)),
            pl.BlockSpec(qconst.shape, lambda i, j: (0, 0)),
        ],
        out_specs=[
            pl.BlockSpec((2, 1, g, tm, dh), lambda i, j: (0, i, 0, j, 0)),
            pl.BlockSpec((1, g, tm, 2 * LANES), lambda i, j: (i, 0, j, 0)),
            pl.BlockSpec((1, g, VT_ROWS, tm), lambda i, j: (i, 0, 0, j)),
            pl.BlockSpec((1, g, tm, LANES), lambda i, j: (i, 0, j, 0)),
            pl.BlockSpec((1, g, VT_ROWS, tm), lambda i, j: (i, 0, 0, j)),
            pl.BlockSpec((1, N_HEADS, tm, LANES), lambda i, j: (i, 0, j, 0)),
            pl.BlockSpec((1, g, tm, LANES), lambda i, j: (i, 0, j, 0)),
        ],
        out_shape=[
            jax.ShapeDtypeStruct((2, b, g, s, dh), BF16),
            jax.ShapeDtypeStruct((b, g, s, 2 * LANES), BF16),
            jax.ShapeDtypeStruct((b, g, VT_ROWS, s), BF16),
            jax.ShapeDtypeStruct((b, g, s, LANES), BF16),
            jax.ShapeDtypeStruct((b, g, VT_ROWS, s), BF16),
            jax.ShapeDtypeStruct((b, N_HEADS, s, LANES), BF16),
            jax.ShapeDtypeStruct((b, g, s, LANES), F32),
        ],
        compiler_params=_params("arbitrary", "arbitrary"),
        name="proj",
    )(h, g_kv.reshape(1, d), g_q.reshape(1, d), w_kv, w_qg, qconst)


def _compress_kernel(r_ref, pe_ref, w1_ref, w2_ref, kca_ref, vct_ref, *, m):
    half = D_CMP * HEAD_DIM
    row = lax.broadcasted_iota(jnp.int32, (m, LANES), 0)
    lane = lax.broadcasted_iota(jnp.int32, (m, LANES), 1)

    def mlp(i):
        r = r_ref[i, 0]
        top = _dot(r, w1_ref[i, 0:half, :])
        bot = _dot(r, w1_ref[i, half:2 * half, :])
        pe_term = _dot(pe_ref[i], w1_ref[i])[0:1, :]
        hid = top + pltpu.roll(bot, m - 1, axis=0) + pe_term
        hid = hid * jax.nn.sigmoid(hid)
        out = _dot(hid.astype(BF16), w2_ref[i])
        return jnp.where(row < m - 1, out, 0.0)

    coarse = jnp.right_shift(row, 2).astype(F32)
    fine = jnp.bitwise_and(row, 3).astype(F32) * D_CMP + (L_CMP - 1) / 2.0
    kca_ref[0] = (mlp(0) + _pos_columns(coarse, fine, lane)).astype(BF16)
    vct_ref[0] = mlp(1).T[0:HEAD_DIM].astype(BF16)


def _compress(kvc, pe2, w1_2, w2_2):
    _, b, g, s, dh = kvc.shape
    m = s // D_CMP
    r = kvc.reshape(2, b * g, m, D_CMP * dh)
    hid = w1_2.shape[-1]
    return pl.pallas_call(
        functools.partial(_compress_kernel, m=m),
        grid=(b * g,),
        in_specs=[
            pl.BlockSpec((2, 1, m, D_CMP * dh), lambda j: (0, j, 0, 0)),
            pl.BlockSpec((2, 8, L_CMP * dh), lambda j: (0, 0, 0)),
            pl.BlockSpec((2, L_CMP * dh, hid), lambda j: (0, 0, 0)),
            pl.BlockSpec((2, hid, LANES), lambda j: (0, 0, 0)),
        ],
        out_specs=[
            pl.BlockSpec((1, m, LANES), lambda j: (j, 0, 0)),
            pl.BlockSpec((1, dh, m), lambda j: (j, 0, 0)),
        ],
        out_shape=[
            jax.ShapeDtypeStruct((b * g, m, LANES), BF16),
            jax.ShapeDtypeStruct((b * g, dh, m), BF16),
        ],
        compiler_params=_params("arbitrary"),
        name="compress",
    )(r, pe2, w1_2, w2_2)


def _head_stack(ot, gt, branch, tq):
    return jnp.concatenate(
        [ot[:, h * tq:(h + 1) * tq] * gt[3 * h + branch:3 * h + branch + 1, :] for h in range(HEADS_PER_GROUP)],
        axis=0)


def _nsa_cmp_kernel(q_ref, kca_ref, vct_ref, gate_ref, oc_ref, negm_ref, used_ref, *, tq, m):
    q0 = pl.program_id(2) * tq
    hpg = HEADS_PER_GROUP
    ncol = hpg * tq
    q = q_ref[0].reshape(ncol, LANES)
    tcol = q0 + jnp.bitwise_and(lax.broadcasted_iota(jnp.int32, (1, ncol), 1), tq - 1)

    def attend(rows):
        st = _dot_nt(kca_ref[0, 0:rows, :], q)
        nrow = lax.broadcasted_iota(jnp.int32, (rows, 1), 0)
        vis = (nrow * D_CMP + (L_CMP - 1)) <= tcol
        st = jnp.where(vis, st, NEG_INF)
        e = jnp.exp2(st - jnp.max(st, axis=0, keepdims=True))
        rcp = jnp.where(tcol >= L_CMP - 1, 1.0 / jnp.sum(e, axis=0, keepdims=True), 0.0)
        p = e * rcp
        ot = _dot(vct_ref[0, :, 0:rows], p.astype(BF16))
        pt = p[:, 0:tq]
        for h in range(1, hpg):
            pt = pt + p[:, h * tq:(h + 1) * tq]
        jj = lax.broadcasted_iota(jnp.int32, (MAX_SLC_BLOCKS, rows), 0) * L_SLC
        nn = lax.broadcasted_iota(jnp.int32, (MAX_SLC_BLOCKS, rows), 1) * D_CMP
        overlap_t = ((nn < jj + L_SLC) & (nn + L_CMP > jj)).astype(BF16)
        p_hi = pt.astype(BF16)
        p_lo = (pt - p_hi.astype(F32)).astype(BF16)
        return ot, _dot(overlap_t, p_hi) + _dot(overlap_t, p_lo)

    row_tile = min(m, CMP_ROW_TILE)
    n_variants = m // row_tile
    if n_variants == 1:
        ot, imp = attend(m)
    else:
        needed = pl.program_id(2) * (tq // D_CMP) + (tq // D_CMP - 1)
        variant = jnp.minimum(jnp.right_shift(needed - 1, row_tile.bit_length() - 1), n_variants - 1)
        ot, imp = lax.switch(variant, [functools.partial(attend, (v + 1) * row_tile) for v in range(n_variants)])
    gt = gate_ref[0, 0].T
    oc_ref[0] = _head_stack(ot, gt, 0, tq).T

    jrow = lax.broadcasted_iota(jnp.int32, (MAX_SLC_BLOCKS, tq), 0)
    cur = jnp.right_shift(q0 + lax.broadcasted_iota(jnp.int32, (1, tq), 1), L_SLC.bit_length() - 1)
    valid = jrow <= cur
    forced = (jrow == 0) | (jrow == cur) | (jrow == cur - 1)
    val = jnp.where(valid & jnp.logical_not(forced), imp, NEG_INF)
    jrow_f = jrow.astype(F32)
    sel = jnp.where(forced, 1.0, 0.0)
    for _ in range(N_SEL - N_FORCED):
        mx = jnp.max(val, axis=0, keepdims=True)
        first = jnp.min(jnp.where(val == mx, jrow_f, float(MAX_SLC_BLOCKS)), axis=0, keepdims=True)
        hit = jrow_f == first
        sel = jnp.where(hit, 1.0, sel)
        val = jnp.where(hit, -3.0e38, val)
    sel_t = jnp.where(valid, sel, 0.0).T
    negm_ref[0, 0] = jnp.where(sel_t > 0.5, 0.0, NEG_INF).astype(BF16)
    for r in range(tq // SUPER_KEYS):
        used_ref[r] = jnp.max(sel_t[r * SUPER_KEYS:(r + 1) * SUPER_KEYS], axis=0, keepdims=True)


def _nsa_cmp(q, kca, vct, gates, tq=256):
    b, nh, s, _ = q.shape
    g, hpg, dh = N_KV_GROUPS, HEADS_PER_GROUP, HEAD_DIM
    m = kca.shape[1]
    assert s % tq == 0 and tq % SUPER_KEYS == 0 and tq & (tq - 1) == 0
    nqt = s // tq
    sub = tq // SUPER_KEYS
    return pl.pallas_call(
        functools.partial(_nsa_cmp_kernel, tq=tq, m=m),
        grid=(b, g, s // tq),
        in_specs=[
            pl.BlockSpec((1, hpg, tq, LANES), lambda i, j, k: (i, j, k, 0)),
            pl.BlockSpec((1, m, LANES), lambda i, j, k: (i * N_KV_GROUPS + j, 0, 0)),
            pl.BlockSpec((1, dh, m), lambda i, j, k: (i * N_KV_GROUPS + j, 0, 0)),
            pl.BlockSpec((1, 1, tq, LANES), lambda i, j, k: (i, j, k, 0)),
        ],
        out_specs=[
            pl.BlockSpec((1, tq, hpg * dh), lambda i, j, k: (i, k, j)),
            pl.BlockSpec((1, 1, tq, MAX_SLC_BLOCKS), lambda i, j, k: (i, j, k, 0)),
            pl.BlockSpec((sub, 1, MAX_SLC_BLOCKS), lambda i, j, k: ((i * N_KV_GROUPS + j) * nqt + k, 0, 0)),
        ],
        out_shape=[
            jax.ShapeDtypeStruct((b, s, nh * dh), F32),
            jax.ShapeDtypeStruct((b, g, s, MAX_SLC_BLOCKS), BF16),
            jax.ShapeDtypeStruct((b * g * nqt * sub, 1, MAX_SLC_BLOCKS), F32),
        ],
        compiler_params=_params("arbitrary", "arbitrary", "arbitrary"),
        name="nsa_cmp",
    )(q, kca, vct, gates)


def _super_lists_kernel(used_ref, lst_ref, cnt_ref, *, tb):
    ut = used_ref[...].T.astype(BF16)
    a_i = lax.broadcasted_iota(jnp.int32, (MAX_SLC_BLOCKS, MAX_SLC_BLOCKS), 0)
    b_i = lax.broadcasted_iota(jnp.int32, (MAX_SLC_BLOCKS, MAX_SLC_BLOCKS), 1)
    pair_t = (a_i == jnp.right_shift(b_i, SUPER_SHIFT)).astype(BF16)
    sbf = jnp.where(_dot(pair_t, ut) > 0.5, 1.0, 0.0).astype(BF16)
    incl = _dot((b_i <= a_i).astype(BF16), sbf)[0:MAX_SUPER]
    cnt_ref[...] = jnp.broadcast_to(incl[MAX_SUPER - 1:MAX_SUPER, :], (8, tb)).astype(jnp.int32)
    for i in range(MAX_SUPER):
        lst_ref[i:i + 1, :] = jnp.sum(jnp.where(incl <= float(i), 1.0, 0.0), axis=0, keepdims=True).astype(jnp.int32)


def _super_lists(used):
    t = used.shape[0]
    tb = min(t, LANES)
    assert t % tb == 0
    return pl.pallas_call(
        functools.partial(_super_lists_kernel, tb=tb),
        grid=(t // tb,),
        in_specs=[pl.BlockSpec((tb, MAX_SLC_BLOCKS), lambda i: (i, 0))],
        out_specs=[
            pl.BlockSpec((MAX_SUPER, tb), lambda i: (0, i)),
            pl.BlockSpec((8, tb), lambda i: (0, i)),
        ],
        out_shape=[
            jax.ShapeDtypeStruct((MAX_SUPER, t), jnp.int32),
            jax.ShapeDtypeStruct((8, t), jnp.int32),
        ],
        compiler_params=_params("arbitrary"),
        name="super_lists",
    )(used)


def _nsa_slc_kernel(lst_ref, cnt_ref, q_ref, kas_ref, vts_ref, kaw_ref, vtw_ref, negm_ref, gate_ref, oc_ref, o_ref,
                    m_ref, acc_ref, *, tq):
    q0 = pl.program_id(2) * tq
    hpg = HEADS_PER_GROUP
    ncol = hpg * tq
    tk = CHUNK_SUPERS * SUPER_KEYS
    q = q_ref[0].reshape(ncol, LANES)
    negm = negm_ref[0, 0]
    qa = jnp.concatenate([q, jnp.concatenate([negm] * hpg, axis=0)], axis=1)
    tcol = q0 + jnp.bitwise_and(lax.broadcasted_iota(jnp.int32, (1, ncol), 1), tq - 1)

    m_ref[...] = jnp.full_like(m_ref, NEG_INF)
    acc_ref[...] = jnp.zeros_like(acc_ref)

    n_steps = pl.num_programs(0) * pl.num_programs(1) * pl.num_programs(2)
    step = (pl.program_id(0) * pl.num_programs(1) + pl.program_id(1)) * pl.num_programs(2) + pl.program_id(2)
    n_used = cnt_ref[step]
    n_chunks = jnp.right_shift(n_used + CHUNK_SUPERS - 1, CHUNK_SUPERS.bit_length() - 1)

    def gather(chunk, partial):
        ks, vs, kbase = [], [], []
        for j in range(CHUNK_SUPERS):
            slot = chunk * CHUNK_SUPERS + j
            if partial:
                ok = (slot >= 0) & (slot < n_used)
                sb = jnp.where(ok, lst_ref[jnp.clip(slot, 0, MAX_SUPER - 1) * n_steps + step], 0)
                kbase.append(jnp.where(ok, sb * SUPER_KEYS, FAR_POS))
            else:
                sb = lst_ref[slot * n_steps + step]
            start = pl.multiple_of(sb * SUPER_KEYS, SUPER_KEYS)
            ks.append(kas_ref[0, 0, pl.ds(start, SUPER_KEYS), :])
            vs.append(vts_ref[0, 0, :, pl.ds(start, SUPER_KEYS)])
        return jnp.concatenate(ks, axis=0), jnp.concatenate(vs, axis=1), kbase

    def scores(kt, kbase):
        st = _dot_nt(kt, qa)
        if kbase:
            r = lax.broadcasted_iota(jnp.int32, (tk, 1), 0)
            rs = jnp.right_shift(r, SUPER_KEYS.bit_length() - 1)
            kpos = jnp.full((tk, 1), kbase[0], jnp.int32)
            for j in range(1, CHUNK_SUPERS):
                kpos = jnp.where(rs == j, kbase[j], kpos)
            kpos = kpos + jnp.bitwise_and(r, SUPER_KEYS - 1)
            st = jnp.where(kpos <= tcol, st, NEG_INF)
        return st

    def update(st, vt, present=None):
        m_old = m_ref[...]
        m_new = jnp.maximum(m_old, jnp.max(st, axis=0, keepdims=True))
        if present is not None:
            m_new = jnp.where(present, m_new, m_old)
        p = jnp.exp2(st - m_new)
        acc_new = jnp.exp2(m_old - m_new) * acc_ref[...] + _dot(vt, p.astype(BF16))
        acc_ref[...] = acc_new if present is None else jnp.where(present, acc_new, acc_ref[...])
        m_ref[...] = m_new

    def body(c, carry):
        kt, vt, _ = gather(c, False)
        update(scores(kt, None), vt)
        return carry

    lax.fori_loop(0, jnp.maximum(n_chunks - FINAL_CHUNKS, 0), body, 0)

    finals = []
    for back in range(1, FINAL_CHUNKS + 1):
        kt, vt, kbase = gather(n_chunks - back, True)
        finals.append((scores(kt, kbase if back == 1 else None), vt, None if back == 1 else n_chunks >= back))

    n_win = WINDOW // SUPER_KEYS + 1
    qt = pl.program_id(2)
    row = lax.broadcasted_iota(jnp.int32, (SUPER_KEYS, 1), 0)
    win_scores, win_v = [], []
    for r in range(n_win):
        sb = qt - (n_win - 1) + r
        start = pl.multiple_of(jnp.maximum(sb, 0) * SUPER_KEYS, SUPER_KEYS)
        sw = _dot_nt(kaw_ref[0, 0, pl.ds(start, SUPER_KEYS), :], q)
        if r == n_win - 1:
            sw = jnp.where(sb * SUPER_KEYS + row <= tcol, sw, NEG_INF)
        elif r == 0:
            sw = jnp.where(jnp.where(sb >= 0, sb * SUPER_KEYS + WINDOW, -FAR_POS) + row > tcol, sw, NEG_INF)
        else:
            sw = sw + jnp.where(sb >= 0, 0.0, NEG_INF)
        win_scores.append(sw)
        win_v.append(vtw_ref[0, 0, :, pl.ds(start, SUPER_KEYS)])

    for st, vt, present in finals:
        update(st, vt, present)
    acc = acc_ref[...]
    o_slc_t = acc[0:HEAD_DIM] * (1.0 / acc[HEAD_DIM:HEAD_DIM + 1])

    stw = jnp.concatenate(win_scores, axis=0)
    e = jnp.exp2(stw - jnp.max(stw, axis=0, keepdims=True))
    acc_w = _dot(jnp.concatenate(win_v, axis=1), e.astype(BF16))
    o_win_t = acc_w[0:HEAD_DIM] * (1.0 / acc_w[HEAD_DIM:HEAD_DIM + 1])

    gt = gate_ref[0, 0].T
    comb = _head_stack(o_slc_t, gt, 1, tq) + _head_stack(o_win_t, gt, 2, tq)
    o_ref[0] = (oc_ref[0] + comb.T).astype(BF16)


def _nsa_slc(lists, counts, q, kas, vts, kaw, vtw, negm, gates, oc, tq=128):
    b, nh, s, _ = q.shape
    g, hpg, dh = N_KV_GROUPS, HEADS_PER_GROUP, HEAD_DIM
    assert tq == SUPER_KEYS and WINDOW % tq == 0 and s >= WINDOW + tq and s % (CHUNK_SUPERS * SUPER_KEYS) == 0
    grid_spec = pltpu.PrefetchScalarGridSpec(
        num_scalar_prefetch=2,
        grid=(b, g, s // tq),
        in_specs=[
            pl.BlockSpec((1, hpg, tq, LANES), lambda i, j, k, *_: (i, j, k, 0)),
            pl.BlockSpec((1, 1, s, 2 * LANES), lambda i, j, k, *_: (i, j, 0, 0)),
            pl.BlockSpec((1, 1, VT_ROWS, s), lambda i, j, k, *_: (i, j, 0, 0)),
            pl.BlockSpec((1, 1, s, LANES), lambda i, j, k, *_: (i, j, 0, 0)),
            pl.BlockSpec((1, 1, VT_ROWS, s), lambda i, j, k, *_: (i, j, 0, 0)),
            pl.BlockSpec((1, 1, tq, MAX_SLC_BLOCKS), lambda i, j, k, *_: (i, j, k, 0)),
            pl.BlockSpec((1, 1, tq, LANES), lambda i, j, k, *_: (i, j, k, 0)),
            pl.BlockSpec((1, tq, hpg * dh), lambda i, j, k, *_: (i, k, j)),
        ],
        out_specs=pl.BlockSpec((1, tq, hpg * dh), lambda i, j, k, *_: (i, k, j)),
        scratch_shapes=[
            pltpu.VMEM((1, hpg * tq), F32),
            pltpu.VMEM((VT_ROWS, hpg * tq), F32),
        ],
    )
    return pl.pallas_call(
        functools.partial(_nsa_slc_kernel, tq=tq),
        grid_spec=grid_spec,
        out_shape=jax.ShapeDtypeStruct((b, s, nh * dh), BF16),
        compiler_params=_params("arbitrary", "arbitrary", "arbitrary"),
        name="nsa_slc",
    )(lists, counts, q, kas, vts, kaw, vtw, negm, gates, oc)


def _split_bf16(x, n):
    pieces = []
    for _ in range(n):
        p = x.astype(BF16).astype(F32)
        pieces.append(p)
        x = x - p
    return pieces


def _q_constants():
    slopes = jnp.exp2(-8.0 * jnp.arange(1, N_HEADS + 1, dtype=F32) / N_HEADS) * LOG2E
    pieces = jnp.stack(_split_bf16(slopes, N_SPLIT), axis=1)
    qc = jnp.zeros((N_HEADS, LANES), F32)
    qc = qc.at[:, POS_LANE:POS_LANE + N_SPLIT].set(pieces * L_SLC)
    qc = qc.at[:, POS_LANE + N_SPLIT:POS_LANE + 2 * N_SPLIT].set(pieces)
    return qc


def _qg_weight(w_qg):
    d = w_qg.shape[0]
    nq = N_HEADS * HEAD_DIM
    wg = jnp.pad(w_qg[:, nq:].reshape(d, N_KV_GROUPS, 3 * HEADS_PER_GROUP),
                 ((0, 0), (0, 0), (0, LANES - 3 * HEADS_PER_GROUP)))
    return jnp.concatenate([w_qg[:, :nq], wg.reshape(d, N_KV_GROUPS * LANES)], axis=1).astype(BF16)


def kernel(x, a_norm, a_w_in, a_conv, a_w_out, kv_norm, w_kv, cmp_pe_k, cmp_w1_k, cmp_w2_k, cmp_pe_v, cmp_w1_v,
           cmp_w2_v, b_norm, b_w_qg, b_w_o, f_norm, f_w_gu, f_w_down, final_norm):
    b, s, d = x.shape
    nq = N_HEADS * HEAD_DIM
    h = _mixer_a(x, a_norm[0], a_w_in[0].astype(BF16), a_conv[0], a_w_out[0].astype(BF16))
    h = _ffn(h.reshape(b * s, d), f_norm[0], f_w_gu[0].astype(BF16), f_w_down[0].astype(BF16), final_norm)
    h = h.reshape(b, s, d)

    kvc, kas, vts, kaw, vtw, q, gates = _proj(h, kv_norm, b_norm[0], w_kv.astype(BF16), _qg_weight(b_w_qg[0]),
                                              _q_constants())
    pe2 = jnp.stack([cmp_pe_k, cmp_pe_v]).reshape(2, 1, L_CMP * HEAD_DIM)
    pe2 = jnp.broadcast_to(pe2, (2, 8, L_CMP * HEAD_DIM)).astype(BF16)
    w2 = jnp.pad(jnp.stack([cmp_w2_k, cmp_w2_v]), ((0, 0), (0, 0), (0, LANES - HEAD_DIM))).astype(BF16)
    kca, vct = _compress(kvc, pe2, jnp.stack([cmp_w1_k, cmp_w1_v]).astype(BF16), w2)

    oc, negm, used = _nsa_cmp(q, kca, vct, gates)
    lists, counts = _super_lists(used.reshape(used.shape[0], MAX_SLC_BLOCKS))
    o = _nsa_slc(lists.reshape(-1), counts[0], q, kas, vts, kaw, vtw, negm, gates, oc)

    out = _ffn(h.reshape(b * s, d), f_norm[1], f_w_gu[1].astype(BF16), f_w_down[1].astype(BF16), final_norm,
               o=o.reshape(b * s, nq), w_o=b_w_o[0].astype(BF16), with_final=True)
    return out.reshape(b, s, d)
```

```python
import functools
import math

import jax
import jax.numpy as jnp
from jax import lax
from jax.experimental import pallas as pl
from jax.experimental.pallas import tpu as pltpu

F32 = jnp.float32
BF16 = jnp.bfloat16

N_HEADS = 16
N_KV_GROUPS = 4
HEADS_PER_GROUP = N_HEADS // N_KV_GROUPS
HEAD_DIM = 64
L_CMP = 32
D_CMP = 16
L_SLC = 64
N_SEL = 16
N_FORCED = 3
WINDOW = 512
N_KV_SETS = 6
CONV_WIDTH = 3
RMS_EPS = 1e-5
NEG_INF = -1e30
PICKED = -3.0e38
LOG2E = math.log2(math.e)
LANES = 128
MAX_SLC_BLOCKS = LANES
SUPER_KEYS = LANES
SUPER_SHIFT = (SUPER_KEYS // L_SLC).bit_length() - 1
MAX_SUPER = MAX_SLC_BLOCKS * L_SLC // SUPER_KEYS
CHUNK_SUPERS = 4
FINAL_CHUNKS = 3
CMP_ROW_TILE = 128
VT_ROWS = HEAD_DIM + 16
FAR_POS = 1 << 30
POS_LANE = HEAD_DIM
N_SPLIT = 3
VMEM_LIMIT_BYTES = 56 * 1024 * 1024


def _rms(x, g):
    return x * lax.rsqrt(jnp.mean(x * x, axis=-1, keepdims=True) + RMS_EPS) * g


def _dot(a, b):
    return jnp.dot(a, b, preferred_element_type=F32)


def _dot_nt(a, b):
    return lax.dot_general(a, b, (((1,), (1,)), ((), ())), preferred_element_type=F32)


def _params(*sem, flags=None):
    return pltpu.CompilerParams(dimension_semantics=sem, vmem_limit_bytes=VMEM_LIMIT_BYTES, flags=flags)


def _pos_columns(coarse, fine, lane):
    is_c = (lane >= POS_LANE) & (lane < POS_LANE + N_SPLIT)
    is_f = (lane >= POS_LANE + N_SPLIT) & (lane < POS_LANE + 2 * N_SPLIT)
    return jnp.where(is_c, coarse, jnp.where(is_f, fine, 0.0))


def _mixer_a_kernel(x_ref, g_ref, win_ref, conv_ref, wout_ref, o_ref, ext_ref, *, tm, d):
    @pl.when(pl.program_id(1) == 0)
    def _():
        ext_ref[0:8, :] = jnp.zeros((8, d), F32)

    x = x_ref[0]
    xb = _rms(x, g_ref[...]).astype(BF16)
    b_gate = _dot(xb, win_ref[:, 0:d])
    cv = _dot(xb, win_ref[:, d:2 * d]) * _dot(xb, win_ref[:, 2 * d:3 * d])
    ext_ref[8:8 + tm, :] = cv
    cv1 = ext_ref[7:7 + tm, :]
    cv2 = ext_ref[6:6 + tm, :]
    u = conv_ref[0:1, :] * cv2 + conv_ref[1:2, :] * cv1 + conv_ref[2:3, :] * cv
    ext_ref[0:8, :] = cv[tm - 8:tm, :]
    o_ref[0] = x + _dot((b_gate * u).astype(BF16), wout_ref[...])


def _mixer_a(x, g, w_in, conv_w, w_out, tm=512):
    b, s, d = x.shape
    assert s % tm == 0
    return pl.pallas_call(
        functools.partial(_mixer_a_kernel, tm=tm, d=d),
        grid=(b, s // tm),
        in_specs=[
            pl.BlockSpec((1, tm, d), lambda i, j: (i, j, 0)),
            pl.BlockSpec((1, d), lambda i, j: (0, 0)),
            pl.BlockSpec((d, 3 * d), lambda i, j: (0, 0)),
            pl.BlockSpec((CONV_WIDTH, d), lambda i, j: (0, 0)),
            pl.BlockSpec((d, d), lambda i, j: (0, 0)),
        ],
        out_specs=pl.BlockSpec((1, tm, d), lambda i, j: (i, j, 0)),
        out_shape=jax.ShapeDtypeStruct((b, s, d), F32),
        scratch_shapes=[pltpu.VMEM((tm + 8, d), F32)],
        compiler_params=_params("arbitrary", "arbitrary"),
        name="mixer_a",
    )(x, g.reshape(1, d), w_in, conv_w, w_out)


def _ffn_kernel(*refs, with_proj, with_final, f, tf):
    if with_proj:
        h_ref, o_ref_in, wo_ref, g_ref, wgu_ref, wd_ref, fg_ref, out_ref = refs
    else:
        h_ref, g_ref, wgu_ref, wd_ref, fg_ref, out_ref = refs
    h = h_ref[...]
    if with_proj:
        h = h + _dot(o_ref_in[...], wo_ref[...])
    xb = _rms(h, g_ref[...]).astype(BF16)
    y = h
    for c in range(f // tf):
        gate = _dot(xb, wgu_ref[:, c * tf:(c + 1) * tf])
        up = _dot(xb, wgu_ref[:, f + c * tf:f + (c + 1) * tf])
        act = (gate * jax.nn.sigmoid(gate)) * up
        y = y + _dot(act.astype(BF16), wd_ref[c * tf:(c + 1) * tf, :])
    if with_final:
        y = _rms(y, fg_ref[...])
    out_ref[...] = y


def _resident(shape):
    return pl.BlockSpec(shape, lambda i: (0,) * len(shape), pipeline_mode=pl.Buffered(1))


def _ffn(h, g, w_gu, w_down, final_g, o=None, w_o=None, with_final=False, tm=512, tf=1408):
    t, d = h.shape
    f = w_down.shape[0]
    assert t % tm == 0 and f % tf == 0
    with_proj = o is not None
    in_specs = [pl.BlockSpec((tm, d), lambda i: (i, 0))]
    args = [h]
    if with_proj:
        in_specs += [pl.BlockSpec((tm, o.shape[1]), lambda i: (i, 0)), _resident(w_o.shape)]
        args += [o, w_o]
    in_specs += [_resident((1, d)), _resident(w_gu.shape), _resident(w_down.shape), _resident((1, d))]
    args += [g.reshape(1, d), w_gu, w_down, final_g.reshape(1, d)]
    return pl.pallas_call(
        functools.partial(_ffn_kernel, with_proj=with_proj, with_final=with_final, f=f, tf=tf),
        grid=(t // tm,),
        in_specs=in_specs,
        out_specs=pl.BlockSpec((tm, d), lambda i: (i, 0)),
        out_shape=jax.ShapeDtypeStruct((t, d), F32),
        compiler_params=_params("arbitrary"),
        name="ffn_proj" if with_proj else "ffn",
    )(*args)


def _halves(x):
    return x, pltpu.roll(x, HEAD_DIM, axis=1)


def _proj_kernel(h_ref, gkv_ref, gq_ref, wkv_ref, wqg_ref, qc_ref,
                 kvc_ref, kas_ref, vts_ref, kaw_ref, vtw_ref, q_ref, gate_ref, *, tm):
    g, dh = N_KV_GROUPS, HEAD_DIM
    gw = g * dh
    s0 = pl.program_id(1) * tm
    x = h_ref[0]
    y = x * lax.rsqrt(jnp.mean(x * x, axis=-1, keepdims=True) + RMS_EPS)
    kv = _dot((y * gkv_ref[...]).astype(BF16), wkv_ref[...])
    for st in range(2):
        for grp in range(g):
            kvc_ref[st, 0, grp] = kv[:, st * gw + grp * dh:st * gw + (grp + 1) * dh].astype(BF16)

    kpos = s0 + lax.broadcasted_iota(jnp.int32, (tm, LANES), 0)
    lane = lax.broadcasted_iota(jnp.int32, (tm, LANES), 1)
    low = lane < dh
    kblk = jnp.right_shift(kpos, L_SLC.bit_length() - 1)
    posc = _pos_columns(kblk.astype(F32), jnp.bitwise_and(kpos, L_SLC - 1).astype(F32), lane)
    onehot = (lane == kblk).astype(BF16)
    ones_rows = (lax.broadcasted_iota(jnp.int32, (VT_ROWS - dh, tm), 0) == 0).astype(BF16)
    for pair in range(g // 2):
        for st, k_ref, vt_ref in ((2, kas_ref, vts_ref), (4, kaw_ref, vtw_ref)):
            k_pair = kv[:, st * gw + pair * LANES:st * gw + (pair + 1) * LANES]
            vt = kv[:, (st + 1) * gw + pair * LANES:(st + 1) * gw + (pair + 1) * LANES].T
            for half, kh in enumerate(_halves(k_pair)):
                grp = 2 * pair + half
                k_ref[0, grp, :, 0:LANES] = jnp.where(low, kh, posc).astype(BF16)
                if st == 2:
                    k_ref[0, grp, :, LANES:2 * LANES] = onehot
                vt_ref[0, grp, 0:dh] = vt[half * dh:(half + 1) * dh].astype(BF16)
                vt_ref[0, grp, dh:VT_ROWS] = ones_rows

    qg = _dot((y * gq_ref[...]).astype(BF16), wqg_ref[...])
    for pair in range(N_HEADS // 2):
        q_pair = qg[:, pair * LANES:(pair + 1) * LANES] * (LOG2E * dh ** -0.5)
        for half, qh in enumerate(_halves(q_pair)):
            hd = 2 * pair + half
            q_ref[0, hd] = jnp.where(low, qh, qc_ref[hd:hd + 1, :]).astype(BF16)
    nq = N_HEADS * dh
    for grp in range(g):
        gate_ref[0, grp] = jax.nn.sigmoid(qg[:, nq + grp * LANES: nq + (grp + 1) * LANES])


def _proj(h, g_kv, g_q, w_kv, w_qg, qconst, tm=512):
    b, s, d = h.shape
    assert s % tm == 0 and s // L_SLC <= MAX_SLC_BLOCKS
    g, dh = N_KV_GROUPS, HEAD_DIM
    return pl.pallas_call(
        functools.partial(_proj_kernel, tm=tm),
        grid=(b, s // tm),
        in_specs=[
            pl.BlockSpec((1, tm, d), lambda i, j: (i, j, 0)),
            pl.BlockSpec((1, d), lambda i, j: (0, 0)),
            pl.BlockSpec((1, d), lambda i, j: (0, 0)),
            pl.BlockSpec(w_kv.shape, lambda i, j: (0, 0)),
            pl.BlockSpec(w_qg.shape, lambda i, j: (0, 0)),
            pl.BlockSpec(qconst.shape, lambda i, j: (0, 0)),
        ],
        out_specs=[
            pl.BlockSpec((2, 1, g, tm, dh), lambda i, j: (0, i, 0, j, 0)),
            pl.BlockSpec((1, g, tm, 2 * LANES), lambda i, j: (i, 0, j, 0)),
            pl.BlockSpec((1, g, VT_ROWS, tm), lambda i, j: (i, 0, 0, j)),
            pl.BlockSpec((1, g, tm, LANES), lambda i, j: (i, 0, j, 0)),
            pl.BlockSpec((1, g, VT_ROWS, tm), lambda i, j: (i, 0, 0, j)),
            pl.BlockSpec((1, N_HEADS, tm, LANES), lambda i, j: (i, 0, j, 0)),
            pl.BlockSpec((1, g, tm, LANES), lambda i, j: (i, 0, j, 0)),
        ],
        out_shape=[
            jax.ShapeDtypeStruct((2, b, g, s, dh), BF16),
            jax.ShapeDtypeStruct((b, g, s, 2 * LANES), BF16),
            jax.ShapeDtypeStruct((b, g, VT_ROWS, s), BF16),
            jax.ShapeDtypeStruct((b, g, s, LANES), BF16),
            jax.ShapeDtypeStruct((b, g, VT_ROWS, s), BF16),
            jax.ShapeDtypeStruct((b, N_HEADS, s, LANES), BF16),
            jax.ShapeDtypeStruct((b, g, s, LANES), F32),
        ],
        compiler_params=_params("arbitrary", "arbitrary"),
        name="proj",
    )(h, g_kv.reshape(1, d), g_q.reshape(1, d), w_kv, w_qg, qconst)


def _compress_kernel(r_ref, pe_ref, w1_ref, w2_ref, kca_ref, vct_ref, *, m):
    half = D_CMP * HEAD_DIM
    row = lax.broadcasted_iota(jnp.int32, (m, LANES), 0)
    lane = lax.broadcasted_iota(jnp.int32, (m, LANES), 1)

    def mlp(i):
        r = r_ref[i, 0]
        top = _dot(r, w1_ref[i, 0:half, :])
        bot = _dot(r, w1_ref[i, half:2 * half, :])
        pe_term = _dot(pe_ref[i], w1_ref[i])[0:1, :]
        hid = top + pltpu.roll(bot, m - 1, axis=0) + pe_term
        hid = hid * jax.nn.sigmoid(hid)
        out = _dot(hid.astype(BF16), w2_ref[i])
        return jnp.where(row < m - 1, out, 0.0)

    coarse = jnp.right_shift(row, 2).astype(F32)
    fine = jnp.bitwise_and(row, 3).astype(F32) * D_CMP + (L_CMP - 1) / 2.0
    kca_ref[0] = (mlp(0) + _pos_columns(coarse, fine, lane)).astype(BF16)
    vct_ref[0] = mlp(1).T[0:HEAD_DIM].astype(BF16)


def _compress(kvc, pe2, w1_2, w2_2):
    _, b, g, s, dh = kvc.shape
    m = s // D_CMP
    r = kvc.reshape(2, b * g, m, D_CMP * dh)
    hid = w1_2.shape[-1]
    return pl.pallas_call(
        functools.partial(_compress_kernel, m=m),
        grid=(b * g,),
        in_specs=[
            pl.BlockSpec((2, 1, m, D_CMP * dh), lambda j: (0, j, 0, 0)),
            pl.BlockSpec((2, 8, L_CMP * dh), lambda j: (0, 0, 0)),
            pl.BlockSpec((2, L_CMP * dh, hid), lambda j: (0, 0, 0)),
            pl.BlockSpec((2, hid, LANES), lambda j: (0, 0, 0)),
        ],
        out_specs=[
            pl.BlockSpec((1, m, LANES), lambda j: (j, 0, 0)),
            pl.BlockSpec((1, dh, m), lambda j: (j, 0, 0)),
        ],
        out_shape=[
            jax.ShapeDtypeStruct((b * g, m, LANES), BF16),
            jax.ShapeDtypeStruct((b * g, dh, m), BF16),
        ],
        compiler_params=_params("arbitrary"),
        name="compress",
    )(r, pe2, w1_2, w2_2)


def _head_stack(ot, gt, branch, tq):
    return jnp.concatenate(
        [ot[:, h * tq:(h + 1) * tq] * gt[3 * h + branch:3 * h + branch + 1, :] for h in range(HEADS_PER_GROUP)],
        axis=0)


def _nsa_cmp_kernel(q_ref, kca_ref, vct_ref, gate_ref, oc_ref, negm_ref, used_ref, *, tq, m):
    q0 = pl.program_id(2) * tq
    hpg = HEADS_PER_GROUP
    ncol = hpg * tq
    q = q_ref[0].reshape(ncol, LANES)
    tcol = q0 + jnp.bitwise_and(lax.broadcasted_iota(jnp.int32, (1, ncol), 1), tq - 1)

    def attend(rows):
        st = _dot_nt(kca_ref[0, 0:rows, :], q)
        nrow = lax.broadcasted_iota(jnp.int32, (rows, 1), 0)
        vis = (nrow * D_CMP + (L_CMP - 1)) <= tcol
        st = jnp.where(vis, st, NEG_INF)
        e = jnp.exp2(st - jnp.max(st, axis=0, keepdims=True))
        rcp = jnp.where(tcol >= L_CMP - 1, 1.0 / jnp.sum(e, axis=0, keepdims=True), 0.0)
        p = e * rcp
        ot = _dot(vct_ref[0, :, 0:rows], p.astype(BF16))
        pt = p[:, 0:tq]
        for h in range(1, hpg):
            pt = pt + p[:, h * tq:(h + 1) * tq]
        jj = lax.broadcasted_iota(jnp.int32, (MAX_SLC_BLOCKS, rows), 0) * L_SLC
        nn = lax.broadcasted_iota(jnp.int32, (MAX_SLC_BLOCKS, rows), 1) * D_CMP
        overlap_t = ((nn < jj + L_SLC) & (nn + L_CMP > jj)).astype(BF16)
        p_hi = pt.astype(BF16)
        p_lo = (pt - p_hi.astype(F32)).astype(BF16)
        return ot, _dot(overlap_t, p_hi) + _dot(overlap_t, p_lo)

    row_tile = min(m, CMP_ROW_TILE)
    n_variants = m // row_tile
    if n_variants == 1:
        ot, imp = attend(m)
    else:
        needed = pl.program_id(2) * (tq // D_CMP) + (tq // D_CMP - 1)
        variant = jnp.minimum(jnp.right_shift(needed - 1, row_tile.bit_length() - 1), n_variants - 1)
        ot, imp = lax.switch(variant, [functools.partial(attend, (v + 1) * row_tile) for v in range(n_variants)])
    gt = gate_ref[0, 0].T
    oc_ref[0] = _head_stack(ot, gt, 0, tq).T

    jrow = lax.broadcasted_iota(jnp.int32, (MAX_SLC_BLOCKS, tq), 0)
    cur = jnp.right_shift(q0 + lax.broadcasted_iota(jnp.int32, (1, tq), 1), L_SLC.bit_length() - 1)
    valid = jrow <= cur
    forced = (jrow == 0) | (jrow == cur) | (jrow == cur - 1)
    val = jnp.where(valid & jnp.logical_not(forced), imp, NEG_INF)
    jrow_f = jrow.astype(F32)
    for _ in range(N_SEL - N_FORCED):
        mx = jnp.max(val, axis=0, keepdims=True)
        first = jnp.min(jnp.where(val == mx, jrow_f, float(MAX_SLC_BLOCKS)), axis=0, keepdims=True)
        val = jnp.where(jrow_f == first, PICKED, val)
    sel_t = jnp.where(valid & (forced | (val == PICKED)), 1.0, 0.0).T
    negm_ref[0, 0] = jnp.where(sel_t > 0.5, 0.0, NEG_INF).astype(BF16)
    for r in range(tq // SUPER_KEYS):
        used_ref[r] = jnp.max(sel_t[r * SUPER_KEYS:(r + 1) * SUPER_KEYS], axis=0, keepdims=True)


def _nsa_cmp(q, kca, vct, gates, tq=512):
    b, nh, s, _ = q.shape
    g, hpg, dh = N_KV_GROUPS, HEADS_PER_GROUP, HEAD_DIM
    m = kca.shape[1]
    assert s % tq == 0 and tq % SUPER_KEYS == 0 and tq & (tq - 1) == 0
    nqt = s // tq
    sub = tq // SUPER_KEYS
    return pl.pallas_call(
        functools.partial(_nsa_cmp_kernel, tq=tq, m=m),
        grid=(b, g, s // tq),
        in_specs=[
            pl.BlockSpec((1, hpg, tq, LANES), lambda i, j, k: (i, j, k, 0)),
            pl.BlockSpec((1, m, LANES), lambda i, j, k: (i * N_KV_GROUPS + j, 0, 0)),
            pl.BlockSpec((1, dh, m), lambda i, j, k: (i * N_KV_GROUPS + j, 0, 0)),
            pl.BlockSpec((1, 1, tq, LANES), lambda i, j, k: (i, j, k, 0)),
        ],
        out_specs=[
            pl.BlockSpec((1, tq, hpg * dh), lambda i, j, k: (i, k, j)),
            pl.BlockSpec((1, 1, tq, MAX_SLC_BLOCKS), lambda i, j, k: (i, j, k, 0)),
            pl.BlockSpec((sub, 1, MAX_SLC_BLOCKS), lambda i, j, k: ((i * N_KV_GROUPS + j) * nqt + k, 0, 0)),
        ],
        out_shape=[
            jax.ShapeDtypeStruct((b, s, nh * dh), F32),
            jax.ShapeDtypeStruct((b, g, s, MAX_SLC_BLOCKS), BF16),
            jax.ShapeDtypeStruct((b * g * nqt * sub, 1, MAX_SLC_BLOCKS), F32),
        ],
        compiler_params=_params("arbitrary", "arbitrary", "arbitrary"),
        name="nsa_cmp",
    )(q, kca, vct, gates)


def _super_lists_kernel(used_ref, lst_ref, cnt_ref, *, tb):
    ut = used_ref[...].T.astype(BF16)
    a_i = lax.broadcasted_iota(jnp.int32, (MAX_SLC_BLOCKS, MAX_SLC_BLOCKS), 0)
    b_i = lax.broadcasted_iota(jnp.int32, (MAX_SLC_BLOCKS, MAX_SLC_BLOCKS), 1)
    pair_t = (a_i == jnp.right_shift(b_i, SUPER_SHIFT)).astype(BF16)
    sbf = jnp.where(_dot(pair_t, ut) > 0.5, 1.0, 0.0).astype(BF16)
    incl = _dot((b_i <= a_i).astype(BF16), sbf)[0:MAX_SUPER]
    cnt_ref[...] = jnp.broadcast_to(incl[MAX_SUPER - 1:MAX_SUPER, :], (8, tb)).astype(jnp.int32)
    for i in range(MAX_SUPER):
        lst_ref[i:i + 1, :] = jnp.sum(jnp.where(incl <= float(i), 1.0, 0.0), axis=0, keepdims=True).astype(jnp.int32)


def _super_lists(used):
    t = used.shape[0]
    tb = min(t, LANES)
    assert t % tb == 0
    return pl.pallas_call(
        functools.partial(_super_lists_kernel, tb=tb),
        grid=(t // tb,),
        in_specs=[pl.BlockSpec((tb, MAX_SLC_BLOCKS), lambda i: (i, 0))],
        out_specs=[
            pl.BlockSpec((MAX_SUPER, tb), lambda i: (0, i)),
            pl.BlockSpec((8, tb), lambda i: (0, i)),
        ],
        out_shape=[
            jax.ShapeDtypeStruct((MAX_SUPER, t), jnp.int32),
            jax.ShapeDtypeStruct((8, t), jnp.int32),
        ],
        compiler_params=_params("arbitrary"),
        name="super_lists",
    )(used)


def _nsa_slc_kernel(lst_ref, cnt_ref, q_ref, kas_ref, vts_ref, kaw_ref, vtw_ref, negm_ref, gate_ref, oc_ref, o_ref,
                    m_ref, acc_ref, *, tq):
    q0 = pl.program_id(2) * tq
    hpg = HEADS_PER_GROUP
    ncol = hpg * tq
    tk = CHUNK_SUPERS * SUPER_KEYS
    q = q_ref[0].reshape(ncol, LANES)
    negm = negm_ref[0, 0]
    qa = jnp.concatenate([q, jnp.concatenate([negm] * hpg, axis=0)], axis=1)
    tcol = q0 + jnp.bitwise_and(lax.broadcasted_iota(jnp.int32, (1, ncol), 1), tq - 1)

    m_ref[...] = jnp.full_like(m_ref, NEG_INF)
    acc_ref[...] = jnp.zeros_like(acc_ref)

    n_steps = pl.num_programs(0) * pl.num_programs(1) * pl.num_programs(2)
    step = (pl.program_id(0) * pl.num_programs(1) + pl.program_id(1)) * pl.num_programs(2) + pl.program_id(2)
    n_used = cnt_ref[step]
    n_chunks = jnp.right_shift(n_used + CHUNK_SUPERS - 1, CHUNK_SUPERS.bit_length() - 1)

    def gather(chunk, partial):
        ks, vs, kbase = [], [], []
        for j in range(CHUNK_SUPERS):
            slot = chunk * CHUNK_SUPERS + j
            if partial:
                ok = (slot >= 0) & (slot < n_used)
                sb = jnp.where(ok, lst_ref[jnp.clip(slot, 0, MAX_SUPER - 1) * n_steps + step], 0)
                kbase.append(jnp.where(ok, sb * SUPER_KEYS, FAR_POS))
            else:
                sb = lst_ref[slot * n_steps + step]
            start = pl.multiple_of(sb * SUPER_KEYS, SUPER_KEYS)
            ks.append(kas_ref[0, 0, pl.ds(start, SUPER_KEYS), :])
            vs.append(vts_ref[0, 0, :, pl.ds(start, SUPER_KEYS)])
        return jnp.concatenate(ks, axis=0), jnp.concatenate(vs, axis=1), kbase

    def scores(kt, kbase):
        st = _dot_nt(kt, qa)
        if kbase:
            r = lax.broadcasted_iota(jnp.int32, (tk, 1), 0)
            rs = jnp.right_shift(r, SUPER_KEYS.bit_length() - 1)
            kpos = jnp.full((tk, 1), kbase[0], jnp.int32)
            for j in range(1, CHUNK_SUPERS):
                kpos = jnp.where(rs == j, kbase[j], kpos)
            kpos = kpos + jnp.bitwise_and(r, SUPER_KEYS - 1)
            st = jnp.where(kpos <= tcol, st, NEG_INF)
        return st

    def update(st, vt, present=None):
        m_old = m_ref[...]
        m_new = jnp.maximum(m_old, jnp.max(st, axis=0, keepdims=True))
        if present is not None:
            m_new = jnp.where(present, m_new, m_old)
        p = jnp.exp2(st - m_new)
        acc_new = jnp.exp2(m_old - m_new) * acc_ref[...] + _dot(vt, p.astype(BF16))
        acc_ref[...] = acc_new if present is None else jnp.where(present, acc_new, acc_ref[...])
        m_ref[...] = m_new

    def body(c, carry):
        kt, vt, _ = gather(c, False)
        update(scores(kt, None), vt)
        return carry

    lax.fori_loop(0, jnp.maximum(n_chunks - FINAL_CHUNKS, 0), body, 0)

    finals = []
    for back in range(1, FINAL_CHUNKS + 1):
        kt, vt, kbase = gather(n_chunks - back, True)
        finals.append((scores(kt, kbase if back == 1 else None), vt, None if back == 1 else n_chunks >= back))

    n_win = WINDOW // SUPER_KEYS + 1
    qt = pl.program_id(2)
    row = lax.broadcasted_iota(jnp.int32, (SUPER_KEYS, 1), 0)
    win_scores, win_v = [], []
    for r in range(n_win):
        sb = qt - (n_win - 1) + r
        start = pl.multiple_of(jnp.maximum(sb, 0) * SUPER_KEYS, SUPER_KEYS)
        sw = _dot_nt(kaw_ref[0, 0, pl.ds(start, SUPER_KEYS), :], q)
        if r == n_win - 1:
            sw = jnp.where(sb * SUPER_KEYS + row <= tcol, sw, NEG_INF)
        elif r == 0:
            sw = jnp.where(jnp.where(sb >= 0, sb * SUPER_KEYS + WINDOW, -FAR_POS) + row > tcol, sw, NEG_INF)
        else:
            sw = sw + jnp.where(sb >= 0, 0.0, NEG_INF)
        win_scores.append(sw)
        win_v.append(vtw_ref[0, 0, :, pl.ds(start, SUPER_KEYS)])

    for st, vt, present in finals:
        update(st, vt, present)
    acc = acc_ref[...]
    o_slc_t = acc[0:HEAD_DIM] * (1.0 / acc[HEAD_DIM:HEAD_DIM + 1])

    stw = jnp.concatenate(win_scores, axis=0)
    e = jnp.exp2(stw - jnp.max(stw, axis=0, keepdims=True))
    acc_w = _dot(jnp.concatenate(win_v, axis=1), e.astype(BF16))
    o_win_t = acc_w[0:HEAD_DIM] * (1.0 / acc_w[HEAD_DIM:HEAD_DIM + 1])

    gt = gate_ref[0, 0].T
    comb = _head_stack(o_slc_t, gt, 1, tq) + _head_stack(o_win_t, gt, 2, tq)
    o_ref[0] = (oc_ref[0] + comb.T).astype(BF16)


def _nsa_slc(lists, counts, q, kas, vts, kaw, vtw, negm, gates, oc, tq=128):
    b, nh, s, _ = q.shape
    g, hpg, dh = N_KV_GROUPS, HEADS_PER_GROUP, HEAD_DIM
    assert tq == SUPER_KEYS and WINDOW % tq == 0 and s >= WINDOW + tq and s % (CHUNK_SUPERS * SUPER_KEYS) == 0
    grid_spec = pltpu.PrefetchScalarGridSpec(
        num_scalar_prefetch=2,
        grid=(b, g, s // tq),
        in_specs=[
            pl.BlockSpec((1, hpg, tq, LANES), lambda i, j, k, *_: (i, j, k, 0)),
            pl.BlockSpec((1, 1, s, 2 * LANES), lambda i, j, k, *_: (i, j, 0, 0)),
            pl.BlockSpec((1, 1, VT_ROWS, s), lambda i, j, k, *_: (i, j, 0, 0)),
            pl.BlockSpec((1, 1, s, LANES), lambda i, j, k, *_: (i, j, 0, 0)),
            pl.BlockSpec((1, 1, VT_ROWS, s), lambda i, j, k, *_: (i, j, 0, 0)),
            pl.BlockSpec((1, 1, tq, MAX_SLC_BLOCKS), lambda i, j, k, *_: (i, j, k, 0)),
            pl.BlockSpec((1, 1, tq, LANES), lambda i, j, k, *_: (i, j, k, 0)),
            pl.BlockSpec((1, tq, hpg * dh), lambda i, j, k, *_: (i, k, j)),
        ],
        out_specs=pl.BlockSpec((1, tq, hpg * dh), lambda i, j, k, *_: (i, k, j)),
        scratch_shapes=[
            pltpu.VMEM((1, hpg * tq), F32),
            pltpu.VMEM((VT_ROWS, hpg * tq), F32),
        ],
    )
    return pl.pallas_call(
        functools.partial(_nsa_slc_kernel, tq=tq),
        grid_spec=grid_spec,
        out_shape=jax.ShapeDtypeStruct((b, s, nh * dh), BF16),
        compiler_params=_params("arbitrary", "arbitrary", "arbitrary"),
        name="nsa_slc",
    )(lists, counts, q, kas, vts, kaw, vtw, negm, gates, oc)


def _split_bf16(x, n):
    pieces = []
    for _ in range(n):
        p = x.astype(BF16).astype(F32)
        pieces.append(p)
        x = x - p
    return pieces


def _q_constants():
    slopes = jnp.exp2(-8.0 * jnp.arange(1, N_HEADS + 1, dtype=F32) / N_HEADS) * LOG2E
    pieces = jnp.stack(_split_bf16(slopes, N_SPLIT), axis=1)
    qc = jnp.zeros((N_HEADS, LANES), F32)
    qc = qc.at[:, POS_LANE:POS_LANE + N_SPLIT].set(pieces * L_SLC)
    qc = qc.at[:, POS_LANE + N_SPLIT:POS_LANE + 2 * N_SPLIT].set(pieces)
    return qc


def _qg_weight(w_qg):
    d = w_qg.shape[0]
    nq = N_HEADS * HEAD_DIM
    wg = jnp.pad(w_qg[:, nq:].reshape(d, N_KV_GROUPS, 3 * HEADS_PER_GROUP),
                 ((0, 0), (0, 0), (0, LANES - 3 * HEADS_PER_GROUP)))
    return jnp.concatenate([w_qg[:, :nq], wg.reshape(d, N_KV_GROUPS * LANES)], axis=1).astype(BF16)


def kernel(x, a_norm, a_w_in, a_conv, a_w_out, kv_norm, w_kv, cmp_pe_k, cmp_w1_k, cmp_w2_k, cmp_pe_v, cmp_w1_v,
           cmp_w2_v, b_norm, b_w_qg, b_w_o, f_norm, f_w_gu, f_w_down, final_norm):
    b, s, d = x.shape
    nq = N_HEADS * HEAD_DIM
    h = _mixer_a(x, a_norm[0], a_w_in[0].astype(BF16), a_conv[0], a_w_out[0].astype(BF16))
    h = _ffn(h.reshape(b * s, d), f_norm[0], f_w_gu[0].astype(BF16), f_w_down[0].astype(BF16), final_norm)
    h = h.reshape(b, s, d)

    kvc, kas, vts, kaw, vtw, q, gates = _proj(h, kv_norm, b_norm[0], w_kv.astype(BF16), _qg_weight(b_w_qg[0]),
                                              _q_constants())
    pe2 = jnp.stack([cmp_pe_k, cmp_pe_v]).reshape(2, 1, L_CMP * HEAD_DIM)
    pe2 = jnp.broadcast_to(pe2, (2, 8, L_CMP * HEAD_DIM)).astype(BF16)
    w2 = jnp.pad(jnp.stack([cmp_w2_k, cmp_w2_v]), ((0, 0), (0, 0), (0, LANES - HEAD_DIM))).astype(BF16)
    kca, vct = _compress(kvc, pe2, jnp.stack([cmp_w1_k, cmp_w1_v]).astype(BF16), w2)

    oc, negm, used = _nsa_cmp(q, kca, vct, gates)
    lists, counts = _super_lists(used.reshape(used.shape[0], MAX_SLC_BLOCKS))
    o = _nsa_slc(lists.reshape(-1), counts[0], q, kas, vts, kaw, vtw, negm, gates, oc)

    out = _ffn(h.reshape(b * s, d), f_norm[1], f_w_gu[1].astype(BF16), f_w_down[1].astype(BF16), final_norm,
               o=o.reshape(b * s, nq), w_o=b_w_o[0].astype(BF16), with_final=True)
    return out.reshape(b, s, d)
```

```python
import functools
import math

import jax
import jax.numpy as jnp
from jax import lax
from jax.experimental import pallas as pl
from jax.experimental.pallas import tpu as pltpu

F32 = jnp.float32
BF16 = jnp.bfloat16

N_HEADS = 16
N_KV_GROUPS = 4
HEADS_PER_GROUP = N_HEADS // N_KV_GROUPS
HEAD_DIM = 64
L_CMP = 32
D_CMP = 16
L_SLC = 64
N_SEL = 16
N_FORCED = 3
WINDOW = 512
N_KV_SETS = 6
CONV_WIDTH = 3
RMS_EPS = 1e-5
NEG_INF = -1e30
PICKED = -3.0e38
LOG2E = math.log2(math.e)
LANES = 128
MAX_SLC_BLOCKS = LANES
SUPER_KEYS = LANES
SUPER_SHIFT = (SUPER_KEYS // L_SLC).bit_length() - 1
MAX_SUPER = MAX_SLC_BLOCKS * L_SLC // SUPER_KEYS
CHUNK_SUPERS = 4
FINAL_CHUNKS = 3
CMP_ROW_TILE = 128
VT_ROWS = HEAD_DIM + 16
FAR_POS = 1 << 30
POS_LANE = HEAD_DIM
N_SPLIT = 3
VMEM_LIMIT_BYTES = 56 * 1024 * 1024


def _rms(x, g):
    return x * lax.rsqrt(jnp.mean(x * x, axis=-1, keepdims=True) + RMS_EPS) * g


def _dot(a, b):
    return jnp.dot(a, b, preferred_element_type=F32)


def _dot_nt(a, b):
    return lax.dot_general(a, b, (((1,), (1,)), ((), ())), preferred_element_type=F32)


def _params(*sem, flags=None):
    return pltpu.CompilerParams(dimension_semantics=sem, vmem_limit_bytes=VMEM_LIMIT_BYTES, flags=flags)


def _pos_columns(coarse, fine, lane):
    is_c = (lane >= POS_LANE) & (lane < POS_LANE + N_SPLIT)
    is_f = (lane >= POS_LANE + N_SPLIT) & (lane < POS_LANE + 2 * N_SPLIT)
    return jnp.where(is_c, coarse, jnp.where(is_f, fine, 0.0))


def _mixer_a_kernel(x_ref, g_ref, win_ref, conv_ref, wout_ref, o_ref, ext_ref, *, tm, d):
    @pl.when(pl.program_id(1) == 0)
    def _():
        ext_ref[0:8, :] = jnp.zeros((8, d), F32)

    x = x_ref[0]
    xb = _rms(x, g_ref[...]).astype(BF16)
    b_gate = _dot(xb, win_ref[:, 0:d])
    cv = _dot(xb, win_ref[:, d:2 * d]) * _dot(xb, win_ref[:, 2 * d:3 * d])
    ext_ref[8:8 + tm, :] = cv
    cv1 = ext_ref[7:7 + tm, :]
    cv2 = ext_ref[6:6 + tm, :]
    u = conv_ref[0:1, :] * cv2 + conv_ref[1:2, :] * cv1 + conv_ref[2:3, :] * cv
    ext_ref[0:8, :] = cv[tm - 8:tm, :]
    o_ref[0] = x + _dot((b_gate * u).astype(BF16), wout_ref[...])


def _mixer_a(x, g, w_in, conv_w, w_out, tm=512):
    b, s, d = x.shape
    assert s % tm == 0
    return pl.pallas_call(
        functools.partial(_mixer_a_kernel, tm=tm, d=d),
        grid=(b, s // tm),
        in_specs=[
            pl.BlockSpec((1, tm, d), lambda i, j: (i, j, 0)),
            pl.BlockSpec((1, d), lambda i, j: (0, 0)),
            pl.BlockSpec((d, 3 * d), lambda i, j: (0, 0)),
            pl.BlockSpec((CONV_WIDTH, d), lambda i, j: (0, 0)),
            pl.BlockSpec((d, d), lambda i, j: (0, 0)),
        ],
        out_specs=pl.BlockSpec((1, tm, d), lambda i, j: (i, j, 0)),
        out_shape=jax.ShapeDtypeStruct((b, s, d), F32),
        scratch_shapes=[pltpu.VMEM((tm + 8, d), F32)],
        compiler_params=_params("arbitrary", "arbitrary"),
        name="mixer_a",
    )(x, g.reshape(1, d), w_in, conv_w, w_out)


def _ffn_kernel(*refs, with_proj, with_final, f, tf):
    if with_proj:
        h_ref, o_ref_in, wo_ref, g_ref, wgu_ref, wd_ref, fg_ref, out_ref = refs
    else:
        h_ref, g_ref, wgu_ref, wd_ref, fg_ref, out_ref = refs
    h = h_ref[...]
    if with_proj:
        h = h + _dot(o_ref_in[...], wo_ref[...])
    xb = _rms(h, g_ref[...]).astype(BF16)
    y = h
    for c in range(f // tf):
        gate = _dot(xb, wgu_ref[:, c * tf:(c + 1) * tf])
        up = _dot(xb, wgu_ref[:, f + c * tf:f + (c + 1) * tf])
        act = (gate * jax.nn.sigmoid(gate)) * up
        y = y + _dot(act.astype(BF16), wd_ref[c * tf:(c + 1) * tf, :])
    if with_final:
        y = _rms(y, fg_ref[...])
    out_ref[...] = y


def _resident(shape):
    return pl.BlockSpec(shape, lambda i: (0,) * len(shape), pipeline_mode=pl.Buffered(1))


def _ffn(h, g, w_gu, w_down, final_g, o=None, w_o=None, with_final=False, tm=512, tf=1408):
    t, d = h.shape
    f = w_down.shape[0]
    assert t % tm == 0 and f % tf == 0
    with_proj = o is not None
    in_specs = [pl.BlockSpec((tm, d), lambda i: (i, 0))]
    args = [h]
    if with_proj:
        in_specs += [pl.BlockSpec((tm, o.shape[1]), lambda i: (i, 0)), _resident(w_o.shape)]
        args += [o, w_o]
    in_specs += [_resident((1, d)), _resident(w_gu.shape), _resident(w_down.shape), _resident((1, d))]
    args += [g.reshape(1, d), w_gu, w_down, final_g.reshape(1, d)]
    return pl.pallas_call(
        functools.partial(_ffn_kernel, with_proj=with_proj, with_final=with_final, f=f, tf=tf),
        grid=(t // tm,),
        in_specs=in_specs,
        out_specs=pl.BlockSpec((tm, d), lambda i: (i, 0)),
        out_shape=jax.ShapeDtypeStruct((t, d), F32),
        compiler_params=_params("arbitrary"),
        name="ffn_proj" if with_proj else "ffn",
    )(*args)


def _halves(x):
    return x, pltpu.roll(x, HEAD_DIM, axis=1)


def _proj_kernel(h_ref, gkv_ref, gq_ref, wkv_ref, wqg_ref, qc_ref,
                 kvc_ref, kas_ref, vts_ref, kaw_ref, vtw_ref, q_ref, gate_ref, *, tm):
    g, dh = N_KV_GROUPS, HEAD_DIM
    gw = g * dh
    s0 = pl.program_id(1) * tm
    x = h_ref[0]
    y = x * lax.rsqrt(jnp.mean(x * x, axis=-1, keepdims=True) + RMS_EPS)
    kv = _dot((y * gkv_ref[...]).astype(BF16), wkv_ref[...])
    for st in range(2):
        for grp in range(g):
            kvc_ref[st, 0, grp] = kv[:, st * gw + grp * dh:st * gw + (grp + 1) * dh].astype(BF16)

    kpos = s0 + lax.broadcasted_iota(jnp.int32, (tm, LANES), 0)
    lane = lax.broadcasted_iota(jnp.int32, (tm, LANES), 1)
    low = lane < dh
    kblk = jnp.right_shift(kpos, L_SLC.bit_length() - 1)
    posc = _pos_columns(kblk.astype(F32), jnp.bitwise_and(kpos, L_SLC - 1).astype(F32), lane)
    onehot = (lane == kblk).astype(BF16)
    ones_rows = (lax.broadcasted_iota(jnp.int32, (VT_ROWS - dh, tm), 0) == 0).astype(BF16)
    for pair in range(g // 2):
        for st, k_ref, vt_ref in ((2, kas_ref, vts_ref), (4, kaw_ref, vtw_ref)):
            k_pair = kv[:, st * gw + pair * LANES:st * gw + (pair + 1) * LANES]
            vt = kv[:, (st + 1) * gw + pair * LANES:(st + 1) * gw + (pair + 1) * LANES].T
            for half, kh in enumerate(_halves(k_pair)):
                grp = 2 * pair + half
                k_ref[0, grp, :, 0:LANES] = jnp.where(low, kh, posc).astype(BF16)
                if st == 2:
                    k_ref[0, grp, :, LANES:2 * LANES] = onehot
                vt_ref[0, grp, 0:dh] = vt[half * dh:(half + 1) * dh].astype(BF16)
                vt_ref[0, grp, dh:VT_ROWS] = ones_rows

    qg = _dot((y * gq_ref[...]).astype(BF16), wqg_ref[...])
    for pair in range(N_HEADS // 2):
        q_pair = qg[:, pair * LANES:(pair + 1) * LANES] * (LOG2E * dh ** -0.5)
        for half, qh in enumerate(_halves(q_pair)):
            hd = 2 * pair + half
            q_ref[0, hd] = jnp.where(low, qh, qc_ref[hd:hd + 1, :]).astype(BF16)
    nq = N_HEADS * dh
    for grp in range(g):
        gate_ref[0, grp] = jax.nn.sigmoid(qg[:, nq + grp * LANES: nq + (grp + 1) * LANES])


def _proj(h, g_kv, g_q, w_kv, w_qg, qconst, tm=512):
    b, s, d = h.shape
    assert s % tm == 0 and s // L_SLC <= MAX_SLC_BLOCKS
    g, dh = N_KV_GROUPS, HEAD_DIM
    return pl.pallas_call(
        functools.partial(_proj_kernel, tm=tm),
        grid=(b, s // tm),
        in_specs=[
            pl.BlockSpec((1, tm, d), lambda i, j: (i, j, 0)),
            pl.BlockSpec((1, d), lambda i, j: (0, 0)),
            pl.BlockSpec((1, d), lambda i, j: (0, 0)),
            pl.BlockSpec(w_kv.shape, lambda i, j: (0, 0)),
            pl.BlockSpec(w_qg.shape, lambda i, j: (0, 0)),
            pl.BlockSpec(qconst.shape, lambda i, j: (0, 0)),
        ],
        out_specs=[
            pl.BlockSpec((2, 1, g, tm, dh), lambda i, j: (0, i, 0, j, 0)),
            pl.BlockSpec((1, g, tm, 2 * LANES), lambda i, j: (i, 0, j, 0)),
            pl.BlockSpec((1, g, VT_ROWS, tm), lambda i, j: (i, 0, 0, j)),
            pl.BlockSpec((1, g, tm, LANES), lambda i, j: (i, 0, j, 0)),
            pl.BlockSpec((1, g, VT_ROWS, tm), lambda i, j: (i, 0, 0, j)),
            pl.BlockSpec((1, N_HEADS, tm, LANES), lambda i, j: (i, 0, j, 0)),
            pl.BlockSpec((1, g, tm, LANES), lambda i, j: (i, 0, j, 0)),
        ],
        out_shape=[
            jax.ShapeDtypeStruct((2, b, g, s, dh), BF16),
            jax.ShapeDtypeStruct((b, g, s, 2 * LANES), BF16),
            jax.ShapeDtypeStruct((b, g, VT_ROWS, s), BF16),
            jax.ShapeDtypeStruct((b, g, s, LANES), BF16),
            jax.ShapeDtypeStruct((b, g, VT_ROWS, s), BF16),
            jax.ShapeDtypeStruct((b, N_HEADS, s, LANES), BF16),
            jax.ShapeDtypeStruct((b, g, s, LANES), F32),
        ],
        compiler_params=_params("arbitrary", "arbitrary"),
        name="proj",
    )(h, g_kv.reshape(1, d), g_q.reshape(1, d), w_kv, w_qg, qconst)


def _compress_kernel(r_ref, pe_ref, w1_ref, w2_ref, kca_ref, vct_ref, *, m):
    half = D_CMP * HEAD_DIM
    row = lax.broadcasted_iota(jnp.int32, (m, LANES), 0)
    lane = lax.broadcasted_iota(jnp.int32, (m, LANES), 1)

    def mlp(i):
        r = r_ref[i, 0]
        top = _dot(r, w1_ref[i, 0:half, :])
        bot = _dot(r, w1_ref[i, half:2 * half, :])
        pe_term = _dot(pe_ref[i], w1_ref[i])[0:1, :]
        hid = top + pltpu.roll(bot, m - 1, axis=0) + pe_term
        hid = hid * jax.nn.sigmoid(hid)
        out = _dot(hid.astype(BF16), w2_ref[i])
        return jnp.where(row < m - 1, out, 0.0)

    coarse = jnp.right_shift(row, 2).astype(F32)
    fine = jnp.bitwise_and(row, 3).astype(F32) * D_CMP + (L_CMP - 1) / 2.0
    kca_ref[0] = (mlp(0) + _pos_columns(coarse, fine, lane)).astype(BF16)
    vct_ref[0] = mlp(1).T[0:HEAD_DIM].astype(BF16)


def _compress(kvc, pe2, w1_2, w2_2):
    _, b, g, s, dh = kvc.shape
    m = s // D_CMP
    r = kvc.reshape(2, b * g, m, D_CMP * dh)
    hid = w1_2.shape[-1]
    return pl.pallas_call(
        functools.partial(_compress_kernel, m=m),
        grid=(b * g,),
        in_specs=[
            pl.BlockSpec((2, 1, m, D_CMP * dh), lambda j: (0, j, 0, 0)),
            pl.BlockSpec((2, 8, L_CMP * dh), lambda j: (0, 0, 0)),
            pl.BlockSpec((2, L_CMP * dh, hid), lambda j: (0, 0, 0)),
            pl.BlockSpec((2, hid, LANES), lambda j: (0, 0, 0)),
        ],
        out_specs=[
            pl.BlockSpec((1, m, LANES), lambda j: (j, 0, 0)),
            pl.BlockSpec((1, dh, m), lambda j: (j, 0, 0)),
        ],
        out_shape=[
            jax.ShapeDtypeStruct((b * g, m, LANES), BF16),
            jax.ShapeDtypeStruct((b * g, dh, m), BF16),
        ],
        compiler_params=_params("arbitrary"),
        name="compress",
    )(r, pe2, w1_2, w2_2)


def _head_stack(ot, gt, branch, tq):
    return jnp.concatenate(
        [ot[:, h * tq:(h + 1) * tq] * gt[3 * h + branch:3 * h + branch + 1, :] for h in range(HEADS_PER_GROUP)],
        axis=0)


def _nsa_cmp_kernel(q_ref, kca_ref, vct_ref, gate_ref, oc_ref, negm_ref, used_ref, *, tq, m):
    q0 = pl.program_id(2) * tq
    hpg = HEADS_PER_GROUP
    ncol = hpg * tq
    q = q_ref[0].reshape(ncol, LANES)
    tcol = q0 + jnp.bitwise_and(lax.broadcasted_iota(jnp.int32, (1, ncol), 1), tq - 1)

    def attend(rows):
        st = _dot_nt(kca_ref[0, 0:rows, :], q)
        nrow = lax.broadcasted_iota(jnp.int32, (rows, 1), 0)
        vis = (nrow * D_CMP + (L_CMP - 1)) <= tcol
        st = jnp.where(vis, st, NEG_INF)
        e = jnp.exp2(st - jnp.max(st, axis=0, keepdims=True))
        rcp = jnp.where(tcol >= L_CMP - 1, 1.0 / jnp.sum(e, axis=0, keepdims=True), 0.0)
        p = e * rcp
        ot = _dot(vct_ref[0, :, 0:rows], p.astype(BF16))
        pt = p[:, 0:tq]
        for h in range(1, hpg):
            pt = pt + p[:, h * tq:(h + 1) * tq]
        jj = lax.broadcasted_iota(jnp.int32, (MAX_SLC_BLOCKS, rows), 0) * L_SLC
        nn = lax.broadcasted_iota(jnp.int32, (MAX_SLC_BLOCKS, rows), 1) * D_CMP
        overlap_t = ((nn < jj + L_SLC) & (nn + L_CMP > jj)).astype(BF16)
        p_hi = pt.astype(BF16)
        p_lo = (pt - p_hi.astype(F32)).astype(BF16)
        return ot, _dot(overlap_t, p_hi) + _dot(overlap_t, p_lo)

    row_tile = min(m, CMP_ROW_TILE)
    n_variants = m // row_tile
    if n_variants == 1:
        ot, imp = attend(m)
    else:
        needed = pl.program_id(2) * (tq // D_CMP) + (tq // D_CMP - 1)
        variant = jnp.minimum(jnp.right_shift(needed - 1, row_tile.bit_length() - 1), n_variants - 1)
        ot, imp = lax.switch(variant, [functools.partial(attend, (v + 1) * row_tile) for v in range(n_variants)])
    gt = gate_ref[0, 0].T
    oc_ref[0] = _head_stack(ot, gt, 0, tq).T

    jrow = lax.broadcasted_iota(jnp.int32, (MAX_SLC_BLOCKS, tq), 0)
    cur = jnp.right_shift(q0 + lax.broadcasted_iota(jnp.int32, (1, tq), 1), L_SLC.bit_length() - 1)
    valid = jrow <= cur
    forced = (jrow == 0) | (jrow == cur) | (jrow == cur - 1)
    val = jnp.where(valid & jnp.logical_not(forced), imp, NEG_INF)
    jrow_f = jrow.astype(F32)
    for _ in range(N_SEL - N_FORCED):
        mx = jnp.max(val, axis=0, keepdims=True)
        first = jnp.min(jnp.where(val == mx, jrow_f, float(MAX_SLC_BLOCKS)), axis=0, keepdims=True)
        val = jnp.where(jrow_f == first, PICKED, val)
    sel_t = jnp.where(valid & (forced | (val == PICKED)), 1.0, 0.0).T
    negm_ref[0, 0] = jnp.where(sel_t > 0.5, 0.0, NEG_INF).astype(BF16)
    for r in range(tq // SUPER_KEYS):
        used_ref[r] = jnp.max(sel_t[r * SUPER_KEYS:(r + 1) * SUPER_KEYS], axis=0, keepdims=True)


def _nsa_cmp(q, kca, vct, gates, tq=512):
    b, nh, s, _ = q.shape
    g, hpg, dh = N_KV_GROUPS, HEADS_PER_GROUP, HEAD_DIM
    m = kca.shape[1]
    assert s % tq == 0 and tq % SUPER_KEYS == 0 and tq & (tq - 1) == 0
    nqt = s // tq
    sub = tq // SUPER_KEYS
    return pl.pallas_call(
        functools.partial(_nsa_cmp_kernel, tq=tq, m=m),
        grid=(b, g, s // tq),
        in_specs=[
            pl.BlockSpec((1, hpg, tq, LANES), lambda i, j, k: (i, j, k, 0)),
            pl.BlockSpec((1, m, LANES), lambda i, j, k: (i * N_KV_GROUPS + j, 0, 0)),
            pl.BlockSpec((1, dh, m), lambda i, j, k: (i * N_KV_GROUPS + j, 0, 0)),
            pl.BlockSpec((1, 1, tq, LANES), lambda i, j, k: (i, j, k, 0)),
        ],
        out_specs=[
            pl.BlockSpec((1, tq, hpg * dh), lambda i, j, k: (i, k, j)),
            pl.BlockSpec((1, 1, tq, MAX_SLC_BLOCKS), lambda i, j, k: (i, j, k, 0)),
            pl.BlockSpec((sub, 1, MAX_SLC_BLOCKS), lambda i, j, k: ((i * N_KV_GROUPS + j) * nqt + k, 0, 0)),
        ],
        out_shape=[
            jax.ShapeDtypeStruct((b, s, nh * dh), F32),
            jax.ShapeDtypeStruct((b, g, s, MAX_SLC_BLOCKS), BF16),
            jax.ShapeDtypeStruct((b * g * nqt * sub, 1, MAX_SLC_BLOCKS), F32),
        ],
        compiler_params=_params("arbitrary", "arbitrary", "arbitrary"),
        name="nsa_cmp",
    )(q, kca, vct, gates)


def _super_lists_kernel(used_ref, lst_ref, cnt_ref, *, tb):
    ut = used_ref[...].T.astype(BF16)
    a_i = lax.broadcasted_iota(jnp.int32, (MAX_SLC_BLOCKS, MAX_SLC_BLOCKS), 0)
    b_i = lax.broadcasted_iota(jnp.int32, (MAX_SLC_BLOCKS, MAX_SLC_BLOCKS), 1)
    pair_t = (a_i == jnp.right_shift(b_i, SUPER_SHIFT)).astype(BF16)
    sbf = jnp.where(_dot(pair_t, ut) > 0.5, 1.0, 0.0).astype(BF16)
    incl = _dot((b_i <= a_i).astype(BF16), sbf)[0:MAX_SUPER]
    cnt_ref[...] = jnp.broadcast_to(incl[MAX_SUPER - 1:MAX_SUPER, :], (8, tb)).astype(jnp.int32)
    for i in range(MAX_SUPER):
        lst_ref[i:i + 1, :] = jnp.sum(jnp.where(incl <= float(i), 1.0, 0.0), axis=0, keepdims=True).astype(jnp.int32)


def _super_lists(used):
    t = used.shape[0]
    tb = min(t, LANES)
    assert t % tb == 0
    return pl.pallas_call(
        functools.partial(_super_lists_kernel, tb=tb),
        grid=(t // tb,),
        in_specs=[pl.BlockSpec((tb, MAX_SLC_BLOCKS), lambda i: (i, 0))],
        out_specs=[
            pl.BlockSpec((MAX_SUPER, tb), lambda i: (0, i)),
            pl.BlockSpec((8, tb), lambda i: (0, i)),
        ],
        out_shape=[
            jax.ShapeDtypeStruct((MAX_SUPER, t), jnp.int32),
            jax.ShapeDtypeStruct((8, t), jnp.int32),
        ],
        compiler_params=_params("arbitrary"),
        name="super_lists",
    )(used)


def _nsa_slc_kernel(lst_ref, cnt_ref, q_ref, kas_ref, vts_ref, kaw_ref, vtw_ref, negm_ref, gate_ref, oc_ref, o_ref,
                    m_ref, acc_ref, *, tq, sub):
    hpg = HEADS_PER_GROUP
    ncol = hpg * tq
    tk = CHUNK_SUPERS * SUPER_KEYS
    n_win = WINDOW // SUPER_KEYS + 1
    n_tiles = pl.num_programs(0) * pl.num_programs(1) * pl.num_programs(2) * sub
    tile0 = ((pl.program_id(0) * pl.num_programs(1) + pl.program_id(1)) * pl.num_programs(2) + pl.program_id(2)) * sub
    row = lax.broadcasted_iota(jnp.int32, (SUPER_KEYS, 1), 0)

    class Tile:
        def __init__(self, s):
            self.s = s
            self.rows = slice(s * tq, (s + 1) * tq)
            self.qt = pl.program_id(2) * sub + s
            self.q = q_ref[0, :, self.rows, :].reshape(ncol, LANES)
            negm = negm_ref[0, 0, self.rows, :]
            self.qa = jnp.concatenate([self.q, jnp.concatenate([negm] * hpg, axis=0)], axis=1)
            self.tcol = self.qt * tq + jnp.bitwise_and(lax.broadcasted_iota(jnp.int32, (1, ncol), 1), tq - 1)
            self.tile = tile0 + s
            self.n_used = cnt_ref[self.tile]
            self.n_chunks = jnp.right_shift(self.n_used + CHUNK_SUPERS - 1, CHUNK_SUPERS.bit_length() - 1)
            m_ref[s] = jnp.full((1, ncol), NEG_INF, F32)
            acc_ref[s] = jnp.zeros((VT_ROWS, ncol), F32)

        def gather(self, chunk, partial):
            ks, vs, kbase = [], [], []
            for j in range(CHUNK_SUPERS):
                slot = chunk * CHUNK_SUPERS + j
                if partial:
                    ok = (slot >= 0) & (slot < self.n_used)
                    sb = jnp.where(ok, lst_ref[jnp.clip(slot, 0, MAX_SUPER - 1) * n_tiles + self.tile], 0)
                    kbase.append(jnp.where(ok, sb * SUPER_KEYS, FAR_POS))
                else:
                    sb = lst_ref[slot * n_tiles + self.tile]
                start = pl.multiple_of(sb * SUPER_KEYS, SUPER_KEYS)
                ks.append(kas_ref[0, 0, pl.ds(start, SUPER_KEYS), :])
                vs.append(vts_ref[0, 0, :, pl.ds(start, SUPER_KEYS)])
            return jnp.concatenate(ks, axis=0), jnp.concatenate(vs, axis=1), kbase

        def scores(self, kt, kbase):
            st = _dot_nt(kt, self.qa)
            if kbase:
                r = lax.broadcasted_iota(jnp.int32, (tk, 1), 0)
                rs = jnp.right_shift(r, SUPER_KEYS.bit_length() - 1)
                kpos = jnp.full((tk, 1), kbase[0], jnp.int32)
                for j in range(1, CHUNK_SUPERS):
                    kpos = jnp.where(rs == j, kbase[j], kpos)
                kpos = kpos + jnp.bitwise_and(r, SUPER_KEYS - 1)
                st = jnp.where(kpos <= self.tcol, st, NEG_INF)
            return st

        def update(self, st, vt, present=None):
            m_old = m_ref[self.s]
            m_new = jnp.maximum(m_old, jnp.max(st, axis=0, keepdims=True))
            if present is not None:
                m_new = jnp.where(present, m_new, m_old)
            p = jnp.exp2(st - m_new)
            acc_new = jnp.exp2(m_old - m_new) * acc_ref[self.s] + _dot(vt, p.astype(BF16))
            acc_ref[self.s] = acc_new if present is None else jnp.where(present, acc_new, acc_ref[self.s])
            m_ref[self.s] = m_new

        def early_chunks(self):
            def body(c, carry):
                kt, vt, _ = self.gather(c, False)
                self.update(self.scores(kt, None), vt)
                return carry

            lax.fori_loop(0, jnp.maximum(self.n_chunks - FINAL_CHUNKS, 0), body, 0)

        def final_scores(self):
            self.finals = []
            for back in range(1, FINAL_CHUNKS + 1):
                kt, vt, kbase = self.gather(self.n_chunks - back, True)
                self.finals.append((self.scores(kt, kbase if back == 1 else None), vt,
                                    None if back == 1 else self.n_chunks >= back))
            win_scores, win_v = [], []
            for r in range(n_win):
                sb = self.qt - (n_win - 1) + r
                start = pl.multiple_of(jnp.maximum(sb, 0) * SUPER_KEYS, SUPER_KEYS)
                sw = _dot_nt(kaw_ref[0, 0, pl.ds(start, SUPER_KEYS), :], self.q)
                if r == n_win - 1:
                    sw = jnp.where(sb * SUPER_KEYS + row <= self.tcol, sw, NEG_INF)
                elif r == 0:
                    lower = jnp.where(sb >= 0, sb * SUPER_KEYS + WINDOW, -FAR_POS) + row
                    sw = jnp.where(lower > self.tcol, sw, NEG_INF)
                else:
                    sw = sw + jnp.where(sb >= 0, 0.0, NEG_INF)
                win_scores.append(sw)
                win_v.append(vtw_ref[0, 0, :, pl.ds(start, SUPER_KEYS)])
            self.stw = jnp.concatenate(win_scores, axis=0)
            self.vw = jnp.concatenate(win_v, axis=1)

        def finish(self):
            for st, vt, present in self.finals:
                self.update(st, vt, present)
            acc = acc_ref[self.s]
            o_slc_t = acc[0:HEAD_DIM] * (1.0 / acc[HEAD_DIM:HEAD_DIM + 1])
            e = jnp.exp2(self.stw - jnp.max(self.stw, axis=0, keepdims=True))
            acc_w = _dot(self.vw, e.astype(BF16))
            o_win_t = acc_w[0:HEAD_DIM] * (1.0 / acc_w[HEAD_DIM:HEAD_DIM + 1])
            gt = gate_ref[0, 0, self.rows, :].T
            comb = _head_stack(o_slc_t, gt, 1, tq) + _head_stack(o_win_t, gt, 2, tq)
            o_ref[0, self.rows, :] = (oc_ref[0, self.rows, :] + comb.T).astype(BF16)

    tiles = [Tile(s) for s in range(sub)]
    for t in tiles:
        t.early_chunks()
    for t in tiles:
        t.final_scores()
    for t in tiles:
        t.finish()


def _nsa_slc(lists, counts, q, kas, vts, kaw, vtw, negm, gates, oc, tq=128, sub=4):
    b, nh, s, _ = q.shape
    g, hpg, dh = N_KV_GROUPS, HEADS_PER_GROUP, HEAD_DIM
    ts = tq * sub
    assert tq == SUPER_KEYS and WINDOW % tq == 0 and s >= WINDOW + tq and s % ts == 0
    grid_spec = pltpu.PrefetchScalarGridSpec(
        num_scalar_prefetch=2,
        grid=(b, g, s // ts),
        in_specs=[
            pl.BlockSpec((1, hpg, ts, LANES), lambda i, j, k, *_: (i, j, k, 0)),
            pl.BlockSpec((1, 1, s, 2 * LANES), lambda i, j, k, *_: (i, j, 0, 0)),
            pl.BlockSpec((1, 1, VT_ROWS, s), lambda i, j, k, *_: (i, j, 0, 0)),
            pl.BlockSpec((1, 1, s, LANES), lambda i, j, k, *_: (i, j, 0, 0)),
            pl.BlockSpec((1, 1, VT_ROWS, s), lambda i, j, k, *_: (i, j, 0, 0)),
            pl.BlockSpec((1, 1, ts, MAX_SLC_BLOCKS), lambda i, j, k, *_: (i, j, k, 0)),
            pl.BlockSpec((1, 1, ts, LANES), lambda i, j, k, *_: (i, j, k, 0)),
            pl.BlockSpec((1, ts, hpg * dh), lambda i, j, k, *_: (i, k, j)),
        ],
        out_specs=pl.BlockSpec((1, ts, hpg * dh), lambda i, j, k, *_: (i, k, j)),
        scratch_shapes=[
            pltpu.VMEM((sub, 1, hpg * tq), F32),
            pltpu.VMEM((sub, VT_ROWS, hpg * tq), F32),
        ],
    )
    return pl.pallas_call(
        functools.partial(_nsa_slc_kernel, tq=tq, sub=sub),
        grid_spec=grid_spec,
        out_shape=jax.ShapeDtypeStruct((b, s, nh * dh), BF16),
        compiler_params=_params("arbitrary", "arbitrary", "arbitrary"),
        name="nsa_slc",
    )(lists, counts, q, kas, vts, kaw, vtw, negm, gates, oc)


def _split_bf16(x, n):
    pieces = []
    for _ in range(n):
        p = x.astype(BF16).astype(F32)
        pieces.append(p)
        x = x - p
    return pieces


def _q_constants():
    slopes = jnp.exp2(-8.0 * jnp.arange(1, N_HEADS + 1, dtype=F32) / N_HEADS) * LOG2E
    pieces = jnp.stack(_split_bf16(slopes, N_SPLIT), axis=1)
    qc = jnp.zeros((N_HEADS, LANES), F32)
    qc = qc.at[:, POS_LANE:POS_LANE + N_SPLIT].set(pieces * L_SLC)
    qc = qc.at[:, POS_LANE + N_SPLIT:POS_LANE + 2 * N_SPLIT].set(pieces)
    return qc


def _qg_weight(w_qg):
    d = w_qg.shape[0]
    nq = N_HEADS * HEAD_DIM
    wg = jnp.pad(w_qg[:, nq:].reshape(d, N_KV_GROUPS, 3 * HEADS_PER_GROUP),
                 ((0, 0), (0, 0), (0, LANES - 3 * HEADS_PER_GROUP)))
    return jnp.concatenate([w_qg[:, :nq], wg.reshape(d, N_KV_GROUPS * LANES)], axis=1).astype(BF16)


def kernel(x, a_norm, a_w_in, a_conv, a_w_out, kv_norm, w_kv, cmp_pe_k, cmp_w1_k, cmp_w2_k, cmp_pe_v, cmp_w1_v,
           cmp_w2_v, b_norm, b_w_qg, b_w_o, f_norm, f_w_gu, f_w_down, final_norm):
    b, s, d = x.shape
    nq = N_HEADS * HEAD_DIM
    h = _mixer_a(x, a_norm[0], a_w_in[0].astype(BF16), a_conv[0], a_w_out[0].astype(BF16))
    h = _ffn(h.reshape(b * s, d), f_norm[0], f_w_gu[0].astype(BF16), f_w_down[0].astype(BF16), final_norm)
    h = h.reshape(b, s, d)

    kvc, kas, vts, kaw, vtw, q, gates = _proj(h, kv_norm, b_norm[0], w_kv.astype(BF16), _qg_weight(b_w_qg[0]),
                                              _q_constants())
    pe2 = jnp.stack([cmp_pe_k, cmp_pe_v]).reshape(2, 1, L_CMP * HEAD_DIM)
    pe2 = jnp.broadcast_to(pe2, (2, 8, L_CMP * HEAD_DIM)).astype(BF16)
    w2 = jnp.pad(jnp.stack([cmp_w2_k, cmp_w2_v]), ((0, 0), (0, 0), (0, LANES - HEAD_DIM))).astype(BF16)
    kca, vct = _compress(kvc, pe2, jnp.stack([cmp_w1_k, cmp_w1_v]).astype(BF16), w2)

    oc, negm, used = _nsa_cmp(q, kca, vct, gates)
    lists, counts = _super_lists(used.reshape(used.shape[0], MAX_SLC_BLOCKS))
    o = _nsa_slc(lists.reshape(-1), counts[0], q, kas, vts, kaw, vtw, negm, gates, oc)

    out = _ffn(h.reshape(b * s, d), f_norm[1], f_w_gu[1].astype(BF16), f_w_down[1].astype(BF16), final_norm,
               o=o.reshape(b * s, nq), w_o=b_w_o[0].astype(BF16), with_final=True)
    return out.reshape(b, s, d)
```

```python
import functools
import math

import jax
import jax.numpy as jnp
from jax import lax
from jax.experimental import pallas as pl
from jax.experimental.pallas import tpu as pltpu

F32 = jnp.float32
BF16 = jnp.bfloat16

N_HEADS = 16
N_KV_GROUPS = 4
HEADS_PER_GROUP = N_HEADS // N_KV_GROUPS
HEAD_DIM = 64
L_CMP = 32
D_CMP = 16
L_SLC = 64
N_SEL = 16
N_FORCED = 3
WINDOW = 512
N_KV_SETS = 6
CONV_WIDTH = 3
RMS_EPS = 1e-5
NEG_INF = -1e30
PICKED = -3.0e38
LOG2E = math.log2(math.e)
LANES = 128
MAX_SLC_BLOCKS = LANES
SUPER_KEYS = LANES
SUPER_SHIFT = (SUPER_KEYS // L_SLC).bit_length() - 1
MAX_SUPER = MAX_SLC_BLOCKS * L_SLC // SUPER_KEYS
CHUNK_SUPERS = 4
FINAL_CHUNKS = 3
CMP_ROW_TILE = 128
VT_ROWS = HEAD_DIM + 16
FAR_POS = 1 << 30
POS_LANE = HEAD_DIM
N_SPLIT = 3
VMEM_LIMIT_BYTES = 56 * 1024 * 1024


def _rms(x, g):
    return x * lax.rsqrt(jnp.mean(x * x, axis=-1, keepdims=True) + RMS_EPS) * g


def _dot(a, b):
    return jnp.dot(a, b, preferred_element_type=F32)


def _dot_nt(a, b):
    return lax.dot_general(a, b, (((1,), (1,)), ((), ())), preferred_element_type=F32)


def _params(*sem, flags=None):
    return pltpu.CompilerParams(dimension_semantics=sem, vmem_limit_bytes=VMEM_LIMIT_BYTES, flags=flags)


def _pos_columns(coarse, fine, lane):
    is_c = (lane >= POS_LANE) & (lane < POS_LANE + N_SPLIT)
    is_f = (lane >= POS_LANE + N_SPLIT) & (lane < POS_LANE + 2 * N_SPLIT)
    return jnp.where(is_c, coarse, jnp.where(is_f, fine, 0.0))


def _mixer_a_kernel(x_ref, g_ref, win_ref, conv_ref, wout_ref, o_ref, ext_ref, *, tm, d, tc):
    @pl.when(pl.program_id(1) == 0)
    def _():
        ext_ref[0:8, :] = jnp.zeros((8, d), F32)

    x = x_ref[0]
    xb = _rms(x, g_ref[...]).astype(BF16)
    y = x
    for c0 in range(0, d, tc):
        cols = slice(c0, c0 + tc)
        b_gate = _dot(xb, win_ref[:, c0:c0 + tc])
        cv = _dot(xb, win_ref[:, d + c0:d + c0 + tc]) * _dot(xb, win_ref[:, 2 * d + c0:2 * d + c0 + tc])
        ext_ref[8:8 + tm, cols] = cv
        cv1 = ext_ref[7:7 + tm, cols]
        cv2 = ext_ref[6:6 + tm, cols]
        u = conv_ref[0:1, cols] * cv2 + conv_ref[1:2, cols] * cv1 + conv_ref[2:3, cols] * cv
        ext_ref[0:8, cols] = cv[tm - 8:tm, :]
        y = y + _dot((b_gate * u).astype(BF16), wout_ref[cols, :])
    o_ref[0] = y


def _mixer_a(x, g, w_in, conv_w, w_out, tm=512, tc=1024):
    b, s, d = x.shape
    assert s % tm == 0 and d % tc == 0
    return pl.pallas_call(
        functools.partial(_mixer_a_kernel, tm=tm, d=d, tc=tc),
        grid=(b, s // tm),
        in_specs=[
            pl.BlockSpec((1, tm, d), lambda i, j: (i, j, 0)),
            pl.BlockSpec((1, d), lambda i, j: (0, 0)),
            pl.BlockSpec((d, 3 * d), lambda i, j: (0, 0)),
            pl.BlockSpec((CONV_WIDTH, d), lambda i, j: (0, 0)),
            pl.BlockSpec((d, d), lambda i, j: (0, 0)),
        ],
        out_specs=pl.BlockSpec((1, tm, d), lambda i, j: (i, j, 0)),
        out_shape=jax.ShapeDtypeStruct((b, s, d), F32),
        scratch_shapes=[pltpu.VMEM((tm + 8, d), F32)],
        compiler_params=_params("arbitrary", "arbitrary"),
        name="mixer_a",
    )(x, g.reshape(1, d), w_in, conv_w, w_out)


def _ffn_kernel(*refs, with_proj, with_final, f, tf):
    if with_proj:
        h_ref, o_ref_in, wo_ref, g_ref, wgu_ref, wd_ref, fg_ref, out_ref = refs
    else:
        h_ref, g_ref, wgu_ref, wd_ref, fg_ref, out_ref = refs
    h = h_ref[...]
    if with_proj:
        h = h + _dot(o_ref_in[...], wo_ref[...])
    xb = _rms(h, g_ref[...]).astype(BF16)
    y = h
    for c in range(f // tf):
        gate = _dot(xb, wgu_ref[:, c * tf:(c + 1) * tf])
        up = _dot(xb, wgu_ref[:, f + c * tf:f + (c + 1) * tf])
        act = (gate * jax.nn.sigmoid(gate)) * up
        y = y + _dot(act.astype(BF16), wd_ref[c * tf:(c + 1) * tf, :])
    if with_final:
        y = _rms(y, fg_ref[...])
    out_ref[...] = y


def _resident(shape):
    return pl.BlockSpec(shape, lambda i: (0,) * len(shape), pipeline_mode=pl.Buffered(1))


def _ffn(h, g, w_gu, w_down, final_g, o=None, w_o=None, with_final=False, tm=1024, tf=256):
    t, d = h.shape
    f = w_down.shape[0]
    assert t % tm == 0 and f % tf == 0
    with_proj = o is not None
    in_specs = [pl.BlockSpec((tm, d), lambda i: (i, 0))]
    args = [h]
    if with_proj:
        in_specs += [pl.BlockSpec((tm, o.shape[1]), lambda i: (i, 0)), _resident(w_o.shape)]
        args += [o, w_o]
    in_specs += [_resident((1, d)), _resident(w_gu.shape), _resident(w_down.shape), _resident((1, d))]
    args += [g.reshape(1, d), w_gu, w_down, final_g.reshape(1, d)]
    return pl.pallas_call(
        functools.partial(_ffn_kernel, with_proj=with_proj, with_final=with_final, f=f, tf=tf),
        grid=(t // tm,),
        in_specs=in_specs,
        out_specs=pl.BlockSpec((tm, d), lambda i: (i, 0)),
        out_shape=jax.ShapeDtypeStruct((t, d), F32),
        compiler_params=_params("arbitrary"),
        name="ffn_proj" if with_proj else "ffn",
    )(*args)


def _halves(x):
    return x, pltpu.roll(x, HEAD_DIM, axis=1)


def _proj_kernel(h_ref, gkv_ref, gq_ref, wkv_ref, wqg_ref, qc_ref,
                 kvc_ref, kas_ref, vts_ref, kaw_ref, vtw_ref, q_ref, gate_ref, *, tm):
    g, dh = N_KV_GROUPS, HEAD_DIM
    gw = g * dh
    s0 = pl.program_id(1) * tm
    x = h_ref[0]
    y = x * lax.rsqrt(jnp.mean(x * x, axis=-1, keepdims=True) + RMS_EPS)
    kv = _dot((y * gkv_ref[...]).astype(BF16), wkv_ref[...])
    for st in range(2):
        for grp in range(g):
            kvc_ref[st, 0, grp] = kv[:, st * gw + grp * dh:st * gw + (grp + 1) * dh].astype(BF16)

    kpos = s0 + lax.broadcasted_iota(jnp.int32, (tm, LANES), 0)
    lane = lax.broadcasted_iota(jnp.int32, (tm, LANES), 1)
    low = lane < dh
    kblk = jnp.right_shift(kpos, L_SLC.bit_length() - 1)
    posc = _pos_columns(kblk.astype(F32), jnp.bitwise_and(kpos, L_SLC - 1).astype(F32), lane)
    onehot = (lane == kblk).astype(BF16)
    ones_rows = (lax.broadcasted_iota(jnp.int32, (VT_ROWS - dh, tm), 0) == 0).astype(BF16)
    for pair in range(g // 2):
        for st, k_ref, vt_ref in ((2, kas_ref, vts_ref), (4, kaw_ref, vtw_ref)):
            k_pair = kv[:, st * gw + pair * LANES:st * gw + (pair + 1) * LANES]
            vt = kv[:, (st + 1) * gw + pair * LANES:(st + 1) * gw + (pair + 1) * LANES].T
            for half, kh in enumerate(_halves(k_pair)):
                grp = 2 * pair + half
                k_ref[0, grp, :, 0:LANES] = jnp.where(low, kh, posc).astype(BF16)
                if st == 2:
                    k_ref[0, grp, :, LANES:2 * LANES] = onehot
                vt_ref[0, grp, 0:dh] = vt[half * dh:(half + 1) * dh].astype(BF16)
                vt_ref[0, grp, dh:VT_ROWS] = ones_rows

    qg = _dot((y * gq_ref[...]).astype(BF16), wqg_ref[...])
    for pair in range(N_HEADS // 2):
        q_pair = qg[:, pair * LANES:(pair + 1) * LANES] * (LOG2E * dh ** -0.5)
        for half, qh in enumerate(_halves(q_pair)):
            hd = 2 * pair + half
            q_ref[0, hd] = jnp.where(low, qh, qc_ref[hd:hd + 1, :]).astype(BF16)
    nq = N_HEADS * dh
    for grp in range(g):
        gate_ref[0, grp] = jax.nn.sigmoid(qg[:, nq + grp * LANES: nq + (grp + 1) * LANES])


def _proj(h, g_kv, g_q, w_kv, w_qg, qconst, tm=512):
    b, s, d = h.shape
    assert s % tm == 0 and s // L_SLC <= MAX_SLC_BLOCKS
    g, dh = N_KV_GROUPS, HEAD_DIM
    return pl.pallas_call(
        functools.partial(_proj_kernel, tm=tm),
        grid=(b, s // tm),
        in_specs=[
            pl.BlockSpec((1, tm, d), lambda i, j: (i, j, 0)),
            pl.BlockSpec((1, d), lambda i, j: (0, 0)),
            pl.BlockSpec((1, d), lambda i, j: (0, 0)),
            pl.BlockSpec(w_kv.shape, lambda i, j: (0, 0)),
            pl.BlockSpec(w_qg.shape, lambda i, j: (0, 0)),
            pl.BlockSpec(qconst.shape, lambda i, j: (0, 0)),
        ],
        out_specs=[
            pl.BlockSpec((2, 1, g, tm, dh), lambda i, j: (0, i, 0, j, 0)),
            pl.BlockSpec((1, g, tm, 2 * LANES), lambda i, j: (i, 0, j, 0)),
            pl.BlockSpec((1, g, VT_ROWS, tm), lambda i, j: (i, 0, 0, j)),
            pl.BlockSpec((1, g, tm, LANES), lambda i, j: (i, 0, j, 0)),
            pl.BlockSpec((1, g, VT_ROWS, tm), lambda i, j: (i, 0, 0, j)),
            pl.BlockSpec((1, N_HEADS, tm, LANES), lambda i, j: (i, 0, j, 0)),
            pl.BlockSpec((1, g, tm, LANES), lambda i, j: (i, 0, j, 0)),
        ],
        out_shape=[
            jax.ShapeDtypeStruct((2, b, g, s, dh), BF16),
            jax.ShapeDtypeStruct((b, g, s, 2 * LANES), BF16),
            jax.ShapeDtypeStruct((b, g, VT_ROWS, s), BF16),
            jax.ShapeDtypeStruct((b, g, s, LANES), BF16),
            jax.ShapeDtypeStruct((b, g, VT_ROWS, s), BF16),
            jax.ShapeDtypeStruct((b, N_HEADS, s, LANES), BF16),
            jax.ShapeDtypeStruct((b, g, s, LANES), F32),
        ],
        compiler_params=_params("arbitrary", "arbitrary"),
        name="proj",
    )(h, g_kv.reshape(1, d), g_q.reshape(1, d), w_kv, w_qg, qconst)


def _compress_kernel(r_ref, pe_ref, w1_ref, w2_ref, kca_ref, vct_ref, *, m):
    half = D_CMP * HEAD_DIM
    row = lax.broadcasted_iota(jnp.int32, (m, LANES), 0)
    lane = lax.broadcasted_iota(jnp.int32, (m, LANES), 1)

    def mlp(i):
        r = r_ref[i, 0]
        top = _dot(r, w1_ref[i, 0:half, :])
        bot = _dot(r, w1_ref[i, half:2 * half, :])
        pe_term = _dot(pe_ref[i], w1_ref[i])[0:1, :]
        hid = top + pltpu.roll(bot, m - 1, axis=0) + pe_term
        hid = hid * jax.nn.sigmoid(hid)
        out = _dot(hid.astype(BF16), w2_ref[i])
        return jnp.where(row < m - 1, out, 0.0)

    coarse = jnp.right_shift(row, 2).astype(F32)
    fine = jnp.bitwise_and(row, 3).astype(F32) * D_CMP + (L_CMP - 1) / 2.0
    kca_ref[0] = (mlp(0) + _pos_columns(coarse, fine, lane)).astype(BF16)
    vct_ref[0] = mlp(1).T[0:HEAD_DIM].astype(BF16)


def _compress(kvc, pe2, w1_2, w2_2):
    _, b, g, s, dh = kvc.shape
    m = s // D_CMP
    r = kvc.reshape(2, b * g, m, D_CMP * dh)
    hid = w1_2.shape[-1]
    return pl.pallas_call(
        functools.partial(_compress_kernel, m=m),
        grid=(b * g,),
        in_specs=[
            pl.BlockSpec((2, 1, m, D_CMP * dh), lambda j: (0, j, 0, 0)),
            pl.BlockSpec((2, 8, L_CMP * dh), lambda j: (0, 0, 0)),
            pl.BlockSpec((2, L_CMP * dh, hid), lambda j: (0, 0, 0)),
            pl.BlockSpec((2, hid, LANES), lambda j: (0, 0, 0)),
        ],
        out_specs=[
            pl.BlockSpec((1, m, LANES), lambda j: (j, 0, 0)),
            pl.BlockSpec((1, dh, m), lambda j: (j, 0, 0)),
        ],
        out_shape=[
            jax.ShapeDtypeStruct((b * g, m, LANES), BF16),
            jax.ShapeDtypeStruct((b * g, dh, m), BF16),
        ],
        compiler_params=_params("arbitrary"),
        name="compress",
    )(r, pe2, w1_2, w2_2)


def _head_stack(ot, gt, branch, tq):
    return jnp.concatenate(
        [ot[:, h * tq:(h + 1) * tq] * gt[3 * h + branch:3 * h + branch + 1, :] for h in range(HEADS_PER_GROUP)],
        axis=0)


def _nsa_cmp_kernel(q_ref, kca_ref, vct_ref, gate_ref, oc_ref, negm_ref, used_ref, *, tq, m):
    q0 = pl.program_id(2) * tq
    hpg = HEADS_PER_GROUP
    ncol = hpg * tq
    q = q_ref[0].reshape(ncol, LANES)
    tcol = q0 + jnp.bitwise_and(lax.broadcasted_iota(jnp.int32, (1, ncol), 1), tq - 1)

    def attend(rows):
        st = _dot_nt(kca_ref[0, 0:rows, :], q)
        nrow = lax.broadcasted_iota(jnp.int32, (rows, 1), 0)
        vis = (nrow * D_CMP + (L_CMP - 1)) <= tcol
        st = jnp.where(vis, st, NEG_INF)
        e = jnp.exp2(st - jnp.max(st, axis=0, keepdims=True))
        rcp = jnp.where(tcol >= L_CMP - 1, 1.0 / jnp.sum(e, axis=0, keepdims=True), 0.0)
        p = e * rcp
        ot = _dot(vct_ref[0, :, 0:rows], p.astype(BF16))
        pt = p[:, 0:tq]
        for h in range(1, hpg):
            pt = pt + p[:, h * tq:(h + 1) * tq]
        nblk = min(MAX_SLC_BLOCKS, rows * D_CMP // L_SLC)
        jj = lax.broadcasted_iota(jnp.int32, (nblk, rows), 0) * L_SLC
        nn = lax.broadcasted_iota(jnp.int32, (nblk, rows), 1) * D_CMP
        overlap_t = ((nn < jj + L_SLC) & (nn + L_CMP > jj)).astype(BF16)
        p_hi = pt.astype(BF16)
        p_lo = (pt - p_hi.astype(F32)).astype(BF16)
        imp = _dot(overlap_t, p_hi) + _dot(overlap_t, p_lo)

        jrow = lax.broadcasted_iota(jnp.int32, (nblk, tq), 0)
        cur = jnp.right_shift(q0 + lax.broadcasted_iota(jnp.int32, (1, tq), 1), L_SLC.bit_length() - 1)
        valid = jrow <= cur
        forced = (jrow == 0) | (jrow == cur) | (jrow == cur - 1)
        val = jnp.where(valid & jnp.logical_not(forced), imp, NEG_INF)
        jrow_f = jrow.astype(F32)
        for _ in range(N_SEL - N_FORCED):
            mx = jnp.max(val, axis=0, keepdims=True)
            first = jnp.min(jnp.where(val == mx, jrow_f, float(MAX_SLC_BLOCKS)), axis=0, keepdims=True)
            val = jnp.where(jrow_f == first, PICKED, val)
        sel = jnp.where(valid & (forced | (val == PICKED)), 1.0, 0.0)
        if nblk < MAX_SLC_BLOCKS:
            sel = jnp.concatenate([sel, jnp.zeros((MAX_SLC_BLOCKS - nblk, tq), F32)], axis=0)
        gt = gate_ref[0, 0].T
        oc_ref[0] = _head_stack(ot, gt, 0, tq).T
        sel_t = sel.T
        negm_ref[0, 0] = jnp.where(sel_t > 0.5, 0.0, NEG_INF).astype(BF16)
        for r in range(tq // SUPER_KEYS):
            used_ref[r] = jnp.max(sel_t[r * SUPER_KEYS:(r + 1) * SUPER_KEYS], axis=0, keepdims=True)

    row_tile = min(m, CMP_ROW_TILE)
    n_variants = m // row_tile
    if n_variants == 1:
        attend(m)
    else:
        needed = pl.program_id(2) * (tq // D_CMP) + (tq // D_CMP - 1)
        variant = jnp.minimum(jnp.right_shift(needed - 1, row_tile.bit_length() - 1), n_variants - 1)
        for v in range(n_variants):
            pl.when(variant == v)(functools.partial(attend, (v + 1) * row_tile))


def _nsa_cmp(q, kca, vct, gates, tq=512):
    b, nh, s, _ = q.shape
    g, hpg, dh = N_KV_GROUPS, HEADS_PER_GROUP, HEAD_DIM
    m = kca.shape[1]
    assert s % tq == 0 and tq % SUPER_KEYS == 0 and tq & (tq - 1) == 0
    nqt = s // tq
    sub = tq // SUPER_KEYS
    return pl.pallas_call(
        functools.partial(_nsa_cmp_kernel, tq=tq, m=m),
        grid=(b, g, s // tq),
        in_specs=[
            pl.BlockSpec((1, hpg, tq, LANES), lambda i, j, k: (i, j, k, 0)),
            pl.BlockSpec((1, m, LANES), lambda i, j, k: (i * N_KV_GROUPS + j, 0, 0)),
            pl.BlockSpec((1, dh, m), lambda i, j, k: (i * N_KV_GROUPS + j, 0, 0)),
            pl.BlockSpec((1, 1, tq, LANES), lambda i, j, k: (i, j, k, 0)),
        ],
        out_specs=[
            pl.BlockSpec((1, tq, hpg * dh), lambda i, j, k: (i, k, j)),
            pl.BlockSpec((1, 1, tq, MAX_SLC_BLOCKS), lambda i, j, k: (i, j, k, 0)),
            pl.BlockSpec((sub, 1, MAX_SLC_BLOCKS), lambda i, j, k: ((i * N_KV_GROUPS + j) * nqt + k, 0, 0)),
        ],
        out_shape=[
            jax.ShapeDtypeStruct((b, s, nh * dh), F32),
            jax.ShapeDtypeStruct((b, g, s, MAX_SLC_BLOCKS), BF16),
            jax.ShapeDtypeStruct((b * g * nqt * sub, 1, MAX_SLC_BLOCKS), F32),
        ],
        compiler_params=_params("arbitrary", "arbitrary", "arbitrary"),
        name="nsa_cmp",
    )(q, kca, vct, gates)


def _super_lists_kernel(used_ref, lst_ref, cnt_ref, *, tb):
    ut = used_ref[...].T.astype(BF16)
    a_i = lax.broadcasted_iota(jnp.int32, (MAX_SLC_BLOCKS, MAX_SLC_BLOCKS), 0)
    b_i = lax.broadcasted_iota(jnp.int32, (MAX_SLC_BLOCKS, MAX_SLC_BLOCKS), 1)
    pair_t = (a_i == jnp.right_shift(b_i, SUPER_SHIFT)).astype(BF16)
    sbf = jnp.where(_dot(pair_t, ut) > 0.5, 1.0, 0.0).astype(BF16)
    incl = _dot((b_i <= a_i).astype(BF16), sbf)[0:MAX_SUPER]
    cnt_ref[...] = jnp.broadcast_to(incl[MAX_SUPER - 1:MAX_SUPER, :], (8, tb)).astype(jnp.int32)
    for i in range(MAX_SUPER):
        lst_ref[i:i + 1, :] = jnp.sum(jnp.where(incl <= float(i), 1.0, 0.0), axis=0, keepdims=True).astype(jnp.int32)


def _super_lists(used):
    t = used.shape[0]
    tb = min(t, LANES)
    assert t % tb == 0
    return pl.pallas_call(
        functools.partial(_super_lists_kernel, tb=tb),
        grid=(t // tb,),
        in_specs=[pl.BlockSpec((tb, MAX_SLC_BLOCKS), lambda i: (i, 0))],
        out_specs=[
            pl.BlockSpec((MAX_SUPER, tb), lambda i: (0, i)),
            pl.BlockSpec((8, tb), lambda i: (0, i)),
        ],
        out_shape=[
            jax.ShapeDtypeStruct((MAX_SUPER, t), jnp.int32),
            jax.ShapeDtypeStruct((8, t), jnp.int32),
        ],
        compiler_params=_params("arbitrary"),
        name="super_lists",
    )(used)


def _nsa_slc_kernel(lst_ref, cnt_ref, q_ref, kas_ref, vts_ref, kaw_ref, vtw_ref, negm_ref, gate_ref, oc_ref, o_ref,
                    m_ref, acc_ref, *, tq, sub):
    hpg = HEADS_PER_GROUP
    ncol = hpg * tq
    tk = CHUNK_SUPERS * SUPER_KEYS
    n_win = WINDOW // SUPER_KEYS + 1
    n_tiles = pl.num_programs(0) * pl.num_programs(1) * pl.num_programs(2) * sub
    tile0 = ((pl.program_id(0) * pl.num_programs(1) + pl.program_id(1)) * pl.num_programs(2) + pl.program_id(2)) * sub
    row = lax.broadcasted_iota(jnp.int32, (SUPER_KEYS, 1), 0)

    class Tile:
        def __init__(self, s):
            self.s = s
            self.rows = slice(s * tq, (s + 1) * tq)
            self.qt = pl.program_id(2) * sub + s
            self.q = q_ref[0, :, self.rows, :].reshape(ncol, LANES)
            negm = negm_ref[0, 0, self.rows, :]
            self.qa = jnp.concatenate([self.q, jnp.concatenate([negm] * hpg, axis=0)], axis=1)
            self.tcol = self.qt * tq + jnp.bitwise_and(lax.broadcasted_iota(jnp.int32, (1, ncol), 1), tq - 1)
            self.tile = tile0 + s
            self.n_used = cnt_ref[self.tile]
            self.n_chunks = jnp.right_shift(self.n_used + CHUNK_SUPERS - 1, CHUNK_SUPERS.bit_length() - 1)
            m_ref[s] = jnp.full((1, ncol), NEG_INF, F32)
            acc_ref[s] = jnp.zeros((VT_ROWS, ncol), F32)

        def gather(self, chunk, partial):
            ks, vs, kbase = [], [], []
            for j in range(CHUNK_SUPERS):
                slot = chunk * CHUNK_SUPERS + j
                if partial:
                    ok = (slot >= 0) & (slot < self.n_used)
                    sb = jnp.where(ok, lst_ref[jnp.clip(slot, 0, MAX_SUPER - 1) * n_tiles + self.tile], 0)
                    kbase.append(jnp.where(ok, sb * SUPER_KEYS, FAR_POS))
                else:
                    sb = lst_ref[slot * n_tiles + self.tile]
                start = pl.multiple_of(sb * SUPER_KEYS, SUPER_KEYS)
                ks.append(kas_ref[0, 0, pl.ds(start, SUPER_KEYS), :])
                vs.append(vts_ref[0, 0, :, pl.ds(start, SUPER_KEYS)])
            return jnp.concatenate(ks, axis=0), jnp.concatenate(vs, axis=1), kbase

        def scores(self, kt, kbase):
            st = _dot_nt(kt, self.qa)
            if kbase:
                r = lax.broadcasted_iota(jnp.int32, (tk, 1), 0)
                rs = jnp.right_shift(r, SUPER_KEYS.bit_length() - 1)
                kpos = jnp.full((tk, 1), kbase[0], jnp.int32)
                for j in range(1, CHUNK_SUPERS):
                    kpos = jnp.where(rs == j, kbase[j], kpos)
                kpos = kpos + jnp.bitwise_and(r, SUPER_KEYS - 1)
                st = jnp.where(kpos <= self.tcol, st, NEG_INF)
            return st

        def update(self, st, vt, present=None):
            m_old = m_ref[self.s]
            m_new = jnp.maximum(m_old, jnp.max(st, axis=0, keepdims=True))
            if present is not None:
                m_new = jnp.where(present, m_new, m_old)
            p = jnp.exp2(st - m_new)
            acc_new = jnp.exp2(m_old - m_new) * acc_ref[self.s] + _dot(vt, p.astype(BF16))
            acc_ref[self.s] = acc_new if present is None else jnp.where(present, acc_new, acc_ref[self.s])
            m_ref[self.s] = m_new

        def early_chunks(self):
            def body(c, carry):
                kt, vt, _ = self.gather(c, False)
                self.update(self.scores(kt, None), vt)
                return carry

            lax.fori_loop(0, jnp.maximum(self.n_chunks - FINAL_CHUNKS, 0), body, 0)

        def final_scores(self):
            self.finals = []
            for back in range(1, FINAL_CHUNKS + 1):
                kt, vt, kbase = self.gather(self.n_chunks - back, True)
                self.finals.append((self.scores(kt, kbase if back == 1 else None), vt,
                                    None if back == 1 else self.n_chunks >= back))
            win_scores, win_v = [], []
            for r in range(n_win):
                sb = self.qt - (n_win - 1) + r
                start = pl.multiple_of(jnp.maximum(sb, 0) * SUPER_KEYS, SUPER_KEYS)
                sw = _dot_nt(kaw_ref[0, 0, pl.ds(start, SUPER_KEYS), :], self.q)
                if r == n_win - 1:
                    sw = jnp.where(sb * SUPER_KEYS + row <= self.tcol, sw, NEG_INF)
                elif r == 0:
                    lower = jnp.where(sb >= 0, sb * SUPER_KEYS + WINDOW, -FAR_POS) + row
                    sw = jnp.where(lower > self.tcol, sw, NEG_INF)
                else:
                    sw = sw + jnp.where(sb >= 0, 0.0, NEG_INF)
                win_scores.append(sw)
                win_v.append(vtw_ref[0, 0, :, pl.ds(start, SUPER_KEYS)])
            self.stw = jnp.concatenate(win_scores, axis=0)
            self.vw = jnp.concatenate(win_v, axis=1)

        def finish(self):
            for st, vt, present in self.finals:
                self.update(st, vt, present)
            acc = acc_ref[self.s]
            o_slc_t = acc[0:HEAD_DIM] * (1.0 / acc[HEAD_DIM:HEAD_DIM + 1])
            e = jnp.exp2(self.stw - jnp.max(self.stw, axis=0, keepdims=True))
            acc_w = _dot(self.vw, e.astype(BF16))
            o_win_t = acc_w[0:HEAD_DIM] * (1.0 / acc_w[HEAD_DIM:HEAD_DIM + 1])
            gt = gate_ref[0, 0, self.rows, :].T
            comb = _head_stack(o_slc_t, gt, 1, tq) + _head_stack(o_win_t, gt, 2, tq)
            o_ref[0, self.rows, :] = (oc_ref[0, self.rows, :] + comb.T).astype(BF16)

    tiles = [Tile(s) for s in range(sub)]
    for t in tiles:
        t.early_chunks()
    for t in tiles:
        t.final_scores()
    for t in tiles:
        t.finish()


def _nsa_slc(lists, counts, q, kas, vts, kaw, vtw, negm, gates, oc, tq=128, sub=4):
    b, nh, s, _ = q.shape
    g, hpg, dh = N_KV_GROUPS, HEADS_PER_GROUP, HEAD_DIM
    ts = tq * sub
    assert tq == SUPER_KEYS and WINDOW % tq == 0 and s >= WINDOW + tq and s % ts == 0
    grid_spec = pltpu.PrefetchScalarGridSpec(
        num_scalar_prefetch=2,
        grid=(b, g, s // ts),
        in_specs=[
            pl.BlockSpec((1, hpg, ts, LANES), lambda i, j, k, *_: (i, j, k, 0)),
            pl.BlockSpec((1, 1, s, 2 * LANES), lambda i, j, k, *_: (i, j, 0, 0)),
            pl.BlockSpec((1, 1, VT_ROWS, s), lambda i, j, k, *_: (i, j, 0, 0)),
            pl.BlockSpec((1, 1, s, LANES), lambda i, j, k, *_: (i, j, 0, 0)),
            pl.BlockSpec((1, 1, VT_ROWS, s), lambda i, j, k, *_: (i, j, 0, 0)),
            pl.BlockSpec((1, 1, ts, MAX_SLC_BLOCKS), lambda i, j, k, *_: (i, j, k, 0)),
            pl.BlockSpec((1, 1, ts, LANES), lambda i, j, k, *_: (i, j, k, 0)),
            pl.BlockSpec((1, ts, hpg * dh), lambda i, j, k, *_: (i, k, j)),
        ],
        out_specs=pl.BlockSpec((1, ts, hpg * dh), lambda i, j, k, *_: (i, k, j)),
        scratch_shapes=[
            pltpu.VMEM((sub, 1, hpg * tq), F32),
            pltpu.VMEM((sub, VT_ROWS, hpg * tq), F32),
        ],
    )
    return pl.pallas_call(
        functools.partial(_nsa_slc_kernel, tq=tq, sub=sub),
        grid_spec=grid_spec,
        out_shape=jax.ShapeDtypeStruct((b, s, nh * dh), BF16),
        compiler_params=_params("arbitrary", "arbitrary", "arbitrary"),
        name="nsa_slc",
    )(lists, counts, q, kas, vts, kaw, vtw, negm, gates, oc)


def _split_bf16(x, n):
    pieces = []
    for _ in range(n):
        p = x.astype(BF16).astype(F32)
        pieces.append(p)
        x = x - p
    return pieces


def _q_constants():
    slopes = jnp.exp2(-8.0 * jnp.arange(1, N_HEADS + 1, dtype=F32) / N_HEADS) * LOG2E
    pieces = jnp.stack(_split_bf16(slopes, N_SPLIT), axis=1)
    qc = jnp.zeros((N_HEADS, LANES), F32)
    qc = qc.at[:, POS_LANE:POS_LANE + N_SPLIT].set(pieces * L_SLC)
    qc = qc.at[:, POS_LANE + N_SPLIT:POS_LANE + 2 * N_SPLIT].set(pieces)
    return qc


def _qg_weight(w_qg):
    d = w_qg.shape[0]
    nq = N_HEADS * HEAD_DIM
    wg = jnp.pad(w_qg[:, nq:].reshape(d, N_KV_GROUPS, 3 * HEADS_PER_GROUP),
                 ((0, 0), (0, 0), (0, LANES - 3 * HEADS_PER_GROUP)))
    return jnp.concatenate([w_qg[:, :nq], wg.reshape(d, N_KV_GROUPS * LANES)], axis=1).astype(BF16)


def kernel(x, a_norm, a_w_in, a_conv, a_w_out, kv_norm, w_kv, cmp_pe_k, cmp_w1_k, cmp_w2_k, cmp_pe_v, cmp_w1_v,
           cmp_w2_v, b_norm, b_w_qg, b_w_o, f_norm, f_w_gu, f_w_down, final_norm):
    b, s, d = x.shape
    nq = N_HEADS * HEAD_DIM
    h = _mixer_a(x, a_norm[0], a_w_in[0].astype(BF16), a_conv[0], a_w_out[0].astype(BF16))
    h = _ffn(h.reshape(b * s, d), f_norm[0], f_w_gu[0].astype(BF16), f_w_down[0].astype(BF16), final_norm)
    h = h.reshape(b, s, d)

    kvc, kas, vts, kaw, vtw, q, gates = _proj(h, kv_norm, b_norm[0], w_kv.astype(BF16), _qg_weight(b_w_qg[0]),
                                              _q_constants())
    pe2 = jnp.stack([cmp_pe_k, cmp_pe_v]).reshape(2, 1, L_CMP * HEAD_DIM)
    pe2 = jnp.broadcast_to(pe2, (2, 8, L_CMP * HEAD_DIM)).astype(BF16)
    w2 = jnp.pad(jnp.stack([cmp_w2_k, cmp_w2_v]), ((0, 0), (0, 0), (0, LANES - HEAD_DIM))).astype(BF16)
    kca, vct = _compress(kvc, pe2, jnp.stack([cmp_w1_k, cmp_w1_v]).astype(BF16), w2)

    oc, negm, used = _nsa_cmp(q, kca, vct, gates)
    lists, counts = _super_lists(used.reshape(used.shape[0], MAX_SLC_BLOCKS))
    o = _nsa_slc(lists.reshape(-1), counts[0], q, kas, vts, kaw, vtw, negm, gates, oc)

    out = _ffn(h.reshape(b * s, d), f_norm[1], f_w_gu[1].astype(BF16), f_w_down[1].astype(BF16), final_norm,
               o=o.reshape(b * s, nq), w_o=b_w_o[0].astype(BF16), with_final=True)
    return out.reshape(b, s, d)
```

```python
import functools
import math

import jax
import jax.numpy as jnp
from jax import lax
from jax.experimental import pallas as pl
from jax.experimental.pallas import tpu as pltpu

F32 = jnp.float32
BF16 = jnp.bfloat16

N_HEADS = 16
N_KV_GROUPS = 4
HEADS_PER_GROUP = N_HEADS // N_KV_GROUPS
HEAD_DIM = 64
L_CMP = 32
D_CMP = 16
L_SLC = 64
N_SEL = 16
N_FORCED = 3
WINDOW = 512
N_KV_SETS = 6
CONV_WIDTH = 3
RMS_EPS = 1e-5
NEG_INF = -1e30
PICKED = -3.0e38
LOG2E = math.log2(math.e)
LANES = 128
MAX_SLC_BLOCKS = LANES
SUPER_KEYS = LANES
SUPER_SHIFT = (SUPER_KEYS // L_SLC).bit_length() - 1
MAX_SUPER = MAX_SLC_BLOCKS * L_SLC // SUPER_KEYS
CHUNK_SUPERS = 4
FINAL_CHUNKS = 3
CMP_ROW_TILE = 128
VT_ROWS = HEAD_DIM + 16
FAR_POS = 1 << 30
POS_LANE = HEAD_DIM
N_SPLIT = 3
VMEM_LIMIT_BYTES = 56 * 1024 * 1024


def _rms(x, g):
    return x * lax.rsqrt(jnp.mean(x * x, axis=-1, keepdims=True) + RMS_EPS) * g


def _dot(a, b):
    return jnp.dot(a, b, preferred_element_type=F32)


def _dot_nt(a, b):
    return lax.dot_general(a, b, (((1,), (1,)), ((), ())), preferred_element_type=F32)


def _params(*sem, flags=None):
    return pltpu.CompilerParams(dimension_semantics=sem, vmem_limit_bytes=VMEM_LIMIT_BYTES, flags=flags)


def _pos_columns(coarse, fine, lane):
    is_c = (lane >= POS_LANE) & (lane < POS_LANE + N_SPLIT)
    is_f = (lane >= POS_LANE + N_SPLIT) & (lane < POS_LANE + 2 * N_SPLIT)
    return jnp.where(is_c, coarse, jnp.where(is_f, fine, 0.0))


def _mixer_a_kernel(x_ref, g_ref, win_ref, conv_ref, wout_ref, o_ref, ext_ref, *, tm, d, tc):
    @pl.when(pl.program_id(1) == 0)
    def _():
        ext_ref[0:8, :] = jnp.zeros((8, d), F32)

    x = x_ref[0]
    xb = _rms(x, g_ref[...]).astype(BF16)
    y = x
    for c0 in range(0, d, tc):
        cols = slice(c0, c0 + tc)
        b_gate = _dot(xb, win_ref[:, c0:c0 + tc])
        cv = _dot(xb, win_ref[:, d + c0:d + c0 + tc]) * _dot(xb, win_ref[:, 2 * d + c0:2 * d + c0 + tc])
        ext_ref[8:8 + tm, cols] = cv
        cv1 = ext_ref[7:7 + tm, cols]
        cv2 = ext_ref[6:6 + tm, cols]
        u = conv_ref[0:1, cols] * cv2 + conv_ref[1:2, cols] * cv1 + conv_ref[2:3, cols] * cv
        ext_ref[0:8, cols] = cv[tm - 8:tm, :]
        y = y + _dot((b_gate * u).astype(BF16), wout_ref[cols, :])
    o_ref[0] = y


def _mixer_a(x, g, w_in, conv_w, w_out, tm=1024, tc=1024):
    b, s, d = x.shape
    assert s % tm == 0 and d % tc == 0
    return pl.pallas_call(
        functools.partial(_mixer_a_kernel, tm=tm, d=d, tc=tc),
        grid=(b, s // tm),
        in_specs=[
            pl.BlockSpec((1, tm, d), lambda i, j: (i, j, 0)),
            _resident((1, d)), _resident((d, 3 * d)), _resident((CONV_WIDTH, d)), _resident((d, d)),
        ],
        out_specs=pl.BlockSpec((1, tm, d), lambda i, j: (i, j, 0)),
        out_shape=jax.ShapeDtypeStruct((b, s, d), F32),
        scratch_shapes=[pltpu.VMEM((tm + 8, d), F32)],
        compiler_params=_params("arbitrary", "arbitrary"),
        name="mixer_a",
    )(x, g.reshape(1, d), w_in, conv_w, w_out)


def _ffn_kernel(*refs, with_proj, with_final, f, tf):
    if with_proj:
        h_ref, o_ref_in, wo_ref, g_ref, wgu_ref, wd_ref, fg_ref, out_ref = refs
    else:
        h_ref, g_ref, wgu_ref, wd_ref, fg_ref, out_ref = refs
    h = h_ref[...]
    if with_proj:
        h = h + _dot(o_ref_in[...], wo_ref[...])
    xb = _rms(h, g_ref[...]).astype(BF16)
    y = h
    for c in range(f // tf):
        gate = _dot(xb, wgu_ref[:, c * tf:(c + 1) * tf])
        up = _dot(xb, wgu_ref[:, f + c * tf:f + (c + 1) * tf])
        act = (gate * jax.nn.sigmoid(gate)) * up
        y = y + _dot(act.astype(BF16), wd_ref[c * tf:(c + 1) * tf, :])
    if with_final:
        y = _rms(y, fg_ref[...])
    out_ref[...] = y


def _resident(shape):
    return pl.BlockSpec(shape, lambda *_: (0,) * len(shape), pipeline_mode=pl.Buffered(1))


def _ffn(h, g, w_gu, w_down, final_g, o=None, w_o=None, with_final=False, tm=1024, tf=256):
    t, d = h.shape
    f = w_down.shape[0]
    assert t % tm == 0 and f % tf == 0
    with_proj = o is not None
    in_specs = [pl.BlockSpec((tm, d), lambda i: (i, 0))]
    args = [h]
    if with_proj:
        in_specs += [pl.BlockSpec((tm, o.shape[1]), lambda i: (i, 0)), _resident(w_o.shape)]
        args += [o, w_o]
    in_specs += [_resident((1, d)), _resident(w_gu.shape), _resident(w_down.shape), _resident((1, d))]
    args += [g.reshape(1, d), w_gu, w_down, final_g.reshape(1, d)]
    return pl.pallas_call(
        functools.partial(_ffn_kernel, with_proj=with_proj, with_final=with_final, f=f, tf=tf),
        grid=(t // tm,),
        in_specs=in_specs,
        out_specs=pl.BlockSpec((tm, d), lambda i: (i, 0)),
        out_shape=jax.ShapeDtypeStruct((t, d), F32),
        compiler_params=_params("arbitrary"),
        name="ffn_proj" if with_proj else "ffn",
    )(*args)


def _halves(x):
    return x, pltpu.roll(x, HEAD_DIM, axis=1)


def _proj_kernel(h_ref, gkv_ref, gq_ref, wkv_ref, wqg_ref, qc_ref,
                 kvc_ref, kas_ref, vts_ref, kaw_ref, vtw_ref, q_ref, gate_ref, rel_ref, *, tm):
    g, dh = N_KV_GROUPS, HEAD_DIM
    gw = g * dh
    s0 = pl.program_id(1) * tm
    x = h_ref[0]
    y = x * lax.rsqrt(jnp.mean(x * x, axis=-1, keepdims=True) + RMS_EPS)
    kv = _dot((y * gkv_ref[...]).astype(BF16), wkv_ref[...])
    low_m = lax.broadcasted_iota(jnp.int32, (tm // D_CMP, LANES), 1) < dh
    for st in range(2):
        for pair in range(g // 2):
            slab = rel_ref.at[st * (g // 2) + pair]
            slab[...] = kv[:, st * gw + pair * LANES:st * gw + (pair + 1) * LANES]
            for j in range(D_CMP // 2):
                a = slab[pl.ds(2 * j, tm // D_CMP, stride=D_CMP), :]
                b = slab[pl.ds(2 * j + 1, tm // D_CMP, stride=D_CMP), :]
                cols = slice(j * LANES, (j + 1) * LANES)
                kvc_ref[st, 0, 2 * pair, :, cols] = jnp.where(low_m, a, pltpu.roll(b, dh, axis=1)).astype(BF16)
                kvc_ref[st, 0, 2 * pair + 1, :, cols] = jnp.where(low_m, pltpu.roll(a, dh, axis=1), b).astype(BF16)

    kpos = s0 + lax.broadcasted_iota(jnp.int32, (tm, LANES), 0)
    lane = lax.broadcasted_iota(jnp.int32, (tm, LANES), 1)
    low = lane < dh
    kblk = jnp.right_shift(kpos, L_SLC.bit_length() - 1)
    posc = _pos_columns(kblk.astype(F32), jnp.bitwise_and(kpos, L_SLC - 1).astype(F32), lane)
    onehot = (lane == kblk).astype(BF16)
    ones_rows = (lax.broadcasted_iota(jnp.int32, (VT_ROWS - dh, tm), 0) == 0).astype(BF16)
    for pair in range(g // 2):
        for st, k_ref, vt_ref in ((2, kas_ref, vts_ref), (4, kaw_ref, vtw_ref)):
            k_pair = kv[:, st * gw + pair * LANES:st * gw + (pair + 1) * LANES]
            vt = kv[:, (st + 1) * gw + pair * LANES:(st + 1) * gw + (pair + 1) * LANES].T
            for half, kh in enumerate(_halves(k_pair)):
                grp = 2 * pair + half
                k_ref[0, grp, :, 0:LANES] = jnp.where(low, kh, posc).astype(BF16)
                if st == 2:
                    k_ref[0, grp, :, LANES:2 * LANES] = onehot
                vt_ref[0, grp, 0:dh] = vt[half * dh:(half + 1) * dh].astype(BF16)
                vt_ref[0, grp, dh:VT_ROWS] = ones_rows

    qg = _dot((y * gq_ref[...]).astype(BF16), wqg_ref[...])
    for pair in range(N_HEADS // 2):
        q_pair = qg[:, pair * LANES:(pair + 1) * LANES] * (LOG2E * dh ** -0.5)
        for half, qh in enumerate(_halves(q_pair)):
            hd = 2 * pair + half
            q_ref[0, hd] = jnp.where(low, qh, qc_ref[hd:hd + 1, :]).astype(BF16)
    nq = N_HEADS * dh
    for grp in range(g):
        gate_ref[0, grp] = jax.nn.sigmoid(qg[:, nq + grp * LANES: nq + (grp + 1) * LANES])


def _proj(h, g_kv, g_q, w_kv, w_qg, qconst, tm=1024):
    b, s, d = h.shape
    assert s % tm == 0 and s // L_SLC <= MAX_SLC_BLOCKS
    g, dh = N_KV_GROUPS, HEAD_DIM
    return pl.pallas_call(
        functools.partial(_proj_kernel, tm=tm),
        grid=(b, s // tm),
        in_specs=[
            pl.BlockSpec((1, tm, d), lambda i, j: (i, j, 0)),
            _resident((1, d)), _resident((1, d)), _resident(w_kv.shape), _resident(w_qg.shape),
            _resident(qconst.shape),
        ],
        out_specs=[
            pl.BlockSpec((2, 1, g, tm // D_CMP, D_CMP * dh), lambda i, j: (0, i, 0, j, 0)),
            pl.BlockSpec((1, g, tm, 2 * LANES), lambda i, j: (i, 0, j, 0)),
            pl.BlockSpec((1, g, VT_ROWS, tm), lambda i, j: (i, 0, 0, j)),
            pl.BlockSpec((1, g, tm, LANES), lambda i, j: (i, 0, j, 0)),
            pl.BlockSpec((1, g, VT_ROWS, tm), lambda i, j: (i, 0, 0, j)),
            pl.BlockSpec((1, N_HEADS, tm, LANES), lambda i, j: (i, 0, j, 0)),
            pl.BlockSpec((1, g, tm, LANES), lambda i, j: (i, 0, j, 0)),
        ],
        out_shape=[
            jax.ShapeDtypeStruct((2, b, g, s // D_CMP, D_CMP * dh), BF16),
            jax.ShapeDtypeStruct((b, g, s, 2 * LANES), BF16),
            jax.ShapeDtypeStruct((b, g, VT_ROWS, s), BF16),
            jax.ShapeDtypeStruct((b, g, s, LANES), BF16),
            jax.ShapeDtypeStruct((b, g, VT_ROWS, s), BF16),
            jax.ShapeDtypeStruct((b, N_HEADS, s, LANES), BF16),
            jax.ShapeDtypeStruct((b, g, s, LANES), F32),
        ],
        scratch_shapes=[pltpu.VMEM((2 * (g // 2), tm, LANES), F32)],
        compiler_params=_params("arbitrary", "arbitrary"),
        name="proj",
    )(h, g_kv.reshape(1, d), g_q.reshape(1, d), w_kv, w_qg, qconst)


def _compress_kernel(r_ref, pe_ref, w1_ref, w2_ref, kca_ref, vct_ref, *, m):
    half = D_CMP * HEAD_DIM
    row = lax.broadcasted_iota(jnp.int32, (m, LANES), 0)
    lane = lax.broadcasted_iota(jnp.int32, (m, LANES), 1)

    def mlp(i):
        r = r_ref[i, 0]
        top = _dot(r, w1_ref[i, 0:half, :])
        bot = _dot(r, w1_ref[i, half:2 * half, :])
        pe_term = _dot(pe_ref[i], w1_ref[i])[0:1, :]
        hid = top + pltpu.roll(bot, m - 1, axis=0) + pe_term
        hid = hid * jax.nn.sigmoid(hid)
        out = _dot(hid.astype(BF16), w2_ref[i])
        return jnp.where(row < m - 1, out, 0.0)

    coarse = jnp.right_shift(row, 2).astype(F32)
    fine = jnp.bitwise_and(row, 3).astype(F32) * D_CMP + (L_CMP - 1) / 2.0
    kca_ref[0] = (mlp(0) + _pos_columns(coarse, fine, lane)).astype(BF16)
    vct_ref[0] = mlp(1).T[0:HEAD_DIM].astype(BF16)


def _compress(kvc, pe2, w1_2, w2_2):
    _, b, g, m, _ = kvc.shape
    dh = HEAD_DIM
    r = kvc.reshape(2, b * g, m, D_CMP * dh)
    hid = w1_2.shape[-1]
    return pl.pallas_call(
        functools.partial(_compress_kernel, m=m),
        grid=(b * g,),
        in_specs=[
            pl.BlockSpec((2, 1, m, D_CMP * dh), lambda j: (0, j, 0, 0)),
            pl.BlockSpec((2, 8, L_CMP * dh), lambda j: (0, 0, 0)),
            pl.BlockSpec((2, L_CMP * dh, hid), lambda j: (0, 0, 0)),
            pl.BlockSpec((2, hid, LANES), lambda j: (0, 0, 0)),
        ],
        out_specs=[
            pl.BlockSpec((1, m, LANES), lambda j: (j, 0, 0)),
            pl.BlockSpec((1, dh, m), lambda j: (j, 0, 0)),
        ],
        out_shape=[
            jax.ShapeDtypeStruct((b * g, m, LANES), BF16),
            jax.ShapeDtypeStruct((b * g, dh, m), BF16),
        ],
        compiler_params=_params("arbitrary"),
        name="compress",
    )(r, pe2, w1_2, w2_2)


def _head_stack(ot, gt, branch, tq):
    return jnp.concatenate(
        [ot[:, h * tq:(h + 1) * tq] * gt[3 * h + branch:3 * h + branch + 1, :] for h in range(HEADS_PER_GROUP)],
        axis=0)


def _nsa_cmp_kernel(q_ref, kca_ref, vct_ref, gate_ref, oc_ref, negm_ref, used_ref, *, tq, m):
    q0 = pl.program_id(2) * tq
    hpg = HEADS_PER_GROUP
    ncol = hpg * tq
    q = q_ref[0].reshape(ncol, LANES)
    tcol = q0 + jnp.bitwise_and(lax.broadcasted_iota(jnp.int32, (1, ncol), 1), tq - 1)

    def attend(rows):
        st = _dot_nt(kca_ref[0, 0:rows, :], q)
        nrow = lax.broadcasted_iota(jnp.int32, (rows, 1), 0)
        vis = (nrow * D_CMP + (L_CMP - 1)) <= tcol
        st = jnp.where(vis, st, NEG_INF)
        e = jnp.exp2(st - jnp.max(st, axis=0, keepdims=True))
        rcp = jnp.where(tcol >= L_CMP - 1, 1.0 / jnp.sum(e, axis=0, keepdims=True), 0.0)
        p = e * rcp
        ot = _dot(vct_ref[0, :, 0:rows], p.astype(BF16))
        pt = p[:, 0:tq]
        for h in range(1, hpg):
            pt = pt + p[:, h * tq:(h + 1) * tq]
        nblk = min(MAX_SLC_BLOCKS, rows * D_CMP // L_SLC)
        jj = lax.broadcasted_iota(jnp.int32, (nblk, rows), 0) * L_SLC
        nn = lax.broadcasted_iota(jnp.int32, (nblk, rows), 1) * D_CMP
        overlap_t = ((nn < jj + L_SLC) & (nn + L_CMP > jj)).astype(BF16)
        p_hi = pt.astype(BF16)
        p_lo = (pt - p_hi.astype(F32)).astype(BF16)
        imp = _dot(overlap_t, p_hi) + _dot(overlap_t, p_lo)

        jrow = lax.broadcasted_iota(jnp.int32, (nblk, tq), 0)
        cur = jnp.right_shift(q0 + lax.broadcasted_iota(jnp.int32, (1, tq), 1), L_SLC.bit_length() - 1)
        valid = jrow <= cur
        forced = (jrow == 0) | (jrow == cur) | (jrow == cur - 1)
        val = jnp.where(valid & jnp.logical_not(forced), imp, NEG_INF)
        jrow_f = jrow.astype(F32)
        for _ in range(N_SEL - N_FORCED):
            mx = jnp.max(val, axis=0, keepdims=True)
            first = jnp.min(jnp.where(val == mx, jrow_f, float(MAX_SLC_BLOCKS)), axis=0, keepdims=True)
            val = jnp.where(jrow_f == first, PICKED, val)
        sel = jnp.where(valid & (forced | (val == PICKED)), 1.0, 0.0)
        if nblk < MAX_SLC_BLOCKS:
            sel = jnp.concatenate([sel, jnp.zeros((MAX_SLC_BLOCKS - nblk, tq), F32)], axis=0)
        gt = gate_ref[0, 0].T
        oc_ref[0] = _head_stack(ot, gt, 0, tq).T
        sel_t = sel.T
        negm_ref[0, 0] = jnp.where(sel_t > 0.5, 0.0, NEG_INF).astype(BF16)
        for r in range(tq // SUPER_KEYS):
            used_ref[r] = jnp.max(sel_t[r * SUPER_KEYS:(r + 1) * SUPER_KEYS], axis=0, keepdims=True)

    row_tile = min(m, CMP_ROW_TILE)
    n_variants = m // row_tile
    if n_variants == 1:
        attend(m)
    else:
        needed = pl.program_id(2) * (tq // D_CMP) + (tq // D_CMP - 1)
        variant = jnp.minimum(jnp.right_shift(needed - 1, row_tile.bit_length() - 1), n_variants - 1)
        for v in range(n_variants):
            pl.when(variant == v)(functools.partial(attend, (v + 1) * row_tile))


def _nsa_cmp(q, kca, vct, gates, tq=512):
    b, nh, s, _ = q.shape
    g, hpg, dh = N_KV_GROUPS, HEADS_PER_GROUP, HEAD_DIM
    m = kca.shape[1]
    assert s % tq == 0 and tq % SUPER_KEYS == 0 and tq & (tq - 1) == 0
    nqt = s // tq
    sub = tq // SUPER_KEYS
    return pl.pallas_call(
        functools.partial(_nsa_cmp_kernel, tq=tq, m=m),
        grid=(b, g, s // tq),
        in_specs=[
            pl.BlockSpec((1, hpg, tq, LANES), lambda i, j, k: (i, j, k, 0)),
            pl.BlockSpec((1, m, LANES), lambda i, j, k: (i * N_KV_GROUPS + j, 0, 0)),
            pl.BlockSpec((1, dh, m), lambda i, j, k: (i * N_KV_GROUPS + j, 0, 0)),
            pl.BlockSpec((1, 1, tq, LANES), lambda i, j, k: (i, j, k, 0)),
        ],
        out_specs=[
            pl.BlockSpec((1, tq, hpg * dh), lambda i, j, k: (i, k, j)),
            pl.BlockSpec((1, 1, tq, MAX_SLC_BLOCKS), lambda i, j, k: (i, j, k, 0)),
            pl.BlockSpec((sub, 1, MAX_SLC_BLOCKS), lambda i, j, k: ((i * N_KV_GROUPS + j) * nqt + k, 0, 0)),
        ],
        out_shape=[
            jax.ShapeDtypeStruct((b, s, nh * dh), F32),
            jax.ShapeDtypeStruct((b, g, s, MAX_SLC_BLOCKS), BF16),
            jax.ShapeDtypeStruct((b * g * nqt * sub, 1, MAX_SLC_BLOCKS), F32),
        ],
        compiler_params=_params("arbitrary", "arbitrary", "arbitrary"),
        name="nsa_cmp",
    )(q, kca, vct, gates)


def _super_lists_kernel(used_ref, lst_ref, cnt_ref, *, tb):
    ut = used_ref[...].T.astype(BF16)
    a_i = lax.broadcasted_iota(jnp.int32, (MAX_SLC_BLOCKS, MAX_SLC_BLOCKS), 0)
    b_i = lax.broadcasted_iota(jnp.int32, (MAX_SLC_BLOCKS, MAX_SLC_BLOCKS), 1)
    pair_t = (a_i == jnp.right_shift(b_i, SUPER_SHIFT)).astype(BF16)
    sbf = jnp.where(_dot(pair_t, ut) > 0.5, 1.0, 0.0).astype(BF16)
    incl = _dot((b_i <= a_i).astype(BF16), sbf)[0:MAX_SUPER]
    cnt_ref[...] = jnp.broadcast_to(incl[MAX_SUPER - 1:MAX_SUPER, :], (8, tb)).astype(jnp.int32)
    for i in range(MAX_SUPER):
        lst_ref[i:i + 1, :] = jnp.sum(jnp.where(incl <= float(i), 1.0, 0.0), axis=0, keepdims=True).astype(jnp.int32)


def _super_lists(used):
    t = used.shape[0]
    tb = min(t, LANES)
    assert t % tb == 0
    return pl.pallas_call(
        functools.partial(_super_lists_kernel, tb=tb),
        grid=(t // tb,),
        in_specs=[pl.BlockSpec((tb, MAX_SLC_BLOCKS), lambda i: (i, 0))],
        out_specs=[
            pl.BlockSpec((MAX_SUPER, tb), lambda i: (0, i)),
            pl.BlockSpec((8, tb), lambda i: (0, i)),
        ],
        out_shape=[
            jax.ShapeDtypeStruct((MAX_SUPER, t), jnp.int32),
            jax.ShapeDtypeStruct((8, t), jnp.int32),
        ],
        compiler_params=_params("arbitrary"),
        name="super_lists",
    )(used)


def _nsa_slc_kernel(lst_ref, cnt_ref, q_ref, kas_ref, vts_ref, kaw_ref, vtw_ref, negm_ref, gate_ref, oc_ref, o_ref,
                    m_ref, acc_ref, *, tq, sub):
    hpg = HEADS_PER_GROUP
    ncol = hpg * tq
    tk = CHUNK_SUPERS * SUPER_KEYS
    n_win = WINDOW // SUPER_KEYS + 1
    n_tiles = pl.num_programs(0) * pl.num_programs(1) * pl.num_programs(2) * sub
    tile0 = ((pl.program_id(0) * pl.num_programs(1) + pl.program_id(1)) * pl.num_programs(2) + pl.program_id(2)) * sub
    row = lax.broadcasted_iota(jnp.int32, (SUPER_KEYS, 1), 0)

    class Tile:
        def __init__(self, s):
            self.s = s
            self.rows = slice(s * tq, (s + 1) * tq)
            self.qt = pl.program_id(2) * sub + s
            self.q = q_ref[0, :, self.rows, :].reshape(ncol, LANES)
            negm = negm_ref[0, 0, self.rows, :]
            self.qa = jnp.concatenate([self.q, jnp.concatenate([negm] * hpg, axis=0)], axis=1)
            self.tcol = self.qt * tq + jnp.bitwise_and(lax.broadcasted_iota(jnp.int32, (1, ncol), 1), tq - 1)
            self.tile = tile0 + s
            self.n_used = cnt_ref[self.tile]
            self.n_chunks = jnp.right_shift(self.n_used + CHUNK_SUPERS - 1, CHUNK_SUPERS.bit_length() - 1)
            m_ref[s] = jnp.full((1, ncol), NEG_INF, F32)
            acc_ref[s] = jnp.zeros((VT_ROWS, ncol), F32)

        def gather(self, chunk, partial):
            ks, vs, kbase = [], [], []
            for j in range(CHUNK_SUPERS):
                slot = chunk * CHUNK_SUPERS + j
                if partial:
                    ok = (slot >= 0) & (slot < self.n_used)
                    sb = jnp.where(ok, lst_ref[jnp.clip(slot, 0, MAX_SUPER - 1) * n_tiles + self.tile], 0)
                    kbase.append(jnp.where(ok, sb * SUPER_KEYS, FAR_POS))
                else:
                    sb = lst_ref[slot * n_tiles + self.tile]
                start = pl.multiple_of(sb * SUPER_KEYS, SUPER_KEYS)
                ks.append(kas_ref[0, 0, pl.ds(start, SUPER_KEYS), :])
                vs.append(vts_ref[0, 0, :, pl.ds(start, SUPER_KEYS)])
            return jnp.concatenate(ks, axis=0), jnp.concatenate(vs, axis=1), kbase

        def scores(self, kt, kbase):
            st = _dot_nt(kt, self.qa)
            if kbase:
                r = lax.broadcasted_iota(jnp.int32, (tk, 1), 0)
                rs = jnp.right_shift(r, SUPER_KEYS.bit_length() - 1)
                kpos = jnp.full((tk, 1), kbase[0], jnp.int32)
                for j in range(1, CHUNK_SUPERS):
                    kpos = jnp.where(rs == j, kbase[j], kpos)
                kpos = kpos + jnp.bitwise_and(r, SUPER_KEYS - 1)
                st = jnp.where(kpos <= self.tcol, st, NEG_INF)
            return st

        def update(self, st, vt, present=None):
            m_old = m_ref[self.s]
            m_new = jnp.maximum(m_old, jnp.max(st, axis=0, keepdims=True))
            if present is not None:
                m_new = jnp.where(present, m_new, m_old)
            p = jnp.exp2(st - m_new)
            acc_new = jnp.exp2(m_old - m_new) * acc_ref[self.s] + _dot(vt, p.astype(BF16))
            acc_ref[self.s] = acc_new if present is None else jnp.where(present, acc_new, acc_ref[self.s])
            m_ref[self.s] = m_new

        def early_chunks(self):
            def body(c, carry):
                kt, vt, _ = self.gather(c, False)
                self.update(self.scores(kt, None), vt)
                return carry

            lax.fori_loop(0, jnp.maximum(self.n_chunks - FINAL_CHUNKS, 0), body, 0)

        def final_scores(self):
            self.finals = []
            for back in range(1, FINAL_CHUNKS + 1):
                kt, vt, kbase = self.gather(self.n_chunks - back, True)
                self.finals.append((self.scores(kt, kbase if back == 1 else None), vt,
                                    None if back == 1 else self.n_chunks >= back))
            win_scores, win_v = [], []
            for r in range(n_win):
                sb = self.qt - (n_win - 1) + r
                start = pl.multiple_of(jnp.maximum(sb, 0) * SUPER_KEYS, SUPER_KEYS)
                sw = _dot_nt(kaw_ref[0, 0, pl.ds(start, SUPER_KEYS), :], self.q)
                if r == n_win - 1:
                    sw = jnp.where(sb * SUPER_KEYS + row <= self.tcol, sw, NEG_INF)
                elif r == 0:
                    lower = jnp.where(sb >= 0, sb * SUPER_KEYS + WINDOW, -FAR_POS) + row
                    sw = jnp.where(lower > self.tcol, sw, NEG_INF)
                else:
                    sw = sw + jnp.where(sb >= 0, 0.0, NEG_INF)
                win_scores.append(sw)
                win_v.append(vtw_ref[0, 0, :, pl.ds(start, SUPER_KEYS)])
            self.stw = jnp.concatenate(win_scores, axis=0)
            self.vw = jnp.concatenate(win_v, axis=1)

        def finish(self):
            for st, vt, present in self.finals:
                self.update(st, vt, present)
            acc = acc_ref[self.s]
            o_slc_t = acc[0:HEAD_DIM] * (1.0 / acc[HEAD_DIM:HEAD_DIM + 1])
            e = jnp.exp2(self.stw - jnp.max(self.stw, axis=0, keepdims=True))
            acc_w = _dot(self.vw, e.astype(BF16))
            o_win_t = acc_w[0:HEAD_DIM] * (1.0 / acc_w[HEAD_DIM:HEAD_DIM + 1])
            gt = gate_ref[0, 0, self.rows, :].T
            comb = _head_stack(o_slc_t, gt, 1, tq) + _head_stack(o_win_t, gt, 2, tq)
            o_ref[0, self.rows, :] = (oc_ref[0, self.rows, :] + comb.T).astype(BF16)

    tiles = [Tile(s) for s in range(sub)]
    for t in tiles:
        t.early_chunks()
    for t in tiles:
        t.final_scores()
    for t in tiles:
        t.finish()


def _nsa_slc(lists, counts, q, kas, vts, kaw, vtw, negm, gates, oc, tq=128, sub=4):
    b, nh, s, _ = q.shape
    g, hpg, dh = N_KV_GROUPS, HEADS_PER_GROUP, HEAD_DIM
    ts = tq * sub
    assert tq == SUPER_KEYS and WINDOW % tq == 0 and s >= WINDOW + tq and s % ts == 0
    grid_spec = pltpu.PrefetchScalarGridSpec(
        num_scalar_prefetch=2,
        grid=(b, g, s // ts),
        in_specs=[
            pl.BlockSpec((1, hpg, ts, LANES), lambda i, j, k, *_: (i, j, k, 0)),
            pl.BlockSpec((1, 1, s, 2 * LANES), lambda i, j, k, *_: (i, j, 0, 0)),
            pl.BlockSpec((1, 1, VT_ROWS, s), lambda i, j, k, *_: (i, j, 0, 0)),
            pl.BlockSpec((1, 1, s, LANES), lambda i, j, k, *_: (i, j, 0, 0)),
            pl.BlockSpec((1, 1, VT_ROWS, s), lambda i, j, k, *_: (i, j, 0, 0)),
            pl.BlockSpec((1, 1, ts, MAX_SLC_BLOCKS), lambda i, j, k, *_: (i, j, k, 0)),
            pl.BlockSpec((1, 1, ts, LANES), lambda i, j, k, *_: (i, j, k, 0)),
            pl.BlockSpec((1, ts, hpg * dh), lambda i, j, k, *_: (i, k, j)),
        ],
        out_specs=pl.BlockSpec((1, ts, hpg * dh), lambda i, j, k, *_: (i, k, j)),
        scratch_shapes=[
            pltpu.VMEM((sub, 1, hpg * tq), F32),
            pltpu.VMEM((sub, VT_ROWS, hpg * tq), F32),
        ],
    )
    return pl.pallas_call(
        functools.partial(_nsa_slc_kernel, tq=tq, sub=sub),
        grid_spec=grid_spec,
        out_shape=jax.ShapeDtypeStruct((b, s, nh * dh), BF16),
        compiler_params=_params("arbitrary", "arbitrary", "arbitrary"),
        name="nsa_slc",
    )(lists, counts, q, kas, vts, kaw, vtw, negm, gates, oc)


def _split_bf16(x, n):
    pieces = []
    for _ in range(n):
        p = x.astype(BF16).astype(F32)
        pieces.append(p)
        x = x - p
    return pieces


def _q_constants():
    slopes = jnp.exp2(-8.0 * jnp.arange(1, N_HEADS + 1, dtype=F32) / N_HEADS) * LOG2E
    pieces = jnp.stack(_split_bf16(slopes, N_SPLIT), axis=1)
    qc = jnp.zeros((N_HEADS, LANES), F32)
    qc = qc.at[:, POS_LANE:POS_LANE + N_SPLIT].set(pieces * L_SLC)
    qc = qc.at[:, POS_LANE + N_SPLIT:POS_LANE + 2 * N_SPLIT].set(pieces)
    return qc


def _qg_weight(w_qg):
    d = w_qg.shape[0]
    nq = N_HEADS * HEAD_DIM
    wg = jnp.pad(w_qg[:, nq:].reshape(d, N_KV_GROUPS, 3 * HEADS_PER_GROUP),
                 ((0, 0), (0, 0), (0, LANES - 3 * HEADS_PER_GROUP)))
    return jnp.concatenate([w_qg[:, :nq], wg.reshape(d, N_KV_GROUPS * LANES)], axis=1).astype(BF16)


def kernel(x, a_norm, a_w_in, a_conv, a_w_out, kv_norm, w_kv, cmp_pe_k, cmp_w1_k, cmp_w2_k, cmp_pe_v, cmp_w1_v,
           cmp_w2_v, b_norm, b_w_qg, b_w_o, f_norm, f_w_gu, f_w_down, final_norm):
    b, s, d = x.shape
    nq = N_HEADS * HEAD_DIM
    h = _mixer_a(x, a_norm[0], a_w_in[0].astype(BF16), a_conv[0], a_w_out[0].astype(BF16))
    h = _ffn(h.reshape(b * s, d), f_norm[0], f_w_gu[0].astype(BF16), f_w_down[0].astype(BF16), final_norm)
    h = h.reshape(b, s, d)

    kvc, kas, vts, kaw, vtw, q, gates = _proj(h, kv_norm, b_norm[0], w_kv.astype(BF16), _qg_weight(b_w_qg[0]),
                                              _q_constants())
    pe2 = jnp.stack([cmp_pe_k, cmp_pe_v]).reshape(2, 1, L_CMP * HEAD_DIM)
    pe2 = jnp.broadcast_to(pe2, (2, 8, L_CMP * HEAD_DIM)).astype(BF16)
    w2 = jnp.pad(jnp.stack([cmp_w2_k, cmp_w2_v]), ((0, 0), (0, 0), (0, LANES - HEAD_DIM))).astype(BF16)
    kca, vct = _compress(kvc, pe2, jnp.stack([cmp_w1_k, cmp_w1_v]).astype(BF16), w2)

    oc, negm, used = _nsa_cmp(q, kca, vct, gates)
    lists, counts = _super_lists(used.reshape(used.shape[0], MAX_SLC_BLOCKS))
    o = _nsa_slc(lists.reshape(-1), counts[0], q, kas, vts, kaw, vtw, negm, gates, oc)

    out = _ffn(h.reshape(b * s, d), f_norm[1], f_w_gu[1].astype(BF16), f_w_down[1].astype(BF16), final_norm,
               o=o.reshape(b * s, nq), w_o=b_w_o[0].astype(BF16), with_final=True)
    return out.reshape(b, s, d)
```

```python
import functools
import math

import jax
import jax.numpy as jnp
from jax import lax
from jax.experimental import pallas as pl
from jax.experimental.pallas import tpu as pltpu

F32 = jnp.float32
BF16 = jnp.bfloat16

N_HEADS = 16
N_KV_GROUPS = 4
HEADS_PER_GROUP = N_HEADS // N_KV_GROUPS
HEAD_DIM = 64
L_CMP = 32
D_CMP = 16
L_SLC = 64
N_SEL = 16
N_FORCED = 3
WINDOW = 512
N_KV_SETS = 6
CONV_WIDTH = 3
RMS_EPS = 1e-5
NEG_INF = -1e30
PICKED = -3.0e38
LOG2E = math.log2(math.e)
LANES = 128
MAX_SLC_BLOCKS = LANES
SUPER_KEYS = LANES
SUPER_SHIFT = (SUPER_KEYS // L_SLC).bit_length() - 1
MAX_SUPER = MAX_SLC_BLOCKS * L_SLC // SUPER_KEYS
CHUNK_SUPERS = 4
FINAL_SLOTS = 10
CMP_ROW_TILE = 128
VT_ROWS = HEAD_DIM + 16
POS_LANE = HEAD_DIM
N_SPLIT = 3
PAD_LANE = POS_LANE + 2 * N_SPLIT
VMEM_LIMIT_BYTES = 56 * 1024 * 1024


def _rms(x, g):
    return x * lax.rsqrt(jnp.mean(x * x, axis=-1, keepdims=True) + RMS_EPS) * g


def _dot(a, b):
    return jnp.dot(a, b, preferred_element_type=F32)


def _dot_nt(a, b):
    return lax.dot_general(a, b, (((1,), (1,)), ((), ())), preferred_element_type=F32)


def _params(*sem, flags=None):
    return pltpu.CompilerParams(dimension_semantics=sem, vmem_limit_bytes=VMEM_LIMIT_BYTES, flags=flags)


def _pos_columns(coarse, fine, lane):
    is_c = (lane >= POS_LANE) & (lane < POS_LANE + N_SPLIT)
    is_f = (lane >= POS_LANE + N_SPLIT) & (lane < POS_LANE + 2 * N_SPLIT)
    return jnp.where(is_c, coarse, jnp.where(is_f, fine, 0.0))


def _mixer_a_kernel(x_ref, g_ref, win_ref, conv_ref, wout_ref, o_ref, ext_ref, *, tm, d, tc):
    @pl.when(pl.program_id(1) == 0)
    def _():
        ext_ref[0:8, :] = jnp.zeros((8, d), F32)

    x = x_ref[0]
    xb = _rms(x, g_ref[...]).astype(BF16)
    y = x
    for c0 in range(0, d, tc):
        cols = slice(c0, c0 + tc)
        b_gate = _dot(xb, win_ref[:, c0:c0 + tc])
        cv = _dot(xb, win_ref[:, d + c0:d + c0 + tc]) * _dot(xb, win_ref[:, 2 * d + c0:2 * d + c0 + tc])
        ext_ref[8:8 + tm, cols] = cv
        cv1 = ext_ref[7:7 + tm, cols]
        cv2 = ext_ref[6:6 + tm, cols]
        u = conv_ref[0:1, cols] * cv2 + conv_ref[1:2, cols] * cv1 + conv_ref[2:3, cols] * cv
        ext_ref[0:8, cols] = cv[tm - 8:tm, :]
        y = y + _dot((b_gate * u).astype(BF16), wout_ref[cols, :])
    o_ref[0] = y


def _mixer_a(x, g, w_in, conv_w, w_out, tm=1024, tc=1024):
    b, s, d = x.shape
    assert s % tm == 0 and d % tc == 0
    return pl.pallas_call(
        functools.partial(_mixer_a_kernel, tm=tm, d=d, tc=tc),
        grid=(b, s // tm),
        in_specs=[
            pl.BlockSpec((1, tm, d), lambda i, j: (i, j, 0)),
            _resident((1, d)), _resident((d, 3 * d)), _resident((CONV_WIDTH, d)), _resident((d, d)),
        ],
        out_specs=pl.BlockSpec((1, tm, d), lambda i, j: (i, j, 0)),
        out_shape=jax.ShapeDtypeStruct((b, s, d), F32),
        scratch_shapes=[pltpu.VMEM((tm + 8, d), F32)],
        compiler_params=_params("arbitrary", "arbitrary"),
        name="mixer_a",
    )(x, g.reshape(1, d), w_in, conv_w, w_out)


def _ffn_kernel(*refs, with_proj, with_final, f, tf):
    if with_proj:
        h_ref, o_ref_in, wo_ref, g_ref, wgu_ref, wd_ref, fg_ref, out_ref = refs
    else:
        h_ref, g_ref, wgu_ref, wd_ref, fg_ref, out_ref = refs
    h = h_ref[...]
    if with_proj:
        h = h + _dot(o_ref_in[...], wo_ref[...])
    xb = _rms(h, g_ref[...]).astype(BF16)
    y = h
    for c in range(f // tf):
        gate = _dot(xb, wgu_ref[:, c * tf:(c + 1) * tf])
        up = _dot(xb, wgu_ref[:, f + c * tf:f + (c + 1) * tf])
        act = (gate * jax.nn.sigmoid(gate)) * up
        y = y + _dot(act.astype(BF16), wd_ref[c * tf:(c + 1) * tf, :])
    if with_final:
        y = _rms(y, fg_ref[...])
    out_ref[...] = y


def _resident(shape):
    return pl.BlockSpec(shape, lambda *_: (0,) * len(shape), pipeline_mode=pl.Buffered(1))


def _ffn(h, g, w_gu, w_down, final_g, o=None, w_o=None, with_final=False, tm=1024, tf=256):
    t, d = h.shape
    f = w_down.shape[0]
    assert t % tm == 0 and f % tf == 0
    with_proj = o is not None
    in_specs = [pl.BlockSpec((tm, d), lambda i: (i, 0))]
    args = [h]
    if with_proj:
        in_specs += [pl.BlockSpec((tm, o.shape[1]), lambda i: (i, 0)), _resident(w_o.shape)]
        args += [o, w_o]
    in_specs += [_resident((1, d)), _resident(w_gu.shape), _resident(w_down.shape), _resident((1, d))]
    args += [g.reshape(1, d), w_gu, w_down, final_g.reshape(1, d)]
    return pl.pallas_call(
        functools.partial(_ffn_kernel, with_proj=with_proj, with_final=with_final, f=f, tf=tf),
        grid=(t // tm,),
        in_specs=in_specs,
        out_specs=pl.BlockSpec((tm, d), lambda i: (i, 0)),
        out_shape=jax.ShapeDtypeStruct((t, d), F32),
        compiler_params=_params("arbitrary"),
        name="ffn_proj" if with_proj else "ffn",
    )(*args)


def _halves(x):
    return x, pltpu.roll(x, HEAD_DIM, axis=1)


def _proj_kernel(h_ref, gkv_ref, gq_ref, wkv_ref, wqg_ref, qc_ref,
                 kvc_ref, kas_ref, vts_ref, kaw_ref, vtw_ref, q_ref, gate_ref, rel_ref, *, tm):
    g, dh = N_KV_GROUPS, HEAD_DIM
    gw = g * dh
    s0 = pl.program_id(1) * tm
    x = h_ref[0]
    y = x * lax.rsqrt(jnp.mean(x * x, axis=-1, keepdims=True) + RMS_EPS)
    kv = _dot((y * gkv_ref[...]).astype(BF16), wkv_ref[...])
    low_m = lax.broadcasted_iota(jnp.int32, (tm // D_CMP, LANES), 1) < dh
    for st in range(2):
        for pair in range(g // 2):
            slab = rel_ref.at[st * (g // 2) + pair]
            slab[...] = kv[:, st * gw + pair * LANES:st * gw + (pair + 1) * LANES]
            for j in range(D_CMP // 2):
                a = slab[pl.ds(2 * j, tm // D_CMP, stride=D_CMP), :]
                b = slab[pl.ds(2 * j + 1, tm // D_CMP, stride=D_CMP), :]
                cols = slice(j * LANES, (j + 1) * LANES)
                kvc_ref[st, 0, 2 * pair, :, cols] = jnp.where(low_m, a, pltpu.roll(b, dh, axis=1)).astype(BF16)
                kvc_ref[st, 0, 2 * pair + 1, :, cols] = jnp.where(low_m, pltpu.roll(a, dh, axis=1), b).astype(BF16)

    kpos = s0 + lax.broadcasted_iota(jnp.int32, (tm, LANES), 0)
    lane = lax.broadcasted_iota(jnp.int32, (tm, LANES), 1)
    low = lane < dh
    kblk = jnp.right_shift(kpos, L_SLC.bit_length() - 1)
    posc = _pos_columns(kblk.astype(F32), jnp.bitwise_and(kpos, L_SLC - 1).astype(F32), lane)
    onehot = (lane == kblk).astype(BF16)
    ones_rows = (lax.broadcasted_iota(jnp.int32, (VT_ROWS - dh, tm), 0) == 0).astype(BF16)
    for pair in range(g // 2):
        for st, k_ref, vt_ref in ((2, kas_ref, vts_ref), (4, kaw_ref, vtw_ref)):
            k_pair = kv[:, st * gw + pair * LANES:st * gw + (pair + 1) * LANES]
            vt = kv[:, (st + 1) * gw + pair * LANES:(st + 1) * gw + (pair + 1) * LANES].T
            for half, kh in enumerate(_halves(k_pair)):
                grp = 2 * pair + half
                k_ref[0, grp, :, 0:LANES] = jnp.where(low, kh, posc).astype(BF16)
                if st == 2:
                    k_ref[0, grp, :, LANES:2 * LANES] = onehot
                vt_ref[0, grp, 0:dh] = vt[half * dh:(half + 1) * dh].astype(BF16)
                vt_ref[0, grp, dh:VT_ROWS] = ones_rows

    qg = _dot((y * gq_ref[...]).astype(BF16), wqg_ref[...])
    for pair in range(N_HEADS // 2):
        q_pair = qg[:, pair * LANES:(pair + 1) * LANES] * (LOG2E * dh ** -0.5)
        for half, qh in enumerate(_halves(q_pair)):
            hd = 2 * pair + half
            q_ref[0, hd] = jnp.where(low, qh, qc_ref[hd:hd + 1, :]).astype(BF16)
    nq = N_HEADS * dh
    for grp in range(g):
        gate_ref[0, grp] = jax.nn.sigmoid(qg[:, nq + grp * LANES: nq + (grp + 1) * LANES])


def _proj(h, g_kv, g_q, w_kv, w_qg, qconst, tm=1024):
    b, s, d = h.shape
    assert s % tm == 0 and s // L_SLC <= MAX_SLC_BLOCKS
    g, dh = N_KV_GROUPS, HEAD_DIM
    return pl.pallas_call(
        functools.partial(_proj_kernel, tm=tm),
        grid=(b, s // tm),
        in_specs=[
            pl.BlockSpec((1, tm, d), lambda i, j: (i, j, 0)),
            _resident((1, d)), _resident((1, d)), _resident(w_kv.shape), _resident(w_qg.shape),
            _resident(qconst.shape),
        ],
        out_specs=[
            pl.BlockSpec((2, 1, g, tm // D_CMP, D_CMP * dh), lambda i, j: (0, i, 0, j, 0)),
            pl.BlockSpec((1, g, tm, 2 * LANES), lambda i, j: (i, 0, j, 0)),
            pl.BlockSpec((1, g, VT_ROWS, tm), lambda i, j: (i, 0, 0, j)),
            pl.BlockSpec((1, g, tm, LANES), lambda i, j: (i, 0, j, 0)),
            pl.BlockSpec((1, g, VT_ROWS, tm), lambda i, j: (i, 0, 0, j)),
            pl.BlockSpec((1, N_HEADS, tm, LANES), lambda i, j: (i, 0, j, 0)),
            pl.BlockSpec((1, g, tm, LANES), lambda i, j: (i, 0, j, 0)),
        ],
        out_shape=[
            jax.ShapeDtypeStruct((2, b, g, s // D_CMP, D_CMP * dh), BF16),
            jax.ShapeDtypeStruct((b, g, s, 2 * LANES), BF16),
            jax.ShapeDtypeStruct((b, g, VT_ROWS, s), BF16),
            jax.ShapeDtypeStruct((b, g, s, LANES), BF16),
            jax.ShapeDtypeStruct((b, g, VT_ROWS, s), BF16),
            jax.ShapeDtypeStruct((b, N_HEADS, s, LANES), BF16),
            jax.ShapeDtypeStruct((b, g, s, LANES), F32),
        ],
        scratch_shapes=[pltpu.VMEM((2 * (g // 2), tm, LANES), F32)],
        compiler_params=_params("arbitrary", "arbitrary"),
        name="proj",
    )(h, g_kv.reshape(1, d), g_q.reshape(1, d), w_kv, w_qg, qconst)


def _compress_kernel(r_ref, pe_ref, w1_ref, w2_ref, kca_ref, vct_ref, *, m):
    half = D_CMP * HEAD_DIM
    row = lax.broadcasted_iota(jnp.int32, (m, LANES), 0)
    lane = lax.broadcasted_iota(jnp.int32, (m, LANES), 1)

    def mlp(i):
        r = r_ref[i, 0]
        top = _dot(r, w1_ref[i, 0:half, :])
        bot = _dot(r, w1_ref[i, half:2 * half, :])
        pe_term = _dot(pe_ref[i], w1_ref[i])[0:1, :]
        hid = top + pltpu.roll(bot, m - 1, axis=0) + pe_term
        hid = hid * jax.nn.sigmoid(hid)
        out = _dot(hid.astype(BF16), w2_ref[i])
        return jnp.where(row < m - 1, out, 0.0)

    coarse = jnp.right_shift(row, 2).astype(F32)
    fine = jnp.bitwise_and(row, 3).astype(F32) * D_CMP + (L_CMP - 1) / 2.0
    kca_ref[0] = (mlp(0) + _pos_columns(coarse, fine, lane)).astype(BF16)
    vct_ref[0] = mlp(1).T[0:HEAD_DIM].astype(BF16)


def _compress(kvc, pe2, w1_2, w2_2):
    _, b, g, m, _ = kvc.shape
    dh = HEAD_DIM
    r = kvc.reshape(2, b * g, m, D_CMP * dh)
    hid = w1_2.shape[-1]
    return pl.pallas_call(
        functools.partial(_compress_kernel, m=m),
        grid=(b * g,),
        in_specs=[
            pl.BlockSpec((2, 1, m, D_CMP * dh), lambda j: (0, j, 0, 0)),
            pl.BlockSpec((2, 8, L_CMP * dh), lambda j: (0, 0, 0)),
            pl.BlockSpec((2, L_CMP * dh, hid), lambda j: (0, 0, 0)),
            pl.BlockSpec((2, hid, LANES), lambda j: (0, 0, 0)),
        ],
        out_specs=[
            pl.BlockSpec((1, m, LANES), lambda j: (j, 0, 0)),
            pl.BlockSpec((1, dh, m), lambda j: (j, 0, 0)),
        ],
        out_shape=[
            jax.ShapeDtypeStruct((b * g, m, LANES), BF16),
            jax.ShapeDtypeStruct((b * g, dh, m), BF16),
        ],
        compiler_params=_params("arbitrary"),
        name="compress",
    )(r, pe2, w1_2, w2_2)


def _select_rows(pred, a, b):
    a32, b32 = pltpu.bitcast(a, jnp.int32), pltpu.bitcast(b, jnp.int32)
    return pltpu.bitcast(jnp.where(pred, a32, b32), BF16)


def _head_stack(ot, gt, branch, tq):
    return jnp.concatenate(
        [ot[:, h * tq:(h + 1) * tq] * gt[3 * h + branch:3 * h + branch + 1, :] for h in range(HEADS_PER_GROUP)],
        axis=0)


def _nsa_cmp_kernel(q_ref, kca_ref, vct_ref, gate_ref, oc_ref, negm_ref, used_ref, *, tq, m):
    q0 = pl.program_id(2) * tq
    hpg = HEADS_PER_GROUP
    ncol = hpg * tq
    q = q_ref[0].reshape(ncol, LANES)
    tcol = q0 + jnp.bitwise_and(lax.broadcasted_iota(jnp.int32, (1, ncol), 1), tq - 1)

    def attend(rows):
        st = _dot_nt(kca_ref[0, 0:rows, :], q)
        nrow = lax.broadcasted_iota(jnp.int32, (rows, 1), 0)
        vis = (nrow * D_CMP + (L_CMP - 1)) <= tcol
        st = jnp.where(vis, st, NEG_INF)
        e = jnp.exp2(st - jnp.max(st, axis=0, keepdims=True))
        rcp = jnp.where(tcol >= L_CMP - 1, 1.0 / jnp.sum(e, axis=0, keepdims=True), 0.0)
        p = e * rcp
        ot = _dot(vct_ref[0, :, 0:rows], p.astype(BF16))
        pt = p[:, 0:tq]
        for h in range(1, hpg):
            pt = pt + p[:, h * tq:(h + 1) * tq]
        nblk = min(MAX_SLC_BLOCKS, rows * D_CMP // L_SLC)
        jj = lax.broadcasted_iota(jnp.int32, (nblk, rows), 0) * L_SLC
        nn = lax.broadcasted_iota(jnp.int32, (nblk, rows), 1) * D_CMP
        overlap_t = ((nn < jj + L_SLC) & (nn + L_CMP > jj)).astype(BF16)
        p_hi = pt.astype(BF16)
        p_lo = (pt - p_hi.astype(F32)).astype(BF16)
        imp = _dot(overlap_t, p_hi) + _dot(overlap_t, p_lo)

        jrow = lax.broadcasted_iota(jnp.int32, (nblk, tq), 0)
        cur = jnp.right_shift(q0 + lax.broadcasted_iota(jnp.int32, (1, tq), 1), L_SLC.bit_length() - 1)
        valid = jrow <= cur
        forced = (jrow == 0) | (jrow == cur) | (jrow == cur - 1)
        val = jnp.where(valid & jnp.logical_not(forced), imp, NEG_INF)
        jrow_f = jrow.astype(F32)
        for _ in range(N_SEL - N_FORCED):
            mx = jnp.max(val, axis=0, keepdims=True)
            first = jnp.min(jnp.where(val == mx, jrow_f, float(MAX_SLC_BLOCKS)), axis=0, keepdims=True)
            val = jnp.where(jrow_f == first, PICKED, val)
        sel = jnp.where(valid & (forced | (val == PICKED)), 1.0, 0.0)
        if nblk < MAX_SLC_BLOCKS:
            sel = jnp.concatenate([sel, jnp.zeros((MAX_SLC_BLOCKS - nblk, tq), F32)], axis=0)
        gt = gate_ref[0, 0].T
        oc_ref[0] = _head_stack(ot, gt, 0, tq).T
        sel_t = sel.T
        negm_ref[0, 0] = jnp.where(sel_t > 0.5, 0.0, NEG_INF).astype(BF16)
        for r in range(tq // SUPER_KEYS):
            used_ref[r] = jnp.max(sel_t[r * SUPER_KEYS:(r + 1) * SUPER_KEYS], axis=0, keepdims=True)

    row_tile = min(m, CMP_ROW_TILE)
    n_variants = m // row_tile
    if n_variants == 1:
        attend(m)
    else:
        needed = pl.program_id(2) * (tq // D_CMP) + (tq // D_CMP - 1)
        variant = jnp.minimum(jnp.right_shift(needed - 1, row_tile.bit_length() - 1), n_variants - 1)
        for v in range(n_variants):
            pl.when(variant == v)(functools.partial(attend, (v + 1) * row_tile))


def _nsa_cmp(q, kca, vct, gates, tq=512):
    b, nh, s, _ = q.shape
    g, hpg, dh = N_KV_GROUPS, HEADS_PER_GROUP, HEAD_DIM
    m = kca.shape[1]
    assert s % tq == 0 and tq % SUPER_KEYS == 0 and tq & (tq - 1) == 0
    nqt = s // tq
    sub = tq // SUPER_KEYS
    return pl.pallas_call(
        functools.partial(_nsa_cmp_kernel, tq=tq, m=m),
        grid=(b, g, s // tq),
        in_specs=[
            pl.BlockSpec((1, hpg, tq, LANES), lambda i, j, k: (i, j, k, 0)),
            pl.BlockSpec((1, m, LANES), lambda i, j, k: (i * N_KV_GROUPS + j, 0, 0)),
            pl.BlockSpec((1, dh, m), lambda i, j, k: (i * N_KV_GROUPS + j, 0, 0)),
            pl.BlockSpec((1, 1, tq, LANES), lambda i, j, k: (i, j, k, 0)),
        ],
        out_specs=[
            pl.BlockSpec((1, tq, hpg * dh), lambda i, j, k: (i, k, j)),
            pl.BlockSpec((1, 1, tq, MAX_SLC_BLOCKS), lambda i, j, k: (i, j, k, 0)),
            pl.BlockSpec((sub, 1, MAX_SLC_BLOCKS), lambda i, j, k: ((i * N_KV_GROUPS + j) * nqt + k, 0, 0)),
        ],
        out_shape=[
            jax.ShapeDtypeStruct((b, s, nh * dh), F32),
            jax.ShapeDtypeStruct((b, g, s, MAX_SLC_BLOCKS), BF16),
            jax.ShapeDtypeStruct((b * g * nqt * sub, 1, MAX_SLC_BLOCKS), F32),
        ],
        compiler_params=_params("arbitrary", "arbitrary", "arbitrary"),
        name="nsa_cmp",
    )(q, kca, vct, gates)


def _super_lists_kernel(used_ref, lst_ref, cnt_ref, *, tb):
    ut = used_ref[...].T.astype(BF16)
    a_i = lax.broadcasted_iota(jnp.int32, (MAX_SLC_BLOCKS, MAX_SLC_BLOCKS), 0)
    b_i = lax.broadcasted_iota(jnp.int32, (MAX_SLC_BLOCKS, MAX_SLC_BLOCKS), 1)
    pair_t = (a_i == jnp.right_shift(b_i, SUPER_SHIFT)).astype(BF16)
    sbf = jnp.where(_dot(pair_t, ut) > 0.5, 1.0, 0.0).astype(BF16)
    incl = _dot((b_i <= a_i).astype(BF16), sbf)[0:MAX_SUPER]
    cnt_ref[...] = jnp.broadcast_to(incl[MAX_SUPER - 1:MAX_SUPER, :], (8, tb)).astype(jnp.int32)
    for i in range(MAX_SUPER):
        lst_ref[i:i + 1, :] = jnp.sum(jnp.where(incl <= float(i), 1.0, 0.0), axis=0, keepdims=True).astype(jnp.int32)


def _super_lists(used):
    t = used.shape[0]
    tb = min(t, LANES)
    assert t % tb == 0
    return pl.pallas_call(
        functools.partial(_super_lists_kernel, tb=tb),
        grid=(t // tb,),
        in_specs=[pl.BlockSpec((tb, MAX_SLC_BLOCKS), lambda i: (i, 0))],
        out_specs=[
            pl.BlockSpec((MAX_SUPER, tb), lambda i: (0, i)),
            pl.BlockSpec((8, tb), lambda i: (0, i)),
        ],
        out_shape=[
            jax.ShapeDtypeStruct((MAX_SUPER, t), jnp.int32),
            jax.ShapeDtypeStruct((8, t), jnp.int32),
        ],
        compiler_params=_params("arbitrary"),
        name="super_lists",
    )(used)


def _nsa_slc_kernel(lst_ref, cnt_ref, q_ref, kas_ref, vts_ref, kaw_ref, vtw_ref, negm_ref, gate_ref, oc_ref, o_ref,
                    m_ref, acc_ref, *, tq, sub):
    hpg = HEADS_PER_GROUP
    ncol = hpg * tq
    n_win = WINDOW // SUPER_KEYS + 1
    pad_k = (lax.broadcasted_iota(jnp.int32, (SUPER_KEYS, 2 * LANES), 1) == PAD_LANE).astype(BF16)
    pad_w = (lax.broadcasted_iota(jnp.int32, (SUPER_KEYS, LANES), 1) == PAD_LANE).astype(BF16)
    n_tiles = pl.num_programs(0) * pl.num_programs(1) * pl.num_programs(2) * sub
    tile0 = ((pl.program_id(0) * pl.num_programs(1) + pl.program_id(1)) * pl.num_programs(2) + pl.program_id(2)) * sub
    row = lax.broadcasted_iota(jnp.int32, (SUPER_KEYS, 1), 0)

    class Tile:
        def __init__(self, s):
            self.s = s
            self.rows = slice(s * tq, (s + 1) * tq)
            self.qt = pl.program_id(2) * sub + s
            self.q = q_ref[0, :, self.rows, :].reshape(ncol, LANES)
            negm = negm_ref[0, 0, self.rows, :]
            self.qa = jnp.concatenate([self.q, jnp.concatenate([negm] * hpg, axis=0)], axis=1)
            self.tcol = self.qt * tq + jnp.bitwise_and(lax.broadcasted_iota(jnp.int32, (1, ncol), 1), tq - 1)
            self.tile = tile0 + s
            self.n_other = cnt_ref[self.tile] - 1
            self.n_early = jnp.right_shift(jnp.maximum(self.n_other - FINAL_SLOTS, 0) + CHUNK_SUPERS - 1,
                                           CHUNK_SUPERS.bit_length() - 1)
            m_ref[s] = jnp.full((1, ncol), NEG_INF, F32)
            acc_ref[s] = jnp.zeros((VT_ROWS, ncol), F32)

        def super_block(self, slot, ok=None):
            sb = lst_ref[slot * n_tiles + self.tile]
            if ok is not None:
                sb = jnp.where(ok, sb, 0)
            start = pl.multiple_of(sb * SUPER_KEYS, SUPER_KEYS)
            return kas_ref[0, 0, pl.ds(start, SUPER_KEYS), :], vts_ref[0, 0, :, pl.ds(start, SUPER_KEYS)]

        def update(self, st, vt):
            m_old = m_ref[self.s]
            m_new = jnp.maximum(m_old, jnp.max(st, axis=0, keepdims=True))
            p = jnp.exp2(st - m_new)
            acc_ref[self.s] = jnp.exp2(m_old - m_new) * acc_ref[self.s] + _dot(vt, p.astype(BF16))
            m_ref[self.s] = m_new

        def early_chunks(self):
            def body(c, carry):
                parts = [self.super_block(c * CHUNK_SUPERS + j) for j in range(CHUNK_SUPERS)]
                kt = jnp.concatenate([k for k, _ in parts], axis=0)
                self.update(_dot_nt(kt, self.qa), jnp.concatenate([v for _, v in parts], axis=1))
                return carry

            lax.fori_loop(0, self.n_early, body, 0)

        def final_scores(self):
            dstart = pl.multiple_of(self.qt * SUPER_KEYS, SUPER_KEYS)
            sd = _dot_nt(kas_ref[0, 0, pl.ds(dstart, SUPER_KEYS), :], self.qa)
            sd = jnp.where(self.qt * SUPER_KEYS + row <= self.tcol, sd, NEG_INF)
            ks, vs = [], [vts_ref[0, 0, :, pl.ds(dstart, SUPER_KEYS)]]
            for j in range(FINAL_SLOTS):
                slot = self.n_early * CHUNK_SUPERS + j
                ok = slot < self.n_other
                k, v = self.super_block(jnp.minimum(slot, MAX_SUPER - 1), ok)
                ks.append(_select_rows(ok, k, pad_k))
                vs.append(v)
            self.st = jnp.concatenate([sd, _dot_nt(jnp.concatenate(ks, axis=0), self.qa)], axis=0)
            self.vt = jnp.concatenate(vs, axis=1)
            win_scores, win_v = [], []
            for r in range(n_win):
                sb = self.qt - (n_win - 1) + r
                start = pl.multiple_of(jnp.maximum(sb, 0) * SUPER_KEYS, SUPER_KEYS)
                kw = kaw_ref[0, 0, pl.ds(start, SUPER_KEYS), :]
                if r < n_win - 1:
                    kw = _select_rows(sb >= 0, kw, pad_w)
                sw = _dot_nt(kw, self.q)
                if r == n_win - 1:
                    sw = jnp.where(sb * SUPER_KEYS + row <= self.tcol, sw, NEG_INF)
                elif r == 0:
                    sw = jnp.where(sb * SUPER_KEYS + WINDOW + row > self.tcol, sw, NEG_INF)
                win_scores.append(sw)
                win_v.append(vtw_ref[0, 0, :, pl.ds(start, SUPER_KEYS)])
            self.stw = jnp.concatenate(win_scores, axis=0)
            self.vw = jnp.concatenate(win_v, axis=1)

        def finish(self):
            self.update(self.st, self.vt)
            acc = acc_ref[self.s]
            o_slc_t = acc[0:HEAD_DIM] * (1.0 / acc[HEAD_DIM:HEAD_DIM + 1])
            e = jnp.exp2(self.stw - jnp.max(self.stw, axis=0, keepdims=True))
            acc_w = _dot(self.vw, e.astype(BF16))
            o_win_t = acc_w[0:HEAD_DIM] * (1.0 / acc_w[HEAD_DIM:HEAD_DIM + 1])
            gt = gate_ref[0, 0, self.rows, :].T
            comb = _head_stack(o_slc_t, gt, 1, tq) + _head_stack(o_win_t, gt, 2, tq)
            o_ref[0, self.rows, :] = (oc_ref[0, self.rows, :] + comb.T).astype(BF16)

    tiles = [Tile(s) for s in range(sub)]
    for t in tiles:
        t.early_chunks()
    for t in tiles:
        t.final_scores()
    for t in tiles:
        t.finish()


def _nsa_slc(lists, counts, q, kas, vts, kaw, vtw, negm, gates, oc, tq=128, sub=4):
    b, nh, s, _ = q.shape
    g, hpg, dh = N_KV_GROUPS, HEADS_PER_GROUP, HEAD_DIM
    ts = tq * sub
    assert tq == SUPER_KEYS and WINDOW % tq == 0 and s >= WINDOW + tq and s % ts == 0
    grid_spec = pltpu.PrefetchScalarGridSpec(
        num_scalar_prefetch=2,
        grid=(b, g, s // ts),
        in_specs=[
            pl.BlockSpec((1, hpg, ts, LANES), lambda i, j, k, *_: (i, j, k, 0)),
            pl.BlockSpec((1, 1, s, 2 * LANES), lambda i, j, k, *_: (i, j, 0, 0)),
            pl.BlockSpec((1, 1, VT_ROWS, s), lambda i, j, k, *_: (i, j, 0, 0)),
            pl.BlockSpec((1, 1, s, LANES), lambda i, j, k, *_: (i, j, 0, 0)),
            pl.BlockSpec((1, 1, VT_ROWS, s), lambda i, j, k, *_: (i, j, 0, 0)),
            pl.BlockSpec((1, 1, ts, MAX_SLC_BLOCKS), lambda i, j, k, *_: (i, j, k, 0)),
            pl.BlockSpec((1, 1, ts, LANES), lambda i, j, k, *_: (i, j, k, 0)),
            pl.BlockSpec((1, ts, hpg * dh), lambda i, j, k, *_: (i, k, j)),
        ],
        out_specs=pl.BlockSpec((1, ts, hpg * dh), lambda i, j, k, *_: (i, k, j)),
        scratch_shapes=[
            pltpu.VMEM((sub, 1, hpg * tq), F32),
            pltpu.VMEM((sub, VT_ROWS, hpg * tq), F32),
        ],
    )
    return pl.pallas_call(
        functools.partial(_nsa_slc_kernel, tq=tq, sub=sub),
        grid_spec=grid_spec,
        out_shape=jax.ShapeDtypeStruct((b, s, nh * dh), BF16),
        compiler_params=_params("arbitrary", "arbitrary", "arbitrary"),
        name="nsa_slc",
    )(lists, counts, q, kas, vts, kaw, vtw, negm, gates, oc)


def _split_bf16(x, n):
    pieces = []
    for _ in range(n):
        p = x.astype(BF16).astype(F32)
        pieces.append(p)
        x = x - p
    return pieces


def _q_constants():
    slopes = jnp.exp2(-8.0 * jnp.arange(1, N_HEADS + 1, dtype=F32) / N_HEADS) * LOG2E
    pieces = jnp.stack(_split_bf16(slopes, N_SPLIT), axis=1)
    qc = jnp.zeros((N_HEADS, LANES), F32)
    qc = qc.at[:, POS_LANE:POS_LANE + N_SPLIT].set(pieces * L_SLC)
    qc = qc.at[:, POS_LANE + N_SPLIT:POS_LANE + 2 * N_SPLIT].set(pieces)
    return qc.at[:, PAD_LANE].set(NEG_INF)


def _qg_weight(w_qg):
    d = w_qg.shape[0]
    nq = N_HEADS * HEAD_DIM
    wg = jnp.pad(w_qg[:, nq:].reshape(d, N_KV_GROUPS, 3 * HEADS_PER_GROUP),
                 ((0, 0), (0, 0), (0, LANES - 3 * HEADS_PER_GROUP)))
    return jnp.concatenate([w_qg[:, :nq], wg.reshape(d, N_KV_GROUPS * LANES)], axis=1).astype(BF16)


def kernel(x, a_norm, a_w_in, a_conv, a_w_out, kv_norm, w_kv, cmp_pe_k, cmp_w1_k, cmp_w2_k, cmp_pe_v, cmp_w1_v,
           cmp_w2_v, b_norm, b_w_qg, b_w_o, f_norm, f_w_gu, f_w_down, final_norm):
    b, s, d = x.shape
    nq = N_HEADS * HEAD_DIM
    h = _mixer_a(x, a_norm[0], a_w_in[0].astype(BF16), a_conv[0], a_w_out[0].astype(BF16))
    h = _ffn(h.reshape(b * s, d), f_norm[0], f_w_gu[0].astype(BF16), f_w_down[0].astype(BF16), final_norm)
    h = h.reshape(b, s, d)

    kvc, kas, vts, kaw, vtw, q, gates = _proj(h, kv_norm, b_norm[0], w_kv.astype(BF16), _qg_weight(b_w_qg[0]),
                                              _q_constants())
    pe2 = jnp.stack([cmp_pe_k, cmp_pe_v]).reshape(2, 1, L_CMP * HEAD_DIM)
    pe2 = jnp.broadcast_to(pe2, (2, 8, L_CMP * HEAD_DIM)).astype(BF16)
    w2 = jnp.pad(jnp.stack([cmp_w2_k, cmp_w2_v]), ((0, 0), (0, 0), (0, LANES - HEAD_DIM))).astype(BF16)
    kca, vct = _compress(kvc, pe2, jnp.stack([cmp_w1_k, cmp_w1_v]).astype(BF16), w2)

    oc, negm, used = _nsa_cmp(q, kca, vct, gates)
    lists, counts = _super_lists(used.reshape(used.shape[0], MAX_SLC_BLOCKS))
    o = _nsa_slc(lists.reshape(-1), counts[0], q, kas, vts, kaw, vtw, negm, gates, oc)

    out = _ffn(h.reshape(b * s, d), f_norm[1], f_w_gu[1].astype(BF16), f_w_down[1].astype(BF16), final_norm,
               o=o.reshape(b * s, nq), w_o=b_w_o[0].astype(BF16), with_final=True)
    return out.reshape(b, s, d)
```

```python
import functools
import math

import jax
import jax.numpy as jnp
from jax import lax
from jax.experimental import pallas as pl
from jax.experimental.pallas import tpu as pltpu

F32 = jnp.float32
BF16 = jnp.bfloat16

N_HEADS = 16
N_KV_GROUPS = 4
HEADS_PER_GROUP = N_HEADS // N_KV_GROUPS
HEAD_DIM = 64
L_CMP = 32
D_CMP = 16
L_SLC = 64
N_SEL = 16
N_FORCED = 3
WINDOW = 512
N_KV_SETS = 6
CONV_WIDTH = 3
RMS_EPS = 1e-5
NEG_INF = -1e30
PICKED = -3.0e38
LOG2E = math.log2(math.e)
LANES = 128
MAX_SLC_BLOCKS = LANES
SUPER_KEYS = LANES
SUPER_SHIFT = (SUPER_KEYS // L_SLC).bit_length() - 1
MAX_SUPER = MAX_SLC_BLOCKS * L_SLC // SUPER_KEYS
CHUNK_SUPERS = 4
FINAL_SLOTS = 10
CMP_ROW_TILE = 128
VT_ROWS = HEAD_DIM + 16
POS_LANE = HEAD_DIM
N_SPLIT = 3
PAD_LANE = POS_LANE + 2 * N_SPLIT
VMEM_LIMIT_BYTES = 56 * 1024 * 1024


def _rms(x, g):
    return x * lax.rsqrt(jnp.mean(x * x, axis=-1, keepdims=True) + RMS_EPS) * g


def _dot(a, b):
    return jnp.dot(a, b, preferred_element_type=F32)


def _dot_nt(a, b):
    return lax.dot_general(a, b, (((1,), (1,)), ((), ())), preferred_element_type=F32)


def _params(*sem, flags=None):
    return pltpu.CompilerParams(dimension_semantics=sem, vmem_limit_bytes=VMEM_LIMIT_BYTES, flags=flags)


def _pos_columns(coarse, fine, lane):
    is_c = (lane >= POS_LANE) & (lane < POS_LANE + N_SPLIT)
    is_f = (lane >= POS_LANE + N_SPLIT) & (lane < POS_LANE + 2 * N_SPLIT)
    return jnp.where(is_c, coarse, jnp.where(is_f, fine, 0.0))


def _mixer_a_kernel(x_ref, g_ref, win_ref, conv_ref, wout_ref, o_ref, ext_ref, *, tm, d, tc):
    @pl.when(pl.program_id(1) == 0)
    def _():
        ext_ref[0:8, :] = jnp.zeros((8, d), F32)

    x = x_ref[0]
    xb = _rms(x, g_ref[...]).astype(BF16)
    y = x
    for c0 in range(0, d, tc):
        cols = slice(c0, c0 + tc)
        b_gate = _dot(xb, win_ref[:, c0:c0 + tc])
        cv = _dot(xb, win_ref[:, d + c0:d + c0 + tc]) * _dot(xb, win_ref[:, 2 * d + c0:2 * d + c0 + tc])
        ext_ref[8:8 + tm, cols] = cv
        cv1 = ext_ref[7:7 + tm, cols]
        cv2 = ext_ref[6:6 + tm, cols]
        u = conv_ref[0:1, cols] * cv2 + conv_ref[1:2, cols] * cv1 + conv_ref[2:3, cols] * cv
        ext_ref[0:8, cols] = cv[tm - 8:tm, :]
        y = y + _dot((b_gate * u).astype(BF16), wout_ref[cols, :])
    o_ref[0] = y


def _mixer_a(x, g, w_in, conv_w, w_out, tm=1024, tc=1024):
    b, s, d = x.shape
    assert s % tm == 0 and d % tc == 0
    return pl.pallas_call(
        functools.partial(_mixer_a_kernel, tm=tm, d=d, tc=tc),
        grid=(b, s // tm),
        in_specs=[
            pl.BlockSpec((1, tm, d), lambda i, j: (i, j, 0)),
            _resident((1, d)), _resident((d, 3 * d)), _resident((CONV_WIDTH, d)), _resident((d, d)),
        ],
        out_specs=pl.BlockSpec((1, tm, d), lambda i, j: (i, j, 0)),
        out_shape=jax.ShapeDtypeStruct((b, s, d), F32),
        scratch_shapes=[pltpu.VMEM((tm + 8, d), F32)],
        compiler_params=_params("arbitrary", "arbitrary"),
        name="mixer_a",
    )(x, g.reshape(1, d), w_in, conv_w, w_out)


def _ffn_kernel(*refs, with_proj, with_final, f, tf):
    if with_proj:
        h_ref, o_ref_in, wo_ref, g_ref, wgu_ref, wd_ref, fg_ref, out_ref = refs
    else:
        h_ref, g_ref, wgu_ref, wd_ref, fg_ref, out_ref = refs
    h = h_ref[...]
    if with_proj:
        h = h + _dot(o_ref_in[...], wo_ref[...])
    xb = _rms(h, g_ref[...]).astype(BF16)
    y = h
    for c in range(f // tf):
        gate = _dot(xb, wgu_ref[:, c * tf:(c + 1) * tf])
        up = _dot(xb, wgu_ref[:, f + c * tf:f + (c + 1) * tf])
        act = (gate * jax.nn.sigmoid(gate)) * up
        y = y + _dot(act.astype(BF16), wd_ref[c * tf:(c + 1) * tf, :])
    if with_final:
        y = _rms(y, fg_ref[...])
    out_ref[...] = y


def _resident(shape):
    return pl.BlockSpec(shape, lambda *_: (0,) * len(shape), pipeline_mode=pl.Buffered(1))


def _ffn(h, g, w_gu, w_down, final_g, o=None, w_o=None, with_final=False, tm=1024, tf=256):
    t, d = h.shape
    f = w_down.shape[0]
    assert t % tm == 0 and f % tf == 0
    with_proj = o is not None
    in_specs = [pl.BlockSpec((tm, d), lambda i: (i, 0))]
    args = [h]
    if with_proj:
        in_specs += [pl.BlockSpec((tm, o.shape[1]), lambda i: (i, 0)), _resident(w_o.shape)]
        args += [o, w_o]
    in_specs += [_resident((1, d)), _resident(w_gu.shape), _resident(w_down.shape), _resident((1, d))]
    args += [g.reshape(1, d), w_gu, w_down, final_g.reshape(1, d)]
    return pl.pallas_call(
        functools.partial(_ffn_kernel, with_proj=with_proj, with_final=with_final, f=f, tf=tf),
        grid=(t // tm,),
        in_specs=in_specs,
        out_specs=pl.BlockSpec((tm, d), lambda i: (i, 0)),
        out_shape=jax.ShapeDtypeStruct((t, d), F32),
        compiler_params=_params("arbitrary"),
        name="ffn_proj" if with_proj else "ffn",
    )(*args)


def _halves(x):
    return x, pltpu.roll(x, HEAD_DIM, axis=1)


def _proj_kernel(h_ref, gkv_ref, gq_ref, wkv_ref, wqg_ref, qc_ref,
                 kvc_ref, kas_ref, vts_ref, kaw_ref, vtw_ref, q_ref, gate_ref, rel_ref, *, tm):
    g, dh = N_KV_GROUPS, HEAD_DIM
    gw = g * dh
    s0 = pl.program_id(1) * tm
    x = h_ref[0]
    y = x * lax.rsqrt(jnp.mean(x * x, axis=-1, keepdims=True) + RMS_EPS)
    kv = _dot((y * gkv_ref[...]).astype(BF16), wkv_ref[...])
    low_m = lax.broadcasted_iota(jnp.int32, (tm // D_CMP, LANES), 1) < dh
    for st in range(2):
        for pair in range(g // 2):
            slab = rel_ref.at[st * (g // 2) + pair]
            slab[...] = kv[:, st * gw + pair * LANES:st * gw + (pair + 1) * LANES]
            for j in range(D_CMP // 2):
                a = slab[pl.ds(2 * j, tm // D_CMP, stride=D_CMP), :]
                b = slab[pl.ds(2 * j + 1, tm // D_CMP, stride=D_CMP), :]
                cols = slice(j * LANES, (j + 1) * LANES)
                kvc_ref[st, 0, 2 * pair, :, cols] = jnp.where(low_m, a, pltpu.roll(b, dh, axis=1)).astype(BF16)
                kvc_ref[st, 0, 2 * pair + 1, :, cols] = jnp.where(low_m, pltpu.roll(a, dh, axis=1), b).astype(BF16)

    kpos = s0 + lax.broadcasted_iota(jnp.int32, (tm, LANES), 0)
    lane = lax.broadcasted_iota(jnp.int32, (tm, LANES), 1)
    low = lane < dh
    kblk = jnp.right_shift(kpos, L_SLC.bit_length() - 1)
    posc = _pos_columns(kblk.astype(F32), jnp.bitwise_and(kpos, L_SLC - 1).astype(F32), lane)
    onehot = (lane == kblk).astype(BF16)
    ones_rows = (lax.broadcasted_iota(jnp.int32, (VT_ROWS - dh, tm), 0) == 0).astype(BF16)
    for pair in range(g // 2):
        for st, k_ref, vt_ref in ((2, kas_ref, vts_ref), (4, kaw_ref, vtw_ref)):
            k_pair = kv[:, st * gw + pair * LANES:st * gw + (pair + 1) * LANES]
            vt = kv[:, (st + 1) * gw + pair * LANES:(st + 1) * gw + (pair + 1) * LANES].T
            for half, kh in enumerate(_halves(k_pair)):
                grp = 2 * pair + half
                k_ref[0, grp, :, 0:LANES] = jnp.where(low, kh, posc).astype(BF16)
                if st == 2:
                    k_ref[0, grp, :, LANES:2 * LANES] = onehot
                vt_ref[0, grp, 0:dh] = vt[half * dh:(half + 1) * dh].astype(BF16)
                vt_ref[0, grp, dh:VT_ROWS] = ones_rows

    qg = _dot((y * gq_ref[...]).astype(BF16), wqg_ref[...])
    for pair in range(N_HEADS // 2):
        q_pair = qg[:, pair * LANES:(pair + 1) * LANES] * (LOG2E * dh ** -0.5)
        for half, qh in enumerate(_halves(q_pair)):
            hd = 2 * pair + half
            q_ref[0, hd] = jnp.where(low, qh, qc_ref[hd:hd + 1, :]).astype(BF16)
    nq = N_HEADS * dh
    for grp in range(g):
        gate_ref[0, grp] = jax.nn.sigmoid(qg[:, nq + grp * LANES: nq + (grp + 1) * LANES])


def _proj(h, g_kv, g_q, w_kv, w_qg, qconst, tm=1024):
    b, s, d = h.shape
    assert s % tm == 0 and s // L_SLC <= MAX_SLC_BLOCKS
    g, dh = N_KV_GROUPS, HEAD_DIM
    return pl.pallas_call(
        functools.partial(_proj_kernel, tm=tm),
        grid=(b, s // tm),
        in_specs=[
            pl.BlockSpec((1, tm, d), lambda i, j: (i, j, 0)),
            _resident((1, d)), _resident((1, d)), _resident(w_kv.shape), _resident(w_qg.shape),
            _resident(qconst.shape),
        ],
        out_specs=[
            pl.BlockSpec((2, 1, g, tm // D_CMP, D_CMP * dh), lambda i, j: (0, i, 0, j, 0)),
            pl.BlockSpec((1, g, tm, 2 * LANES), lambda i, j: (i, 0, j, 0)),
            pl.BlockSpec((1, g, VT_ROWS, tm), lambda i, j: (i, 0, 0, j)),
            pl.BlockSpec((1, g, tm, LANES), lambda i, j: (i, 0, j, 0)),
            pl.BlockSpec((1, g, VT_ROWS, tm), lambda i, j: (i, 0, 0, j)),
            pl.BlockSpec((1, N_HEADS, tm, LANES), lambda i, j: (i, 0, j, 0)),
            pl.BlockSpec((1, g, tm, LANES), lambda i, j: (i, 0, j, 0)),
        ],
        out_shape=[
            jax.ShapeDtypeStruct((2, b, g, s // D_CMP, D_CMP * dh), BF16),
            jax.ShapeDtypeStruct((b, g, s, 2 * LANES), BF16),
            jax.ShapeDtypeStruct((b, g, VT_ROWS, s), BF16),
            jax.ShapeDtypeStruct((b, g, s, LANES), BF16),
            jax.ShapeDtypeStruct((b, g, VT_ROWS, s), BF16),
            jax.ShapeDtypeStruct((b, N_HEADS, s, LANES), BF16),
            jax.ShapeDtypeStruct((b, g, s, LANES), F32),
        ],
        scratch_shapes=[pltpu.VMEM((2 * (g // 2), tm, LANES), F32)],
        compiler_params=_params("arbitrary", "arbitrary"),
        name="proj",
    )(h, g_kv.reshape(1, d), g_q.reshape(1, d), w_kv, w_qg, qconst)


def _compress_kernel(r_ref, pe_ref, w1_ref, w2_ref, kca_ref, vct_ref, *, m):
    half = D_CMP * HEAD_DIM
    row = lax.broadcasted_iota(jnp.int32, (m, LANES), 0)
    lane = lax.broadcasted_iota(jnp.int32, (m, LANES), 1)

    def mlp(i):
        r = r_ref[i, 0]
        top = _dot(r, w1_ref[i, 0:half, :])
        bot = _dot(r, w1_ref[i, half:2 * half, :])
        pe_term = _dot(pe_ref[i], w1_ref[i])[0:1, :]
        hid = top + pltpu.roll(bot, m - 1, axis=0) + pe_term
        hid = hid * jax.nn.sigmoid(hid)
        out = _dot(hid.astype(BF16), w2_ref[i])
        return jnp.where(row < m - 1, out, 0.0)

    coarse = jnp.right_shift(row, 2).astype(F32)
    fine = jnp.bitwise_and(row, 3).astype(F32) * D_CMP + (L_CMP - 1) / 2.0
    kca_ref[0] = (mlp(0) + _pos_columns(coarse, fine, lane)).astype(BF16)
    vct_ref[0] = jnp.where(lane == HEAD_DIM, 1.0, mlp(1)).T[0:VT_ROWS].astype(BF16)


def _compress(kvc, pe2, w1_2, w2_2):
    _, b, g, m, _ = kvc.shape
    dh = HEAD_DIM
    r = kvc.reshape(2, b * g, m, D_CMP * dh)
    hid = w1_2.shape[-1]
    return pl.pallas_call(
        functools.partial(_compress_kernel, m=m),
        grid=(b * g,),
        in_specs=[
            pl.BlockSpec((2, 1, m, D_CMP * dh), lambda j: (0, j, 0, 0)),
            pl.BlockSpec((2, 8, L_CMP * dh), lambda j: (0, 0, 0)),
            pl.BlockSpec((2, L_CMP * dh, hid), lambda j: (0, 0, 0)),
            pl.BlockSpec((2, hid, LANES), lambda j: (0, 0, 0)),
        ],
        out_specs=[
            pl.BlockSpec((1, m, LANES), lambda j: (j, 0, 0)),
            pl.BlockSpec((1, VT_ROWS, m), lambda j: (j, 0, 0)),
        ],
        out_shape=[
            jax.ShapeDtypeStruct((b * g, m, LANES), BF16),
            jax.ShapeDtypeStruct((b * g, VT_ROWS, m), BF16),
        ],
        compiler_params=_params("arbitrary"),
        name="compress",
    )(r, pe2, w1_2, w2_2)


def _select_rows(pred, a, b):
    a32, b32 = pltpu.bitcast(a, jnp.int32), pltpu.bitcast(b, jnp.int32)
    return pltpu.bitcast(jnp.where(pred, a32, b32), BF16)


def _head_stack(ot, gt, branch, tq):
    return jnp.concatenate(
        [ot[:, h * tq:(h + 1) * tq] * gt[3 * h + branch:3 * h + branch + 1, :] for h in range(HEADS_PER_GROUP)],
        axis=0)


def _nsa_cmp_kernel(q_ref, kca_ref, vct_ref, gate_ref, oc_ref, negm_ref, used_ref, *, tq, m):
    q0 = pl.program_id(2) * tq
    hpg = HEADS_PER_GROUP
    ncol = hpg * tq
    q = q_ref[0].reshape(ncol, LANES)
    tcol = q0 + jnp.bitwise_and(lax.broadcasted_iota(jnp.int32, (1, ncol), 1), tq - 1)

    def attend(rows):
        st = _dot_nt(kca_ref[0, 0:rows, :], q)
        band = max(rows - row_tile - 8, 0)
        nrow = band + lax.broadcasted_iota(jnp.int32, (rows - band, 1), 0)
        low = jnp.where((nrow * D_CMP + (L_CMP - 1)) <= tcol, st[band:], NEG_INF)
        st = low if band == 0 else jnp.concatenate([st[0:band], low], axis=0)
        e = jnp.exp2(st - jnp.max(st, axis=0, keepdims=True)).astype(BF16)
        acc = _dot(vct_ref[0, :, 0:rows], e)
        rcp = jnp.where(tcol >= L_CMP - 1, 1.0 / acc[HEAD_DIM:HEAD_DIM + 1], 0.0)
        ot = acc[0:HEAD_DIM] * rcp
        nblk = min(MAX_SLC_BLOCKS, rows * D_CMP // L_SLC)
        jj = lax.broadcasted_iota(jnp.int32, (nblk, rows), 0) * L_SLC
        nn = lax.broadcasted_iota(jnp.int32, (nblk, rows), 1) * D_CMP
        overlap_t = ((nn < jj + L_SLC) & (nn + L_CMP > jj)).astype(BF16)
        imp_h = _dot(overlap_t, e) * rcp
        imp = imp_h[:, 0:tq]
        for h in range(1, hpg):
            imp = imp + imp_h[:, h * tq:(h + 1) * tq]

        jrow = lax.broadcasted_iota(jnp.int32, (nblk, tq), 0)
        cur = jnp.right_shift(q0 + lax.broadcasted_iota(jnp.int32, (1, tq), 1), L_SLC.bit_length() - 1)
        valid = jrow <= cur
        forced = (jrow == 0) | (jrow == cur) | (jrow == cur - 1)
        val = jnp.where(valid & jnp.logical_not(forced), imp, NEG_INF)
        jrow_f = jrow.astype(F32)
        for _ in range(N_SEL - N_FORCED):
            mx = jnp.max(val, axis=0, keepdims=True)
            first = jnp.min(jnp.where(val == mx, jrow_f, float(MAX_SLC_BLOCKS)), axis=0, keepdims=True)
            val = jnp.where(jrow_f == first, PICKED, val)
        sel = jnp.where(valid & (forced | (val == PICKED)), 1.0, 0.0)
        if nblk < MAX_SLC_BLOCKS:
            sel = jnp.concatenate([sel, jnp.zeros((MAX_SLC_BLOCKS - nblk, tq), F32)], axis=0)
        gt = gate_ref[0, 0].T
        oc_ref[0] = _head_stack(ot, gt, 0, tq).T
        sel_t = sel.T
        negm_ref[0, 0] = jnp.where(sel_t > 0.5, 0.0, NEG_INF).astype(BF16)
        for r in range(tq // SUPER_KEYS):
            used_ref[r] = jnp.max(sel_t[r * SUPER_KEYS:(r + 1) * SUPER_KEYS], axis=0, keepdims=True)

    row_tile = min(m, CMP_ROW_TILE)
    n_variants = m // row_tile
    if n_variants == 1:
        attend(m)
    else:
        needed = pl.program_id(2) * (tq // D_CMP) + (tq // D_CMP - 1)
        variant = jnp.minimum(jnp.right_shift(needed - 1, row_tile.bit_length() - 1), n_variants - 1)
        for v in range(n_variants):
            pl.when(variant == v)(functools.partial(attend, (v + 1) * row_tile))


def _nsa_cmp(q, kca, vct, gates, tq=512):
    b, nh, s, _ = q.shape
    g, hpg, dh = N_KV_GROUPS, HEADS_PER_GROUP, HEAD_DIM
    m = kca.shape[1]
    assert s % tq == 0 and tq % SUPER_KEYS == 0 and tq & (tq - 1) == 0
    nqt = s // tq
    sub = tq // SUPER_KEYS
    return pl.pallas_call(
        functools.partial(_nsa_cmp_kernel, tq=tq, m=m),
        grid=(b, g, s // tq),
        in_specs=[
            pl.BlockSpec((1, hpg, tq, LANES), lambda i, j, k: (i, j, k, 0)),
            pl.BlockSpec((1, m, LANES), lambda i, j, k: (i * N_KV_GROUPS + j, 0, 0)),
            pl.BlockSpec((1, VT_ROWS, m), lambda i, j, k: (i * N_KV_GROUPS + j, 0, 0)),
            pl.BlockSpec((1, 1, tq, LANES), lambda i, j, k: (i, j, k, 0)),
        ],
        out_specs=[
            pl.BlockSpec((1, tq, hpg * dh), lambda i, j, k: (i, k, j)),
            pl.BlockSpec((1, 1, tq, MAX_SLC_BLOCKS), lambda i, j, k: (i, j, k, 0)),
            pl.BlockSpec((sub, 1, MAX_SLC_BLOCKS), lambda i, j, k: ((i * N_KV_GROUPS + j) * nqt + k, 0, 0)),
        ],
        out_shape=[
            jax.ShapeDtypeStruct((b, s, nh * dh), F32),
            jax.ShapeDtypeStruct((b, g, s, MAX_SLC_BLOCKS), BF16),
            jax.ShapeDtypeStruct((b * g * nqt * sub, 1, MAX_SLC_BLOCKS), F32),
        ],
        compiler_params=_params("arbitrary", "arbitrary", "arbitrary"),
        name="nsa_cmp",
    )(q, kca, vct, gates)


def _super_lists_kernel(used_ref, lst_ref, cnt_ref, *, tb):
    ut = used_ref[...].T.astype(BF16)
    a_i = lax.broadcasted_iota(jnp.int32, (MAX_SLC_BLOCKS, MAX_SLC_BLOCKS), 0)
    b_i = lax.broadcasted_iota(jnp.int32, (MAX_SLC_BLOCKS, MAX_SLC_BLOCKS), 1)
    pair_t = (a_i == jnp.right_shift(b_i, SUPER_SHIFT)).astype(BF16)
    sbf = jnp.where(_dot(pair_t, ut) > 0.5, 1.0, 0.0).astype(BF16)
    incl = _dot((b_i <= a_i).astype(BF16), sbf)[0:MAX_SUPER]
    cnt_ref[...] = jnp.broadcast_to(incl[MAX_SUPER - 1:MAX_SUPER, :], (8, tb)).astype(jnp.int32)
    for i in range(MAX_SUPER):
        lst_ref[i:i + 1, :] = jnp.sum(jnp.where(incl <= float(i), 1.0, 0.0), axis=0, keepdims=True).astype(jnp.int32)


def _super_lists(used):
    t = used.shape[0]
    tb = min(t, LANES)
    assert t % tb == 0
    return pl.pallas_call(
        functools.partial(_super_lists_kernel, tb=tb),
        grid=(t // tb,),
        in_specs=[pl.BlockSpec((tb, MAX_SLC_BLOCKS), lambda i: (i, 0))],
        out_specs=[
            pl.BlockSpec((MAX_SUPER, tb), lambda i: (0, i)),
            pl.BlockSpec((8, tb), lambda i: (0, i)),
        ],
        out_shape=[
            jax.ShapeDtypeStruct((MAX_SUPER, t), jnp.int32),
            jax.ShapeDtypeStruct((8, t), jnp.int32),
        ],
        compiler_params=_params("arbitrary"),
        name="super_lists",
    )(used)


def _nsa_slc_kernel(lst_ref, cnt_ref, q_ref, kas_ref, vts_ref, kaw_ref, vtw_ref, negm_ref, gate_ref, oc_ref, o_ref,
                    m_ref, acc_ref, *, tq, sub):
    hpg = HEADS_PER_GROUP
    ncol = hpg * tq
    n_win = WINDOW // SUPER_KEYS + 1
    pad_k = (lax.broadcasted_iota(jnp.int32, (SUPER_KEYS, 2 * LANES), 1) == PAD_LANE).astype(BF16)
    pad_w = (lax.broadcasted_iota(jnp.int32, (SUPER_KEYS, LANES), 1) == PAD_LANE).astype(BF16)
    n_tiles = pl.num_programs(0) * pl.num_programs(1) * pl.num_programs(2) * sub
    tile0 = ((pl.program_id(0) * pl.num_programs(1) + pl.program_id(1)) * pl.num_programs(2) + pl.program_id(2)) * sub
    row = lax.broadcasted_iota(jnp.int32, (SUPER_KEYS, 1), 0)

    class Tile:
        def __init__(self, s):
            self.s = s
            self.rows = slice(s * tq, (s + 1) * tq)
            self.qt = pl.program_id(2) * sub + s
            self.q = q_ref[0, :, self.rows, :].reshape(ncol, LANES)
            negm = negm_ref[0, 0, self.rows, :]
            self.qa = jnp.concatenate([self.q, jnp.concatenate([negm] * hpg, axis=0)], axis=1)
            self.tcol = self.qt * tq + jnp.bitwise_and(lax.broadcasted_iota(jnp.int32, (1, ncol), 1), tq - 1)
            self.tile = tile0 + s
            self.n_other = cnt_ref[self.tile] - 1
            self.n_early = jnp.right_shift(jnp.maximum(self.n_other - FINAL_SLOTS, 0) + CHUNK_SUPERS - 1,
                                           CHUNK_SUPERS.bit_length() - 1)
            m_ref[s] = jnp.full((1, ncol), NEG_INF, F32)
            acc_ref[s] = jnp.zeros((VT_ROWS, ncol), F32)

        def super_block(self, slot, ok=None):
            sb = lst_ref[slot * n_tiles + self.tile]
            if ok is not None:
                sb = jnp.where(ok, sb, 0)
            start = pl.multiple_of(sb * SUPER_KEYS, SUPER_KEYS)
            return kas_ref[0, 0, pl.ds(start, SUPER_KEYS), :], vts_ref[0, 0, :, pl.ds(start, SUPER_KEYS)]

        def update(self, st, vt):
            m_old = m_ref[self.s]
            m_new = jnp.maximum(m_old, jnp.max(st, axis=0, keepdims=True))
            p = jnp.exp2(st - m_new)
            acc_ref[self.s] = jnp.exp2(m_old - m_new) * acc_ref[self.s] + _dot(vt, p.astype(BF16))
            m_ref[self.s] = m_new

        def early_chunks(self):
            def body(c, carry):
                parts = [self.super_block(c * CHUNK_SUPERS + j) for j in range(CHUNK_SUPERS)]
                kt = jnp.concatenate([k for k, _ in parts], axis=0)
                self.update(_dot_nt(kt, self.qa), jnp.concatenate([v for _, v in parts], axis=1))
                return carry

            lax.fori_loop(0, self.n_early, body, 0)

        def final_scores(self):
            dstart = pl.multiple_of(self.qt * SUPER_KEYS, SUPER_KEYS)
            sd = _dot_nt(kas_ref[0, 0, pl.ds(dstart, SUPER_KEYS), :], self.qa)
            sd = jnp.where(self.qt * SUPER_KEYS + row <= self.tcol, sd, NEG_INF)
            ks, vs = [], [vts_ref[0, 0, :, pl.ds(dstart, SUPER_KEYS)]]
            for j in range(FINAL_SLOTS):
                slot = self.n_early * CHUNK_SUPERS + j
                ok = slot < self.n_other
                k, v = self.super_block(jnp.minimum(slot, MAX_SUPER - 1), ok)
                ks.append(_select_rows(ok, k, pad_k))
                vs.append(v)
            self.st = jnp.concatenate([sd, _dot_nt(jnp.concatenate(ks, axis=0), self.qa)], axis=0)
            self.vt = jnp.concatenate(vs, axis=1)
            win_scores, win_v = [], []
            for r in range(n_win):
                sb = self.qt - (n_win - 1) + r
                start = pl.multiple_of(jnp.maximum(sb, 0) * SUPER_KEYS, SUPER_KEYS)
                kw = kaw_ref[0, 0, pl.ds(start, SUPER_KEYS), :]
                if r < n_win - 1:
                    kw = _select_rows(sb >= 0, kw, pad_w)
                sw = _dot_nt(kw, self.q)
                if r == n_win - 1:
                    sw = jnp.where(sb * SUPER_KEYS + row <= self.tcol, sw, NEG_INF)
                elif r == 0:
                    sw = jnp.where(sb * SUPER_KEYS + WINDOW + row > self.tcol, sw, NEG_INF)
                win_scores.append(sw)
                win_v.append(vtw_ref[0, 0, :, pl.ds(start, SUPER_KEYS)])
            self.stw = jnp.concatenate(win_scores, axis=0)
            self.vw = jnp.concatenate(win_v, axis=1)

        def finish(self):
            self.update(self.st, self.vt)
            acc = acc_ref[self.s]
            o_slc_t = acc[0:HEAD_DIM] * (1.0 / acc[HEAD_DIM:HEAD_DIM + 1])
            e = jnp.exp2(self.stw - jnp.max(self.stw, axis=0, keepdims=True))
            acc_w = _dot(self.vw, e.astype(BF16))
            o_win_t = acc_w[0:HEAD_DIM] * (1.0 / acc_w[HEAD_DIM:HEAD_DIM + 1])
            gt = gate_ref[0, 0, self.rows, :].T
            comb = _head_stack(o_slc_t, gt, 1, tq) + _head_stack(o_win_t, gt, 2, tq)
            o_ref[0, self.rows, :] = (oc_ref[0, self.rows, :] + comb.T).astype(BF16)

    tiles = [Tile(s) for s in range(sub)]
    for t in tiles:
        t.early_chunks()
    for t in tiles:
        t.final_scores()
    for t in tiles:
        t.finish()


def _nsa_slc(lists, counts, q, kas, vts, kaw, vtw, negm, gates, oc, tq=128, sub=4):
    b, nh, s, _ = q.shape
    g, hpg, dh = N_KV_GROUPS, HEADS_PER_GROUP, HEAD_DIM
    ts = tq * sub
    assert tq == SUPER_KEYS and WINDOW % tq == 0 and s >= WINDOW + tq and s % ts == 0
    grid_spec = pltpu.PrefetchScalarGridSpec(
        num_scalar_prefetch=2,
        grid=(b, g, s // ts),
        in_specs=[
            pl.BlockSpec((1, hpg, ts, LANES), lambda i, j, k, *_: (i, j, k, 0)),
            pl.BlockSpec((1, 1, s, 2 * LANES), lambda i, j, k, *_: (i, j, 0, 0)),
            pl.BlockSpec((1, 1, VT_ROWS, s), lambda i, j, k, *_: (i, j, 0, 0)),
            pl.BlockSpec((1, 1, s, LANES), lambda i, j, k, *_: (i, j, 0, 0)),
            pl.BlockSpec((1, 1, VT_ROWS, s), lambda i, j, k, *_: (i, j, 0, 0)),
            pl.BlockSpec((1, 1, ts, MAX_SLC_BLOCKS), lambda i, j, k, *_: (i, j, k, 0)),
            pl.BlockSpec((1, 1, ts, LANES), lambda i, j, k, *_: (i, j, k, 0)),
            pl.BlockSpec((1, ts, hpg * dh), lambda i, j, k, *_: (i, k, j)),
        ],
        out_specs=pl.BlockSpec((1, ts, hpg * dh), lambda i, j, k, *_: (i, k, j)),
        scratch_shapes=[
            pltpu.VMEM((sub, 1, hpg * tq), F32),
            pltpu.VMEM((sub, VT_ROWS, hpg * tq), F32),
        ],
    )
    return pl.pallas_call(
        functools.partial(_nsa_slc_kernel, tq=tq, sub=sub),
        grid_spec=grid_spec,
        out_shape=jax.ShapeDtypeStruct((b, s, nh * dh), BF16),
        compiler_params=_params("arbitrary", "arbitrary", "arbitrary"),
        name="nsa_slc",
    )(lists, counts, q, kas, vts, kaw, vtw, negm, gates, oc)


def _split_bf16(x, n):
    pieces = []
    for _ in range(n):
        p = x.astype(BF16).astype(F32)
        pieces.append(p)
        x = x - p
    return pieces


def _q_constants():
    slopes = jnp.exp2(-8.0 * jnp.arange(1, N_HEADS + 1, dtype=F32) / N_HEADS) * LOG2E
    pieces = jnp.stack(_split_bf16(slopes, N_SPLIT), axis=1)
    qc = jnp.zeros((N_HEADS, LANES), F32)
    qc = qc.at[:, POS_LANE:POS_LANE + N_SPLIT].set(pieces * L_SLC)
    qc = qc.at[:, POS_LANE + N_SPLIT:POS_LANE + 2 * N_SPLIT].set(pieces)
    return qc.at[:, PAD_LANE].set(NEG_INF)


def _qg_weight(w_qg):
    d = w_qg.shape[0]
    nq = N_HEADS * HEAD_DIM
    wg = jnp.pad(w_qg[:, nq:].reshape(d, N_KV_GROUPS, 3 * HEADS_PER_GROUP),
                 ((0, 0), (0, 0), (0, LANES - 3 * HEADS_PER_GROUP)))
    return jnp.concatenate([w_qg[:, :nq], wg.reshape(d, N_KV_GROUPS * LANES)], axis=1).astype(BF16)


def kernel(x, a_norm, a_w_in, a_conv, a_w_out, kv_norm, w_kv, cmp_pe_k, cmp_w1_k, cmp_w2_k, cmp_pe_v, cmp_w1_v,
           cmp_w2_v, b_norm, b_w_qg, b_w_o, f_norm, f_w_gu, f_w_down, final_norm):
    b, s, d = x.shape
    nq = N_HEADS * HEAD_DIM
    h = _mixer_a(x, a_norm[0], a_w_in[0].astype(BF16), a_conv[0], a_w_out[0].astype(BF16))
    h = _ffn(h.reshape(b * s, d), f_norm[0], f_w_gu[0].astype(BF16), f_w_down[0].astype(BF16), final_norm)
    h = h.reshape(b, s, d)

    kvc, kas, vts, kaw, vtw, q, gates = _proj(h, kv_norm, b_norm[0], w_kv.astype(BF16), _qg_weight(b_w_qg[0]),
                                              _q_constants())
    pe2 = jnp.stack([cmp_pe_k, cmp_pe_v]).reshape(2, 1, L_CMP * HEAD_DIM)
    pe2 = jnp.broadcast_to(pe2, (2, 8, L_CMP * HEAD_DIM)).astype(BF16)
    w2 = jnp.pad(jnp.stack([cmp_w2_k, cmp_w2_v]), ((0, 0), (0, 0), (0, LANES - HEAD_DIM))).astype(BF16)
    kca, vct = _compress(kvc, pe2, jnp.stack([cmp_w1_k, cmp_w1_v]).astype(BF16), w2)

    oc, negm, used = _nsa_cmp(q, kca, vct, gates)
    lists, counts = _super_lists(used.reshape(used.shape[0], MAX_SLC_BLOCKS))
    o = _nsa_slc(lists.reshape(-1), counts[0], q, kas, vts, kaw, vtw, negm, gates, oc)

    out = _ffn(h.reshape(b * s, d), f_norm[1], f_w_gu[1].astype(BF16), f_w_down[1].astype(BF16), final_norm,
               o=o.reshape(b * s, nq), w_o=b_w_o[0].astype(BF16), with_final=True)
    return out.reshape(b, s, d)
```

```python
import functools
import math

import jax
import jax.numpy as jnp
from jax import lax
from jax.experimental import pallas as pl
from jax.experimental.pallas import tpu as pltpu

F32 = jnp.float32
BF16 = jnp.bfloat16

N_HEADS = 16
N_KV_GROUPS = 4
HEADS_PER_GROUP = N_HEADS // N_KV_GROUPS
HEAD_DIM = 64
L_CMP = 32
D_CMP = 16
L_SLC = 64
N_SEL = 16
N_FORCED = 3
WINDOW = 512
N_KV_SETS = 6
CONV_WIDTH = 3
RMS_EPS = 1e-5
NEG_INF = -1e30
PICKED = -3.0e38
LOG2E = math.log2(math.e)
LANES = 128
MAX_SLC_BLOCKS = LANES
SUPER_KEYS = LANES
SUPER_SHIFT = (SUPER_KEYS // L_SLC).bit_length() - 1
MAX_SUPER = MAX_SLC_BLOCKS * L_SLC // SUPER_KEYS
CHUNK_SUPERS = 4
FINAL_SLOTS = 10
SHORT_SLOTS = 8
CMP_ROW_TILE = 128
VT_ROWS = HEAD_DIM + 16
POS_LANE = HEAD_DIM
N_SPLIT = 3
PAD_LANE = POS_LANE + 2 * N_SPLIT
VMEM_LIMIT_BYTES = 56 * 1024 * 1024


def _rms(x, g):
    return x * lax.rsqrt(jnp.mean(x * x, axis=-1, keepdims=True) + RMS_EPS) * g


def _dot(a, b):
    return jnp.dot(a, b, preferred_element_type=F32)


def _dot_nt(a, b):
    return lax.dot_general(a, b, (((1,), (1,)), ((), ())), preferred_element_type=F32)


def _params(*sem, flags=None):
    return pltpu.CompilerParams(dimension_semantics=sem, vmem_limit_bytes=VMEM_LIMIT_BYTES, flags=flags)


def _pos_columns(coarse, fine, lane):
    is_c = (lane >= POS_LANE) & (lane < POS_LANE + N_SPLIT)
    is_f = (lane >= POS_LANE + N_SPLIT) & (lane < POS_LANE + 2 * N_SPLIT)
    return jnp.where(is_c, coarse, jnp.where(is_f, fine, 0.0))


def _mixer_a_kernel(x_ref, g_ref, win_ref, conv_ref, wout_ref, o_ref, ext_ref, *, tm, d, tc):
    @pl.when(pl.program_id(1) == 0)
    def _():
        ext_ref[0:8, :] = jnp.zeros((8, d), F32)

    x = x_ref[0]
    xb = _rms(x, g_ref[...]).astype(BF16)
    y = x
    for c0 in range(0, d, tc):
        cols = slice(c0, c0 + tc)
        b_gate = _dot(xb, win_ref[:, c0:c0 + tc])
        cv = _dot(xb, win_ref[:, d + c0:d + c0 + tc]) * _dot(xb, win_ref[:, 2 * d + c0:2 * d + c0 + tc])
        ext_ref[8:8 + tm, cols] = cv
        cv1 = ext_ref[7:7 + tm, cols]
        cv2 = ext_ref[6:6 + tm, cols]
        u = conv_ref[0:1, cols] * cv2 + conv_ref[1:2, cols] * cv1 + conv_ref[2:3, cols] * cv
        ext_ref[0:8, cols] = cv[tm - 8:tm, :]
        y = y + _dot((b_gate * u).astype(BF16), wout_ref[cols, :])
    o_ref[0] = y


def _mixer_a(x, g, w_in, conv_w, w_out, tm=1024, tc=1024):
    b, s, d = x.shape
    assert s % tm == 0 and d % tc == 0
    return pl.pallas_call(
        functools.partial(_mixer_a_kernel, tm=tm, d=d, tc=tc),
        grid=(b, s // tm),
        in_specs=[
            pl.BlockSpec((1, tm, d), lambda i, j: (i, j, 0)),
            _resident((1, d)), _resident((d, 3 * d)), _resident((CONV_WIDTH, d)), _resident((d, d)),
        ],
        out_specs=pl.BlockSpec((1, tm, d), lambda i, j: (i, j, 0)),
        out_shape=jax.ShapeDtypeStruct((b, s, d), F32),
        scratch_shapes=[pltpu.VMEM((tm + 8, d), F32)],
        compiler_params=_params("arbitrary", "arbitrary"),
        name="mixer_a",
    )(x, g.reshape(1, d), w_in, conv_w, w_out)


def _ffn_kernel(*refs, with_proj, with_final, f, tf):
    if with_proj:
        h_ref, o_ref_in, wo_ref, g_ref, wgu_ref, wd_ref, fg_ref, out_ref = refs
    else:
        h_ref, g_ref, wgu_ref, wd_ref, fg_ref, out_ref = refs
    h = h_ref[...]
    if with_proj:
        h = h + _dot(o_ref_in[...], wo_ref[...])
    xb = _rms(h, g_ref[...]).astype(BF16)
    y = h
    for c in range(f // tf):
        gate = _dot(xb, wgu_ref[:, c * tf:(c + 1) * tf])
        up = _dot(xb, wgu_ref[:, f + c * tf:f + (c + 1) * tf])
        act = (gate * jax.nn.sigmoid(gate)) * up
        y = y + _dot(act.astype(BF16), wd_ref[c * tf:(c + 1) * tf, :])
    if with_final:
        y = _rms(y, fg_ref[...])
    out_ref[...] = y


def _resident(shape):
    return pl.BlockSpec(shape, lambda *_: (0,) * len(shape), pipeline_mode=pl.Buffered(1))


def _ffn(h, g, w_gu, w_down, final_g, o=None, w_o=None, with_final=False, tm=1024, tf=256):
    t, d = h.shape
    f = w_down.shape[0]
    assert t % tm == 0 and f % tf == 0
    with_proj = o is not None
    in_specs = [pl.BlockSpec((tm, d), lambda i: (i, 0))]
    args = [h]
    if with_proj:
        in_specs += [pl.BlockSpec((tm, o.shape[1]), lambda i: (i, 0)), _resident(w_o.shape)]
        args += [o, w_o]
    in_specs += [_resident((1, d)), _resident(w_gu.shape), _resident(w_down.shape), _resident((1, d))]
    args += [g.reshape(1, d), w_gu, w_down, final_g.reshape(1, d)]
    return pl.pallas_call(
        functools.partial(_ffn_kernel, with_proj=with_proj, with_final=with_final, f=f, tf=tf),
        grid=(t // tm,),
        in_specs=in_specs,
        out_specs=pl.BlockSpec((tm, d), lambda i: (i, 0)),
        out_shape=jax.ShapeDtypeStruct((t, d), F32),
        compiler_params=_params("arbitrary"),
        name="ffn_proj" if with_proj else "ffn",
    )(*args)


def _halves(x):
    return x, pltpu.roll(x, HEAD_DIM, axis=1)


def _proj_kernel(h_ref, gkv_ref, gq_ref, wkv_ref, wqg_ref, qc_ref,
                 kvc_ref, kas_ref, vts_ref, kaw_ref, vtw_ref, q_ref, gate_ref, rel_ref, *, tm):
    g, dh = N_KV_GROUPS, HEAD_DIM
    gw = g * dh
    s0 = pl.program_id(1) * tm
    x = h_ref[0]
    y = x * lax.rsqrt(jnp.mean(x * x, axis=-1, keepdims=True) + RMS_EPS)
    kv = _dot((y * gkv_ref[...]).astype(BF16), wkv_ref[...])
    low_m = lax.broadcasted_iota(jnp.int32, (tm // D_CMP, LANES), 1) < dh
    for st in range(2):
        for pair in range(g // 2):
            slab = rel_ref.at[st * (g // 2) + pair]
            slab[...] = kv[:, st * gw + pair * LANES:st * gw + (pair + 1) * LANES]
            for j in range(D_CMP // 2):
                a = slab[pl.ds(2 * j, tm // D_CMP, stride=D_CMP), :]
                b = slab[pl.ds(2 * j + 1, tm // D_CMP, stride=D_CMP), :]
                cols = slice(j * LANES, (j + 1) * LANES)
                kvc_ref[st, 0, 2 * pair, :, cols] = jnp.where(low_m, a, pltpu.roll(b, dh, axis=1)).astype(BF16)
                kvc_ref[st, 0, 2 * pair + 1, :, cols] = jnp.where(low_m, pltpu.roll(a, dh, axis=1), b).astype(BF16)

    kpos = s0 + lax.broadcasted_iota(jnp.int32, (tm, LANES), 0)
    lane = lax.broadcasted_iota(jnp.int32, (tm, LANES), 1)
    low = lane < dh
    kblk = jnp.right_shift(kpos, L_SLC.bit_length() - 1)
    posc = _pos_columns(kblk.astype(F32), jnp.bitwise_and(kpos, L_SLC - 1).astype(F32), lane)
    onehot = (lane == kblk).astype(BF16)
    ones_rows = (lax.broadcasted_iota(jnp.int32, (VT_ROWS - dh, tm), 0) == 0).astype(BF16)
    for pair in range(g // 2):
        for st, k_ref, vt_ref in ((2, kas_ref, vts_ref), (4, kaw_ref, vtw_ref)):
            k_pair = kv[:, st * gw + pair * LANES:st * gw + (pair + 1) * LANES]
            vt = kv[:, (st + 1) * gw + pair * LANES:(st + 1) * gw + (pair + 1) * LANES].T
            for half, kh in enumerate(_halves(k_pair)):
                grp = 2 * pair + half
                k_ref[0, grp, :, 0:LANES] = jnp.where(low, kh, posc).astype(BF16)
                if st == 2:
                    k_ref[0, grp, :, LANES:2 * LANES] = onehot
                vt_ref[0, grp, 0:dh] = vt[half * dh:(half + 1) * dh].astype(BF16)
                vt_ref[0, grp, dh:VT_ROWS] = ones_rows

    qg = _dot((y * gq_ref[...]).astype(BF16), wqg_ref[...])
    for pair in range(N_HEADS // 2):
        q_pair = qg[:, pair * LANES:(pair + 1) * LANES] * (LOG2E * dh ** -0.5)
        for half, qh in enumerate(_halves(q_pair)):
            hd = 2 * pair + half
            q_ref[0, hd] = jnp.where(low, qh, qc_ref[hd:hd + 1, :]).astype(BF16)
    nq = N_HEADS * dh
    for grp in range(g):
        gate_ref[0, grp] = jax.nn.sigmoid(qg[:, nq + grp * LANES: nq + (grp + 1) * LANES])


def _proj(h, g_kv, g_q, w_kv, w_qg, qconst, tm=1024):
    b, s, d = h.shape
    assert s % tm == 0 and s // L_SLC <= MAX_SLC_BLOCKS
    g, dh = N_KV_GROUPS, HEAD_DIM
    return pl.pallas_call(
        functools.partial(_proj_kernel, tm=tm),
        grid=(b, s // tm),
        in_specs=[
            pl.BlockSpec((1, tm, d), lambda i, j: (i, j, 0)),
            _resident((1, d)), _resident((1, d)), _resident(w_kv.shape), _resident(w_qg.shape),
            _resident(qconst.shape),
        ],
        out_specs=[
            pl.BlockSpec((2, 1, g, tm // D_CMP, D_CMP * dh), lambda i, j: (0, i, 0, j, 0)),
            pl.BlockSpec((1, g, tm, 2 * LANES), lambda i, j: (i, 0, j, 0)),
            pl.BlockSpec((1, g, VT_ROWS, tm), lambda i, j: (i, 0, 0, j)),
            pl.BlockSpec((1, g, tm, LANES), lambda i, j: (i, 0, j, 0)),
            pl.BlockSpec((1, g, VT_ROWS, tm), lambda i, j: (i, 0, 0, j)),
            pl.BlockSpec((1, N_HEADS, tm, LANES), lambda i, j: (i, 0, j, 0)),
            pl.BlockSpec((1, g, tm, LANES), lambda i, j: (i, 0, j, 0)),
        ],
        out_shape=[
            jax.ShapeDtypeStruct((2, b, g, s // D_CMP, D_CMP * dh), BF16),
            jax.ShapeDtypeStruct((b, g, s, 2 * LANES), BF16),
            jax.ShapeDtypeStruct((b, g, VT_ROWS, s), BF16),
            jax.ShapeDtypeStruct((b, g, s, LANES), BF16),
            jax.ShapeDtypeStruct((b, g, VT_ROWS, s), BF16),
            jax.ShapeDtypeStruct((b, N_HEADS, s, LANES), BF16),
            jax.ShapeDtypeStruct((b, g, s, LANES), F32),
        ],
        scratch_shapes=[pltpu.VMEM((2 * (g // 2), tm, LANES), F32)],
        compiler_params=_params("arbitrary", "arbitrary"),
        name="proj",
    )(h, g_kv.reshape(1, d), g_q.reshape(1, d), w_kv, w_qg, qconst)


def _compress_kernel(r_ref, pe_ref, w1_ref, w2_ref, kca_ref, vct_ref, *, m):
    half = D_CMP * HEAD_DIM
    row = lax.broadcasted_iota(jnp.int32, (m, LANES), 0)
    lane = lax.broadcasted_iota(jnp.int32, (m, LANES), 1)

    def mlp(i):
        r = r_ref[i, 0]
        top = _dot(r, w1_ref[i, 0:half, :])
        bot = _dot(r, w1_ref[i, half:2 * half, :])
        pe_term = _dot(pe_ref[i], w1_ref[i])[0:1, :]
        hid = top + pltpu.roll(bot, m - 1, axis=0) + pe_term
        hid = hid * jax.nn.sigmoid(hid)
        out = _dot(hid.astype(BF16), w2_ref[i])
        return jnp.where(row < m - 1, out, 0.0)

    coarse = jnp.right_shift(row, 2).astype(F32)
    fine = jnp.bitwise_and(row, 3).astype(F32) * D_CMP + (L_CMP - 1) / 2.0
    kca_ref[0] = (mlp(0) + _pos_columns(coarse, fine, lane)).astype(BF16)
    vct_ref[0] = jnp.where(lane == HEAD_DIM, 1.0, mlp(1)).T[0:VT_ROWS].astype(BF16)


def _compress(kvc, pe2, w1_2, w2_2):
    _, b, g, m, _ = kvc.shape
    dh = HEAD_DIM
    r = kvc.reshape(2, b * g, m, D_CMP * dh)
    hid = w1_2.shape[-1]
    return pl.pallas_call(
        functools.partial(_compress_kernel, m=m),
        grid=(b * g,),
        in_specs=[
            pl.BlockSpec((2, 1, m, D_CMP * dh), lambda j: (0, j, 0, 0)),
            pl.BlockSpec((2, 8, L_CMP * dh), lambda j: (0, 0, 0)),
            pl.BlockSpec((2, L_CMP * dh, hid), lambda j: (0, 0, 0)),
            pl.BlockSpec((2, hid, LANES), lambda j: (0, 0, 0)),
        ],
        out_specs=[
            pl.BlockSpec((1, m, LANES), lambda j: (j, 0, 0)),
            pl.BlockSpec((1, VT_ROWS, m), lambda j: (j, 0, 0)),
        ],
        out_shape=[
            jax.ShapeDtypeStruct((b * g, m, LANES), BF16),
            jax.ShapeDtypeStruct((b * g, VT_ROWS, m), BF16),
        ],
        compiler_params=_params("arbitrary"),
        name="compress",
    )(r, pe2, w1_2, w2_2)


def _select_rows(pred, a, b):
    a32, b32 = pltpu.bitcast(a, jnp.int32), pltpu.bitcast(b, jnp.int32)
    return pltpu.bitcast(jnp.where(pred, a32, b32), BF16)


def _head_stack(ot, gt, branch, tq):
    return jnp.concatenate(
        [ot[:, h * tq:(h + 1) * tq] * gt[3 * h + branch:3 * h + branch + 1, :] for h in range(HEADS_PER_GROUP)],
        axis=0)


def _nsa_cmp_kernel(q_ref, kca_ref, vct_ref, gate_ref, oc_ref, negm_ref, used_ref, *, tq, m):
    q0 = pl.program_id(2) * tq
    hpg = HEADS_PER_GROUP
    ncol = hpg * tq
    q = q_ref[0].reshape(ncol, LANES)
    tcol = q0 + jnp.bitwise_and(lax.broadcasted_iota(jnp.int32, (1, ncol), 1), tq - 1)

    def attend(rows):
        st = _dot_nt(kca_ref[0, 0:rows, :], q)
        band = max(rows - row_tile - 8, 0)
        nrow = band + lax.broadcasted_iota(jnp.int32, (rows - band, 1), 0)
        low = jnp.where((nrow * D_CMP + (L_CMP - 1)) <= tcol, st[band:], NEG_INF)
        st = low if band == 0 else jnp.concatenate([st[0:band], low], axis=0)
        e = jnp.exp2(st - jnp.max(st, axis=0, keepdims=True)).astype(BF16)
        acc = _dot(vct_ref[0, :, 0:rows], e)
        rcp = jnp.where(tcol >= L_CMP - 1, 1.0 / acc[HEAD_DIM:HEAD_DIM + 1], 0.0)
        ot = acc[0:HEAD_DIM] * rcp
        nblk = min(MAX_SLC_BLOCKS, rows * D_CMP // L_SLC)
        jj = lax.broadcasted_iota(jnp.int32, (nblk, rows), 0) * L_SLC
        nn = lax.broadcasted_iota(jnp.int32, (nblk, rows), 1) * D_CMP
        overlap_t = ((nn < jj + L_SLC) & (nn + L_CMP > jj)).astype(BF16)
        imp_h = _dot(overlap_t, e) * rcp
        imp = imp_h[:, 0:tq]
        for h in range(1, hpg):
            imp = imp + imp_h[:, h * tq:(h + 1) * tq]

        jrow = lax.broadcasted_iota(jnp.int32, (nblk, tq), 0)
        cur = jnp.right_shift(q0 + lax.broadcasted_iota(jnp.int32, (1, tq), 1), L_SLC.bit_length() - 1)
        valid = jrow <= cur
        forced = (jrow == 0) | (jrow == cur) | (jrow == cur - 1)
        val = jnp.where(valid & jnp.logical_not(forced), imp, NEG_INF)
        jrow_f = jrow.astype(F32)
        for _ in range(N_SEL - N_FORCED):
            mx = jnp.max(val, axis=0, keepdims=True)
            first = jnp.min(jnp.where(val == mx, jrow_f, float(MAX_SLC_BLOCKS)), axis=0, keepdims=True)
            val = jnp.where(jrow_f == first, PICKED, val)
        sel = jnp.where(valid & (forced | (val == PICKED)), 1.0, 0.0)
        if nblk < MAX_SLC_BLOCKS:
            sel = jnp.concatenate([sel, jnp.zeros((MAX_SLC_BLOCKS - nblk, tq), F32)], axis=0)
        gt = gate_ref[0, 0].T
        oc_ref[0] = _head_stack(ot, gt, 0, tq).T
        sel_t = sel.T
        negm_ref[0, 0] = jnp.where(sel_t > 0.5, 0.0, NEG_INF).astype(BF16)
        for r in range(tq // SUPER_KEYS):
            used_ref[r] = jnp.max(sel_t[r * SUPER_KEYS:(r + 1) * SUPER_KEYS], axis=0, keepdims=True)

    row_tile = min(m, CMP_ROW_TILE)
    n_variants = m // row_tile
    if n_variants == 1:
        attend(m)
    else:
        needed = pl.program_id(2) * (tq // D_CMP) + (tq // D_CMP - 1)
        variant = jnp.minimum(jnp.right_shift(needed - 1, row_tile.bit_length() - 1), n_variants - 1)
        for v in range(n_variants):
            pl.when(variant == v)(functools.partial(attend, (v + 1) * row_tile))


def _nsa_cmp(q, kca, vct, gates, tq=512):
    b, nh, s, _ = q.shape
    g, hpg, dh = N_KV_GROUPS, HEADS_PER_GROUP, HEAD_DIM
    m = kca.shape[1]
    assert s % tq == 0 and tq % SUPER_KEYS == 0 and tq & (tq - 1) == 0
    nqt = s // tq
    sub = tq // SUPER_KEYS
    return pl.pallas_call(
        functools.partial(_nsa_cmp_kernel, tq=tq, m=m),
        grid=(b, g, s // tq),
        in_specs=[
            pl.BlockSpec((1, hpg, tq, LANES), lambda i, j, k: (i, j, k, 0)),
            pl.BlockSpec((1, m, LANES), lambda i, j, k: (i * N_KV_GROUPS + j, 0, 0)),
            pl.BlockSpec((1, VT_ROWS, m), lambda i, j, k: (i * N_KV_GROUPS + j, 0, 0)),
            pl.BlockSpec((1, 1, tq, LANES), lambda i, j, k: (i, j, k, 0)),
        ],
        out_specs=[
            pl.BlockSpec((1, tq, hpg * dh), lambda i, j, k: (i, k, j)),
            pl.BlockSpec((1, 1, tq, MAX_SLC_BLOCKS), lambda i, j, k: (i, j, k, 0)),
            pl.BlockSpec((sub, 1, MAX_SLC_BLOCKS), lambda i, j, k: ((i * N_KV_GROUPS + j) * nqt + k, 0, 0)),
        ],
        out_shape=[
            jax.ShapeDtypeStruct((b, s, nh * dh), F32),
            jax.ShapeDtypeStruct((b, g, s, MAX_SLC_BLOCKS), BF16),
            jax.ShapeDtypeStruct((b * g * nqt * sub, 1, MAX_SLC_BLOCKS), F32),
        ],
        compiler_params=_params("arbitrary", "arbitrary", "arbitrary"),
        name="nsa_cmp",
    )(q, kca, vct, gates)


def _super_lists_kernel(used_ref, lst_ref, cnt_ref, *, tb):
    ut = used_ref[...].T.astype(BF16)
    a_i = lax.broadcasted_iota(jnp.int32, (MAX_SLC_BLOCKS, MAX_SLC_BLOCKS), 0)
    b_i = lax.broadcasted_iota(jnp.int32, (MAX_SLC_BLOCKS, MAX_SLC_BLOCKS), 1)
    pair_t = (a_i == jnp.right_shift(b_i, SUPER_SHIFT)).astype(BF16)
    sbf = jnp.where(_dot(pair_t, ut) > 0.5, 1.0, 0.0).astype(BF16)
    incl = _dot((b_i <= a_i).astype(BF16), sbf)[0:MAX_SUPER]
    cnt_ref[...] = jnp.broadcast_to(incl[MAX_SUPER - 1:MAX_SUPER, :], (8, tb)).astype(jnp.int32)
    for i in range(MAX_SUPER):
        lst_ref[i:i + 1, :] = jnp.sum(jnp.where(incl <= float(i), 1.0, 0.0), axis=0, keepdims=True).astype(jnp.int32)


def _super_lists(used):
    t = used.shape[0]
    tb = min(t, LANES)
    assert t % tb == 0
    return pl.pallas_call(
        functools.partial(_super_lists_kernel, tb=tb),
        grid=(t // tb,),
        in_specs=[pl.BlockSpec((tb, MAX_SLC_BLOCKS), lambda i: (i, 0))],
        out_specs=[
            pl.BlockSpec((MAX_SUPER, tb), lambda i: (0, i)),
            pl.BlockSpec((8, tb), lambda i: (0, i)),
        ],
        out_shape=[
            jax.ShapeDtypeStruct((MAX_SUPER, t), jnp.int32),
            jax.ShapeDtypeStruct((8, t), jnp.int32),
        ],
        compiler_params=_params("arbitrary"),
        name="super_lists",
    )(used)


def _nsa_slc_kernel(lst_ref, cnt_ref, q_ref, kas_ref, vts_ref, kaw_ref, vtw_ref, negm_ref, gate_ref, oc_ref, o_ref,
                    m_ref, acc_ref, *, tq, sub):
    hpg = HEADS_PER_GROUP
    ncol = hpg * tq
    n_win = WINDOW // SUPER_KEYS + 1
    pad_k = (lax.broadcasted_iota(jnp.int32, (SUPER_KEYS, 2 * LANES), 1) == PAD_LANE).astype(BF16)
    pad_w = (lax.broadcasted_iota(jnp.int32, (SUPER_KEYS, LANES), 1) == PAD_LANE).astype(BF16)
    n_tiles = pl.num_programs(0) * pl.num_programs(1) * pl.num_programs(2) * sub
    tile0 = ((pl.program_id(0) * pl.num_programs(1) + pl.program_id(1)) * pl.num_programs(2) + pl.program_id(2)) * sub
    row = lax.broadcasted_iota(jnp.int32, (SUPER_KEYS, 1), 0)

    class Tile:
        def __init__(self, s, slots):
            self.s = s
            self.slots = slots
            self.rows = slice(s * tq, (s + 1) * tq)
            self.qt = pl.program_id(2) * sub + s
            self.q = q_ref[0, :, self.rows, :].reshape(ncol, LANES)
            negm = negm_ref[0, 0, self.rows, :]
            self.qa = jnp.concatenate([self.q, jnp.concatenate([negm] * hpg, axis=0)], axis=1)
            self.tcol = self.qt * tq + jnp.bitwise_and(lax.broadcasted_iota(jnp.int32, (1, ncol), 1), tq - 1)
            self.tile = tile0 + s
            self.n_other = cnt_ref[self.tile] - 1
            self.n_early = jnp.right_shift(jnp.maximum(self.n_other - slots, 0) + CHUNK_SUPERS - 1,
                                           CHUNK_SUPERS.bit_length() - 1)
            m_ref[s] = jnp.full((1, ncol), NEG_INF, F32)
            acc_ref[s] = jnp.zeros((VT_ROWS, ncol), F32)

        def super_block(self, slot, ok=None):
            sb = lst_ref[slot * n_tiles + self.tile]
            if ok is not None:
                sb = jnp.where(ok, sb, 0)
            start = pl.multiple_of(sb * SUPER_KEYS, SUPER_KEYS)
            return kas_ref[0, 0, pl.ds(start, SUPER_KEYS), :], vts_ref[0, 0, :, pl.ds(start, SUPER_KEYS)]

        def update(self, st, vt):
            m_old = m_ref[self.s]
            m_new = jnp.maximum(m_old, jnp.max(st, axis=0, keepdims=True))
            p = jnp.exp2(st - m_new)
            acc_ref[self.s] = jnp.exp2(m_old - m_new) * acc_ref[self.s] + _dot(vt, p.astype(BF16))
            m_ref[self.s] = m_new

        def early_chunks(self):
            def body(c, carry):
                parts = [self.super_block(c * CHUNK_SUPERS + j) for j in range(CHUNK_SUPERS)]
                kt = jnp.concatenate([k for k, _ in parts], axis=0)
                self.update(_dot_nt(kt, self.qa), jnp.concatenate([v for _, v in parts], axis=1))
                return carry

            lax.fori_loop(0, self.n_early, body, 0)

        def final_scores(self):
            dstart = pl.multiple_of(self.qt * SUPER_KEYS, SUPER_KEYS)
            sd = _dot_nt(kas_ref[0, 0, pl.ds(dstart, SUPER_KEYS), :], self.qa)
            sd = jnp.where(self.qt * SUPER_KEYS + row <= self.tcol, sd, NEG_INF)
            ks, vs = [], [vts_ref[0, 0, :, pl.ds(dstart, SUPER_KEYS)]]
            for j in range(self.slots):
                slot = self.n_early * CHUNK_SUPERS + j
                ok = slot < self.n_other
                k, v = self.super_block(jnp.minimum(slot, MAX_SUPER - 1), ok)
                ks.append(_select_rows(ok, k, pad_k))
                vs.append(v)
            self.st = jnp.concatenate([sd, _dot_nt(jnp.concatenate(ks, axis=0), self.qa)], axis=0)
            self.vt = jnp.concatenate(vs, axis=1)
            win_scores, win_v = [], []
            for r in range(n_win):
                sb = self.qt - (n_win - 1) + r
                start = pl.multiple_of(jnp.maximum(sb, 0) * SUPER_KEYS, SUPER_KEYS)
                kw = kaw_ref[0, 0, pl.ds(start, SUPER_KEYS), :]
                if r < n_win - 1:
                    kw = _select_rows(sb >= 0, kw, pad_w)
                sw = _dot_nt(kw, self.q)
                if r == n_win - 1:
                    sw = jnp.where(sb * SUPER_KEYS + row <= self.tcol, sw, NEG_INF)
                elif r == 0:
                    sw = jnp.where(sb * SUPER_KEYS + WINDOW + row > self.tcol, sw, NEG_INF)
                win_scores.append(sw)
                win_v.append(vtw_ref[0, 0, :, pl.ds(start, SUPER_KEYS)])
            self.stw = jnp.concatenate(win_scores, axis=0)
            self.vw = jnp.concatenate(win_v, axis=1)

        def finish(self):
            self.update(self.st, self.vt)
            acc = acc_ref[self.s]
            o_slc_t = acc[0:HEAD_DIM] * (1.0 / acc[HEAD_DIM:HEAD_DIM + 1])
            e = jnp.exp2(self.stw - jnp.max(self.stw, axis=0, keepdims=True))
            acc_w = _dot(self.vw, e.astype(BF16))
            o_win_t = acc_w[0:HEAD_DIM] * (1.0 / acc_w[HEAD_DIM:HEAD_DIM + 1])
            gt = gate_ref[0, 0, self.rows, :].T
            comb = _head_stack(o_slc_t, gt, 1, tq) + _head_stack(o_win_t, gt, 2, tq)
            o_ref[0, self.rows, :] = (oc_ref[0, self.rows, :] + comb.T).astype(BF16)

    def run(slots):
        tiles = [Tile(s, slots) for s in range(sub)]
        for t in tiles:
            t.early_chunks()
        for t in tiles:
            t.final_scores()
        for t in tiles:
            t.finish()

    longest = cnt_ref[tile0] - 1
    for s in range(1, sub):
        longest = jnp.maximum(longest, cnt_ref[tile0 + s] - 1)
    pl.when(longest <= SHORT_SLOTS)(functools.partial(run, SHORT_SLOTS))
    pl.when(longest > SHORT_SLOTS)(functools.partial(run, FINAL_SLOTS))


def _nsa_slc(lists, counts, q, kas, vts, kaw, vtw, negm, gates, oc, tq=128, sub=4):
    b, nh, s, _ = q.shape
    g, hpg, dh = N_KV_GROUPS, HEADS_PER_GROUP, HEAD_DIM
    ts = tq * sub
    assert tq == SUPER_KEYS and WINDOW % tq == 0 and s >= WINDOW + tq and s % ts == 0
    grid_spec = pltpu.PrefetchScalarGridSpec(
        num_scalar_prefetch=2,
        grid=(b, g, s // ts),
        in_specs=[
            pl.BlockSpec((1, hpg, ts, LANES), lambda i, j, k, *_: (i, j, k, 0)),
            pl.BlockSpec((1, 1, s, 2 * LANES), lambda i, j, k, *_: (i, j, 0, 0)),
            pl.BlockSpec((1, 1, VT_ROWS, s), lambda i, j, k, *_: (i, j, 0, 0)),
            pl.BlockSpec((1, 1, s, LANES), lambda i, j, k, *_: (i, j, 0, 0)),
            pl.BlockSpec((1, 1, VT_ROWS, s), lambda i, j, k, *_: (i, j, 0, 0)),
            pl.BlockSpec((1, 1, ts, MAX_SLC_BLOCKS), lambda i, j, k, *_: (i, j, k, 0)),
            pl.BlockSpec((1, 1, ts, LANES), lambda i, j, k, *_: (i, j, k, 0)),
            pl.BlockSpec((1, ts, hpg * dh), lambda i, j, k, *_: (i, k, j)),
        ],
        out_specs=pl.BlockSpec((1, ts, hpg * dh), lambda i, j, k, *_: (i, k, j)),
        scratch_shapes=[
            pltpu.VMEM((sub, 1, hpg * tq), F32),
            pltpu.VMEM((sub, VT_ROWS, hpg * tq), F32),
        ],
    )
    return pl.pallas_call(
        functools.partial(_nsa_slc_kernel, tq=tq, sub=sub),
        grid_spec=grid_spec,
        out_shape=jax.ShapeDtypeStruct((b, s, nh * dh), BF16),
        compiler_params=_params("arbitrary", "arbitrary", "arbitrary"),
        name="nsa_slc",
    )(lists, counts, q, kas, vts, kaw, vtw, negm, gates, oc)


def _split_bf16(x, n):
    pieces = []
    for _ in range(n):
        p = x.astype(BF16).astype(F32)
        pieces.append(p)
        x = x - p
    return pieces


def _q_constants():
    slopes = jnp.exp2(-8.0 * jnp.arange(1, N_HEADS + 1, dtype=F32) / N_HEADS) * LOG2E
    pieces = jnp.stack(_split_bf16(slopes, N_SPLIT), axis=1)
    qc = jnp.zeros((N_HEADS, LANES), F32)
    qc = qc.at[:, POS_LANE:POS_LANE + N_SPLIT].set(pieces * L_SLC)
    qc = qc.at[:, POS_LANE + N_SPLIT:POS_LANE + 2 * N_SPLIT].set(pieces)
    return qc.at[:, PAD_LANE].set(NEG_INF)


def _qg_weight(w_qg):
    d = w_qg.shape[0]
    nq = N_HEADS * HEAD_DIM
    wg = jnp.pad(w_qg[:, nq:].reshape(d, N_KV_GROUPS, 3 * HEADS_PER_GROUP),
                 ((0, 0), (0, 0), (0, LANES - 3 * HEADS_PER_GROUP)))
    return jnp.concatenate([w_qg[:, :nq], wg.reshape(d, N_KV_GROUPS * LANES)], axis=1).astype(BF16)


def kernel(x, a_norm, a_w_in, a_conv, a_w_out, kv_norm, w_kv, cmp_pe_k, cmp_w1_k, cmp_w2_k, cmp_pe_v, cmp_w1_v,
           cmp_w2_v, b_norm, b_w_qg, b_w_o, f_norm, f_w_gu, f_w_down, final_norm):
    b, s, d = x.shape
    nq = N_HEADS * HEAD_DIM
    h = _mixer_a(x, a_norm[0], a_w_in[0].astype(BF16), a_conv[0], a_w_out[0].astype(BF16))
    h = _ffn(h.reshape(b * s, d), f_norm[0], f_w_gu[0].astype(BF16), f_w_down[0].astype(BF16), final_norm)
    h = h.reshape(b, s, d)

    kvc, kas, vts, kaw, vtw, q, gates = _proj(h, kv_norm, b_norm[0], w_kv.astype(BF16), _qg_weight(b_w_qg[0]),
                                              _q_constants())
    pe2 = jnp.stack([cmp_pe_k, cmp_pe_v]).reshape(2, 1, L_CMP * HEAD_DIM)
    pe2 = jnp.broadcast_to(pe2, (2, 8, L_CMP * HEAD_DIM)).astype(BF16)
    w2 = jnp.pad(jnp.stack([cmp_w2_k, cmp_w2_v]), ((0, 0), (0, 0), (0, LANES - HEAD_DIM))).astype(BF16)
    kca, vct = _compress(kvc, pe2, jnp.stack([cmp_w1_k, cmp_w1_v]).astype(BF16), w2)

    oc, negm, used = _nsa_cmp(q, kca, vct, gates)
    lists, counts = _super_lists(used.reshape(used.shape[0], MAX_SLC_BLOCKS))
    o = _nsa_slc(lists.reshape(-1), counts[0], q, kas, vts, kaw, vtw, negm, gates, oc)

    out = _ffn(h.reshape(b * s, d), f_norm[1], f_w_gu[1].astype(BF16), f_w_down[1].astype(BF16), final_norm,
               o=o.reshape(b * s, nq), w_o=b_w_o[0].astype(BF16), with_final=True)
    return out.reshape(b, s, d)
```

```python
import functools
import math

import jax
import jax.numpy as jnp
from jax import lax
from jax.experimental import pallas as pl
from jax.experimental.pallas import tpu as pltpu

F32 = jnp.float32
BF16 = jnp.bfloat16

N_HEADS = 16
N_KV_GROUPS = 4
HEADS_PER_GROUP = N_HEADS // N_KV_GROUPS
HEAD_DIM = 64
L_CMP = 32
D_CMP = 16
L_SLC = 64
N_SEL = 16
N_FORCED = 3
WINDOW = 512
CONV_WIDTH = 3
RMS_EPS = 1e-5
NEG_INF = -1e30
PICKED = -3.0e38
LOG2E = math.log2(math.e)
LANES = 128
MAX_SLC_BLOCKS = LANES
SUPER_KEYS = LANES
SUPER_SHIFT = (SUPER_KEYS // L_SLC).bit_length() - 1
MAX_SUPER = MAX_SLC_BLOCKS * L_SLC // SUPER_KEYS
CHUNK_SUPERS = 4
FINAL_SLOTS = 10
SHORT_SLOTS = 8
CMP_ROW_TILE = 128
VT_ROWS = HEAD_DIM + 16
POS_LANE = HEAD_DIM
N_SPLIT = 3
PAD_LANE = POS_LANE + 2 * N_SPLIT
VMEM_LIMIT_BYTES = 56 * 1024 * 1024


def _rms(x, g):
    return x * lax.rsqrt(jnp.mean(x * x, axis=-1, keepdims=True) + RMS_EPS) * g


def _dot(a, b):
    return jnp.dot(a, b, preferred_element_type=F32)


def _dot_nt(a, b):
    return lax.dot_general(a, b, (((1,), (1,)), ((), ())), preferred_element_type=F32)


def _params(*sem):
    return pltpu.CompilerParams(dimension_semantics=sem, vmem_limit_bytes=VMEM_LIMIT_BYTES)


def _pos_columns(coarse, fine, lane):
    is_c = (lane >= POS_LANE) & (lane < POS_LANE + N_SPLIT)
    is_f = (lane >= POS_LANE + N_SPLIT) & (lane < POS_LANE + 2 * N_SPLIT)
    return jnp.where(is_c, coarse, jnp.where(is_f, fine, 0.0))


def _mixer_a_kernel(x_ref, g_ref, win_ref, conv_ref, wout_ref, o_ref, ext_ref, *, tm, d):
    @pl.when(pl.program_id(1) == 0)
    def _():
        ext_ref[0:8, :] = jnp.zeros((8, d), F32)

    x = x_ref[0]
    xb = _rms(x, g_ref[...]).astype(BF16)
    b_gate = _dot(xb, win_ref[:, 0:d])
    cv = _dot(xb, win_ref[:, d:2 * d]) * _dot(xb, win_ref[:, 2 * d:3 * d])
    ext_ref[8:8 + tm, :] = cv
    cv1 = ext_ref[7:7 + tm, :]
    cv2 = ext_ref[6:6 + tm, :]
    u = conv_ref[0:1, :] * cv2 + conv_ref[1:2, :] * cv1 + conv_ref[2:3, :] * cv
    ext_ref[0:8, :] = cv[tm - 8:tm, :]
    o_ref[0] = x + _dot((b_gate * u).astype(BF16), wout_ref[...])


def _mixer_a(x, g, w_in, conv_w, w_out, tm=1024):
    b, s, d = x.shape
    assert s % tm == 0
    return pl.pallas_call(
        functools.partial(_mixer_a_kernel, tm=tm, d=d),
        grid=(b, s // tm),
        in_specs=[
            pl.BlockSpec((1, tm, d), lambda i, j: (i, j, 0)),
            _resident((1, d)), _resident((d, 3 * d)), _resident((CONV_WIDTH, d)), _resident((d, d)),
        ],
        out_specs=pl.BlockSpec((1, tm, d), lambda i, j: (i, j, 0)),
        out_shape=jax.ShapeDtypeStruct((b, s, d), F32),
        scratch_shapes=[pltpu.VMEM((tm + 8, d), F32)],
        compiler_params=_params("arbitrary", "arbitrary"),
        name="mixer_a",
    )(x, g.reshape(1, d), w_in, conv_w, w_out)


def _ffn_kernel(*refs, with_proj, with_final, f, tf):
    if with_proj:
        h_ref, o_ref_in, wo_ref, g_ref, wgu_ref, wd_ref, fg_ref, out_ref = refs
    else:
        h_ref, g_ref, wgu_ref, wd_ref, fg_ref, out_ref = refs
    h = h_ref[...]
    if with_proj:
        h = h + _dot(o_ref_in[...], wo_ref[...])
    xb = _rms(h, g_ref[...]).astype(BF16)
    y = h
    for c in range(f // tf):
        gate = _dot(xb, wgu_ref[:, c * tf:(c + 1) * tf])
        up = _dot(xb, wgu_ref[:, f + c * tf:f + (c + 1) * tf])
        act = (gate * jax.nn.sigmoid(gate)) * up
        y = y + _dot(act.astype(BF16), wd_ref[c * tf:(c + 1) * tf, :])
    if with_final:
        y = _rms(y, fg_ref[...])
    out_ref[...] = y


def _resident(shape):
    return pl.BlockSpec(shape, lambda *_: (0,) * len(shape), pipeline_mode=pl.Buffered(1))


def _ffn(h, g, w_gu, w_down, final_g, o=None, w_o=None, with_final=False, tm=1024, tf=256):
    t, d = h.shape
    f = w_down.shape[0]
    assert t % tm == 0 and f % tf == 0
    with_proj = o is not None
    in_specs = [pl.BlockSpec((tm, d), lambda i: (i, 0))]
    args = [h]
    if with_proj:
        in_specs += [pl.BlockSpec((tm, o.shape[1]), lambda i: (i, 0)), _resident(w_o.shape)]
        args += [o, w_o]
    in_specs += [_resident((1, d)), _resident(w_gu.shape), _resident(w_down.shape), _resident((1, d))]
    args += [g.reshape(1, d), w_gu, w_down, final_g.reshape(1, d)]
    return pl.pallas_call(
        functools.partial(_ffn_kernel, with_proj=with_proj, with_final=with_final, f=f, tf=tf),
        grid=(t // tm,),
        in_specs=in_specs,
        out_specs=pl.BlockSpec((tm, d), lambda i: (i, 0)),
        out_shape=jax.ShapeDtypeStruct((t, d), F32),
        compiler_params=_params("arbitrary"),
        name="ffn_proj" if with_proj else "ffn",
    )(*args)


def _halves(x):
    return x, pltpu.roll(x, HEAD_DIM, axis=1)


def _proj_kernel(h_ref, gkv_ref, gq_ref, wkv_ref, wqg_ref, qc_ref,
                 kvc_ref, kas_ref, vts_ref, kaw_ref, vtw_ref, q_ref, gate_ref, rel_ref, *, tm):
    g, dh = N_KV_GROUPS, HEAD_DIM
    gw = g * dh
    s0 = pl.program_id(1) * tm
    x = h_ref[0]
    y = x * lax.rsqrt(jnp.mean(x * x, axis=-1, keepdims=True) + RMS_EPS)
    kv = _dot((y * gkv_ref[...]).astype(BF16), wkv_ref[...])
    low_m = lax.broadcasted_iota(jnp.int32, (tm // D_CMP, LANES), 1) < dh
    for st in range(2):
        for pair in range(g // 2):
            slab = rel_ref.at[st * (g // 2) + pair]
            slab[...] = kv[:, st * gw + pair * LANES:st * gw + (pair + 1) * LANES]
            for j in range(D_CMP // 2):
                a = slab[pl.ds(2 * j, tm // D_CMP, stride=D_CMP), :]
                b = slab[pl.ds(2 * j + 1, tm // D_CMP, stride=D_CMP), :]
                cols = slice(j * LANES, (j + 1) * LANES)
                kvc_ref[st, 0, 2 * pair, :, cols] = jnp.where(low_m, a, pltpu.roll(b, dh, axis=1)).astype(BF16)
                kvc_ref[st, 0, 2 * pair + 1, :, cols] = jnp.where(low_m, pltpu.roll(a, dh, axis=1), b).astype(BF16)

    kpos = s0 + lax.broadcasted_iota(jnp.int32, (tm, LANES), 0)
    lane = lax.broadcasted_iota(jnp.int32, (tm, LANES), 1)
    low = lane < dh
    kblk = jnp.right_shift(kpos, L_SLC.bit_length() - 1)
    posc = _pos_columns(kblk.astype(F32), jnp.bitwise_and(kpos, L_SLC - 1).astype(F32), lane)
    onehot = (lane == kblk).astype(BF16)
    ones_rows = (lax.broadcasted_iota(jnp.int32, (VT_ROWS - dh, tm), 0) == 0).astype(BF16)
    for pair in range(g // 2):
        for st, k_ref, vt_ref in ((2, kas_ref, vts_ref), (4, kaw_ref, vtw_ref)):
            k_pair = kv[:, st * gw + pair * LANES:st * gw + (pair + 1) * LANES]
            vt = kv[:, (st + 1) * gw + pair * LANES:(st + 1) * gw + (pair + 1) * LANES].T
            for half, kh in enumerate(_halves(k_pair)):
                grp = 2 * pair + half
                k_ref[0, grp, :, 0:LANES] = jnp.where(low, kh, posc).astype(BF16)
                if st == 2:
                    k_ref[0, grp, :, LANES:2 * LANES] = onehot
                vt_ref[0, grp, 0:dh] = vt[half * dh:(half + 1) * dh].astype(BF16)
                vt_ref[0, grp, dh:VT_ROWS] = ones_rows

    qg = _dot((y * gq_ref[...]).astype(BF16), wqg_ref[...])
    for pair in range(N_HEADS // 2):
        q_pair = qg[:, pair * LANES:(pair + 1) * LANES] * (LOG2E * dh ** -0.5)
        for half, qh in enumerate(_halves(q_pair)):
            hd = 2 * pair + half
            q_ref[0, hd] = jnp.where(low, qh, qc_ref[hd:hd + 1, :]).astype(BF16)
    nq = N_HEADS * dh
    for grp in range(g):
        gate_ref[0, grp] = jax.nn.sigmoid(qg[:, nq + grp * LANES: nq + (grp + 1) * LANES])


def _proj(h, g_kv, g_q, w_kv, w_qg, qconst, tm=1024):
    b, s, d = h.shape
    assert s % tm == 0 and s // L_SLC <= MAX_SLC_BLOCKS
    g, dh = N_KV_GROUPS, HEAD_DIM
    return pl.pallas_call(
        functools.partial(_proj_kernel, tm=tm),
        grid=(b, s // tm),
        in_specs=[
            pl.BlockSpec((1, tm, d), lambda i, j: (i, j, 0)),
            _resident((1, d)), _resident((1, d)), _resident(w_kv.shape), _resident(w_qg.shape),
            _resident(qconst.shape),
        ],
        out_specs=[
            pl.BlockSpec((2, 1, g, tm // D_CMP, D_CMP * dh), lambda i, j: (0, i, 0, j, 0)),
            pl.BlockSpec((1, g, tm, 2 * LANES), lambda i, j: (i, 0, j, 0)),
            pl.BlockSpec((1, g, VT_ROWS, tm), lambda i, j: (i, 0, 0, j)),
            pl.BlockSpec((1, g, tm, LANES), lambda i, j: (i, 0, j, 0)),
            pl.BlockSpec((1, g, VT_ROWS, tm), lambda i, j: (i, 0, 0, j)),
            pl.BlockSpec((1, N_HEADS, tm, LANES), lambda i, j: (i, 0, j, 0)),
            pl.BlockSpec((1, g, tm, LANES), lambda i, j: (i, 0, j, 0)),
        ],
        out_shape=[
            jax.ShapeDtypeStruct((2, b, g, s // D_CMP, D_CMP * dh), BF16),
            jax.ShapeDtypeStruct((b, g, s, 2 * LANES), BF16),
            jax.ShapeDtypeStruct((b, g, VT_ROWS, s), BF16),
            jax.ShapeDtypeStruct((b, g, s, LANES), BF16),
            jax.ShapeDtypeStruct((b, g, VT_ROWS, s), BF16),
            jax.ShapeDtypeStruct((b, N_HEADS, s, LANES), BF16),
            jax.ShapeDtypeStruct((b, g, s, LANES), F32),
        ],
        scratch_shapes=[pltpu.VMEM((2 * (g // 2), tm, LANES), F32)],
        compiler_params=_params("arbitrary", "arbitrary"),
        name="proj",
    )(h, g_kv.reshape(1, d), g_q.reshape(1, d), w_kv, w_qg, qconst)


def _compress_kernel(r_ref, pe_ref, w1_ref, w2_ref, kca_ref, vct_ref, *, m):
    half = D_CMP * HEAD_DIM
    row = lax.broadcasted_iota(jnp.int32, (m, LANES), 0)
    lane = lax.broadcasted_iota(jnp.int32, (m, LANES), 1)

    def mlp(i):
        r = r_ref[i, 0]
        top = _dot(r, w1_ref[i, 0:half, :])
        bot = _dot(r, w1_ref[i, half:2 * half, :])
        pe_term = _dot(pe_ref[i], w1_ref[i])[0:1, :]
        hid = top + pltpu.roll(bot, m - 1, axis=0) + pe_term
        hid = hid * jax.nn.sigmoid(hid)
        out = _dot(hid.astype(BF16), w2_ref[i])
        return jnp.where(row < m - 1, out, 0.0)

    coarse = jnp.right_shift(row, 2).astype(F32)
    fine = jnp.bitwise_and(row, 3).astype(F32) * D_CMP + (L_CMP - 1) / 2.0
    kca_ref[0] = (mlp(0) + _pos_columns(coarse, fine, lane)).astype(BF16)
    vct_ref[0] = jnp.where(lane == HEAD_DIM, 1.0, mlp(1)).T[0:VT_ROWS].astype(BF16)


def _compress(kvc, pe2, w1_2, w2_2):
    _, b, g, m, _ = kvc.shape
    dh = HEAD_DIM
    r = kvc.reshape(2, b * g, m, D_CMP * dh)
    hid = w1_2.shape[-1]
    return pl.pallas_call(
        functools.partial(_compress_kernel, m=m),
        grid=(b * g,),
        in_specs=[
            pl.BlockSpec((2, 1, m, D_CMP * dh), lambda j: (0, j, 0, 0)),
            pl.BlockSpec((2, 8, L_CMP * dh), lambda j: (0, 0, 0)),
            pl.BlockSpec((2, L_CMP * dh, hid), lambda j: (0, 0, 0)),
            pl.BlockSpec((2, hid, LANES), lambda j: (0, 0, 0)),
        ],
        out_specs=[
            pl.BlockSpec((1, m, LANES), lambda j: (j, 0, 0)),
            pl.BlockSpec((1, VT_ROWS, m), lambda j: (j, 0, 0)),
        ],
        out_shape=[
            jax.ShapeDtypeStruct((b * g, m, LANES), BF16),
            jax.ShapeDtypeStruct((b * g, VT_ROWS, m), BF16),
        ],
        compiler_params=_params("arbitrary"),
        name="compress",
    )(r, pe2, w1_2, w2_2)


def _select_rows(pred, a, b):
    a32, b32 = pltpu.bitcast(a, jnp.int32), pltpu.bitcast(b, jnp.int32)
    return pltpu.bitcast(jnp.where(pred, a32, b32), BF16)


def _head_stack(ot, gt, branch, tq):
    return jnp.concatenate(
        [ot[:, h * tq:(h + 1) * tq] * gt[3 * h + branch:3 * h + branch + 1, :] for h in range(HEADS_PER_GROUP)],
        axis=0)


def _nsa_cmp_kernel(q_ref, kca_ref, vct_ref, gate_ref, oc_ref, negm_ref, used_ref, *, tq, m):
    q0 = pl.program_id(2) * tq
    hpg = HEADS_PER_GROUP
    ncol = hpg * tq
    q = q_ref[0].reshape(ncol, LANES)
    tcol = q0 + jnp.bitwise_and(lax.broadcasted_iota(jnp.int32, (1, ncol), 1), tq - 1)

    def attend(rows):
        st = _dot_nt(kca_ref[0, 0:rows, :], q)
        band = max(rows - row_tile - 8, 0)
        nrow = band + lax.broadcasted_iota(jnp.int32, (rows - band, 1), 0)
        low = jnp.where((nrow * D_CMP + (L_CMP - 1)) <= tcol, st[band:], NEG_INF)
        st = low if band == 0 else jnp.concatenate([st[0:band], low], axis=0)
        e = jnp.exp2(st - jnp.max(st, axis=0, keepdims=True)).astype(BF16)
        acc = _dot(vct_ref[0, :, 0:rows], e)
        rcp = jnp.where(tcol >= L_CMP - 1, 1.0 / acc[HEAD_DIM:HEAD_DIM + 1], 0.0)
        ot = acc[0:HEAD_DIM] * rcp
        nblk = min(MAX_SLC_BLOCKS, rows * D_CMP // L_SLC)
        jj = lax.broadcasted_iota(jnp.int32, (nblk, rows), 0) * L_SLC
        nn = lax.broadcasted_iota(jnp.int32, (nblk, rows), 1) * D_CMP
        overlap_t = ((nn < jj + L_SLC) & (nn + L_CMP > jj)).astype(BF16)
        imp_h = _dot(overlap_t, e) * rcp
        imp = imp_h[:, 0:tq]
        for h in range(1, hpg):
            imp = imp + imp_h[:, h * tq:(h + 1) * tq]

        jrow = lax.broadcasted_iota(jnp.int32, (nblk, tq), 0)
        cur = jnp.right_shift(q0 + lax.broadcasted_iota(jnp.int32, (1, tq), 1), L_SLC.bit_length() - 1)
        valid = jrow <= cur
        forced = (jrow == 0) | (jrow == cur) | (jrow == cur - 1)
        val = jnp.where(valid & jnp.logical_not(forced), imp, NEG_INF)
        jrow_f = jrow.astype(F32)
        for _ in range(N_SEL - N_FORCED):
            mx = jnp.max(val, axis=0, keepdims=True)
            first = jnp.min(jnp.where(val == mx, jrow_f, float(MAX_SLC_BLOCKS)), axis=0, keepdims=True)
            val = jnp.where(jrow_f == first, PICKED, val)
        sel = jnp.where(valid & (forced | (val == PICKED)), 1.0, 0.0)
        if nblk < MAX_SLC_BLOCKS:
            sel = jnp.concatenate([sel, jnp.zeros((MAX_SLC_BLOCKS - nblk, tq), F32)], axis=0)
        gt = gate_ref[0, 0].T
        oc_ref[0] = _head_stack(ot, gt, 0, tq).T
        sel_t = sel.T
        negm_ref[0, 0] = jnp.where(sel_t > 0.5, 0.0, NEG_INF).astype(BF16)
        for r in range(tq // SUPER_KEYS):
            used_ref[r] = jnp.max(sel_t[r * SUPER_KEYS:(r + 1) * SUPER_KEYS], axis=0, keepdims=True)

    row_tile = min(m, CMP_ROW_TILE)
    n_variants = m // row_tile
    if n_variants == 1:
        attend(m)
    else:
        needed = pl.program_id(2) * (tq // D_CMP) + (tq // D_CMP - 1)
        variant = jnp.minimum(jnp.right_shift(needed - 1, row_tile.bit_length() - 1), n_variants - 1)
        for v in range(n_variants):
            pl.when(variant == v)(functools.partial(attend, (v + 1) * row_tile))


def _nsa_cmp(q, kca, vct, gates, tq=512):
    b, nh, s, _ = q.shape
    g, hpg, dh = N_KV_GROUPS, HEADS_PER_GROUP, HEAD_DIM
    m = kca.shape[1]
    assert s % tq == 0 and tq % SUPER_KEYS == 0 and tq & (tq - 1) == 0
    nqt = s // tq
    sub = tq // SUPER_KEYS
    return pl.pallas_call(
        functools.partial(_nsa_cmp_kernel, tq=tq, m=m),
        grid=(b, g, s // tq),
        in_specs=[
            pl.BlockSpec((1, hpg, tq, LANES), lambda i, j, k: (i, j, k, 0)),
            pl.BlockSpec((1, m, LANES), lambda i, j, k: (i * N_KV_GROUPS + j, 0, 0)),
            pl.BlockSpec((1, VT_ROWS, m), lambda i, j, k: (i * N_KV_GROUPS + j, 0, 0)),
            pl.BlockSpec((1, 1, tq, LANES), lambda i, j, k: (i, j, k, 0)),
        ],
        out_specs=[
            pl.BlockSpec((1, tq, hpg * dh), lambda i, j, k: (i, k, j)),
            pl.BlockSpec((1, 1, tq, MAX_SLC_BLOCKS), lambda i, j, k: (i, j, k, 0)),
            pl.BlockSpec((sub, 1, MAX_SLC_BLOCKS), lambda i, j, k: ((i * N_KV_GROUPS + j) * nqt + k, 0, 0)),
        ],
        out_shape=[
            jax.ShapeDtypeStruct((b, s, nh * dh), F32),
            jax.ShapeDtypeStruct((b, g, s, MAX_SLC_BLOCKS), BF16),
            jax.ShapeDtypeStruct((b * g * nqt * sub, 1, MAX_SLC_BLOCKS), F32),
        ],
        compiler_params=_params("arbitrary", "arbitrary", "arbitrary"),
        name="nsa_cmp",
    )(q, kca, vct, gates)


def _super_lists_kernel(used_ref, lst_ref, cnt_ref, *, tb):
    ut = used_ref[...].T.astype(BF16)
    a_i = lax.broadcasted_iota(jnp.int32, (MAX_SLC_BLOCKS, MAX_SLC_BLOCKS), 0)
    b_i = lax.broadcasted_iota(jnp.int32, (MAX_SLC_BLOCKS, MAX_SLC_BLOCKS), 1)
    pair_t = (a_i == jnp.right_shift(b_i, SUPER_SHIFT)).astype(BF16)
    sbf = jnp.where(_dot(pair_t, ut) > 0.5, 1.0, 0.0).astype(BF16)
    incl = _dot((b_i <= a_i).astype(BF16), sbf)[0:MAX_SUPER]
    cnt_ref[...] = jnp.broadcast_to(incl[MAX_SUPER - 1:MAX_SUPER, :], (8, tb)).astype(jnp.int32)
    for i in range(MAX_SUPER):
        lst_ref[i:i + 1, :] = jnp.sum(jnp.where(incl <= float(i), 1.0, 0.0), axis=0, keepdims=True).astype(jnp.int32)


def _super_lists(used):
    t = used.shape[0]
    tb = min(t, LANES)
    assert t % tb == 0
    return pl.pallas_call(
        functools.partial(_super_lists_kernel, tb=tb),
        grid=(t // tb,),
        in_specs=[pl.BlockSpec((tb, MAX_SLC_BLOCKS), lambda i: (i, 0))],
        out_specs=[
            pl.BlockSpec((MAX_SUPER, tb), lambda i: (0, i)),
            pl.BlockSpec((8, tb), lambda i: (0, i)),
        ],
        out_shape=[
            jax.ShapeDtypeStruct((MAX_SUPER, t), jnp.int32),
            jax.ShapeDtypeStruct((8, t), jnp.int32),
        ],
        compiler_params=_params("arbitrary"),
        name="super_lists",
    )(used)


def _nsa_slc_kernel(lst_ref, cnt_ref, q_ref, kas_ref, vts_ref, kaw_ref, vtw_ref, negm_ref, gate_ref, oc_ref, o_ref,
                    m_ref, acc_ref, *, tq, sub):
    hpg = HEADS_PER_GROUP
    ncol = hpg * tq
    n_win = WINDOW // SUPER_KEYS + 1
    pad_k = (lax.broadcasted_iota(jnp.int32, (SUPER_KEYS, 2 * LANES), 1) == PAD_LANE).astype(BF16)
    pad_w = (lax.broadcasted_iota(jnp.int32, (SUPER_KEYS, LANES), 1) == PAD_LANE).astype(BF16)
    n_tiles = pl.num_programs(0) * pl.num_programs(1) * pl.num_programs(2) * sub
    tile0 = ((pl.program_id(0) * pl.num_programs(1) + pl.program_id(1)) * pl.num_programs(2) + pl.program_id(2)) * sub
    row = lax.broadcasted_iota(jnp.int32, (SUPER_KEYS, 1), 0)

    class Tile:
        def __init__(self, s, slots):
            self.s = s
            self.slots = slots
            self.rows = slice(s * tq, (s + 1) * tq)
            self.qt = pl.program_id(2) * sub + s
            self.q = q_ref[0, :, self.rows, :].reshape(ncol, LANES)
            negm = negm_ref[0, 0, self.rows, :]
            self.qa = jnp.concatenate([self.q, jnp.concatenate([negm] * hpg, axis=0)], axis=1)
            self.tcol = self.qt * tq + jnp.bitwise_and(lax.broadcasted_iota(jnp.int32, (1, ncol), 1), tq - 1)
            self.tile = tile0 + s
            self.n_other = cnt_ref[self.tile] - 1
            self.n_early = jnp.right_shift(jnp.maximum(self.n_other - slots, 0) + CHUNK_SUPERS - 1,
                                           CHUNK_SUPERS.bit_length() - 1)
            m_ref[s] = jnp.full((1, ncol), NEG_INF, F32)
            acc_ref[s] = jnp.zeros((VT_ROWS, ncol), F32)

        def super_block(self, slot, ok=None):
            sb = lst_ref[slot * n_tiles + self.tile]
            if ok is not None:
                sb = jnp.where(ok, sb, 0)
            start = pl.multiple_of(sb * SUPER_KEYS, SUPER_KEYS)
            return kas_ref[0, 0, pl.ds(start, SUPER_KEYS), :], vts_ref[0, 0, :, pl.ds(start, SUPER_KEYS)]

        def update(self, st, vt):
            m_old = m_ref[self.s]
            m_new = jnp.maximum(m_old, jnp.max(st, axis=0, keepdims=True))
            p = jnp.exp2(st - m_new)
            acc_ref[self.s] = jnp.exp2(m_old - m_new) * acc_ref[self.s] + _dot(vt, p.astype(BF16))
            m_ref[self.s] = m_new

        def early_chunks(self):
            def body(c, carry):
                parts = [self.super_block(c * CHUNK_SUPERS + j) for j in range(CHUNK_SUPERS)]
                kt = jnp.concatenate([k for k, _ in parts], axis=0)
                self.update(_dot_nt(kt, self.qa), jnp.concatenate([v for _, v in parts], axis=1))
                return carry

            lax.fori_loop(0, self.n_early, body, 0)

        def final_scores(self):
            dstart = pl.multiple_of(self.qt * SUPER_KEYS, SUPER_KEYS)
            sd = _dot_nt(kas_ref[0, 0, pl.ds(dstart, SUPER_KEYS), :], self.qa)
            sd = jnp.where(self.qt * SUPER_KEYS + row <= self.tcol, sd, NEG_INF)
            ks, vs = [], [vts_ref[0, 0, :, pl.ds(dstart, SUPER_KEYS)]]
            for j in range(self.slots):
                slot = self.n_early * CHUNK_SUPERS + j
                ok = slot < self.n_other
                k, v = self.super_block(jnp.minimum(slot, MAX_SUPER - 1), ok)
                ks.append(_select_rows(ok, k, pad_k))
                vs.append(v)
            self.st = jnp.concatenate([sd, _dot_nt(jnp.concatenate(ks, axis=0), self.qa)], axis=0)
            self.vt = jnp.concatenate(vs, axis=1)
            win_scores, win_v = [], []
            for r in range(n_win):
                sb = self.qt - (n_win - 1) + r
                start = pl.multiple_of(jnp.maximum(sb, 0) * SUPER_KEYS, SUPER_KEYS)
                kw = kaw_ref[0, 0, pl.ds(start, SUPER_KEYS), :]
                if r < n_win - 1:
                    kw = _select_rows(sb >= 0, kw, pad_w)
                sw = _dot_nt(kw, self.q)
                if r == n_win - 1:
                    sw = jnp.where(sb * SUPER_KEYS + row <= self.tcol, sw, NEG_INF)
                elif r == 0:
                    sw = jnp.where(sb * SUPER_KEYS + WINDOW + row > self.tcol, sw, NEG_INF)
                win_scores.append(sw)
                win_v.append(vtw_ref[0, 0, :, pl.ds(start, SUPER_KEYS)])
            self.stw = jnp.concatenate(win_scores, axis=0)
            self.vw = jnp.concatenate(win_v, axis=1)

        def finish(self):
            self.update(self.st, self.vt)
            acc = acc_ref[self.s]
            o_slc_t = acc[0:HEAD_DIM] * (1.0 / acc[HEAD_DIM:HEAD_DIM + 1])
            e = jnp.exp2(self.stw - jnp.max(self.stw, axis=0, keepdims=True))
            acc_w = _dot(self.vw, e.astype(BF16))
            o_win_t = acc_w[0:HEAD_DIM] * (1.0 / acc_w[HEAD_DIM:HEAD_DIM + 1])
            gt = gate_ref[0, 0, self.rows, :].T
            comb = _head_stack(o_slc_t, gt, 1, tq) + _head_stack(o_win_t, gt, 2, tq)
            o_ref[0, self.rows, :] = (oc_ref[0, self.rows, :] + comb.T).astype(BF16)

    def run(slots):
        tiles = [Tile(s, slots) for s in range(sub)]
        for t in tiles:
            t.early_chunks()
        for t in tiles:
            t.final_scores()
        for t in tiles:
            t.finish()

    longest = cnt_ref[tile0] - 1
    for s in range(1, sub):
        longest = jnp.maximum(longest, cnt_ref[tile0 + s] - 1)
    pl.when(longest <= SHORT_SLOTS)(functools.partial(run, SHORT_SLOTS))
    pl.when(longest > SHORT_SLOTS)(functools.partial(run, FINAL_SLOTS))


def _nsa_slc(lists, counts, q, kas, vts, kaw, vtw, negm, gates, oc, tq=128, sub=4):
    b, nh, s, _ = q.shape
    g, hpg, dh = N_KV_GROUPS, HEADS_PER_GROUP, HEAD_DIM
    ts = tq * sub
    assert tq == SUPER_KEYS and WINDOW % tq == 0 and s >= WINDOW + tq and s % ts == 0
    grid_spec = pltpu.PrefetchScalarGridSpec(
        num_scalar_prefetch=2,
        grid=(b, g, s // ts),
        in_specs=[
            pl.BlockSpec((1, hpg, ts, LANES), lambda i, j, k, *_: (i, j, k, 0)),
            pl.BlockSpec((1, 1, s, 2 * LANES), lambda i, j, k, *_: (i, j, 0, 0)),
            pl.BlockSpec((1, 1, VT_ROWS, s), lambda i, j, k, *_: (i, j, 0, 0)),
            pl.BlockSpec((1, 1, s, LANES), lambda i, j, k, *_: (i, j, 0, 0)),
            pl.BlockSpec((1, 1, VT_ROWS, s), lambda i, j, k, *_: (i, j, 0, 0)),
            pl.BlockSpec((1, 1, ts, MAX_SLC_BLOCKS), lambda i, j, k, *_: (i, j, k, 0)),
            pl.BlockSpec((1, 1, ts, LANES), lambda i, j, k, *_: (i, j, k, 0)),
            pl.BlockSpec((1, ts, hpg * dh), lambda i, j, k, *_: (i, k, j)),
        ],
        out_specs=pl.BlockSpec((1, ts, hpg * dh), lambda i, j, k, *_: (i, k, j)),
        scratch_shapes=[
            pltpu.VMEM((sub, 1, hpg * tq), F32),
            pltpu.VMEM((sub, VT_ROWS, hpg * tq), F32),
        ],
    )
    return pl.pallas_call(
        functools.partial(_nsa_slc_kernel, tq=tq, sub=sub),
        grid_spec=grid_spec,
        out_shape=jax.ShapeDtypeStruct((b, s, nh * dh), BF16),
        compiler_params=_params("arbitrary", "arbitrary", "arbitrary"),
        name="nsa_slc",
    )(lists, counts, q, kas, vts, kaw, vtw, negm, gates, oc)


def _split_bf16(x, n):
    pieces = []
    for _ in range(n):
        p = x.astype(BF16).astype(F32)
        pieces.append(p)
        x = x - p
    return pieces


def _q_constants():
    slopes = jnp.exp2(-8.0 * jnp.arange(1, N_HEADS + 1, dtype=F32) / N_HEADS) * LOG2E
    pieces = jnp.stack(_split_bf16(slopes, N_SPLIT), axis=1)
    qc = jnp.zeros((N_HEADS, LANES), F32)
    qc = qc.at[:, POS_LANE:POS_LANE + N_SPLIT].set(pieces * L_SLC)
    qc = qc.at[:, POS_LANE + N_SPLIT:POS_LANE + 2 * N_SPLIT].set(pieces)
    return qc.at[:, PAD_LANE].set(NEG_INF)


def _qg_weight(w_qg):
    d = w_qg.shape[0]
    nq = N_HEADS * HEAD_DIM
    wg = jnp.pad(w_qg[:, nq:].reshape(d, N_KV_GROUPS, 3 * HEADS_PER_GROUP),
                 ((0, 0), (0, 0), (0, LANES - 3 * HEADS_PER_GROUP)))
    return jnp.concatenate([w_qg[:, :nq], wg.reshape(d, N_KV_GROUPS * LANES)], axis=1).astype(BF16)


def kernel(x, a_norm, a_w_in, a_conv, a_w_out, kv_norm, w_kv, cmp_pe_k, cmp_w1_k, cmp_w2_k, cmp_pe_v, cmp_w1_v,
           cmp_w2_v, b_norm, b_w_qg, b_w_o, f_norm, f_w_gu, f_w_down, final_norm):
    b, s, d = x.shape
    nq = N_HEADS * HEAD_DIM
    h = _mixer_a(x, a_norm[0], a_w_in[0].astype(BF16), a_conv[0], a_w_out[0].astype(BF16))
    h = _ffn(h.reshape(b * s, d), f_norm[0], f_w_gu[0].astype(BF16), f_w_down[0].astype(BF16), final_norm)
    h = h.reshape(b, s, d)

    kvc, kas, vts, kaw, vtw, q, gates = _proj(h, kv_norm, b_norm[0], w_kv.astype(BF16), _qg_weight(b_w_qg[0]),
                                              _q_constants())
    pe2 = jnp.stack([cmp_pe_k, cmp_pe_v]).reshape(2, 1, L_CMP * HEAD_DIM)
    pe2 = jnp.broadcast_to(pe2, (2, 8, L_CMP * HEAD_DIM)).astype(BF16)
    w2 = jnp.pad(jnp.stack([cmp_w2_k, cmp_w2_v]), ((0, 0), (0, 0), (0, LANES - HEAD_DIM))).astype(BF16)
    kca, vct = _compress(kvc, pe2, jnp.stack([cmp_w1_k, cmp_w1_v]).astype(BF16), w2)

    oc, negm, used = _nsa_cmp(q, kca, vct, gates)
    lists, counts = _super_lists(used.reshape(used.shape[0], MAX_SLC_BLOCKS))
    o = _nsa_slc(lists.reshape(-1), counts[0], q, kas, vts, kaw, vtw, negm, gates, oc)

    out = _ffn(h.reshape(b * s, d), f_norm[1], f_w_gu[1].astype(BF16), f_w_down[1].astype(BF16), final_norm,
               o=o.reshape(b * s, nq), w_o=b_w_o[0].astype(BF16), with_final=True)
    return out.reshape(b, s, d)
```

```python
import functools
import math

import jax
import jax.numpy as jnp
from jax import lax
from jax.experimental import pallas as pl
from jax.experimental.pallas import tpu as pltpu

F32 = jnp.float32
BF16 = jnp.bfloat16

N_HEADS = 16
N_KV_GROUPS = 4
HEADS_PER_GROUP = N_HEADS // N_KV_GROUPS
HEAD_DIM = 64
L_CMP = 32
D_CMP = 16
L_SLC = 64
N_SEL = 16
N_FORCED = 3
WINDOW = 512
CONV_WIDTH = 3
RMS_EPS = 1e-5
NEG_INF = -1e30
PICKED = -3.0e38
LOG2E = math.log2(math.e)
LANES = 128
MAX_SLC_BLOCKS = LANES
SUPER_KEYS = LANES
SUPER_SHIFT = (SUPER_KEYS // L_SLC).bit_length() - 1
MAX_SUPER = MAX_SLC_BLOCKS * L_SLC // SUPER_KEYS
CHUNK_SUPERS = 4
FINAL_SLOTS = 10
SHORT_SLOTS = 8
CMP_ROW_TILE = 128
VT_ROWS = HEAD_DIM + 16
POS_LANE = HEAD_DIM
N_SPLIT = 3
PAD_LANE = POS_LANE + 2 * N_SPLIT
VMEM_LIMIT_BYTES = 56 * 1024 * 1024


def _rms(x, g):
    return x * lax.rsqrt(jnp.mean(x * x, axis=-1, keepdims=True) + RMS_EPS) * g


def _dot(a, b):
    return jnp.dot(a, b, preferred_element_type=F32)


def _dot_nt(a, b):
    return lax.dot_general(a, b, (((1,), (1,)), ((), ())), preferred_element_type=F32)


def _params(*sem):
    return pltpu.CompilerParams(dimension_semantics=sem, vmem_limit_bytes=VMEM_LIMIT_BYTES)


def _pos_columns(coarse, fine, lane):
    is_c = (lane >= POS_LANE) & (lane < POS_LANE + N_SPLIT)
    is_f = (lane >= POS_LANE + N_SPLIT) & (lane < POS_LANE + 2 * N_SPLIT)
    return jnp.where(is_c, coarse, jnp.where(is_f, fine, 0.0))


def _mixer_a_kernel(x_ref, g_ref, win_ref, conv_ref, wout_ref, o_ref, ext_ref, *, tm, d):
    @pl.when(pl.program_id(1) == 0)
    def _():
        ext_ref[0:8, :] = jnp.zeros((8, d), F32)

    x = x_ref[0]
    xb = _rms(x, g_ref[...]).astype(BF16)
    b_gate = _dot(xb, win_ref[:, 0:d])
    cv = _dot(xb, win_ref[:, d:2 * d]) * _dot(xb, win_ref[:, 2 * d:3 * d])
    ext_ref[8:8 + tm, :] = cv
    cv1 = ext_ref[7:7 + tm, :]
    cv2 = ext_ref[6:6 + tm, :]
    u = conv_ref[0:1, :] * cv2 + conv_ref[1:2, :] * cv1 + conv_ref[2:3, :] * cv
    ext_ref[0:8, :] = cv[tm - 8:tm, :]
    o_ref[0] = x + _dot((b_gate * u).astype(BF16), wout_ref[...])


def _mixer_a(x, g, w_in, conv_w, w_out, tm=1024):
    b, s, d = x.shape
    assert s % tm == 0
    return pl.pallas_call(
        functools.partial(_mixer_a_kernel, tm=tm, d=d),
        grid=(b, s // tm),
        in_specs=[
            pl.BlockSpec((1, tm, d), lambda i, j: (i, j, 0)),
            _resident((1, d)), _resident((d, 3 * d)), _resident((CONV_WIDTH, d)), _resident((d, d)),
        ],
        out_specs=pl.BlockSpec((1, tm, d), lambda i, j: (i, j, 0)),
        out_shape=jax.ShapeDtypeStruct((b, s, d), F32),
        scratch_shapes=[pltpu.VMEM((tm + 8, d), F32)],
        compiler_params=_params("arbitrary", "arbitrary"),
        name="mixer_a",
    )(x, g.reshape(1, d), w_in, conv_w, w_out)


def _ffn_kernel(*refs, with_proj, with_final, f, tf):
    if with_proj:
        h_ref, o_ref_in, wo_ref, g_ref, wgu_ref, wd_ref, fg_ref, out_ref = refs
    else:
        h_ref, g_ref, wgu_ref, wd_ref, fg_ref, out_ref = refs
    h = h_ref[...]
    if with_proj:
        h = h + _dot(o_ref_in[...], wo_ref[...])
    xb = _rms(h, g_ref[...]).astype(BF16)
    y = h
    for c in range(f // tf):
        gate = _dot(xb, wgu_ref[:, c * tf:(c + 1) * tf])
        up = _dot(xb, wgu_ref[:, f + c * tf:f + (c + 1) * tf])
        act = (gate * jax.nn.sigmoid(gate)) * up
        y = y + _dot(act.astype(BF16), wd_ref[c * tf:(c + 1) * tf, :])
    if with_final:
        y = _rms(y, fg_ref[...])
    out_ref[...] = y


def _resident(shape):
    return pl.BlockSpec(shape, lambda *_: (0,) * len(shape), pipeline_mode=pl.Buffered(1))


def _ffn(h, g, w_gu, w_down, final_g, o=None, w_o=None, with_final=False, tm=1024, tf=256):
    t, d = h.shape
    f = w_down.shape[0]
    assert t % tm == 0 and f % tf == 0
    with_proj = o is not None
    in_specs = [pl.BlockSpec((tm, d), lambda i: (i, 0))]
    args = [h]
    if with_proj:
        in_specs += [pl.BlockSpec((tm, o.shape[1]), lambda i: (i, 0)), _resident(w_o.shape)]
        args += [o, w_o]
    in_specs += [_resident((1, d)), _resident(w_gu.shape), _resident(w_down.shape), _resident((1, d))]
    args += [g.reshape(1, d), w_gu, w_down, final_g.reshape(1, d)]
    return pl.pallas_call(
        functools.partial(_ffn_kernel, with_proj=with_proj, with_final=with_final, f=f, tf=tf),
        grid=(t // tm,),
        in_specs=in_specs,
        out_specs=pl.BlockSpec((tm, d), lambda i: (i, 0)),
        out_shape=jax.ShapeDtypeStruct((t, d), F32),
        compiler_params=_params("arbitrary"),
        name="ffn_proj" if with_proj else "ffn",
    )(*args)


def _halves(x):
    return x, pltpu.roll(x, HEAD_DIM, axis=1)


def _proj_kernel(h_ref, gkv_ref, gq_ref, wkv_ref, wqg_ref, qc_ref,
                 kvc_ref, kas_ref, vts_ref, kaw_ref, vtw_ref, q_ref, gate_ref, rel_ref, *, tm):
    g, dh = N_KV_GROUPS, HEAD_DIM
    gw = g * dh
    s0 = pl.program_id(1) * tm
    x = h_ref[0]
    y = x * lax.rsqrt(jnp.mean(x * x, axis=-1, keepdims=True) + RMS_EPS)
    kv = _dot((y * gkv_ref[...]).astype(BF16), wkv_ref[...])
    low_m = lax.broadcasted_iota(jnp.int32, (tm // D_CMP, LANES), 1) < dh
    for st in range(2):
        for pair in range(g // 2):
            slab = rel_ref.at[st * (g // 2) + pair]
            slab[...] = kv[:, st * gw + pair * LANES:st * gw + (pair + 1) * LANES]
            for j in range(D_CMP // 2):
                a = slab[pl.ds(2 * j, tm // D_CMP, stride=D_CMP), :]
                b = slab[pl.ds(2 * j + 1, tm // D_CMP, stride=D_CMP), :]
                cols = slice(j * LANES, (j + 1) * LANES)
                kvc_ref[st, 0, 2 * pair, :, cols] = jnp.where(low_m, a, pltpu.roll(b, dh, axis=1)).astype(BF16)
                kvc_ref[st, 0, 2 * pair + 1, :, cols] = jnp.where(low_m, pltpu.roll(a, dh, axis=1), b).astype(BF16)

    kpos = s0 + lax.broadcasted_iota(jnp.int32, (tm, LANES), 0)
    lane = lax.broadcasted_iota(jnp.int32, (tm, LANES), 1)
    low = lane < dh
    kblk = jnp.right_shift(kpos, L_SLC.bit_length() - 1)
    posc = _pos_columns(kblk.astype(F32), jnp.bitwise_and(kpos, L_SLC - 1).astype(F32), lane)
    onehot = (lane == kblk).astype(BF16)
    ones_rows = (lax.broadcasted_iota(jnp.int32, (VT_ROWS - dh, tm), 0) == 0).astype(BF16)
    for pair in range(g // 2):
        for st, k_ref, vt_ref in ((2, kas_ref, vts_ref), (4, kaw_ref, vtw_ref)):
            k_pair = kv[:, st * gw + pair * LANES:st * gw + (pair + 1) * LANES]
            vt = kv[:, (st + 1) * gw + pair * LANES:(st + 1) * gw + (pair + 1) * LANES].T
            for half, kh in enumerate(_halves(k_pair)):
                grp = 2 * pair + half
                k_ref[0, grp, :, 0:LANES] = jnp.where(low, kh, posc).astype(BF16)
                if st == 2:
                    k_ref[0, grp, :, LANES:2 * LANES] = onehot
                vt_ref[0, grp, 0:dh] = vt[half * dh:(half + 1) * dh].astype(BF16)
                vt_ref[0, grp, dh:VT_ROWS] = ones_rows

    qg = _dot((y * gq_ref[...]).astype(BF16), wqg_ref[...])
    for pair in range(N_HEADS // 2):
        q_pair = qg[:, pair * LANES:(pair + 1) * LANES] * (LOG2E * dh ** -0.5)
        for half, qh in enumerate(_halves(q_pair)):
            hd = 2 * pair + half
            q_ref[0, hd] = jnp.where(low, qh, qc_ref[hd:hd + 1, :]).astype(BF16)
    nq = N_HEADS * dh
    for grp in range(g):
        gate_ref[0, grp] = jax.nn.sigmoid(qg[:, nq + grp * LANES: nq + (grp + 1) * LANES])


def _proj(h, g_kv, g_q, w_kv, w_qg, qconst, tm=1024):
    b, s, d = h.shape
    assert s % tm == 0 and s // L_SLC <= MAX_SLC_BLOCKS
    g, dh = N_KV_GROUPS, HEAD_DIM
    return pl.pallas_call(
        functools.partial(_proj_kernel, tm=tm),
        grid=(b, s // tm),
        in_specs=[
            pl.BlockSpec((1, tm, d), lambda i, j: (i, j, 0)),
            _resident((1, d)), _resident((1, d)), _resident(w_kv.shape), _resident(w_qg.shape),
            _resident(qconst.shape),
        ],
        out_specs=[
            pl.BlockSpec((2, 1, g, tm // D_CMP, D_CMP * dh), lambda i, j: (0, i, 0, j, 0)),
            pl.BlockSpec((1, g, tm, 2 * LANES), lambda i, j: (i, 0, j, 0)),
            pl.BlockSpec((1, g, VT_ROWS, tm), lambda i, j: (i, 0, 0, j)),
            pl.BlockSpec((1, g, tm, LANES), lambda i, j: (i, 0, j, 0)),
            pl.BlockSpec((1, g, VT_ROWS, tm), lambda i, j: (i, 0, 0, j)),
            pl.BlockSpec((1, N_HEADS, tm, LANES), lambda i, j: (i, 0, j, 0)),
            pl.BlockSpec((1, g, tm, LANES), lambda i, j: (i, 0, j, 0)),
        ],
        out_shape=[
            jax.ShapeDtypeStruct((2, b, g, s // D_CMP, D_CMP * dh), BF16),
            jax.ShapeDtypeStruct((b, g, s, 2 * LANES), BF16),
            jax.ShapeDtypeStruct((b, g, VT_ROWS, s), BF16),
            jax.ShapeDtypeStruct((b, g, s, LANES), BF16),
            jax.ShapeDtypeStruct((b, g, VT_ROWS, s), BF16),
            jax.ShapeDtypeStruct((b, N_HEADS, s, LANES), BF16),
            jax.ShapeDtypeStruct((b, g, s, LANES), F32),
        ],
        scratch_shapes=[pltpu.VMEM((2 * (g // 2), tm, LANES), F32)],
        compiler_params=_params("arbitrary", "arbitrary"),
        name="proj",
    )(h, g_kv.reshape(1, d), g_q.reshape(1, d), w_kv, w_qg, qconst)


def _compress_kernel(r_ref, pe_ref, w1_ref, w2_ref, kca_ref, vct_ref, *, m):
    half = D_CMP * HEAD_DIM
    row = lax.broadcasted_iota(jnp.int32, (m, LANES), 0)
    lane = lax.broadcasted_iota(jnp.int32, (m, LANES), 1)

    def mlp(i):
        r = r_ref[i, 0]
        top = _dot(r, w1_ref[i, 0:half, :])
        bot = _dot(r, w1_ref[i, half:2 * half, :])
        pe_term = _dot(pe_ref[i], w1_ref[i])[0:1, :]
        hid = top + pltpu.roll(bot, m - 1, axis=0) + pe_term
        hid = hid * jax.nn.sigmoid(hid)
        out = _dot(hid.astype(BF16), w2_ref[i])
        return jnp.where(row < m - 1, out, 0.0)

    coarse = jnp.right_shift(row, 2).astype(F32)
    fine = jnp.bitwise_and(row, 3).astype(F32) * D_CMP + (L_CMP - 1) / 2.0
    kca_ref[0] = (mlp(0) + _pos_columns(coarse, fine, lane)).astype(BF16)
    vct_ref[0] = jnp.where(lane == HEAD_DIM, 1.0, mlp(1)).T[0:VT_ROWS].astype(BF16)


def _compress(kvc, pe2, w1_2, w2_2):
    _, b, g, m, _ = kvc.shape
    dh = HEAD_DIM
    r = kvc.reshape(2, b * g, m, D_CMP * dh)
    hid = w1_2.shape[-1]
    return pl.pallas_call(
        functools.partial(_compress_kernel, m=m),
        grid=(b * g,),
        in_specs=[
            pl.BlockSpec((2, 1, m, D_CMP * dh), lambda j: (0, j, 0, 0)),
            pl.BlockSpec((2, 8, L_CMP * dh), lambda j: (0, 0, 0)),
            pl.BlockSpec((2, L_CMP * dh, hid), lambda j: (0, 0, 0)),
            pl.BlockSpec((2, hid, LANES), lambda j: (0, 0, 0)),
        ],
        out_specs=[
            pl.BlockSpec((1, m, LANES), lambda j: (j, 0, 0)),
            pl.BlockSpec((1, VT_ROWS, m), lambda j: (j, 0, 0)),
        ],
        out_shape=[
            jax.ShapeDtypeStruct((b * g, m, LANES), BF16),
            jax.ShapeDtypeStruct((b * g, VT_ROWS, m), BF16),
        ],
        compiler_params=_params("arbitrary"),
        name="compress",
    )(r, pe2, w1_2, w2_2)


def _select_rows(pred, a, b):
    a32, b32 = pltpu.bitcast(a, jnp.int32), pltpu.bitcast(b, jnp.int32)
    return pltpu.bitcast(jnp.where(pred, a32, b32), BF16)


def _head_stack(ot, gt, branch, tq):
    return jnp.concatenate(
        [ot[:, h * tq:(h + 1) * tq] * gt[3 * h + branch:3 * h + branch + 1, :] for h in range(HEADS_PER_GROUP)],
        axis=0)


def _nsa_cmp_kernel(q_ref, kca_ref, vct_ref, gate_ref, oc_ref, negm_ref, used_ref, *, tq, m):
    q0 = pl.program_id(2) * tq
    hpg = HEADS_PER_GROUP
    ncol = hpg * tq
    q = q_ref[0].reshape(ncol, LANES)
    tcol = q0 + jnp.bitwise_and(lax.broadcasted_iota(jnp.int32, (1, ncol), 1), tq - 1)

    def attend(rows):
        st = _dot_nt(kca_ref[0, 0:rows, :], q)
        band = max(rows - row_tile - 8, 0)
        nrow = band + lax.broadcasted_iota(jnp.int32, (rows - band, 1), 0)
        low = jnp.where((nrow * D_CMP + (L_CMP - 1)) <= tcol, st[band:], NEG_INF)
        st = low if band == 0 else jnp.concatenate([st[0:band], low], axis=0)
        e = jnp.exp2(st - jnp.max(st, axis=0, keepdims=True)).astype(BF16)
        acc = _dot(vct_ref[0, :, 0:rows], e)
        rcp = jnp.where(tcol >= L_CMP - 1, 1.0 / acc[HEAD_DIM:HEAD_DIM + 1], 0.0)
        ot = acc[0:HEAD_DIM] * rcp
        nblk = min(MAX_SLC_BLOCKS, rows * D_CMP // L_SLC)
        jj = lax.broadcasted_iota(jnp.int32, (nblk, rows), 0) * L_SLC
        nn = lax.broadcasted_iota(jnp.int32, (nblk, rows), 1) * D_CMP
        overlap_t = ((nn < jj + L_SLC) & (nn + L_CMP > jj)).astype(BF16)
        imp_h = _dot(overlap_t, e) * rcp
        imp = imp_h[:, 0:tq]
        for h in range(1, hpg):
            imp = imp + imp_h[:, h * tq:(h + 1) * tq]

        jrow = lax.broadcasted_iota(jnp.int32, (nblk, tq), 0)
        cur = jnp.right_shift(q0 + lax.broadcasted_iota(jnp.int32, (1, tq), 1), L_SLC.bit_length() - 1)
        valid = jrow <= cur
        forced = (jrow == 0) | (jrow == cur) | (jrow == cur - 1)
        val = jnp.where(valid & jnp.logical_not(forced), imp, NEG_INF)
        jrow_f = jrow.astype(F32)
        for _ in range(N_SEL - N_FORCED):
            mx = jnp.max(val, axis=0, keepdims=True)
            first = jnp.min(jnp.where(val == mx, jrow_f, float(MAX_SLC_BLOCKS)), axis=0, keepdims=True)
            val = jnp.where(jrow_f == first, PICKED, val)
        sel = jnp.where(valid & (forced | (val == PICKED)), 1.0, 0.0)
        if nblk < MAX_SLC_BLOCKS:
            sel = jnp.concatenate([sel, jnp.zeros((MAX_SLC_BLOCKS - nblk, tq), F32)], axis=0)
        gt = gate_ref[0, 0].T
        oc_ref[0] = _head_stack(ot, gt, 0, tq).T
        sel_t = sel.T
        negm_ref[0, 0] = jnp.where(sel_t > 0.5, 0.0, NEG_INF).astype(BF16)
        for r in range(tq // SUPER_KEYS):
            used_ref[r] = jnp.max(sel_t[r * SUPER_KEYS:(r + 1) * SUPER_KEYS], axis=0, keepdims=True)

    row_tile = min(m, CMP_ROW_TILE)
    n_variants = m // row_tile
    if n_variants == 1:
        attend(m)
    else:
        needed = pl.program_id(2) * (tq // D_CMP) + (tq // D_CMP - 1)
        variant = jnp.minimum(jnp.right_shift(needed - 1, row_tile.bit_length() - 1), n_variants - 1)
        for v in range(n_variants):
            pl.when(variant == v)(functools.partial(attend, (v + 1) * row_tile))


def _nsa_cmp(q, kca, vct, gates, tq=1024):
    b, nh, s, _ = q.shape
    g, hpg, dh = N_KV_GROUPS, HEADS_PER_GROUP, HEAD_DIM
    m = kca.shape[1]
    assert s % tq == 0 and tq % SUPER_KEYS == 0 and tq & (tq - 1) == 0
    nqt = s // tq
    sub = tq // SUPER_KEYS
    return pl.pallas_call(
        functools.partial(_nsa_cmp_kernel, tq=tq, m=m),
        grid=(b, g, s // tq),
        in_specs=[
            pl.BlockSpec((1, hpg, tq, LANES), lambda i, j, k: (i, j, k, 0)),
            pl.BlockSpec((1, m, LANES), lambda i, j, k: (i * N_KV_GROUPS + j, 0, 0)),
            pl.BlockSpec((1, VT_ROWS, m), lambda i, j, k: (i * N_KV_GROUPS + j, 0, 0)),
            pl.BlockSpec((1, 1, tq, LANES), lambda i, j, k: (i, j, k, 0)),
        ],
        out_specs=[
            pl.BlockSpec((1, tq, hpg * dh), lambda i, j, k: (i, k, j)),
            pl.BlockSpec((1, 1, tq, MAX_SLC_BLOCKS), lambda i, j, k: (i, j, k, 0)),
            pl.BlockSpec((sub, 1, MAX_SLC_BLOCKS), lambda i, j, k: ((i * N_KV_GROUPS + j) * nqt + k, 0, 0)),
        ],
        out_shape=[
            jax.ShapeDtypeStruct((b, s, nh * dh), F32),
            jax.ShapeDtypeStruct((b, g, s, MAX_SLC_BLOCKS), BF16),
            jax.ShapeDtypeStruct((b * g * nqt * sub, 1, MAX_SLC_BLOCKS), F32),
        ],
        compiler_params=_params("arbitrary", "arbitrary", "arbitrary"),
        name="nsa_cmp",
    )(q, kca, vct, gates)


def _super_lists_kernel(used_ref, lst_ref, cnt_ref, *, tb):
    ut = used_ref[...].T.astype(BF16)
    a_i = lax.broadcasted_iota(jnp.int32, (MAX_SLC_BLOCKS, MAX_SLC_BLOCKS), 0)
    b_i = lax.broadcasted_iota(jnp.int32, (MAX_SLC_BLOCKS, MAX_SLC_BLOCKS), 1)
    pair_t = (a_i == jnp.right_shift(b_i, SUPER_SHIFT)).astype(BF16)
    sbf = jnp.where(_dot(pair_t, ut) > 0.5, 1.0, 0.0).astype(BF16)
    incl = _dot((b_i <= a_i).astype(BF16), sbf)[0:MAX_SUPER]
    cnt_ref[...] = jnp.broadcast_to(incl[MAX_SUPER - 1:MAX_SUPER, :], (8, tb)).astype(jnp.int32)
    for i in range(MAX_SUPER):
        lst_ref[i:i + 1, :] = jnp.sum(jnp.where(incl <= float(i), 1.0, 0.0), axis=0, keepdims=True).astype(jnp.int32)


def _super_lists(used):
    t = used.shape[0]
    tb = min(t, LANES)
    assert t % tb == 0
    return pl.pallas_call(
        functools.partial(_super_lists_kernel, tb=tb),
        grid=(t // tb,),
        in_specs=[pl.BlockSpec((tb, MAX_SLC_BLOCKS), lambda i: (i, 0))],
        out_specs=[
            pl.BlockSpec((MAX_SUPER, tb), lambda i: (0, i)),
            pl.BlockSpec((8, tb), lambda i: (0, i)),
        ],
        out_shape=[
            jax.ShapeDtypeStruct((MAX_SUPER, t), jnp.int32),
            jax.ShapeDtypeStruct((8, t), jnp.int32),
        ],
        compiler_params=_params("arbitrary"),
        name="super_lists",
    )(used)


def _nsa_slc_kernel(lst_ref, cnt_ref, q_ref, kas_ref, vts_ref, kaw_ref, vtw_ref, negm_ref, gate_ref, oc_ref, o_ref,
                    m_ref, acc_ref, *, tq, sub):
    hpg = HEADS_PER_GROUP
    ncol = hpg * tq
    n_win = WINDOW // SUPER_KEYS + 1
    pad_k = (lax.broadcasted_iota(jnp.int32, (SUPER_KEYS, 2 * LANES), 1) == PAD_LANE).astype(BF16)
    pad_w = (lax.broadcasted_iota(jnp.int32, (SUPER_KEYS, LANES), 1) == PAD_LANE).astype(BF16)
    n_tiles = pl.num_programs(0) * pl.num_programs(1) * pl.num_programs(2) * sub
    tile0 = ((pl.program_id(0) * pl.num_programs(1) + pl.program_id(1)) * pl.num_programs(2) + pl.program_id(2)) * sub
    row = lax.broadcasted_iota(jnp.int32, (SUPER_KEYS, 1), 0)

    class Tile:
        def __init__(self, s, slots):
            self.s = s
            self.slots = slots
            self.rows = slice(s * tq, (s + 1) * tq)
            self.qt = pl.program_id(2) * sub + s
            self.q = q_ref[0, :, self.rows, :].reshape(ncol, LANES)
            negm = negm_ref[0, 0, self.rows, :]
            self.qa = jnp.concatenate([self.q, jnp.concatenate([negm] * hpg, axis=0)], axis=1)
            self.tcol = self.qt * tq + jnp.bitwise_and(lax.broadcasted_iota(jnp.int32, (1, ncol), 1), tq - 1)
            self.tile = tile0 + s
            self.n_other = cnt_ref[self.tile] - 1
            self.n_early = jnp.right_shift(jnp.maximum(self.n_other - slots, 0) + CHUNK_SUPERS - 1,
                                           CHUNK_SUPERS.bit_length() - 1)
            m_ref[s] = jnp.full((1, ncol), NEG_INF, F32)
            acc_ref[s] = jnp.zeros((VT_ROWS, ncol), F32)

        def super_block(self, slot, ok=None):
            sb = lst_ref[slot * n_tiles + self.tile]
            if ok is not None:
                sb = jnp.where(ok, sb, 0)
            start = pl.multiple_of(sb * SUPER_KEYS, SUPER_KEYS)
            return kas_ref[0, 0, pl.ds(start, SUPER_KEYS), :], vts_ref[0, 0, :, pl.ds(start, SUPER_KEYS)]

        def update(self, st, vt):
            m_old = m_ref[self.s]
            m_new = jnp.maximum(m_old, jnp.max(st, axis=0, keepdims=True))
            p = jnp.exp2(st - m_new)
            acc_ref[self.s] = jnp.exp2(m_old - m_new) * acc_ref[self.s] + _dot(vt, p.astype(BF16))
            m_ref[self.s] = m_new

        def early_chunks(self):
            def body(c, carry):
                parts = [self.super_block(c * CHUNK_SUPERS + j) for j in range(CHUNK_SUPERS)]
                kt = jnp.concatenate([k for k, _ in parts], axis=0)
                self.update(_dot_nt(kt, self.qa), jnp.concatenate([v for _, v in parts], axis=1))
                return carry

            lax.fori_loop(0, self.n_early, body, 0)

        def final_scores(self):
            dstart = pl.multiple_of(self.qt * SUPER_KEYS, SUPER_KEYS)
            sd = _dot_nt(kas_ref[0, 0, pl.ds(dstart, SUPER_KEYS), :], self.qa)
            sd = jnp.where(self.qt * SUPER_KEYS + row <= self.tcol, sd, NEG_INF)
            ks, vs = [], [vts_ref[0, 0, :, pl.ds(dstart, SUPER_KEYS)]]
            for j in range(self.slots):
                slot = self.n_early * CHUNK_SUPERS + j
                ok = slot < self.n_other
                k, v = self.super_block(jnp.minimum(slot, MAX_SUPER - 1), ok)
                ks.append(_select_rows(ok, k, pad_k))
                vs.append(v)
            self.st = jnp.concatenate([sd, _dot_nt(jnp.concatenate(ks, axis=0), self.qa)], axis=0)
            self.vt = jnp.concatenate(vs, axis=1)
            win_scores, win_v = [], []
            for r in range(n_win):
                sb = self.qt - (n_win - 1) + r
                start = pl.multiple_of(jnp.maximum(sb, 0) * SUPER_KEYS, SUPER_KEYS)
                kw = kaw_ref[0, 0, pl.ds(start, SUPER_KEYS), :]
                if r < n_win - 1:
                    kw = _select_rows(sb >= 0, kw, pad_w)
                sw = _dot_nt(kw, self.q)
                if r == n_win - 1:
                    sw = jnp.where(sb * SUPER_KEYS + row <= self.tcol, sw, NEG_INF)
                elif r == 0:
                    sw = jnp.where(sb * SUPER_KEYS + WINDOW + row > self.tcol, sw, NEG_INF)
                win_scores.append(sw)
                win_v.append(vtw_ref[0, 0, :, pl.ds(start, SUPER_KEYS)])
            self.stw = jnp.concatenate(win_scores, axis=0)
            self.vw = jnp.concatenate(win_v, axis=1)

        def finish(self):
            self.update(self.st, self.vt)
            acc = acc_ref[self.s]
            o_slc_t = acc[0:HEAD_DIM] * (1.0 / acc[HEAD_DIM:HEAD_DIM + 1])
            e = jnp.exp2(self.stw - jnp.max(self.stw, axis=0, keepdims=True))
            acc_w = _dot(self.vw, e.astype(BF16))
            o_win_t = acc_w[0:HEAD_DIM] * (1.0 / acc_w[HEAD_DIM:HEAD_DIM + 1])
            gt = gate_ref[0, 0, self.rows, :].T
            comb = _head_stack(o_slc_t, gt, 1, tq) + _head_stack(o_win_t, gt, 2, tq)
            o_ref[0, self.rows, :] = (oc_ref[0, self.rows, :] + comb.T).astype(BF16)

    def run(slots):
        tiles = [Tile(s, slots) for s in range(sub)]
        for t in tiles:
            t.early_chunks()
        for t in tiles:
            t.final_scores()
        for t in tiles:
            t.finish()

    longest = cnt_ref[tile0] - 1
    for s in range(1, sub):
        longest = jnp.maximum(longest, cnt_ref[tile0 + s] - 1)
    pl.when(longest <= SHORT_SLOTS)(functools.partial(run, SHORT_SLOTS))
    pl.when(longest > SHORT_SLOTS)(functools.partial(run, FINAL_SLOTS))


def _nsa_slc(lists, counts, q, kas, vts, kaw, vtw, negm, gates, oc, tq=128, sub=4):
    b, nh, s, _ = q.shape
    g, hpg, dh = N_KV_GROUPS, HEADS_PER_GROUP, HEAD_DIM
    ts = tq * sub
    assert tq == SUPER_KEYS and WINDOW % tq == 0 and s >= WINDOW + tq and s % ts == 0
    grid_spec = pltpu.PrefetchScalarGridSpec(
        num_scalar_prefetch=2,
        grid=(b, g, s // ts),
        in_specs=[
            pl.BlockSpec((1, hpg, ts, LANES), lambda i, j, k, *_: (i, j, k, 0)),
            pl.BlockSpec((1, 1, s, 2 * LANES), lambda i, j, k, *_: (i, j, 0, 0)),
            pl.BlockSpec((1, 1, VT_ROWS, s), lambda i, j, k, *_: (i, j, 0, 0)),
            pl.BlockSpec((1, 1, s, LANES), lambda i, j, k, *_: (i, j, 0, 0)),
            pl.BlockSpec((1, 1, VT_ROWS, s), lambda i, j, k, *_: (i, j, 0, 0)),
            pl.BlockSpec((1, 1, ts, MAX_SLC_BLOCKS), lambda i, j, k, *_: (i, j, k, 0)),
            pl.BlockSpec((1, 1, ts, LANES), lambda i, j, k, *_: (i, j, k, 0)),
            pl.BlockSpec((1, ts, hpg * dh), lambda i, j, k, *_: (i, k, j)),
        ],
        out_specs=pl.BlockSpec((1, ts, hpg * dh), lambda i, j, k, *_: (i, k, j)),
        scratch_shapes=[
            pltpu.VMEM((sub, 1, hpg * tq), F32),
            pltpu.VMEM((sub, VT_ROWS, hpg * tq), F32),
        ],
    )
    return pl.pallas_call(
        functools.partial(_nsa_slc_kernel, tq=tq, sub=sub),
        grid_spec=grid_spec,
        out_shape=jax.ShapeDtypeStruct((b, s, nh * dh), BF16),
        compiler_params=_params("arbitrary", "arbitrary", "arbitrary"),
        name="nsa_slc",
    )(lists, counts, q, kas, vts, kaw, vtw, negm, gates, oc)


def _split_bf16(x, n):
    pieces = []
    for _ in range(n):
        p = x.astype(BF16).astype(F32)
        pieces.append(p)
        x = x - p
    return pieces


def _q_constants():
    slopes = jnp.exp2(-8.0 * jnp.arange(1, N_HEADS + 1, dtype=F32) / N_HEADS) * LOG2E
    pieces = jnp.stack(_split_bf16(slopes, N_SPLIT), axis=1)
    qc = jnp.zeros((N_HEADS, LANES), F32)
    qc = qc.at[:, POS_LANE:POS_LANE + N_SPLIT].set(pieces * L_SLC)
    qc = qc.at[:, POS_LANE + N_SPLIT:POS_LANE + 2 * N_SPLIT].set(pieces)
    return qc.at[:, PAD_LANE].set(NEG_INF)


def _qg_weight(w_qg):
    d = w_qg.shape[0]
    nq = N_HEADS * HEAD_DIM
    wg = jnp.pad(w_qg[:, nq:].reshape(d, N_KV_GROUPS, 3 * HEADS_PER_GROUP),
                 ((0, 0), (0, 0), (0, LANES - 3 * HEADS_PER_GROUP)))
    return jnp.concatenate([w_qg[:, :nq], wg.reshape(d, N_KV_GROUPS * LANES)], axis=1).astype(BF16)


def kernel(x, a_norm, a_w_in, a_conv, a_w_out, kv_norm, w_kv, cmp_pe_k, cmp_w1_k, cmp_w2_k, cmp_pe_v, cmp_w1_v,
           cmp_w2_v, b_norm, b_w_qg, b_w_o, f_norm, f_w_gu, f_w_down, final_norm):
    b, s, d = x.shape
    nq = N_HEADS * HEAD_DIM
    h = _mixer_a(x, a_norm[0], a_w_in[0].astype(BF16), a_conv[0], a_w_out[0].astype(BF16))
    h = _ffn(h.reshape(b * s, d), f_norm[0], f_w_gu[0].astype(BF16), f_w_down[0].astype(BF16), final_norm)
    h = h.reshape(b, s, d)

    kvc, kas, vts, kaw, vtw, q, gates = _proj(h, kv_norm, b_norm[0], w_kv.astype(BF16), _qg_weight(b_w_qg[0]),
                                              _q_constants())
    pe2 = jnp.stack([cmp_pe_k, cmp_pe_v]).reshape(2, 1, L_CMP * HEAD_DIM)
    pe2 = jnp.broadcast_to(pe2, (2, 8, L_CMP * HEAD_DIM)).astype(BF16)
    w2 = jnp.pad(jnp.stack([cmp_w2_k, cmp_w2_v]), ((0, 0), (0, 0), (0, LANES - HEAD_DIM))).astype(BF16)
    kca, vct = _compress(kvc, pe2, jnp.stack([cmp_w1_k, cmp_w1_v]).astype(BF16), w2)

    oc, negm, used = _nsa_cmp(q, kca, vct, gates)
    lists, counts = _super_lists(used.reshape(used.shape[0], MAX_SLC_BLOCKS))
    o = _nsa_slc(lists.reshape(-1), counts[0], q, kas, vts, kaw, vtw, negm, gates, oc)

    out = _ffn(h.reshape(b * s, d), f_norm[1], f_w_gu[1].astype(BF16), f_w_down[1].astype(BF16), final_norm,
               o=o.reshape(b * s, nq), w_o=b_w_o[0].astype(BF16), with_final=True)
    return out.reshape(b, s, d)
```

```python
import functools
import math

import jax
import jax.numpy as jnp
from jax import lax
from jax.experimental import pallas as pl
from jax.experimental.pallas import tpu as pltpu

F32 = jnp.float32
BF16 = jnp.bfloat16

N_HEADS = 16
N_KV_GROUPS = 4
HEADS_PER_GROUP = N_HEADS // N_KV_GROUPS
HEAD_DIM = 64
L_CMP = 32
D_CMP = 16
L_SLC = 64
N_SEL = 16
N_FORCED = 3
WINDOW = 512
CONV_WIDTH = 3
RMS_EPS = 1e-5
NEG_INF = -1e30
PICKED = -3.0e38
LOG2E = math.log2(math.e)
LANES = 128
MAX_SLC_BLOCKS = LANES
SUPER_KEYS = LANES
SUPER_SHIFT = (SUPER_KEYS // L_SLC).bit_length() - 1
MAX_SUPER = MAX_SLC_BLOCKS * L_SLC // SUPER_KEYS
CHUNK_SUPERS = 4
FINAL_SLOTS = 10
SHORT_SLOTS = 8
CMP_ROW_TILE = 64
VT_ROWS = HEAD_DIM + 16
POS_LANE = HEAD_DIM
N_SPLIT = 3
PAD_LANE = POS_LANE + 2 * N_SPLIT
VMEM_LIMIT_BYTES = 56 * 1024 * 1024


def _rms(x, g):
    return x * lax.rsqrt(jnp.mean(x * x, axis=-1, keepdims=True) + RMS_EPS) * g


def _dot(a, b):
    return jnp.dot(a, b, preferred_element_type=F32)


def _dot_nt(a, b):
    return lax.dot_general(a, b, (((1,), (1,)), ((), ())), preferred_element_type=F32)


def _params(*sem):
    return pltpu.CompilerParams(dimension_semantics=sem, vmem_limit_bytes=VMEM_LIMIT_BYTES)


def _pos_columns(coarse, fine, lane):
    is_c = (lane >= POS_LANE) & (lane < POS_LANE + N_SPLIT)
    is_f = (lane >= POS_LANE + N_SPLIT) & (lane < POS_LANE + 2 * N_SPLIT)
    return jnp.where(is_c, coarse, jnp.where(is_f, fine, 0.0))


def _mixer_a_kernel(x_ref, g_ref, win_ref, conv_ref, wout_ref, o_ref, ext_ref, *, tm, d):
    @pl.when(pl.program_id(1) == 0)
    def _():
        ext_ref[0:8, :] = jnp.zeros((8, d), F32)

    x = x_ref[0]
    xb = _rms(x, g_ref[...]).astype(BF16)
    b_gate = _dot(xb, win_ref[:, 0:d])
    cv = _dot(xb, win_ref[:, d:2 * d]) * _dot(xb, win_ref[:, 2 * d:3 * d])
    ext_ref[8:8 + tm, :] = cv
    cv1 = ext_ref[7:7 + tm, :]
    cv2 = ext_ref[6:6 + tm, :]
    u = conv_ref[0:1, :] * cv2 + conv_ref[1:2, :] * cv1 + conv_ref[2:3, :] * cv
    ext_ref[0:8, :] = cv[tm - 8:tm, :]
    o_ref[0] = x + _dot((b_gate * u).astype(BF16), wout_ref[...])


def _mixer_a(x, g, w_in, conv_w, w_out, tm=1024):
    b, s, d = x.shape
    assert s % tm == 0
    return pl.pallas_call(
        functools.partial(_mixer_a_kernel, tm=tm, d=d),
        grid=(b, s // tm),
        in_specs=[
            pl.BlockSpec((1, tm, d), lambda i, j: (i, j, 0)),
            _resident((1, d)), _resident((d, 3 * d)), _resident((CONV_WIDTH, d)), _resident((d, d)),
        ],
        out_specs=pl.BlockSpec((1, tm, d), lambda i, j: (i, j, 0)),
        out_shape=jax.ShapeDtypeStruct((b, s, d), F32),
        scratch_shapes=[pltpu.VMEM((tm + 8, d), F32)],
        compiler_params=_params("arbitrary", "arbitrary"),
        name="mixer_a",
    )(x, g.reshape(1, d), w_in, conv_w, w_out)


def _ffn_kernel(*refs, with_proj, with_final, f, tf):
    if with_proj:
        h_ref, o_ref_in, wo_ref, g_ref, wgu_ref, wd_ref, fg_ref, out_ref = refs
    else:
        h_ref, g_ref, wgu_ref, wd_ref, fg_ref, out_ref = refs
    h = h_ref[...]
    if with_proj:
        h = h + _dot(o_ref_in[...], wo_ref[...])
    xb = _rms(h, g_ref[...]).astype(BF16)
    y = h
    for c in range(f // tf):
        gate = _dot(xb, wgu_ref[:, c * tf:(c + 1) * tf])
        up = _dot(xb, wgu_ref[:, f + c * tf:f + (c + 1) * tf])
        act = (gate * jax.nn.sigmoid(gate)) * up
        y = y + _dot(act.astype(BF16), wd_ref[c * tf:(c + 1) * tf, :])
    if with_final:
        y = _rms(y, fg_ref[...])
    out_ref[...] = y


def _resident(shape):
    return pl.BlockSpec(shape, lambda *_: (0,) * len(shape), pipeline_mode=pl.Buffered(1))


def _ffn(h, g, w_gu, w_down, final_g, o=None, w_o=None, with_final=False, tm=1024, tf=256):
    t, d = h.shape
    f = w_down.shape[0]
    assert t % tm == 0 and f % tf == 0
    with_proj = o is not None
    in_specs = [pl.BlockSpec((tm, d), lambda i: (i, 0))]
    args = [h]
    if with_proj:
        in_specs += [pl.BlockSpec((tm, o.shape[1]), lambda i: (i, 0)), _resident(w_o.shape)]
        args += [o, w_o]
    in_specs += [_resident((1, d)), _resident(w_gu.shape), _resident(w_down.shape), _resident((1, d))]
    args += [g.reshape(1, d), w_gu, w_down, final_g.reshape(1, d)]
    return pl.pallas_call(
        functools.partial(_ffn_kernel, with_proj=with_proj, with_final=with_final, f=f, tf=tf),
        grid=(t // tm,),
        in_specs=in_specs,
        out_specs=pl.BlockSpec((tm, d), lambda i: (i, 0)),
        out_shape=jax.ShapeDtypeStruct((t, d), F32),
        compiler_params=_params("arbitrary"),
        name="ffn_proj" if with_proj else "ffn",
    )(*args)


def _halves(x):
    return x, pltpu.roll(x, HEAD_DIM, axis=1)


def _proj_kernel(h_ref, gkv_ref, gq_ref, wkv_ref, wqg_ref, qc_ref,
                 kvc_ref, kas_ref, vts_ref, kaw_ref, vtw_ref, q_ref, gate_ref, rel_ref, *, tm):
    g, dh = N_KV_GROUPS, HEAD_DIM
    gw = g * dh
    s0 = pl.program_id(1) * tm
    x = h_ref[0]
    y = x * lax.rsqrt(jnp.mean(x * x, axis=-1, keepdims=True) + RMS_EPS)
    kv = _dot((y * gkv_ref[...]).astype(BF16), wkv_ref[...])
    low_m = lax.broadcasted_iota(jnp.int32, (tm // D_CMP, LANES), 1) < dh
    for st in range(2):
        for pair in range(g // 2):
            slab = rel_ref.at[st * (g // 2) + pair]
            slab[...] = kv[:, st * gw + pair * LANES:st * gw + (pair + 1) * LANES]
            for j in range(D_CMP // 2):
                a = slab[pl.ds(2 * j, tm // D_CMP, stride=D_CMP), :]
                b = slab[pl.ds(2 * j + 1, tm // D_CMP, stride=D_CMP), :]
                cols = slice(j * LANES, (j + 1) * LANES)
                kvc_ref[st, 0, 2 * pair, :, cols] = jnp.where(low_m, a, pltpu.roll(b, dh, axis=1)).astype(BF16)
                kvc_ref[st, 0, 2 * pair + 1, :, cols] = jnp.where(low_m, pltpu.roll(a, dh, axis=1), b).astype(BF16)

    kpos = s0 + lax.broadcasted_iota(jnp.int32, (tm, LANES), 0)
    lane = lax.broadcasted_iota(jnp.int32, (tm, LANES), 1)
    low = lane < dh
    kblk = jnp.right_shift(kpos, L_SLC.bit_length() - 1)
    posc = _pos_columns(kblk.astype(F32), jnp.bitwise_and(kpos, L_SLC - 1).astype(F32), lane)
    onehot = (lane == kblk).astype(BF16)
    ones_rows = (lax.broadcasted_iota(jnp.int32, (VT_ROWS - dh, tm), 0) == 0).astype(BF16)
    for pair in range(g // 2):
        for st, k_ref, vt_ref in ((2, kas_ref, vts_ref), (4, kaw_ref, vtw_ref)):
            k_pair = kv[:, st * gw + pair * LANES:st * gw + (pair + 1) * LANES]
            vt = kv[:, (st + 1) * gw + pair * LANES:(st + 1) * gw + (pair + 1) * LANES].T
            for half, kh in enumerate(_halves(k_pair)):
                grp = 2 * pair + half
                k_ref[0, grp, :, 0:LANES] = jnp.where(low, kh, posc).astype(BF16)
                if st == 2:
                    k_ref[0, grp, :, LANES:2 * LANES] = onehot
                vt_ref[0, grp, 0:dh] = vt[half * dh:(half + 1) * dh].astype(BF16)
                vt_ref[0, grp, dh:VT_ROWS] = ones_rows

    qg = _dot((y * gq_ref[...]).astype(BF16), wqg_ref[...])
    for pair in range(N_HEADS // 2):
        q_pair = qg[:, pair * LANES:(pair + 1) * LANES] * (LOG2E * dh ** -0.5)
        for half, qh in enumerate(_halves(q_pair)):
            hd = 2 * pair + half
            q_ref[0, hd] = jnp.where(low, qh, qc_ref[hd:hd + 1, :]).astype(BF16)
    nq = N_HEADS * dh
    for grp in range(g):
        gate_ref[0, grp] = jax.nn.sigmoid(qg[:, nq + grp * LANES: nq + (grp + 1) * LANES])


def _proj(h, g_kv, g_q, w_kv, w_qg, qconst, tm=1024):
    b, s, d = h.shape
    assert s % tm == 0 and s // L_SLC <= MAX_SLC_BLOCKS
    g, dh = N_KV_GROUPS, HEAD_DIM
    return pl.pallas_call(
        functools.partial(_proj_kernel, tm=tm),
        grid=(b, s // tm),
        in_specs=[
            pl.BlockSpec((1, tm, d), lambda i, j: (i, j, 0)),
            _resident((1, d)), _resident((1, d)), _resident(w_kv.shape), _resident(w_qg.shape),
            _resident(qconst.shape),
        ],
        out_specs=[
            pl.BlockSpec((2, 1, g, tm // D_CMP, D_CMP * dh), lambda i, j: (0, i, 0, j, 0)),
            pl.BlockSpec((1, g, tm, 2 * LANES), lambda i, j: (i, 0, j, 0)),
            pl.BlockSpec((1, g, VT_ROWS, tm), lambda i, j: (i, 0, 0, j)),
            pl.BlockSpec((1, g, tm, LANES), lambda i, j: (i, 0, j, 0)),
            pl.BlockSpec((1, g, VT_ROWS, tm), lambda i, j: (i, 0, 0, j)),
            pl.BlockSpec((1, N_HEADS, tm, LANES), lambda i, j: (i, 0, j, 0)),
            pl.BlockSpec((1, g, tm, LANES), lambda i, j: (i, 0, j, 0)),
        ],
        out_shape=[
            jax.ShapeDtypeStruct((2, b, g, s // D_CMP, D_CMP * dh), BF16),
            jax.ShapeDtypeStruct((b, g, s, 2 * LANES), BF16),
            jax.ShapeDtypeStruct((b, g, VT_ROWS, s), BF16),
            jax.ShapeDtypeStruct((b, g, s, LANES), BF16),
            jax.ShapeDtypeStruct((b, g, VT_ROWS, s), BF16),
            jax.ShapeDtypeStruct((b, N_HEADS, s, LANES), BF16),
            jax.ShapeDtypeStruct((b, g, s, LANES), F32),
        ],
        scratch_shapes=[pltpu.VMEM((2 * (g // 2), tm, LANES), F32)],
        compiler_params=_params("arbitrary", "arbitrary"),
        name="proj",
    )(h, g_kv.reshape(1, d), g_q.reshape(1, d), w_kv, w_qg, qconst)


def _compress_kernel(r_ref, pe_ref, w1_ref, w2_ref, kca_ref, vct_ref, *, m):
    half = D_CMP * HEAD_DIM
    row = lax.broadcasted_iota(jnp.int32, (m, LANES), 0)
    lane = lax.broadcasted_iota(jnp.int32, (m, LANES), 1)

    def mlp(i):
        r = r_ref[i, 0]
        top = _dot(r, w1_ref[i, 0:half, :])
        bot = _dot(r, w1_ref[i, half:2 * half, :])
        pe_term = _dot(pe_ref[i], w1_ref[i])[0:1, :]
        hid = top + pltpu.roll(bot, m - 1, axis=0) + pe_term
        hid = hid * jax.nn.sigmoid(hid)
        out = _dot(hid.astype(BF16), w2_ref[i])
        return jnp.where(row < m - 1, out, 0.0)

    coarse = jnp.right_shift(row, 2).astype(F32)
    fine = jnp.bitwise_and(row, 3).astype(F32) * D_CMP + (L_CMP - 1) / 2.0
    kca_ref[0] = (mlp(0) + _pos_columns(coarse, fine, lane)).astype(BF16)
    vct_ref[0] = jnp.where(lane == HEAD_DIM, 1.0, mlp(1)).T[0:VT_ROWS].astype(BF16)


def _compress(kvc, pe2, w1_2, w2_2):
    _, b, g, m, _ = kvc.shape
    dh = HEAD_DIM
    r = kvc.reshape(2, b * g, m, D_CMP * dh)
    hid = w1_2.shape[-1]
    return pl.pallas_call(
        functools.partial(_compress_kernel, m=m),
        grid=(b * g,),
        in_specs=[
            pl.BlockSpec((2, 1, m, D_CMP * dh), lambda j: (0, j, 0, 0)),
            pl.BlockSpec((2, 8, L_CMP * dh), lambda j: (0, 0, 0)),
            pl.BlockSpec((2, L_CMP * dh, hid), lambda j: (0, 0, 0)),
            pl.BlockSpec((2, hid, LANES), lambda j: (0, 0, 0)),
        ],
        out_specs=[
            pl.BlockSpec((1, m, LANES), lambda j: (j, 0, 0)),
            pl.BlockSpec((1, VT_ROWS, m), lambda j: (j, 0, 0)),
        ],
        out_shape=[
            jax.ShapeDtypeStruct((b * g, m, LANES), BF16),
            jax.ShapeDtypeStruct((b * g, VT_ROWS, m), BF16),
        ],
        compiler_params=_params("arbitrary"),
        name="compress",
    )(r, pe2, w1_2, w2_2)


def _select_rows(pred, a, b):
    a32, b32 = pltpu.bitcast(a, jnp.int32), pltpu.bitcast(b, jnp.int32)
    return pltpu.bitcast(jnp.where(pred, a32, b32), BF16)


def _head_stack(ot, gt, branch, tq):
    return jnp.concatenate(
        [ot[:, h * tq:(h + 1) * tq] * gt[3 * h + branch:3 * h + branch + 1, :] for h in range(HEADS_PER_GROUP)],
        axis=0)


def _nsa_cmp_kernel(q_ref, kca_ref, vct_ref, gate_ref, oc_ref, negm_ref, used_ref, *, tq, m):
    q0 = pl.program_id(2) * tq
    hpg = HEADS_PER_GROUP
    ncol = hpg * tq
    q = q_ref[0].reshape(ncol, LANES)
    tcol = q0 + jnp.bitwise_and(lax.broadcasted_iota(jnp.int32, (1, ncol), 1), tq - 1)

    def attend(rows):
        st = _dot_nt(kca_ref[0, 0:rows, :], q)
        band = max(rows - row_tile - 8, 0)
        nrow = band + lax.broadcasted_iota(jnp.int32, (rows - band, 1), 0)
        low = jnp.where((nrow * D_CMP + (L_CMP - 1)) <= tcol, st[band:], NEG_INF)
        st = low if band == 0 else jnp.concatenate([st[0:band], low], axis=0)
        e = jnp.exp2(st - jnp.max(st, axis=0, keepdims=True)).astype(BF16)
        acc = _dot(vct_ref[0, :, 0:rows], e)
        rcp = jnp.where(tcol >= L_CMP - 1, 1.0 / acc[HEAD_DIM:HEAD_DIM + 1], 0.0)
        ot = acc[0:HEAD_DIM] * rcp
        nblk = min(MAX_SLC_BLOCKS, rows * D_CMP // L_SLC)
        jj = lax.broadcasted_iota(jnp.int32, (nblk, rows), 0) * L_SLC
        nn = lax.broadcasted_iota(jnp.int32, (nblk, rows), 1) * D_CMP
        overlap_t = ((nn < jj + L_SLC) & (nn + L_CMP > jj)).astype(BF16)
        imp_h = _dot(overlap_t, e) * rcp
        imp = imp_h[:, 0:tq]
        for h in range(1, hpg):
            imp = imp + imp_h[:, h * tq:(h + 1) * tq]

        jrow = lax.broadcasted_iota(jnp.int32, (nblk, tq), 0)
        cur = jnp.right_shift(q0 + lax.broadcasted_iota(jnp.int32, (1, tq), 1), L_SLC.bit_length() - 1)
        valid = jrow <= cur
        forced = (jrow == 0) | (jrow == cur) | (jrow == cur - 1)
        val = jnp.where(valid & jnp.logical_not(forced), imp, NEG_INF)
        jrow_f = jrow.astype(F32)
        for _ in range(N_SEL - N_FORCED):
            mx = jnp.max(val, axis=0, keepdims=True)
            first = jnp.min(jnp.where(val == mx, jrow_f, float(MAX_SLC_BLOCKS)), axis=0, keepdims=True)
            val = jnp.where(jrow_f == first, PICKED, val)
        sel = jnp.where(valid & (forced | (val == PICKED)), 1.0, 0.0)
        if nblk < MAX_SLC_BLOCKS:
            sel = jnp.concatenate([sel, jnp.zeros((MAX_SLC_BLOCKS - nblk, tq), F32)], axis=0)
        gt = gate_ref[0, 0].T
        oc_ref[0] = _head_stack(ot, gt, 0, tq).T
        sel_t = sel.T
        negm_ref[0, 0] = jnp.where(sel_t > 0.5, 0.0, NEG_INF).astype(BF16)
        for r in range(tq // SUPER_KEYS):
            used_ref[r] = jnp.max(sel_t[r * SUPER_KEYS:(r + 1) * SUPER_KEYS], axis=0, keepdims=True)

    row_tile = min(m, CMP_ROW_TILE)
    n_variants = m // row_tile
    assert n_variants == 1 or row_tile % (tq // D_CMP) == 0
    if n_variants == 1:
        attend(m)
    else:
        needed = pl.program_id(2) * (tq // D_CMP) + (tq // D_CMP - 1)
        variant = jnp.minimum(jnp.right_shift(needed - 1, row_tile.bit_length() - 1), n_variants - 1)
        for v in range(n_variants):
            pl.when(variant == v)(functools.partial(attend, (v + 1) * row_tile))


def _nsa_cmp(q, kca, vct, gates, tq=1024):
    b, nh, s, _ = q.shape
    g, hpg, dh = N_KV_GROUPS, HEADS_PER_GROUP, HEAD_DIM
    m = kca.shape[1]
    assert s % tq == 0 and tq % SUPER_KEYS == 0 and tq & (tq - 1) == 0
    nqt = s // tq
    sub = tq // SUPER_KEYS
    return pl.pallas_call(
        functools.partial(_nsa_cmp_kernel, tq=tq, m=m),
        grid=(b, g, s // tq),
        in_specs=[
            pl.BlockSpec((1, hpg, tq, LANES), lambda i, j, k: (i, j, k, 0)),
            pl.BlockSpec((1, m, LANES), lambda i, j, k: (i * N_KV_GROUPS + j, 0, 0)),
            pl.BlockSpec((1, VT_ROWS, m), lambda i, j, k: (i * N_KV_GROUPS + j, 0, 0)),
            pl.BlockSpec((1, 1, tq, LANES), lambda i, j, k: (i, j, k, 0)),
        ],
        out_specs=[
            pl.BlockSpec((1, tq, hpg * dh), lambda i, j, k: (i, k, j)),
            pl.BlockSpec((1, 1, tq, MAX_SLC_BLOCKS), lambda i, j, k: (i, j, k, 0)),
            pl.BlockSpec((sub, 1, MAX_SLC_BLOCKS), lambda i, j, k: ((i * N_KV_GROUPS + j) * nqt + k, 0, 0)),
        ],
        out_shape=[
            jax.ShapeDtypeStruct((b, s, nh * dh), F32),
            jax.ShapeDtypeStruct((b, g, s, MAX_SLC_BLOCKS), BF16),
            jax.ShapeDtypeStruct((b * g * nqt * sub, 1, MAX_SLC_BLOCKS), F32),
        ],
        compiler_params=_params("arbitrary", "arbitrary", "arbitrary"),
        name="nsa_cmp",
    )(q, kca, vct, gates)


def _super_lists_kernel(used_ref, lst_ref, cnt_ref, *, tb):
    ut = used_ref[...].T.astype(BF16)
    a_i = lax.broadcasted_iota(jnp.int32, (MAX_SLC_BLOCKS, MAX_SLC_BLOCKS), 0)
    b_i = lax.broadcasted_iota(jnp.int32, (MAX_SLC_BLOCKS, MAX_SLC_BLOCKS), 1)
    pair_t = (a_i == jnp.right_shift(b_i, SUPER_SHIFT)).astype(BF16)
    sbf = jnp.where(_dot(pair_t, ut) > 0.5, 1.0, 0.0).astype(BF16)
    incl = _dot((b_i <= a_i).astype(BF16), sbf)[0:MAX_SUPER]
    cnt_ref[...] = jnp.broadcast_to(incl[MAX_SUPER - 1:MAX_SUPER, :], (8, tb)).astype(jnp.int32)
    for i in range(MAX_SUPER):
        lst_ref[i:i + 1, :] = jnp.sum(jnp.where(incl <= float(i), 1.0, 0.0), axis=0, keepdims=True).astype(jnp.int32)


def _super_lists(used):
    t = used.shape[0]
    tb = min(t, LANES)
    assert t % tb == 0
    return pl.pallas_call(
        functools.partial(_super_lists_kernel, tb=tb),
        grid=(t // tb,),
        in_specs=[pl.BlockSpec((tb, MAX_SLC_BLOCKS), lambda i: (i, 0))],
        out_specs=[
            pl.BlockSpec((MAX_SUPER, tb), lambda i: (0, i)),
            pl.BlockSpec((8, tb), lambda i: (0, i)),
        ],
        out_shape=[
            jax.ShapeDtypeStruct((MAX_SUPER, t), jnp.int32),
            jax.ShapeDtypeStruct((8, t), jnp.int32),
        ],
        compiler_params=_params("arbitrary"),
        name="super_lists",
    )(used)


def _nsa_slc_kernel(lst_ref, cnt_ref, q_ref, kas_ref, vts_ref, kaw_ref, vtw_ref, negm_ref, gate_ref, oc_ref, o_ref,
                    m_ref, acc_ref, *, tq, sub):
    hpg = HEADS_PER_GROUP
    ncol = hpg * tq
    n_win = WINDOW // SUPER_KEYS + 1
    pad_k = (lax.broadcasted_iota(jnp.int32, (SUPER_KEYS, 2 * LANES), 1) == PAD_LANE).astype(BF16)
    pad_w = (lax.broadcasted_iota(jnp.int32, (SUPER_KEYS, LANES), 1) == PAD_LANE).astype(BF16)
    n_tiles = pl.num_programs(0) * pl.num_programs(1) * pl.num_programs(2) * sub
    tile0 = ((pl.program_id(0) * pl.num_programs(1) + pl.program_id(1)) * pl.num_programs(2) + pl.program_id(2)) * sub
    row = lax.broadcasted_iota(jnp.int32, (SUPER_KEYS, 1), 0)

    class Tile:
        def __init__(self, s, slots):
            self.s = s
            self.slots = slots
            self.rows = slice(s * tq, (s + 1) * tq)
            self.qt = pl.program_id(2) * sub + s
            self.q = q_ref[0, :, self.rows, :].reshape(ncol, LANES)
            negm = negm_ref[0, 0, self.rows, :]
            self.qa = jnp.concatenate([self.q, jnp.concatenate([negm] * hpg, axis=0)], axis=1)
            self.tcol = self.qt * tq + jnp.bitwise_and(lax.broadcasted_iota(jnp.int32, (1, ncol), 1), tq - 1)
            self.tile = tile0 + s
            self.n_other = cnt_ref[self.tile] - 1
            self.n_early = jnp.right_shift(jnp.maximum(self.n_other - slots, 0) + CHUNK_SUPERS - 1,
                                           CHUNK_SUPERS.bit_length() - 1)
            m_ref[s] = jnp.full((1, ncol), NEG_INF, F32)
            acc_ref[s] = jnp.zeros((VT_ROWS, ncol), F32)

        def super_block(self, slot, ok=None):
            sb = lst_ref[slot * n_tiles + self.tile]
            if ok is not None:
                sb = jnp.where(ok, sb, 0)
            start = pl.multiple_of(sb * SUPER_KEYS, SUPER_KEYS)
            return kas_ref[0, 0, pl.ds(start, SUPER_KEYS), :], vts_ref[0, 0, :, pl.ds(start, SUPER_KEYS)]

        def update(self, st, vt):
            m_old = m_ref[self.s]
            m_new = jnp.maximum(m_old, jnp.max(st, axis=0, keepdims=True))
            p = jnp.exp2(st - m_new)
            acc_ref[self.s] = jnp.exp2(m_old - m_new) * acc_ref[self.s] + _dot(vt, p.astype(BF16))
            m_ref[self.s] = m_new

        def early_chunks(self):
            def body(c, carry):
                parts = [self.super_block(c * CHUNK_SUPERS + j) for j in range(CHUNK_SUPERS)]
                kt = jnp.concatenate([k for k, _ in parts], axis=0)
                self.update(_dot_nt(kt, self.qa), jnp.concatenate([v for _, v in parts], axis=1))
                return carry

            lax.fori_loop(0, self.n_early, body, 0)

        def final_scores(self):
            dstart = pl.multiple_of(self.qt * SUPER_KEYS, SUPER_KEYS)
            sd = _dot_nt(kas_ref[0, 0, pl.ds(dstart, SUPER_KEYS), :], self.qa)
            sd = jnp.where(self.qt * SUPER_KEYS + row <= self.tcol, sd, NEG_INF)
            ks, vs = [], [vts_ref[0, 0, :, pl.ds(dstart, SUPER_KEYS)]]
            for j in range(self.slots):
                slot = self.n_early * CHUNK_SUPERS + j
                ok = slot < self.n_other
                k, v = self.super_block(jnp.minimum(slot, MAX_SUPER - 1), ok)
                ks.append(_select_rows(ok, k, pad_k))
                vs.append(v)
            self.st = jnp.concatenate([sd, _dot_nt(jnp.concatenate(ks, axis=0), self.qa)], axis=0)
            self.vt = jnp.concatenate(vs, axis=1)
            win_scores, win_v = [], []
            for r in range(n_win):
                sb = self.qt - (n_win - 1) + r
                start = pl.multiple_of(jnp.maximum(sb, 0) * SUPER_KEYS, SUPER_KEYS)
                kw = kaw_ref[0, 0, pl.ds(start, SUPER_KEYS), :]
                if r < n_win - 1:
                    kw = _select_rows(sb >= 0, kw, pad_w)
                sw = _dot_nt(kw, self.q)
                if r == n_win - 1:
                    sw = jnp.where(sb * SUPER_KEYS + row <= self.tcol, sw, NEG_INF)
                elif r == 0:
                    sw = jnp.where(sb * SUPER_KEYS + WINDOW + row > self.tcol, sw, NEG_INF)
                win_scores.append(sw)
                win_v.append(vtw_ref[0, 0, :, pl.ds(start, SUPER_KEYS)])
            self.stw = jnp.concatenate(win_scores, axis=0)
            self.vw = jnp.concatenate(win_v, axis=1)

        def finish(self):
            self.update(self.st, self.vt)
            acc = acc_ref[self.s]
            o_slc_t = acc[0:HEAD_DIM] * (1.0 / acc[HEAD_DIM:HEAD_DIM + 1])
            e = jnp.exp2(self.stw - jnp.max(self.stw, axis=0, keepdims=True))
            acc_w = _dot(self.vw, e.astype(BF16))
            o_win_t = acc_w[0:HEAD_DIM] * (1.0 / acc_w[HEAD_DIM:HEAD_DIM + 1])
            gt = gate_ref[0, 0, self.rows, :].T
            comb = _head_stack(o_slc_t, gt, 1, tq) + _head_stack(o_win_t, gt, 2, tq)
            o_ref[0, self.rows, :] = (oc_ref[0, self.rows, :] + comb.T).astype(BF16)

    def run(slots):
        tiles = [Tile(s, slots) for s in range(sub)]
        for t in tiles:
            t.early_chunks()
        for t in tiles:
            t.final_scores()
        for t in tiles:
            t.finish()

    longest = cnt_ref[tile0] - 1
    for s in range(1, sub):
        longest = jnp.maximum(longest, cnt_ref[tile0 + s] - 1)
    pl.when(longest <= SHORT_SLOTS)(functools.partial(run, SHORT_SLOTS))
    pl.when(longest > SHORT_SLOTS)(functools.partial(run, FINAL_SLOTS))


def _nsa_slc(lists, counts, q, kas, vts, kaw, vtw, negm, gates, oc, tq=128, sub=4):
    b, nh, s, _ = q.shape
    g, hpg, dh = N_KV_GROUPS, HEADS_PER_GROUP, HEAD_DIM
    ts = tq * sub
    assert tq == SUPER_KEYS and WINDOW % tq == 0 and s >= WINDOW + tq and s % ts == 0
    grid_spec = pltpu.PrefetchScalarGridSpec(
        num_scalar_prefetch=2,
        grid=(b, g, s // ts),
        in_specs=[
            pl.BlockSpec((1, hpg, ts, LANES), lambda i, j, k, *_: (i, j, k, 0)),
            pl.BlockSpec((1, 1, s, 2 * LANES), lambda i, j, k, *_: (i, j, 0, 0)),
            pl.BlockSpec((1, 1, VT_ROWS, s), lambda i, j, k, *_: (i, j, 0, 0)),
            pl.BlockSpec((1, 1, s, LANES), lambda i, j, k, *_: (i, j, 0, 0)),
            pl.BlockSpec((1, 1, VT_ROWS, s), lambda i, j, k, *_: (i, j, 0, 0)),
            pl.BlockSpec((1, 1, ts, MAX_SLC_BLOCKS), lambda i, j, k, *_: (i, j, k, 0)),
            pl.BlockSpec((1, 1, ts, LANES), lambda i, j, k, *_: (i, j, k, 0)),
            pl.BlockSpec((1, ts, hpg * dh), lambda i, j, k, *_: (i, k, j)),
        ],
        out_specs=pl.BlockSpec((1, ts, hpg * dh), lambda i, j, k, *_: (i, k, j)),
        scratch_shapes=[
            pltpu.VMEM((sub, 1, hpg * tq), F32),
            pltpu.VMEM((sub, VT_ROWS, hpg * tq), F32),
        ],
    )
    return pl.pallas_call(
        functools.partial(_nsa_slc_kernel, tq=tq, sub=sub),
        grid_spec=grid_spec,
        out_shape=jax.ShapeDtypeStruct((b, s, nh * dh), BF16),
        compiler_params=_params("arbitrary", "arbitrary", "arbitrary"),
        name="nsa_slc",
    )(lists, counts, q, kas, vts, kaw, vtw, negm, gates, oc)


def _split_bf16(x, n):
    pieces = []
    for _ in range(n):
        p = x.astype(BF16).astype(F32)
        pieces.append(p)
        x = x - p
    return pieces


def _q_constants():
    slopes = jnp.exp2(-8.0 * jnp.arange(1, N_HEADS + 1, dtype=F32) / N_HEADS) * LOG2E
    pieces = jnp.stack(_split_bf16(slopes, N_SPLIT), axis=1)
    qc = jnp.zeros((N_HEADS, LANES), F32)
    qc = qc.at[:, POS_LANE:POS_LANE + N_SPLIT].set(pieces * L_SLC)
    qc = qc.at[:, POS_LANE + N_SPLIT:POS_LANE + 2 * N_SPLIT].set(pieces)
    return qc.at[:, PAD_LANE].set(NEG_INF)


def _qg_weight(w_qg):
    d = w_qg.shape[0]
    nq = N_HEADS * HEAD_DIM
    wg = jnp.pad(w_qg[:, nq:].reshape(d, N_KV_GROUPS, 3 * HEADS_PER_GROUP),
                 ((0, 0), (0, 0), (0, LANES - 3 * HEADS_PER_GROUP)))
    return jnp.concatenate([w_qg[:, :nq], wg.reshape(d, N_KV_GROUPS * LANES)], axis=1).astype(BF16)


def kernel(x, a_norm, a_w_in, a_conv, a_w_out, kv_norm, w_kv, cmp_pe_k, cmp_w1_k, cmp_w2_k, cmp_pe_v, cmp_w1_v,
           cmp_w2_v, b_norm, b_w_qg, b_w_o, f_norm, f_w_gu, f_w_down, final_norm):
    b, s, d = x.shape
    nq = N_HEADS * HEAD_DIM
    h = _mixer_a(x, a_norm[0], a_w_in[0].astype(BF16), a_conv[0], a_w_out[0].astype(BF16))
    h = _ffn(h.reshape(b * s, d), f_norm[0], f_w_gu[0].astype(BF16), f_w_down[0].astype(BF16), final_norm)
    h = h.reshape(b, s, d)

    kvc, kas, vts, kaw, vtw, q, gates = _proj(h, kv_norm, b_norm[0], w_kv.astype(BF16), _qg_weight(b_w_qg[0]),
                                              _q_constants())
    pe2 = jnp.stack([cmp_pe_k, cmp_pe_v]).reshape(2, 1, L_CMP * HEAD_DIM)
    pe2 = jnp.broadcast_to(pe2, (2, 8, L_CMP * HEAD_DIM)).astype(BF16)
    w2 = jnp.pad(jnp.stack([cmp_w2_k, cmp_w2_v]), ((0, 0), (0, 0), (0, LANES - HEAD_DIM))).astype(BF16)
    kca, vct = _compress(kvc, pe2, jnp.stack([cmp_w1_k, cmp_w1_v]).astype(BF16), w2)

    oc, negm, used = _nsa_cmp(q, kca, vct, gates)
    lists, counts = _super_lists(used.reshape(used.shape[0], MAX_SLC_BLOCKS))
    o = _nsa_slc(lists.reshape(-1), counts[0], q, kas, vts, kaw, vtw, negm, gates, oc)

    out = _ffn(h.reshape(b * s, d), f_norm[1], f_w_gu[1].astype(BF16), f_w_down[1].astype(BF16), final_norm,
               o=o.reshape(b * s, nq), w_o=b_w_o[0].astype(BF16), with_final=True)
    return out.reshape(b, s, d)
```

```python
import functools
import math

import jax
import jax.numpy as jnp
from jax import lax
from jax.experimental import pallas as pl
from jax.experimental.pallas import tpu as pltpu

F32 = jnp.float32
BF16 = jnp.bfloat16

N_HEADS = 16
N_KV_GROUPS = 4
HEADS_PER_GROUP = N_HEADS // N_KV_GROUPS
HEAD_DIM = 64
L_CMP = 32
D_CMP = 16
L_SLC = 64
N_SEL = 16
N_FORCED = 3
WINDOW = 512
CONV_WIDTH = 3
RMS_EPS = 1e-5
NEG_INF = -1e30
PICKED = -3.0e38
LOG2E = math.log2(math.e)
LANES = 128
MAX_SLC_BLOCKS = LANES
SUPER_KEYS = LANES
SUPER_SHIFT = (SUPER_KEYS // L_SLC).bit_length() - 1
MAX_SUPER = MAX_SLC_BLOCKS * L_SLC // SUPER_KEYS
CHUNK_SUPERS = 4
FINAL_SLOTS = 10
SHORT_SLOTS = 8
CMP_ROW_TILE = 64
VT_ROWS = HEAD_DIM + 16
GATE_ROWS = 16
POS_LANE = HEAD_DIM
N_SPLIT = 3
PAD_LANE = POS_LANE + 2 * N_SPLIT
VMEM_LIMIT_BYTES = 60 * 1024 * 1024


def _rms(x, g):
    return x * lax.rsqrt(jnp.mean(x * x, axis=-1, keepdims=True) + RMS_EPS) * g


def _dot(a, b):
    return jnp.dot(a, b, preferred_element_type=F32)


def _dot_nt(a, b):
    return lax.dot_general(a, b, (((1,), (1,)), ((), ())), preferred_element_type=F32)


def _params(*sem):
    return pltpu.CompilerParams(dimension_semantics=sem, vmem_limit_bytes=VMEM_LIMIT_BYTES)


def _pos_columns(coarse, fine, lane):
    is_c = (lane >= POS_LANE) & (lane < POS_LANE + N_SPLIT)
    is_f = (lane >= POS_LANE + N_SPLIT) & (lane < POS_LANE + 2 * N_SPLIT)
    return jnp.where(is_c, coarse, jnp.where(is_f, fine, 0.0))


def _mixer_a_kernel(x_ref, g_ref, win_ref, conv_ref, wout_ref, o_ref, ext_ref, *, tm, d):
    @pl.when(pl.program_id(1) == 0)
    def _():
        ext_ref[0:8, :] = jnp.zeros((8, d), F32)

    x = x_ref[0]
    xb = _rms(x, g_ref[...]).astype(BF16)
    b_gate = _dot(xb, win_ref[:, 0:d])
    cv = _dot(xb, win_ref[:, d:2 * d]) * _dot(xb, win_ref[:, 2 * d:3 * d])
    ext_ref[8:8 + tm, :] = cv
    cv1 = ext_ref[7:7 + tm, :]
    cv2 = ext_ref[6:6 + tm, :]
    u = conv_ref[0:1, :] * cv2 + conv_ref[1:2, :] * cv1 + conv_ref[2:3, :] * cv
    ext_ref[0:8, :] = cv[tm - 8:tm, :]
    o_ref[0] = x + _dot((b_gate * u).astype(BF16), wout_ref[...])


def _mixer_a(x, g, w_in, conv_w, w_out, tm=1024):
    b, s, d = x.shape
    assert s % tm == 0
    return pl.pallas_call(
        functools.partial(_mixer_a_kernel, tm=tm, d=d),
        grid=(b, s // tm),
        in_specs=[
            pl.BlockSpec((1, tm, d), lambda i, j: (i, j, 0)),
            _resident((1, d)), _resident((d, 3 * d)), _resident((CONV_WIDTH, d)), _resident((d, d)),
        ],
        out_specs=pl.BlockSpec((1, tm, d), lambda i, j: (i, j, 0)),
        out_shape=jax.ShapeDtypeStruct((b, s, d), F32),
        scratch_shapes=[pltpu.VMEM((tm + 8, d), F32)],
        compiler_params=_params("arbitrary", "arbitrary"),
        name="mixer_a",
    )(x, g.reshape(1, d), w_in, conv_w, w_out)


def _ffn_kernel(*refs, with_proj, with_final, f, tf):
    if with_proj:
        h_ref, o_ref_in, wo_ref, g_ref, wgu_ref, wd_ref, fg_ref, out_ref = refs
    else:
        h_ref, g_ref, wgu_ref, wd_ref, fg_ref, out_ref = refs
    h = h_ref[...]
    if with_proj:
        h = h + _dot(o_ref_in[...], wo_ref[...])
    xb = _rms(h, g_ref[...]).astype(BF16)
    y = h
    for c in range(f // tf):
        gate = _dot(xb, wgu_ref[:, c * tf:(c + 1) * tf])
        up = _dot(xb, wgu_ref[:, f + c * tf:f + (c + 1) * tf])
        act = (gate * jax.nn.sigmoid(gate)) * up
        y = y + _dot(act.astype(BF16), wd_ref[c * tf:(c + 1) * tf, :])
    if with_final:
        y = _rms(y, fg_ref[...])
    out_ref[...] = y


def _resident(shape):
    return pl.BlockSpec(shape, lambda *_: (0,) * len(shape), pipeline_mode=pl.Buffered(1))


def _ffn(h, g, w_gu, w_down, final_g, o=None, w_o=None, with_final=False, tm=1024, tf=256):
    t, d = h.shape
    f = w_down.shape[0]
    assert t % tm == 0 and f % tf == 0
    with_proj = o is not None
    in_specs = [pl.BlockSpec((tm, d), lambda i: (i, 0))]
    args = [h]
    if with_proj:
        in_specs += [pl.BlockSpec((tm, o.shape[1]), lambda i: (i, 0)), _resident(w_o.shape)]
        args += [o, w_o]
    in_specs += [_resident((1, d)), _resident(w_gu.shape), _resident(w_down.shape), _resident((1, d))]
    args += [g.reshape(1, d), w_gu, w_down, final_g.reshape(1, d)]
    return pl.pallas_call(
        functools.partial(_ffn_kernel, with_proj=with_proj, with_final=with_final, f=f, tf=tf),
        grid=(t // tm,),
        in_specs=in_specs,
        out_specs=pl.BlockSpec((tm, d), lambda i: (i, 0)),
        out_shape=jax.ShapeDtypeStruct((t, d), F32),
        compiler_params=_params("arbitrary"),
        name="ffn_proj" if with_proj else "ffn",
    )(*args)


def _halves(x):
    return x, pltpu.roll(x, HEAD_DIM, axis=1)


def _proj_kernel(h_ref, gkv_ref, gq_ref, wkv_ref, wqg_ref, qc_ref,
                 kvc_ref, kas_ref, vts_ref, kaw_ref, vtw_ref, q_ref, gate_ref, rel_ref, *, tm):
    g, dh = N_KV_GROUPS, HEAD_DIM
    gw = g * dh
    s0 = pl.program_id(1) * tm
    x = h_ref[0]
    y = x * lax.rsqrt(jnp.mean(x * x, axis=-1, keepdims=True) + RMS_EPS)
    kv = _dot((y * gkv_ref[...]).astype(BF16), wkv_ref[...])
    low_m = lax.broadcasted_iota(jnp.int32, (tm // D_CMP, LANES), 1) < dh
    for st in range(2):
        for pair in range(g // 2):
            slab = rel_ref.at[st * (g // 2) + pair]
            slab[...] = kv[:, st * gw + pair * LANES:st * gw + (pair + 1) * LANES]
            for j in range(D_CMP // 2):
                a = slab[pl.ds(2 * j, tm // D_CMP, stride=D_CMP), :]
                b = slab[pl.ds(2 * j + 1, tm // D_CMP, stride=D_CMP), :]
                cols = slice(j * LANES, (j + 1) * LANES)
                kvc_ref[st, 0, 2 * pair, :, cols] = jnp.where(low_m, a, pltpu.roll(b, dh, axis=1)).astype(BF16)
                kvc_ref[st, 0, 2 * pair + 1, :, cols] = jnp.where(low_m, pltpu.roll(a, dh, axis=1), b).astype(BF16)

    kpos = s0 + lax.broadcasted_iota(jnp.int32, (tm, LANES), 0)
    lane = lax.broadcasted_iota(jnp.int32, (tm, LANES), 1)
    low = lane < dh
    kblk = jnp.right_shift(kpos, L_SLC.bit_length() - 1)
    posc = _pos_columns(kblk.astype(F32), jnp.bitwise_and(kpos, L_SLC - 1).astype(F32), lane)
    onehot = (lane == kblk).astype(BF16)
    ones_rows = (lax.broadcasted_iota(jnp.int32, (VT_ROWS - dh, tm), 0) == 0).astype(BF16)
    for pair in range(g // 2):
        for st, k_ref, vt_ref in ((2, kas_ref, vts_ref), (4, kaw_ref, vtw_ref)):
            k_pair = kv[:, st * gw + pair * LANES:st * gw + (pair + 1) * LANES]
            vt = kv[:, (st + 1) * gw + pair * LANES:(st + 1) * gw + (pair + 1) * LANES].T
            for half, kh in enumerate(_halves(k_pair)):
                grp = 2 * pair + half
                k_ref[0, grp, :, 0:LANES] = jnp.where(low, kh, posc).astype(BF16)
                if st == 2:
                    k_ref[0, grp, :, LANES:2 * LANES] = onehot
                vt_ref[0, grp, 0:dh] = vt[half * dh:(half + 1) * dh].astype(BF16)
                vt_ref[0, grp, dh:VT_ROWS] = ones_rows

    qg = _dot((y * gq_ref[...]).astype(BF16), wqg_ref[...])
    for pair in range(N_HEADS // 2):
        q_pair = qg[:, pair * LANES:(pair + 1) * LANES] * (LOG2E * dh ** -0.5)
        for half, qh in enumerate(_halves(q_pair)):
            hd = 2 * pair + half
            q_ref[0, hd] = jnp.where(low, qh, qc_ref[hd:hd + 1, :]).astype(BF16)
    nq = N_HEADS * dh
    for grp in range(g):
        gate_ref[0, grp] = jax.nn.sigmoid(qg[:, nq + grp * LANES: nq + (grp + 1) * LANES]).T[0:GATE_ROWS]


def _proj(h, g_kv, g_q, w_kv, w_qg, qconst, tm=1024):
    b, s, d = h.shape
    assert s % tm == 0 and s // L_SLC <= MAX_SLC_BLOCKS
    g, dh = N_KV_GROUPS, HEAD_DIM
    return pl.pallas_call(
        functools.partial(_proj_kernel, tm=tm),
        grid=(b, s // tm),
        in_specs=[
            pl.BlockSpec((1, tm, d), lambda i, j: (i, j, 0)),
            _resident((1, d)), _resident((1, d)), _resident(w_kv.shape), _resident(w_qg.shape),
            _resident(qconst.shape),
        ],
        out_specs=[
            pl.BlockSpec((2, 1, g, tm // D_CMP, D_CMP * dh), lambda i, j: (0, i, 0, j, 0)),
            pl.BlockSpec((1, g, tm, 2 * LANES), lambda i, j: (i, 0, j, 0)),
            pl.BlockSpec((1, g, VT_ROWS, tm), lambda i, j: (i, 0, 0, j)),
            pl.BlockSpec((1, g, tm, LANES), lambda i, j: (i, 0, j, 0)),
            pl.BlockSpec((1, g, VT_ROWS, tm), lambda i, j: (i, 0, 0, j)),
            pl.BlockSpec((1, N_HEADS, tm, LANES), lambda i, j: (i, 0, j, 0)),
            pl.BlockSpec((1, g, GATE_ROWS, tm), lambda i, j: (i, 0, 0, j)),
        ],
        out_shape=[
            jax.ShapeDtypeStruct((2, b, g, s // D_CMP, D_CMP * dh), BF16),
            jax.ShapeDtypeStruct((b, g, s, 2 * LANES), BF16),
            jax.ShapeDtypeStruct((b, g, VT_ROWS, s), BF16),
            jax.ShapeDtypeStruct((b, g, s, LANES), BF16),
            jax.ShapeDtypeStruct((b, g, VT_ROWS, s), BF16),
            jax.ShapeDtypeStruct((b, N_HEADS, s, LANES), BF16),
            jax.ShapeDtypeStruct((b, g, GATE_ROWS, s), F32),
        ],
        scratch_shapes=[pltpu.VMEM((2 * (g // 2), tm, LANES), F32)],
        compiler_params=_params("arbitrary", "arbitrary"),
        name="proj",
    )(h, g_kv.reshape(1, d), g_q.reshape(1, d), w_kv, w_qg, qconst)


def _compress_kernel(r_ref, pe_ref, w1_ref, w2_ref, kca_ref, vct_ref, *, m):
    half = D_CMP * HEAD_DIM
    row = lax.broadcasted_iota(jnp.int32, (m, LANES), 0)
    lane = lax.broadcasted_iota(jnp.int32, (m, LANES), 1)

    def mlp(i):
        r = r_ref[i, 0]
        top = _dot(r, w1_ref[i, 0:half, :])
        bot = _dot(r, w1_ref[i, half:2 * half, :])
        pe_term = _dot(pe_ref[i], w1_ref[i])[0:1, :]
        hid = top + pltpu.roll(bot, m - 1, axis=0) + pe_term
        hid = hid * jax.nn.sigmoid(hid)
        out = _dot(hid.astype(BF16), w2_ref[i])
        return jnp.where(row < m - 1, out, 0.0)

    coarse = jnp.right_shift(row, 2).astype(F32)
    fine = jnp.bitwise_and(row, 3).astype(F32) * D_CMP + (L_CMP - 1) / 2.0
    kca_ref[0] = (mlp(0) + _pos_columns(coarse, fine, lane)).astype(BF16)
    vct_ref[0] = jnp.where(lane == HEAD_DIM, 1.0, mlp(1)).T[0:VT_ROWS].astype(BF16)


def _compress(kvc, pe2, w1_2, w2_2):
    _, b, g, m, _ = kvc.shape
    dh = HEAD_DIM
    r = kvc.reshape(2, b * g, m, D_CMP * dh)
    hid = w1_2.shape[-1]
    return pl.pallas_call(
        functools.partial(_compress_kernel, m=m),
        grid=(b * g,),
        in_specs=[
            pl.BlockSpec((2, 1, m, D_CMP * dh), lambda j: (0, j, 0, 0)),
            pl.BlockSpec((2, 8, L_CMP * dh), lambda j: (0, 0, 0)),
            pl.BlockSpec((2, L_CMP * dh, hid), lambda j: (0, 0, 0)),
            pl.BlockSpec((2, hid, LANES), lambda j: (0, 0, 0)),
        ],
        out_specs=[
            pl.BlockSpec((1, m, LANES), lambda j: (j, 0, 0)),
            pl.BlockSpec((1, VT_ROWS, m), lambda j: (j, 0, 0)),
        ],
        out_shape=[
            jax.ShapeDtypeStruct((b * g, m, LANES), BF16),
            jax.ShapeDtypeStruct((b * g, VT_ROWS, m), BF16),
        ],
        compiler_params=_params("arbitrary"),
        name="compress",
    )(r, pe2, w1_2, w2_2)


def _select_rows(pred, a, b):
    a32, b32 = pltpu.bitcast(a, jnp.int32), pltpu.bitcast(b, jnp.int32)
    return pltpu.bitcast(jnp.where(pred, a32, b32), BF16)


def _head_stack(ot, gt, branch, tq):
    return jnp.concatenate(
        [ot[:, h * tq:(h + 1) * tq] * gt[3 * h + branch:3 * h + branch + 1, :] for h in range(HEADS_PER_GROUP)],
        axis=0)


def _nsa_cmp_kernel(q_ref, kca_ref, vct_ref, gate_ref, oc_ref, negm_ref, used_ref, *, tq, m):
    q0 = pl.program_id(2) * tq
    hpg = HEADS_PER_GROUP
    ncol = hpg * tq
    q = q_ref[0].reshape(ncol, LANES)
    tcol = q0 + jnp.bitwise_and(lax.broadcasted_iota(jnp.int32, (1, ncol), 1), tq - 1)

    def attend(rows):
        st = _dot_nt(kca_ref[0, 0:rows, :], q)
        band = max(rows - row_tile - 8, 0)
        nrow = band + lax.broadcasted_iota(jnp.int32, (rows - band, 1), 0)
        low = jnp.where((nrow * D_CMP + (L_CMP - 1)) <= tcol, st[band:], NEG_INF)
        st = low if band == 0 else jnp.concatenate([st[0:band], low], axis=0)
        e = jnp.exp2(st - jnp.max(st, axis=0, keepdims=True)).astype(BF16)
        acc = _dot(vct_ref[0, :, 0:rows], e)
        rcp = jnp.where(tcol >= L_CMP - 1, 1.0 / acc[HEAD_DIM:HEAD_DIM + 1], 0.0)
        ot = acc[0:HEAD_DIM] * rcp
        nblk = min(MAX_SLC_BLOCKS, rows * D_CMP // L_SLC)
        jj = lax.broadcasted_iota(jnp.int32, (nblk, rows), 0) * L_SLC
        nn = lax.broadcasted_iota(jnp.int32, (nblk, rows), 1) * D_CMP
        overlap_t = ((nn < jj + L_SLC) & (nn + L_CMP > jj)).astype(BF16)
        imp_h = _dot(overlap_t, e) * rcp
        imp = imp_h[:, 0:tq]
        for h in range(1, hpg):
            imp = imp + imp_h[:, h * tq:(h + 1) * tq]

        jrow = lax.broadcasted_iota(jnp.int32, (nblk, tq), 0)
        cur = jnp.right_shift(q0 + lax.broadcasted_iota(jnp.int32, (1, tq), 1), L_SLC.bit_length() - 1)
        valid = jrow <= cur
        forced = (jrow == 0) | (jrow == cur) | (jrow == cur - 1)
        val = jnp.where(valid & jnp.logical_not(forced), imp, NEG_INF)
        jrow_f = jrow.astype(F32)
        for _ in range(N_SEL - N_FORCED):
            mx = jnp.max(val, axis=0, keepdims=True)
            first = jnp.min(jnp.where(val == mx, jrow_f, float(MAX_SLC_BLOCKS)), axis=0, keepdims=True)
            val = jnp.where(jrow_f == first, PICKED, val)
        sel = jnp.where(valid & (forced | (val == PICKED)), 1.0, 0.0)
        if nblk < MAX_SLC_BLOCKS:
            sel = jnp.concatenate([sel, jnp.zeros((MAX_SLC_BLOCKS - nblk, tq), F32)], axis=0)
        gt = gate_ref[0, 0]
        oc_ref[0] = _head_stack(ot, gt, 0, tq).T
        sel_t = sel.T
        negm_ref[0, 0] = jnp.where(sel_t > 0.5, 0.0, NEG_INF).astype(BF16)
        for r in range(tq // SUPER_KEYS):
            used_ref[r] = jnp.max(sel_t[r * SUPER_KEYS:(r + 1) * SUPER_KEYS], axis=0, keepdims=True)

    row_tile = min(m, CMP_ROW_TILE)
    n_variants = m // row_tile
    assert n_variants == 1 or row_tile % (tq // D_CMP) == 0
    if n_variants == 1:
        attend(m)
    else:
        needed = pl.program_id(2) * (tq // D_CMP) + (tq // D_CMP - 1)
        variant = jnp.minimum(jnp.right_shift(needed - 1, row_tile.bit_length() - 1), n_variants - 1)
        for v in range(n_variants):
            pl.when(variant == v)(functools.partial(attend, (v + 1) * row_tile))


def _nsa_cmp(q, kca, vct, gates, tq=1024):
    b, nh, s, _ = q.shape
    g, hpg, dh = N_KV_GROUPS, HEADS_PER_GROUP, HEAD_DIM
    m = kca.shape[1]
    assert s % tq == 0 and tq % SUPER_KEYS == 0 and tq & (tq - 1) == 0
    nqt = s // tq
    sub = tq // SUPER_KEYS
    return pl.pallas_call(
        functools.partial(_nsa_cmp_kernel, tq=tq, m=m),
        grid=(b, g, s // tq),
        in_specs=[
            pl.BlockSpec((1, hpg, tq, LANES), lambda i, j, k: (i, j, k, 0)),
            pl.BlockSpec((1, m, LANES), lambda i, j, k: (i * N_KV_GROUPS + j, 0, 0)),
            pl.BlockSpec((1, VT_ROWS, m), lambda i, j, k: (i * N_KV_GROUPS + j, 0, 0)),
            pl.BlockSpec((1, 1, GATE_ROWS, tq), lambda i, j, k: (i, j, 0, k)),
        ],
        out_specs=[
            pl.BlockSpec((1, tq, hpg * dh), lambda i, j, k: (i, k, j)),
            pl.BlockSpec((1, 1, tq, MAX_SLC_BLOCKS), lambda i, j, k: (i, j, k, 0)),
            pl.BlockSpec((sub, 1, MAX_SLC_BLOCKS), lambda i, j, k: ((i * N_KV_GROUPS + j) * nqt + k, 0, 0)),
        ],
        out_shape=[
            jax.ShapeDtypeStruct((b, s, nh * dh), F32),
            jax.ShapeDtypeStruct((b, g, s, MAX_SLC_BLOCKS), BF16),
            jax.ShapeDtypeStruct((b * g * nqt * sub, 1, MAX_SLC_BLOCKS), F32),
        ],
        compiler_params=_params("arbitrary", "arbitrary", "arbitrary"),
        name="nsa_cmp",
    )(q, kca, vct, gates)


def _super_lists_kernel(used_ref, lst_ref, cnt_ref, *, tb):
    ut = used_ref[...].T.astype(BF16)
    a_i = lax.broadcasted_iota(jnp.int32, (MAX_SLC_BLOCKS, MAX_SLC_BLOCKS), 0)
    b_i = lax.broadcasted_iota(jnp.int32, (MAX_SLC_BLOCKS, MAX_SLC_BLOCKS), 1)
    pair_t = (a_i == jnp.right_shift(b_i, SUPER_SHIFT)).astype(BF16)
    sbf = jnp.where(_dot(pair_t, ut) > 0.5, 1.0, 0.0).astype(BF16)
    incl = _dot((b_i <= a_i).astype(BF16), sbf)[0:MAX_SUPER]
    cnt_ref[...] = jnp.broadcast_to(incl[MAX_SUPER - 1:MAX_SUPER, :], (8, tb)).astype(jnp.int32)
    for i in range(MAX_SUPER):
        lst_ref[i:i + 1, :] = jnp.sum(jnp.where(incl <= float(i), 1.0, 0.0), axis=0, keepdims=True).astype(jnp.int32)


def _super_lists(used):
    t = used.shape[0]
    tb = min(t, LANES)
    assert t % tb == 0
    return pl.pallas_call(
        functools.partial(_super_lists_kernel, tb=tb),
        grid=(t // tb,),
        in_specs=[pl.BlockSpec((tb, MAX_SLC_BLOCKS), lambda i: (i, 0))],
        out_specs=[
            pl.BlockSpec((MAX_SUPER, tb), lambda i: (0, i)),
            pl.BlockSpec((8, tb), lambda i: (0, i)),
        ],
        out_shape=[
            jax.ShapeDtypeStruct((MAX_SUPER, t), jnp.int32),
            jax.ShapeDtypeStruct((8, t), jnp.int32),
        ],
        compiler_params=_params("arbitrary"),
        name="super_lists",
    )(used)


def _nsa_slc_kernel(lst_ref, cnt_ref, q_ref, kas_ref, vts_ref, kaw_ref, vtw_ref, negm_ref, gate_ref, oc_ref, o_ref,
                    m_ref, acc_ref, *, tq, sub):
    hpg = HEADS_PER_GROUP
    ncol = hpg * tq
    n_win = WINDOW // SUPER_KEYS + 1
    pad_k = (lax.broadcasted_iota(jnp.int32, (SUPER_KEYS, 2 * LANES), 1) == PAD_LANE).astype(BF16)
    pad_w = (lax.broadcasted_iota(jnp.int32, (SUPER_KEYS, LANES), 1) == PAD_LANE).astype(BF16)
    n_tiles = pl.num_programs(0) * pl.num_programs(1) * pl.num_programs(2) * sub
    tile0 = ((pl.program_id(0) * pl.num_programs(1) + pl.program_id(1)) * pl.num_programs(2) + pl.program_id(2)) * sub
    row = lax.broadcasted_iota(jnp.int32, (SUPER_KEYS, 1), 0)

    class Tile:
        def __init__(self, s, slots):
            self.s = s
            self.slots = slots
            self.rows = slice(s * tq, (s + 1) * tq)
            self.qt = pl.program_id(2) * sub + s
            self.q = q_ref[0, :, self.rows, :].reshape(ncol, LANES)
            negm = negm_ref[0, 0, self.rows, :]
            self.qa = jnp.concatenate([self.q, jnp.concatenate([negm] * hpg, axis=0)], axis=1)
            self.tcol = self.qt * tq + jnp.bitwise_and(lax.broadcasted_iota(jnp.int32, (1, ncol), 1), tq - 1)
            self.tile = tile0 + s
            self.n_other = cnt_ref[self.tile] - 1
            self.n_early = jnp.right_shift(jnp.maximum(self.n_other - slots, 0) + CHUNK_SUPERS - 1,
                                           CHUNK_SUPERS.bit_length() - 1)
            m_ref[s] = jnp.full((1, ncol), NEG_INF, F32)
            acc_ref[s] = jnp.zeros((VT_ROWS, ncol), F32)

        def super_block(self, slot, ok=None):
            sb = lst_ref[slot * n_tiles + self.tile]
            if ok is not None:
                sb = jnp.where(ok, sb, 0)
            start = pl.multiple_of(sb * SUPER_KEYS, SUPER_KEYS)
            return kas_ref[0, 0, pl.ds(start, SUPER_KEYS), :], vts_ref[0, 0, :, pl.ds(start, SUPER_KEYS)]

        def update(self, st, vt):
            m_old = m_ref[self.s]
            m_new = jnp.maximum(m_old, jnp.max(st, axis=0, keepdims=True))
            p = jnp.exp2(st - m_new)
            acc_ref[self.s] = jnp.exp2(m_old - m_new) * acc_ref[self.s] + _dot(vt, p.astype(BF16))
            m_ref[self.s] = m_new

        def early_chunks(self):
            def body(c, carry):
                parts = [self.super_block(c * CHUNK_SUPERS + j) for j in range(CHUNK_SUPERS)]
                kt = jnp.concatenate([k for k, _ in parts], axis=0)
                self.update(_dot_nt(kt, self.qa), jnp.concatenate([v for _, v in parts], axis=1))
                return carry

            lax.fori_loop(0, self.n_early, body, 0)

        def final_scores(self):
            dstart = pl.multiple_of(self.qt * SUPER_KEYS, SUPER_KEYS)
            sd = _dot_nt(kas_ref[0, 0, pl.ds(dstart, SUPER_KEYS), :], self.qa)
            sd = jnp.where(self.qt * SUPER_KEYS + row <= self.tcol, sd, NEG_INF)
            ks, vs = [], [vts_ref[0, 0, :, pl.ds(dstart, SUPER_KEYS)]]
            for j in range(self.slots):
                slot = self.n_early * CHUNK_SUPERS + j
                ok = slot < self.n_other
                k, v = self.super_block(jnp.minimum(slot, MAX_SUPER - 1), ok)
                ks.append(_select_rows(ok, k, pad_k))
                vs.append(v)
            self.st = jnp.concatenate([sd, _dot_nt(jnp.concatenate(ks, axis=0), self.qa)], axis=0)
            self.vt = jnp.concatenate(vs, axis=1)
            win_scores, win_v = [], []
            for r in range(n_win):
                sb = self.qt - (n_win - 1) + r
                start = pl.multiple_of(jnp.maximum(sb, 0) * SUPER_KEYS, SUPER_KEYS)
                kw = kaw_ref[0, 0, pl.ds(start, SUPER_KEYS), :]
                if r < n_win - 1:
                    kw = _select_rows(sb >= 0, kw, pad_w)
                sw = _dot_nt(kw, self.q)
                if r == n_win - 1:
                    sw = jnp.where(sb * SUPER_KEYS + row <= self.tcol, sw, NEG_INF)
                elif r == 0:
                    sw = jnp.where(sb * SUPER_KEYS + WINDOW + row > self.tcol, sw, NEG_INF)
                win_scores.append(sw)
                win_v.append(vtw_ref[0, 0, :, pl.ds(start, SUPER_KEYS)])
            self.stw = jnp.concatenate(win_scores, axis=0)
            self.vw = jnp.concatenate(win_v, axis=1)

        def finish(self):
            self.update(self.st, self.vt)
            acc = acc_ref[self.s]
            o_slc_t = acc[0:HEAD_DIM] * (1.0 / acc[HEAD_DIM:HEAD_DIM + 1])
            e = jnp.exp2(self.stw - jnp.max(self.stw, axis=0, keepdims=True))
            acc_w = _dot(self.vw, e.astype(BF16))
            o_win_t = acc_w[0:HEAD_DIM] * (1.0 / acc_w[HEAD_DIM:HEAD_DIM + 1])
            gt = gate_ref[0, 0, :, self.rows]
            comb = _head_stack(o_slc_t, gt, 1, tq) + _head_stack(o_win_t, gt, 2, tq)
            o_ref[0, self.rows, :] = (oc_ref[0, self.rows, :] + comb.T).astype(BF16)

    def run(slots):
        tiles = [Tile(s, slots) for s in range(sub)]
        for t in tiles:
            t.early_chunks()
        for t in tiles:
            t.final_scores()
        for t in tiles:
            t.finish()

    longest = cnt_ref[tile0] - 1
    for s in range(1, sub):
        longest = jnp.maximum(longest, cnt_ref[tile0 + s] - 1)
    pl.when(longest <= SHORT_SLOTS)(functools.partial(run, SHORT_SLOTS))
    pl.when(longest > SHORT_SLOTS)(functools.partial(run, FINAL_SLOTS))


def _nsa_slc(lists, counts, q, kas, vts, kaw, vtw, negm, gates, oc, tq=128, sub=8):
    b, nh, s, _ = q.shape
    g, hpg, dh = N_KV_GROUPS, HEADS_PER_GROUP, HEAD_DIM
    ts = tq * sub
    assert tq == SUPER_KEYS and WINDOW % tq == 0 and s >= WINDOW + tq and s % ts == 0
    grid_spec = pltpu.PrefetchScalarGridSpec(
        num_scalar_prefetch=2,
        grid=(b, g, s // ts),
        in_specs=[
            pl.BlockSpec((1, hpg, ts, LANES), lambda i, j, k, *_: (i, j, k, 0)),
            pl.BlockSpec((1, 1, s, 2 * LANES), lambda i, j, k, *_: (i, j, 0, 0)),
            pl.BlockSpec((1, 1, VT_ROWS, s), lambda i, j, k, *_: (i, j, 0, 0)),
            pl.BlockSpec((1, 1, s, LANES), lambda i, j, k, *_: (i, j, 0, 0)),
            pl.BlockSpec((1, 1, VT_ROWS, s), lambda i, j, k, *_: (i, j, 0, 0)),
            pl.BlockSpec((1, 1, ts, MAX_SLC_BLOCKS), lambda i, j, k, *_: (i, j, k, 0)),
            pl.BlockSpec((1, 1, GATE_ROWS, ts), lambda i, j, k, *_: (i, j, 0, k)),
            pl.BlockSpec((1, ts, hpg * dh), lambda i, j, k, *_: (i, k, j)),
        ],
        out_specs=pl.BlockSpec((1, ts, hpg * dh), lambda i, j, k, *_: (i, k, j)),
        scratch_shapes=[
            pltpu.VMEM((sub, 1, hpg * tq), F32),
            pltpu.VMEM((sub, VT_ROWS, hpg * tq), F32),
        ],
    )
    return pl.pallas_call(
        functools.partial(_nsa_slc_kernel, tq=tq, sub=sub),
        grid_spec=grid_spec,
        out_shape=jax.ShapeDtypeStruct((b, s, nh * dh), BF16),
        compiler_params=_params("arbitrary", "arbitrary", "arbitrary"),
        name="nsa_slc",
    )(lists, counts, q, kas, vts, kaw, vtw, negm, gates, oc)


def _split_bf16(x, n):
    pieces = []
    for _ in range(n):
        p = x.astype(BF16).astype(F32)
        pieces.append(p)
        x = x - p
    return pieces


def _q_constants():
    slopes = jnp.exp2(-8.0 * jnp.arange(1, N_HEADS + 1, dtype=F32) / N_HEADS) * LOG2E
    pieces = jnp.stack(_split_bf16(slopes, N_SPLIT), axis=1)
    qc = jnp.zeros((N_HEADS, LANES), F32)
    qc = qc.at[:, POS_LANE:POS_LANE + N_SPLIT].set(pieces * L_SLC)
    qc = qc.at[:, POS_LANE + N_SPLIT:POS_LANE + 2 * N_SPLIT].set(pieces)
    return qc.at[:, PAD_LANE].set(NEG_INF)


def _qg_weight(w_qg):
    d = w_qg.shape[0]
    nq = N_HEADS * HEAD_DIM
    wg = jnp.pad(w_qg[:, nq:].reshape(d, N_KV_GROUPS, 3 * HEADS_PER_GROUP),
                 ((0, 0), (0, 0), (0, LANES - 3 * HEADS_PER_GROUP)))
    return jnp.concatenate([w_qg[:, :nq], wg.reshape(d, N_KV_GROUPS * LANES)], axis=1).astype(BF16)


def kernel(x, a_norm, a_w_in, a_conv, a_w_out, kv_norm, w_kv, cmp_pe_k, cmp_w1_k, cmp_w2_k, cmp_pe_v, cmp_w1_v,
           cmp_w2_v, b_norm, b_w_qg, b_w_o, f_norm, f_w_gu, f_w_down, final_norm):
    b, s, d = x.shape
    nq = N_HEADS * HEAD_DIM
    h = _mixer_a(x, a_norm[0], a_w_in[0].astype(BF16), a_conv[0], a_w_out[0].astype(BF16))
    h = _ffn(h.reshape(b * s, d), f_norm[0], f_w_gu[0].astype(BF16), f_w_down[0].astype(BF16), final_norm)
    h = h.reshape(b, s, d)

    kvc, kas, vts, kaw, vtw, q, gates = _proj(h, kv_norm, b_norm[0], w_kv.astype(BF16), _qg_weight(b_w_qg[0]),
                                              _q_constants())
    pe2 = jnp.stack([cmp_pe_k, cmp_pe_v]).reshape(2, 1, L_CMP * HEAD_DIM)
    pe2 = jnp.broadcast_to(pe2, (2, 8, L_CMP * HEAD_DIM)).astype(BF16)
    w2 = jnp.pad(jnp.stack([cmp_w2_k, cmp_w2_v]), ((0, 0), (0, 0), (0, LANES - HEAD_DIM))).astype(BF16)
    kca, vct = _compress(kvc, pe2, jnp.stack([cmp_w1_k, cmp_w1_v]).astype(BF16), w2)

    oc, negm, used = _nsa_cmp(q, kca, vct, gates)
    lists, counts = _super_lists(used.reshape(used.shape[0], MAX_SLC_BLOCKS))
    o = _nsa_slc(lists.reshape(-1), counts[0], q, kas, vts, kaw, vtw, negm, gates, oc)

    out = _ffn(h.reshape(b * s, d), f_norm[1], f_w_gu[1].astype(BF16), f_w_down[1].astype(BF16), final_norm,
               o=o.reshape(b * s, nq), w_o=b_w_o[0].astype(BF16), with_final=True)
    return out.reshape(b, s, d)
```

```python
import functools
import math

import jax
import jax.numpy as jnp
from jax import lax
from jax.experimental import pallas as pl
from jax.experimental.pallas import tpu as pltpu

F32 = jnp.float32
BF16 = jnp.bfloat16

N_HEADS = 16
N_KV_GROUPS = 4
HEADS_PER_GROUP = N_HEADS // N_KV_GROUPS
HEAD_DIM = 64
L_CMP = 32
D_CMP = 16
L_SLC = 64
N_SEL = 16
N_FORCED = 3
WINDOW = 512
CONV_WIDTH = 3
RMS_EPS = 1e-5
NEG_INF = -1e30
PICKED = -3.0e38
LOG2E = math.log2(math.e)
LANES = 128
MAX_SLC_BLOCKS = LANES
SUPER_KEYS = LANES
SUPER_SHIFT = (SUPER_KEYS // L_SLC).bit_length() - 1
MAX_SUPER = MAX_SLC_BLOCKS * L_SLC // SUPER_KEYS
CHUNK_SUPERS = 4
FINAL_SLOTS = 10
SHORT_SLOTS = 8
CMP_ROW_TILE = 64
VT_ROWS = HEAD_DIM + 16
POS_LANE = HEAD_DIM
N_SPLIT = 3
PAD_LANE = POS_LANE + 2 * N_SPLIT
VMEM_LIMIT_BYTES = 60 * 1024 * 1024


def _rms(x, g):
    return x * lax.rsqrt(jnp.mean(x * x, axis=-1, keepdims=True) + RMS_EPS) * g


def _dot(a, b):
    return jnp.dot(a, b, preferred_element_type=F32)


def _dot_nt(a, b):
    return lax.dot_general(a, b, (((1,), (1,)), ((), ())), preferred_element_type=F32)


def _params(*sem):
    return pltpu.CompilerParams(dimension_semantics=sem, vmem_limit_bytes=VMEM_LIMIT_BYTES)


def _pos_columns(coarse, fine, lane):
    is_c = (lane >= POS_LANE) & (lane < POS_LANE + N_SPLIT)
    is_f = (lane >= POS_LANE + N_SPLIT) & (lane < POS_LANE + 2 * N_SPLIT)
    return jnp.where(is_c, coarse, jnp.where(is_f, fine, 0.0))


def _mixer_a_kernel(x_ref, g_ref, win_ref, conv_ref, wout_ref, o_ref, ext_ref, *, tm, d):
    @pl.when(pl.program_id(1) == 0)
    def _():
        ext_ref[0:8, :] = jnp.zeros((8, d), F32)

    x = x_ref[0]
    xb = _rms(x, g_ref[...]).astype(BF16)
    b_gate = _dot(xb, win_ref[:, 0:d])
    cv = _dot(xb, win_ref[:, d:2 * d]) * _dot(xb, win_ref[:, 2 * d:3 * d])
    ext_ref[8:8 + tm, :] = cv
    cv1 = ext_ref[7:7 + tm, :]
    cv2 = ext_ref[6:6 + tm, :]
    u = conv_ref[0:1, :] * cv2 + conv_ref[1:2, :] * cv1 + conv_ref[2:3, :] * cv
    ext_ref[0:8, :] = cv[tm - 8:tm, :]
    o_ref[0] = x + _dot((b_gate * u).astype(BF16), wout_ref[...])


def _mixer_a(x, g, w_in, conv_w, w_out, tm=1024):
    b, s, d = x.shape
    assert s % tm == 0
    return pl.pallas_call(
        functools.partial(_mixer_a_kernel, tm=tm, d=d),
        grid=(b, s // tm),
        in_specs=[
            pl.BlockSpec((1, tm, d), lambda i, j: (i, j, 0)),
            _resident((1, d)), _resident((d, 3 * d)), _resident((CONV_WIDTH, d)), _resident((d, d)),
        ],
        out_specs=pl.BlockSpec((1, tm, d), lambda i, j: (i, j, 0)),
        out_shape=jax.ShapeDtypeStruct((b, s, d), F32),
        scratch_shapes=[pltpu.VMEM((tm + 8, d), F32)],
        compiler_params=_params("arbitrary", "arbitrary"),
        name="mixer_a",
    )(x, g.reshape(1, d), w_in, conv_w, w_out)


def _ffn_kernel(*refs, with_proj, with_final, f, tf):
    if with_proj:
        h_ref, o_ref_in, wo_ref, g_ref, wgu_ref, wd_ref, fg_ref, out_ref = refs
    else:
        h_ref, g_ref, wgu_ref, wd_ref, fg_ref, out_ref = refs
    h = h_ref[...]
    if with_proj:
        h = h + _dot(o_ref_in[...], wo_ref[...])
    xb = _rms(h, g_ref[...]).astype(BF16)
    y = h
    for c in range(f // tf):
        gate = _dot(xb, wgu_ref[:, c * tf:(c + 1) * tf])
        up = _dot(xb, wgu_ref[:, f + c * tf:f + (c + 1) * tf])
        act = (gate * jax.nn.sigmoid(gate)) * up
        y = y + _dot(act.astype(BF16), wd_ref[c * tf:(c + 1) * tf, :])
    if with_final:
        y = _rms(y, fg_ref[...])
    out_ref[...] = y


def _resident(shape):
    return pl.BlockSpec(shape, lambda *_: (0,) * len(shape), pipeline_mode=pl.Buffered(1))


def _ffn(h, g, w_gu, w_down, final_g, o=None, w_o=None, with_final=False, tm=1024, tf=256):
    t, d = h.shape
    f = w_down.shape[0]
    assert t % tm == 0 and f % tf == 0
    with_proj = o is not None
    in_specs = [pl.BlockSpec((tm, d), lambda i: (i, 0))]
    args = [h]
    if with_proj:
        in_specs += [pl.BlockSpec((tm, o.shape[1]), lambda i: (i, 0)), _resident(w_o.shape)]
        args += [o, w_o]
    in_specs += [_resident((1, d)), _resident(w_gu.shape), _resident(w_down.shape), _resident((1, d))]
    args += [g.reshape(1, d), w_gu, w_down, final_g.reshape(1, d)]
    return pl.pallas_call(
        functools.partial(_ffn_kernel, with_proj=with_proj, with_final=with_final, f=f, tf=tf),
        grid=(t // tm,),
        in_specs=in_specs,
        out_specs=pl.BlockSpec((tm, d), lambda i: (i, 0)),
        out_shape=jax.ShapeDtypeStruct((t, d), F32),
        compiler_params=_params("arbitrary"),
        name="ffn_proj" if with_proj else "ffn",
    )(*args)


def _halves(x):
    return x, pltpu.roll(x, HEAD_DIM, axis=1)


def _proj_kernel(h_ref, gkv_ref, gq_ref, wkv_ref, wqg_ref, qc_ref,
                 kvc_ref, kas_ref, vts_ref, kaw_ref, vtw_ref, q_ref, gate_ref, rel_ref, *, tm):
    g, dh = N_KV_GROUPS, HEAD_DIM
    gw = g * dh
    s0 = pl.program_id(1) * tm
    x = h_ref[0]
    y = x * lax.rsqrt(jnp.mean(x * x, axis=-1, keepdims=True) + RMS_EPS)
    kv = _dot((y * gkv_ref[...]).astype(BF16), wkv_ref[...])
    low_m = lax.broadcasted_iota(jnp.int32, (tm // D_CMP, LANES), 1) < dh
    for st in range(2):
        for pair in range(g // 2):
            slab = rel_ref.at[st * (g // 2) + pair]
            slab[...] = kv[:, st * gw + pair * LANES:st * gw + (pair + 1) * LANES]
            for j in range(D_CMP // 2):
                a = slab[pl.ds(2 * j, tm // D_CMP, stride=D_CMP), :]
                b = slab[pl.ds(2 * j + 1, tm // D_CMP, stride=D_CMP), :]
                cols = slice(j * LANES, (j + 1) * LANES)
                kvc_ref[st, 0, 2 * pair, :, cols] = jnp.where(low_m, a, pltpu.roll(b, dh, axis=1)).astype(BF16)
                kvc_ref[st, 0, 2 * pair + 1, :, cols] = jnp.where(low_m, pltpu.roll(a, dh, axis=1), b).astype(BF16)

    kpos = s0 + lax.broadcasted_iota(jnp.int32, (tm, LANES), 0)
    lane = lax.broadcasted_iota(jnp.int32, (tm, LANES), 1)
    low = lane < dh
    kblk = jnp.right_shift(kpos, L_SLC.bit_length() - 1)
    posc = _pos_columns(kblk.astype(F32), jnp.bitwise_and(kpos, L_SLC - 1).astype(F32), lane)
    onehot = (lane == kblk).astype(BF16)
    ones_rows = (lax.broadcasted_iota(jnp.int32, (VT_ROWS - dh, tm), 0) == 0).astype(BF16)
    for pair in range(g // 2):
        for st, k_ref, vt_ref in ((2, kas_ref, vts_ref), (4, kaw_ref, vtw_ref)):
            k_pair = kv[:, st * gw + pair * LANES:st * gw + (pair + 1) * LANES]
            vt = kv[:, (st + 1) * gw + pair * LANES:(st + 1) * gw + (pair + 1) * LANES].T
            for half, kh in enumerate(_halves(k_pair)):
                grp = 2 * pair + half
                k_ref[0, grp, :, 0:LANES] = jnp.where(low, kh, posc).astype(BF16)
                if st == 2:
                    k_ref[0, grp, :, LANES:2 * LANES] = onehot
                vt_ref[0, grp, 0:dh] = vt[half * dh:(half + 1) * dh].astype(BF16)
                vt_ref[0, grp, dh:VT_ROWS] = ones_rows

    qg = _dot((y * gq_ref[...]).astype(BF16), wqg_ref[...])
    for pair in range(N_HEADS // 2):
        q_pair = qg[:, pair * LANES:(pair + 1) * LANES] * (LOG2E * dh ** -0.5)
        for half, qh in enumerate(_halves(q_pair)):
            hd = 2 * pair + half
            q_ref[0, hd] = jnp.where(low, qh, qc_ref[hd:hd + 1, :]).astype(BF16)
    nq = N_HEADS * dh
    for grp in range(g):
        gate_ref[0, grp] = jax.nn.sigmoid(qg[:, nq + grp * LANES: nq + (grp + 1) * LANES])


def _proj(h, g_kv, g_q, w_kv, w_qg, qconst, tm=1024):
    b, s, d = h.shape
    assert s % tm == 0 and s // L_SLC <= MAX_SLC_BLOCKS
    g, dh = N_KV_GROUPS, HEAD_DIM
    return pl.pallas_call(
        functools.partial(_proj_kernel, tm=tm),
        grid=(b, s // tm),
        in_specs=[
            pl.BlockSpec((1, tm, d), lambda i, j: (i, j, 0)),
            _resident((1, d)), _resident((1, d)), _resident(w_kv.shape), _resident(w_qg.shape),
            _resident(qconst.shape),
        ],
        out_specs=[
            pl.BlockSpec((2, 1, g, tm // D_CMP, D_CMP * dh), lambda i, j: (0, i, 0, j, 0)),
            pl.BlockSpec((1, g, tm, 2 * LANES), lambda i, j: (i, 0, j, 0)),
            pl.BlockSpec((1, g, VT_ROWS, tm), lambda i, j: (i, 0, 0, j)),
            pl.BlockSpec((1, g, tm, LANES), lambda i, j: (i, 0, j, 0)),
            pl.BlockSpec((1, g, VT_ROWS, tm), lambda i, j: (i, 0, 0, j)),
            pl.BlockSpec((1, N_HEADS, tm, LANES), lambda i, j: (i, 0, j, 0)),
            pl.BlockSpec((1, g, tm, LANES), lambda i, j: (i, 0, j, 0)),
        ],
        out_shape=[
            jax.ShapeDtypeStruct((2, b, g, s // D_CMP, D_CMP * dh), BF16),
            jax.ShapeDtypeStruct((b, g, s, 2 * LANES), BF16),
            jax.ShapeDtypeStruct((b, g, VT_ROWS, s), BF16),
            jax.ShapeDtypeStruct((b, g, s, LANES), BF16),
            jax.ShapeDtypeStruct((b, g, VT_ROWS, s), BF16),
            jax.ShapeDtypeStruct((b, N_HEADS, s, LANES), BF16),
            jax.ShapeDtypeStruct((b, g, s, LANES), F32),
        ],
        scratch_shapes=[pltpu.VMEM((2 * (g // 2), tm, LANES), F32)],
        compiler_params=_params("arbitrary", "arbitrary"),
        name="proj",
    )(h, g_kv.reshape(1, d), g_q.reshape(1, d), w_kv, w_qg, qconst)


def _compress_kernel(r_ref, pe_ref, w1_ref, w2_ref, kca_ref, vct_ref, *, m):
    half = D_CMP * HEAD_DIM
    row = lax.broadcasted_iota(jnp.int32, (m, LANES), 0)
    lane = lax.broadcasted_iota(jnp.int32, (m, LANES), 1)

    def mlp(i):
        r = r_ref[i, 0]
        top = _dot(r, w1_ref[i, 0:half, :])
        bot = _dot(r, w1_ref[i, half:2 * half, :])
        pe_term = _dot(pe_ref[i], w1_ref[i])[0:1, :]
        hid = top + pltpu.roll(bot, m - 1, axis=0) + pe_term
        hid = hid * jax.nn.sigmoid(hid)
        out = _dot(hid.astype(BF16), w2_ref[i])
        return jnp.where(row < m - 1, out, 0.0)

    coarse = jnp.right_shift(row, 2).astype(F32)
    fine = jnp.bitwise_and(row, 3).astype(F32) * D_CMP + (L_CMP - 1) / 2.0
    kca_ref[0] = (mlp(0) + _pos_columns(coarse, fine, lane)).astype(BF16)
    vct_ref[0] = jnp.where(lane == HEAD_DIM, 1.0, mlp(1)).T[0:VT_ROWS].astype(BF16)


def _compress(kvc, pe2, w1_2, w2_2):
    _, b, g, m, _ = kvc.shape
    dh = HEAD_DIM
    r = kvc.reshape(2, b * g, m, D_CMP * dh)
    hid = w1_2.shape[-1]
    return pl.pallas_call(
        functools.partial(_compress_kernel, m=m),
        grid=(b * g,),
        in_specs=[
            pl.BlockSpec((2, 1, m, D_CMP * dh), lambda j: (0, j, 0, 0)),
            pl.BlockSpec((2, 8, L_CMP * dh), lambda j: (0, 0, 0)),
            pl.BlockSpec((2, L_CMP * dh, hid), lambda j: (0, 0, 0)),
            pl.BlockSpec((2, hid, LANES), lambda j: (0, 0, 0)),
        ],
        out_specs=[
            pl.BlockSpec((1, m, LANES), lambda j: (j, 0, 0)),
            pl.BlockSpec((1, VT_ROWS, m), lambda j: (j, 0, 0)),
        ],
        out_shape=[
            jax.ShapeDtypeStruct((b * g, m, LANES), BF16),
            jax.ShapeDtypeStruct((b * g, VT_ROWS, m), BF16),
        ],
        compiler_params=_params("arbitrary"),
        name="compress",
    )(r, pe2, w1_2, w2_2)


def _select_rows(pred, a, b):
    a32, b32 = pltpu.bitcast(a, jnp.int32), pltpu.bitcast(b, jnp.int32)
    return pltpu.bitcast(jnp.where(pred, a32, b32), BF16)


def _head_stack(ot, gt, branch, tq):
    return jnp.concatenate(
        [ot[:, h * tq:(h + 1) * tq] * gt[3 * h + branch:3 * h + branch + 1, :] for h in range(HEADS_PER_GROUP)],
        axis=0)


def _nsa_cmp_kernel(q_ref, kca_ref, vct_ref, gate_ref, oc_ref, negm_ref, used_ref, *, tq, m):
    q0 = pl.program_id(2) * tq
    hpg = HEADS_PER_GROUP
    ncol = hpg * tq
    q = q_ref[0].reshape(ncol, LANES)
    tcol = q0 + jnp.bitwise_and(lax.broadcasted_iota(jnp.int32, (1, ncol), 1), tq - 1)

    def attend(rows):
        st = _dot_nt(kca_ref[0, 0:rows, :], q)
        band = max(rows - row_tile - 8, 0)
        nrow = band + lax.broadcasted_iota(jnp.int32, (rows - band, 1), 0)
        low = jnp.where((nrow * D_CMP + (L_CMP - 1)) <= tcol, st[band:], NEG_INF)
        st = low if band == 0 else jnp.concatenate([st[0:band], low], axis=0)
        e = jnp.exp2(st - jnp.max(st, axis=0, keepdims=True)).astype(BF16)
        acc = _dot(vct_ref[0, :, 0:rows], e)
        rcp = jnp.where(tcol >= L_CMP - 1, 1.0 / acc[HEAD_DIM:HEAD_DIM + 1], 0.0)
        ot = acc[0:HEAD_DIM] * rcp
        nblk = min(MAX_SLC_BLOCKS, rows * D_CMP // L_SLC)
        jj = lax.broadcasted_iota(jnp.int32, (nblk, rows), 0) * L_SLC
        nn = lax.broadcasted_iota(jnp.int32, (nblk, rows), 1) * D_CMP
        overlap_t = ((nn < jj + L_SLC) & (nn + L_CMP > jj)).astype(BF16)
        imp_h = _dot(overlap_t, e) * rcp
        imp = imp_h[:, 0:tq]
        for h in range(1, hpg):
            imp = imp + imp_h[:, h * tq:(h + 1) * tq]

        jrow = lax.broadcasted_iota(jnp.int32, (nblk, tq), 0)
        cur = jnp.right_shift(q0 + lax.broadcasted_iota(jnp.int32, (1, tq), 1), L_SLC.bit_length() - 1)
        valid = jrow <= cur
        forced = (jrow == 0) | (jrow == cur) | (jrow == cur - 1)
        val = jnp.where(valid & jnp.logical_not(forced), imp, NEG_INF)
        for _ in range(N_SEL - N_FORCED):
            first = jnp.argmax(val, axis=0, keepdims=True)
            val = jnp.where(jrow == first, PICKED, val)
        sel = jnp.where(valid & (forced | (val == PICKED)), 1.0, 0.0)
        if nblk < MAX_SLC_BLOCKS:
            sel = jnp.concatenate([sel, jnp.zeros((MAX_SLC_BLOCKS - nblk, tq), F32)], axis=0)
        gt = gate_ref[0, 0].T
        oc_ref[0] = _head_stack(ot, gt, 0, tq).T
        sel_t = sel.T
        negm_ref[0, 0] = jnp.where(sel_t > 0.5, 0.0, NEG_INF).astype(BF16)
        for r in range(tq // SUPER_KEYS):
            used_ref[r] = jnp.max(sel_t[r * SUPER_KEYS:(r + 1) * SUPER_KEYS], axis=0, keepdims=True)

    row_tile = min(m, CMP_ROW_TILE)
    n_variants = m // row_tile
    assert n_variants == 1 or row_tile % (tq // D_CMP) == 0
    if n_variants == 1:
        attend(m)
    else:
        needed = pl.program_id(2) * (tq // D_CMP) + (tq // D_CMP - 1)
        variant = jnp.minimum(jnp.right_shift(needed - 1, row_tile.bit_length() - 1), n_variants - 1)
        for v in range(n_variants):
            pl.when(variant == v)(functools.partial(attend, (v + 1) * row_tile))


def _nsa_cmp(q, kca, vct, gates, tq=1024):
    b, nh, s, _ = q.shape
    g, hpg, dh = N_KV_GROUPS, HEADS_PER_GROUP, HEAD_DIM
    m = kca.shape[1]
    assert s % tq == 0 and tq % SUPER_KEYS == 0 and tq & (tq - 1) == 0
    nqt = s // tq
    sub = tq // SUPER_KEYS
    return pl.pallas_call(
        functools.partial(_nsa_cmp_kernel, tq=tq, m=m),
        grid=(b, g, s // tq),
        in_specs=[
            pl.BlockSpec((1, hpg, tq, LANES), lambda i, j, k: (i, j, k, 0)),
            pl.BlockSpec((1, m, LANES), lambda i, j, k: (i * N_KV_GROUPS + j, 0, 0)),
            pl.BlockSpec((1, VT_ROWS, m), lambda i, j, k: (i * N_KV_GROUPS + j, 0, 0)),
            pl.BlockSpec((1, 1, tq, LANES), lambda i, j, k: (i, j, k, 0)),
        ],
        out_specs=[
            pl.BlockSpec((1, tq, hpg * dh), lambda i, j, k: (i, k, j)),
            pl.BlockSpec((1, 1, tq, MAX_SLC_BLOCKS), lambda i, j, k: (i, j, k, 0)),
            pl.BlockSpec((sub, 1, MAX_SLC_BLOCKS), lambda i, j, k: ((i * N_KV_GROUPS + j) * nqt + k, 0, 0)),
        ],
        out_shape=[
            jax.ShapeDtypeStruct((b, s, nh * dh), F32),
            jax.ShapeDtypeStruct((b, g, s, MAX_SLC_BLOCKS), BF16),
            jax.ShapeDtypeStruct((b * g * nqt * sub, 1, MAX_SLC_BLOCKS), F32),
        ],
        compiler_params=_params("arbitrary", "arbitrary", "arbitrary"),
        name="nsa_cmp",
    )(q, kca, vct, gates)


def _super_lists_kernel(used_ref, lst_ref, cnt_ref, *, tb):
    ut = used_ref[...].T.astype(BF16)
    a_i = lax.broadcasted_iota(jnp.int32, (MAX_SLC_BLOCKS, MAX_SLC_BLOCKS), 0)
    b_i = lax.broadcasted_iota(jnp.int32, (MAX_SLC_BLOCKS, MAX_SLC_BLOCKS), 1)
    pair_t = (a_i == jnp.right_shift(b_i, SUPER_SHIFT)).astype(BF16)
    sbf = jnp.where(_dot(pair_t, ut) > 0.5, 1.0, 0.0).astype(BF16)
    incl = _dot((b_i <= a_i).astype(BF16), sbf)[0:MAX_SUPER]
    cnt_ref[...] = jnp.broadcast_to(incl[MAX_SUPER - 1:MAX_SUPER, :], (8, tb)).astype(jnp.int32)
    for i in range(MAX_SUPER):
        lst_ref[i:i + 1, :] = jnp.sum(jnp.where(incl <= float(i), 1.0, 0.0), axis=0, keepdims=True).astype(jnp.int32)


def _super_lists(used):
    t = used.shape[0]
    tb = min(t, LANES)
    assert t % tb == 0
    return pl.pallas_call(
        functools.partial(_super_lists_kernel, tb=tb),
        grid=(t // tb,),
        in_specs=[pl.BlockSpec((tb, MAX_SLC_BLOCKS), lambda i: (i, 0))],
        out_specs=[
            pl.BlockSpec((MAX_SUPER, tb), lambda i: (0, i)),
            pl.BlockSpec((8, tb), lambda i: (0, i)),
        ],
        out_shape=[
            jax.ShapeDtypeStruct((MAX_SUPER, t), jnp.int32),
            jax.ShapeDtypeStruct((8, t), jnp.int32),
        ],
        compiler_params=_params("arbitrary"),
        name="super_lists",
    )(used)


def _nsa_slc_kernel(lst_ref, cnt_ref, q_ref, kas_ref, vts_ref, kaw_ref, vtw_ref, negm_ref, gate_ref, oc_ref, o_ref,
                    m_ref, acc_ref, *, tq, sub):
    hpg = HEADS_PER_GROUP
    ncol = hpg * tq
    n_win = WINDOW // SUPER_KEYS + 1
    pad_k = (lax.broadcasted_iota(jnp.int32, (SUPER_KEYS, 2 * LANES), 1) == PAD_LANE).astype(BF16)
    pad_w = (lax.broadcasted_iota(jnp.int32, (SUPER_KEYS, LANES), 1) == PAD_LANE).astype(BF16)
    n_tiles = pl.num_programs(0) * pl.num_programs(1) * pl.num_programs(2) * sub
    tile0 = ((pl.program_id(0) * pl.num_programs(1) + pl.program_id(1)) * pl.num_programs(2) + pl.program_id(2)) * sub
    row = lax.broadcasted_iota(jnp.int32, (SUPER_KEYS, 1), 0)

    class Tile:
        def __init__(self, s, slots):
            self.s = s
            self.slots = slots
            self.rows = slice(s * tq, (s + 1) * tq)
            self.qt = pl.program_id(2) * sub + s
            self.q = q_ref[0, :, self.rows, :].reshape(ncol, LANES)
            negm = negm_ref[0, 0, self.rows, :]
            self.qa = jnp.concatenate([self.q, jnp.concatenate([negm] * hpg, axis=0)], axis=1)
            self.tcol = self.qt * tq + jnp.bitwise_and(lax.broadcasted_iota(jnp.int32, (1, ncol), 1), tq - 1)
            self.tile = tile0 + s
            self.n_other = cnt_ref[self.tile] - 1
            self.n_early = jnp.right_shift(jnp.maximum(self.n_other - slots, 0) + CHUNK_SUPERS - 1,
                                           CHUNK_SUPERS.bit_length() - 1)
            m_ref[s] = jnp.full((1, ncol), NEG_INF, F32)
            acc_ref[s] = jnp.zeros((VT_ROWS, ncol), F32)

        def super_block(self, slot, ok=None):
            sb = lst_ref[slot * n_tiles + self.tile]
            if ok is not None:
                sb = jnp.where(ok, sb, 0)
            start = pl.multiple_of(sb * SUPER_KEYS, SUPER_KEYS)
            return kas_ref[0, 0, pl.ds(start, SUPER_KEYS), :], vts_ref[0, 0, :, pl.ds(start, SUPER_KEYS)]

        def update(self, st, vt):
            m_old = m_ref[self.s]
            m_new = jnp.maximum(m_old, jnp.max(st, axis=0, keepdims=True))
            p = jnp.exp2(st - m_new)
            acc_ref[self.s] = jnp.exp2(m_old - m_new) * acc_ref[self.s] + _dot(vt, p.astype(BF16))
            m_ref[self.s] = m_new

        def early_chunks(self):
            def body(c, carry):
                parts = [self.super_block(c * CHUNK_SUPERS + j) for j in range(CHUNK_SUPERS)]
                kt = jnp.concatenate([k for k, _ in parts], axis=0)
                self.update(_dot_nt(kt, self.qa), jnp.concatenate([v for _, v in parts], axis=1))
                return carry

            lax.fori_loop(0, self.n_early, body, 0)

        def final_scores(self):
            dstart = pl.multiple_of(self.qt * SUPER_KEYS, SUPER_KEYS)
            sd = _dot_nt(kas_ref[0, 0, pl.ds(dstart, SUPER_KEYS), :], self.qa)
            sd = jnp.where(self.qt * SUPER_KEYS + row <= self.tcol, sd, NEG_INF)
            ks, vs = [], [vts_ref[0, 0, :, pl.ds(dstart, SUPER_KEYS)]]
            for j in range(self.slots):
                slot = self.n_early * CHUNK_SUPERS + j
                ok = slot < self.n_other
                k, v = self.super_block(jnp.minimum(slot, MAX_SUPER - 1), ok)
                ks.append(_select_rows(ok, k, pad_k))
                vs.append(v)
            self.st = jnp.concatenate([sd, _dot_nt(jnp.concatenate(ks, axis=0), self.qa)], axis=0)
            self.vt = jnp.concatenate(vs, axis=1)
            win_scores, win_v = [], []
            for r in range(n_win):
                sb = self.qt - (n_win - 1) + r
                start = pl.multiple_of(jnp.maximum(sb, 0) * SUPER_KEYS, SUPER_KEYS)
                kw = kaw_ref[0, 0, pl.ds(start, SUPER_KEYS), :]
                if r < n_win - 1:
                    kw = _select_rows(sb >= 0, kw, pad_w)
                sw = _dot_nt(kw, self.q)
                if r == n_win - 1:
                    sw = jnp.where(sb * SUPER_KEYS + row <= self.tcol, sw, NEG_INF)
                elif r == 0:
                    sw = jnp.where(sb * SUPER_KEYS + WINDOW + row > self.tcol, sw, NEG_INF)
                win_scores.append(sw)
                win_v.append(vtw_ref[0, 0, :, pl.ds(start, SUPER_KEYS)])
            self.stw = jnp.concatenate(win_scores, axis=0)
            self.vw = jnp.concatenate(win_v, axis=1)

        def finish(self):
            self.update(self.st, self.vt)
            acc = acc_ref[self.s]
            o_slc_t = acc[0:HEAD_DIM] * (1.0 / acc[HEAD_DIM:HEAD_DIM + 1])
            e = jnp.exp2(self.stw - jnp.max(self.stw, axis=0, keepdims=True))
            acc_w = _dot(self.vw, e.astype(BF16))
            o_win_t = acc_w[0:HEAD_DIM] * (1.0 / acc_w[HEAD_DIM:HEAD_DIM + 1])
            gt = gate_ref[0, 0, self.rows, :].T
            comb = _head_stack(o_slc_t, gt, 1, tq) + _head_stack(o_win_t, gt, 2, tq)
            o_ref[0, self.rows, :] = (oc_ref[0, self.rows, :] + comb.T).astype(BF16)

    def run(slots):
        tiles = [Tile(s, slots) for s in range(sub)]
        for t in tiles:
            t.early_chunks()
        for t in tiles:
            t.final_scores()
        for t in tiles:
            t.finish()

    longest = cnt_ref[tile0] - 1
    for s in range(1, sub):
        longest = jnp.maximum(longest, cnt_ref[tile0 + s] - 1)
    pl.when(longest <= SHORT_SLOTS)(functools.partial(run, SHORT_SLOTS))
    pl.when(longest > SHORT_SLOTS)(functools.partial(run, FINAL_SLOTS))


def _nsa_slc(lists, counts, q, kas, vts, kaw, vtw, negm, gates, oc, tq=128, sub=8):
    b, nh, s, _ = q.shape
    g, hpg, dh = N_KV_GROUPS, HEADS_PER_GROUP, HEAD_DIM
    ts = tq * sub
    assert tq == SUPER_KEYS and WINDOW % tq == 0 and s >= WINDOW + tq and s % ts == 0
    grid_spec = pltpu.PrefetchScalarGridSpec(
        num_scalar_prefetch=2,
        grid=(b, g, s // ts),
        in_specs=[
            pl.BlockSpec((1, hpg, ts, LANES), lambda i, j, k, *_: (i, j, k, 0)),
            pl.BlockSpec((1, 1, s, 2 * LANES), lambda i, j, k, *_: (i, j, 0, 0)),
            pl.BlockSpec((1, 1, VT_ROWS, s), lambda i, j, k, *_: (i, j, 0, 0)),
            pl.BlockSpec((1, 1, s, LANES), lambda i, j, k, *_: (i, j, 0, 0)),
            pl.BlockSpec((1, 1, VT_ROWS, s), lambda i, j, k, *_: (i, j, 0, 0)),
            pl.BlockSpec((1, 1, ts, MAX_SLC_BLOCKS), lambda i, j, k, *_: (i, j, k, 0)),
            pl.BlockSpec((1, 1, ts, LANES), lambda i, j, k, *_: (i, j, k, 0)),
            pl.BlockSpec((1, ts, hpg * dh), lambda i, j, k, *_: (i, k, j)),
        ],
        out_specs=pl.BlockSpec((1, ts, hpg * dh), lambda i, j, k, *_: (i, k, j)),
        scratch_shapes=[
            pltpu.VMEM((sub, 1, hpg * tq), F32),
            pltpu.VMEM((sub, VT_ROWS, hpg * tq), F32),
        ],
    )
    return pl.pallas_call(
        functools.partial(_nsa_slc_kernel, tq=tq, sub=sub),
        grid_spec=grid_spec,
        out_shape=jax.ShapeDtypeStruct((b, s, nh * dh), BF16),
        compiler_params=_params("arbitrary", "arbitrary", "arbitrary"),
        name="nsa_slc",
    )(lists, counts, q, kas, vts, kaw, vtw, negm, gates, oc)


def _split_bf16(x, n):
    pieces = []
    for _ in range(n):
        p = x.astype(BF16).astype(F32)
        pieces.append(p)
        x = x - p
    return pieces


def _q_constants():
    slopes = jnp.exp2(-8.0 * jnp.arange(1, N_HEADS + 1, dtype=F32) / N_HEADS) * LOG2E
    pieces = jnp.stack(_split_bf16(slopes, N_SPLIT), axis=1)
    qc = jnp.zeros((N_HEADS, LANES), F32)
    qc = qc.at[:, POS_LANE:POS_LANE + N_SPLIT].set(pieces * L_SLC)
    qc = qc.at[:, POS_LANE + N_SPLIT:POS_LANE + 2 * N_SPLIT].set(pieces)
    return qc.at[:, PAD_LANE].set(NEG_INF)


def _qg_weight(w_qg):
    d = w_qg.shape[0]
    nq = N_HEADS * HEAD_DIM
    wg = jnp.pad(w_qg[:, nq:].reshape(d, N_KV_GROUPS, 3 * HEADS_PER_GROUP),
                 ((0, 0), (0, 0), (0, LANES - 3 * HEADS_PER_GROUP)))
    return jnp.concatenate([w_qg[:, :nq], wg.reshape(d, N_KV_GROUPS * LANES)], axis=1).astype(BF16)


def kernel(x, a_norm, a_w_in, a_conv, a_w_out, kv_norm, w_kv, cmp_pe_k, cmp_w1_k, cmp_w2_k, cmp_pe_v, cmp_w1_v,
           cmp_w2_v, b_norm, b_w_qg, b_w_o, f_norm, f_w_gu, f_w_down, final_norm):
    b, s, d = x.shape
    nq = N_HEADS * HEAD_DIM
    h = _mixer_a(x, a_norm[0], a_w_in[0].astype(BF16), a_conv[0], a_w_out[0].astype(BF16))
    h = _ffn(h.reshape(b * s, d), f_norm[0], f_w_gu[0].astype(BF16), f_w_down[0].astype(BF16), final_norm)
    h = h.reshape(b, s, d)

    kvc, kas, vts, kaw, vtw, q, gates = _proj(h, kv_norm, b_norm[0], w_kv.astype(BF16), _qg_weight(b_w_qg[0]),
                                              _q_constants())
    pe2 = jnp.stack([cmp_pe_k, cmp_pe_v]).reshape(2, 1, L_CMP * HEAD_DIM)
    pe2 = jnp.broadcast_to(pe2, (2, 8, L_CMP * HEAD_DIM)).astype(BF16)
    w2 = jnp.pad(jnp.stack([cmp_w2_k, cmp_w2_v]), ((0, 0), (0, 0), (0, LANES - HEAD_DIM))).astype(BF16)
    kca, vct = _compress(kvc, pe2, jnp.stack([cmp_w1_k, cmp_w1_v]).astype(BF16), w2)

    oc, negm, used = _nsa_cmp(q, kca, vct, gates)
    lists, counts = _super_lists(used.reshape(used.shape[0], MAX_SLC_BLOCKS))
    o = _nsa_slc(lists.reshape(-1), counts[0], q, kas, vts, kaw, vtw, negm, gates, oc)

    out = _ffn(h.reshape(b * s, d), f_norm[1], f_w_gu[1].astype(BF16), f_w_down[1].astype(BF16), final_norm,
               o=o.reshape(b * s, nq), w_o=b_w_o[0].astype(BF16), with_final=True)
    return out.reshape(b, s, d)
```

```python
import functools
import math

import jax
import jax.numpy as jnp
from jax import lax
from jax.experimental import pallas as pl
from jax.experimental.pallas import tpu as pltpu

F32 = jnp.float32
BF16 = jnp.bfloat16

N_HEADS = 16
N_KV_GROUPS = 4
HEADS_PER_GROUP = N_HEADS // N_KV_GROUPS
HEAD_DIM = 64
L_CMP = 32
D_CMP = 16
L_SLC = 64
N_SEL = 16
N_FORCED = 3
WINDOW = 512
CONV_WIDTH = 3
RMS_EPS = 1e-5
NEG_INF = -1e30
PICKED = -3.0e38
LOG2E = math.log2(math.e)
LANES = 128
MAX_SLC_BLOCKS = LANES
SUPER_KEYS = LANES
SUPER_SHIFT = (SUPER_KEYS // L_SLC).bit_length() - 1
MAX_SUPER = MAX_SLC_BLOCKS * L_SLC // SUPER_KEYS
CHUNK_SUPERS = 4
FINAL_SLOTS = 10
SHORT_SLOTS = 8
CMP_ROW_TILE = 64
VT_ROWS = HEAD_DIM + 16
POS_LANE = HEAD_DIM
N_SPLIT = 3
PAD_LANE = POS_LANE + 2 * N_SPLIT
VMEM_LIMIT_BYTES = 60 * 1024 * 1024


def _rms(x, g):
    return x * lax.rsqrt(jnp.mean(x * x, axis=-1, keepdims=True) + RMS_EPS) * g


def _dot(a, b):
    return jnp.dot(a, b, preferred_element_type=F32)


def _dot_nt(a, b):
    return lax.dot_general(a, b, (((1,), (1,)), ((), ())), preferred_element_type=F32)


def _params(*sem):
    return pltpu.CompilerParams(dimension_semantics=sem, vmem_limit_bytes=VMEM_LIMIT_BYTES)


def _pos_columns(coarse, fine, lane):
    is_c = (lane >= POS_LANE) & (lane < POS_LANE + N_SPLIT)
    is_f = (lane >= POS_LANE + N_SPLIT) & (lane < POS_LANE + 2 * N_SPLIT)
    return jnp.where(is_c, coarse, jnp.where(is_f, fine, 0.0))


def _mixer_a_kernel(x_ref, g_ref, win_ref, conv_ref, wout_ref, o_ref, ext_ref, *, tm, d):
    @pl.when(pl.program_id(1) == 0)
    def _():
        ext_ref[0:8, :] = jnp.zeros((8, d), F32)

    x = x_ref[0]
    xb = _rms(x, g_ref[...]).astype(BF16)
    b_gate = _dot(xb, win_ref[:, 0:d])
    cv = _dot(xb, win_ref[:, d:2 * d]) * _dot(xb, win_ref[:, 2 * d:3 * d])
    ext_ref[8:8 + tm, :] = cv
    cv1 = ext_ref[7:7 + tm, :]
    cv2 = ext_ref[6:6 + tm, :]
    u = conv_ref[0:1, :] * cv2 + conv_ref[1:2, :] * cv1 + conv_ref[2:3, :] * cv
    ext_ref[0:8, :] = cv[tm - 8:tm, :]
    o_ref[0] = x + _dot((b_gate * u).astype(BF16), wout_ref[...])


def _mixer_a(x, g, w_in, conv_w, w_out, tm=1024):
    b, s, d = x.shape
    assert s % tm == 0
    return pl.pallas_call(
        functools.partial(_mixer_a_kernel, tm=tm, d=d),
        grid=(b, s // tm),
        in_specs=[
            pl.BlockSpec((1, tm, d), lambda i, j: (i, j, 0)),
            _resident((1, d)), _resident((d, 3 * d)), _resident((CONV_WIDTH, d)), _resident((d, d)),
        ],
        out_specs=pl.BlockSpec((1, tm, d), lambda i, j: (i, j, 0)),
        out_shape=jax.ShapeDtypeStruct((b, s, d), F32),
        scratch_shapes=[pltpu.VMEM((tm + 8, d), F32)],
        compiler_params=_params("arbitrary", "arbitrary"),
        name="mixer_a",
    )(x, g.reshape(1, d), w_in, conv_w, w_out)


def _ffn_kernel(*refs, with_proj, with_final, f, tf):
    if with_proj:
        h_ref, o_ref_in, wo_ref, g_ref, wgu_ref, wd_ref, fg_ref, out_ref = refs
    else:
        h_ref, g_ref, wgu_ref, wd_ref, fg_ref, out_ref = refs
    h = h_ref[...]
    if with_proj:
        h = h + _dot(o_ref_in[...], wo_ref[...])
    xb = _rms(h, g_ref[...]).astype(BF16)
    y = h
    for c in range(f // tf):
        gate = _dot(xb, wgu_ref[:, c * tf:(c + 1) * tf])
        up = _dot(xb, wgu_ref[:, f + c * tf:f + (c + 1) * tf])
        act = (gate * jax.nn.sigmoid(gate)) * up
        y = y + _dot(act.astype(BF16), wd_ref[c * tf:(c + 1) * tf, :])
    if with_final:
        y = _rms(y, fg_ref[...])
    out_ref[...] = y


def _resident(shape):
    return pl.BlockSpec(shape, lambda *_: (0,) * len(shape), pipeline_mode=pl.Buffered(1))


def _ffn(h, g, w_gu, w_down, final_g, o=None, w_o=None, with_final=False, tm=1024, tf=256):
    t, d = h.shape
    f = w_down.shape[0]
    assert t % tm == 0 and f % tf == 0
    with_proj = o is not None
    in_specs = [pl.BlockSpec((tm, d), lambda i: (i, 0))]
    args = [h]
    if with_proj:
        in_specs += [pl.BlockSpec((tm, o.shape[1]), lambda i: (i, 0)), _resident(w_o.shape)]
        args += [o, w_o]
    in_specs += [_resident((1, d)), _resident(w_gu.shape), _resident(w_down.shape), _resident((1, d))]
    args += [g.reshape(1, d), w_gu, w_down, final_g.reshape(1, d)]
    return pl.pallas_call(
        functools.partial(_ffn_kernel, with_proj=with_proj, with_final=with_final, f=f, tf=tf),
        grid=(t // tm,),
        in_specs=in_specs,
        out_specs=pl.BlockSpec((tm, d), lambda i: (i, 0)),
        out_shape=jax.ShapeDtypeStruct((t, d), F32),
        compiler_params=_params("arbitrary"),
        name="ffn_proj" if with_proj else "ffn",
    )(*args)


def _halves(x):
    return x, pltpu.roll(x, HEAD_DIM, axis=1)


def _proj_kernel(h_ref, gkv_ref, gq_ref, wkv_ref, wqg_ref, qc_ref,
                 kvc_ref, kas_ref, vts_ref, kaw_ref, vtw_ref, q_ref, gate_ref, rel_ref, *, tm, rt):
    g, dh = N_KV_GROUPS, HEAD_DIM
    gw = g * dh
    for r0 in range(0, tm, rt):
        rows = slice(r0, r0 + rt)
        x = h_ref[0, rows, :]
        y = x * lax.rsqrt(jnp.mean(x * x, axis=-1, keepdims=True) + RMS_EPS)
        kv = _dot((y * gkv_ref[...]).astype(BF16), wkv_ref[...])
        low_m = lax.broadcasted_iota(jnp.int32, (rt // D_CMP, LANES), 1) < dh
        mrows = slice(r0 // D_CMP, (r0 + rt) // D_CMP)
        for st in range(2):
            for pair in range(g // 2):
                slab = rel_ref.at[st * (g // 2) + pair]
                slab[rows, :] = kv[:, st * gw + pair * LANES:st * gw + (pair + 1) * LANES]
                for j in range(D_CMP // 2):
                    a = slab[pl.ds(r0 + 2 * j, rt // D_CMP, stride=D_CMP), :]
                    b = slab[pl.ds(r0 + 2 * j + 1, rt // D_CMP, stride=D_CMP), :]
                    cols = slice(j * LANES, (j + 1) * LANES)
                    kvc_ref[st, 0, 2 * pair, mrows, cols] = jnp.where(low_m, a, pltpu.roll(b, dh, axis=1)).astype(BF16)
                    kvc_ref[st, 0, 2 * pair + 1, mrows, cols] = jnp.where(low_m, pltpu.roll(a, dh, axis=1),
                                                                           b).astype(BF16)

        kpos = pl.program_id(1) * tm + r0 + lax.broadcasted_iota(jnp.int32, (rt, LANES), 0)
        lane = lax.broadcasted_iota(jnp.int32, (rt, LANES), 1)
        low = lane < dh
        kblk = jnp.right_shift(kpos, L_SLC.bit_length() - 1)
        posc = _pos_columns(kblk.astype(F32), jnp.bitwise_and(kpos, L_SLC - 1).astype(F32), lane)
        onehot = (lane == kblk).astype(BF16)
        ones_rows = (lax.broadcasted_iota(jnp.int32, (VT_ROWS - dh, rt), 0) == 0).astype(BF16)
        for pair in range(g // 2):
            for st, k_ref, vt_ref in ((2, kas_ref, vts_ref), (4, kaw_ref, vtw_ref)):
                k_pair = kv[:, st * gw + pair * LANES:st * gw + (pair + 1) * LANES]
                vt = kv[:, (st + 1) * gw + pair * LANES:(st + 1) * gw + (pair + 1) * LANES].T
                for half, kh in enumerate(_halves(k_pair)):
                    grp = 2 * pair + half
                    k_ref[0, grp, rows, 0:LANES] = jnp.where(low, kh, posc).astype(BF16)
                    if st == 2:
                        k_ref[0, grp, rows, LANES:2 * LANES] = onehot
                    vt_ref[0, grp, 0:dh, rows] = vt[half * dh:(half + 1) * dh].astype(BF16)
                    vt_ref[0, grp, dh:VT_ROWS, rows] = ones_rows

        qg = _dot((y * gq_ref[...]).astype(BF16), wqg_ref[...])
        for pair in range(N_HEADS // 2):
            q_pair = qg[:, pair * LANES:(pair + 1) * LANES] * (LOG2E * dh ** -0.5)
            for half, qh in enumerate(_halves(q_pair)):
                hd = 2 * pair + half
                q_ref[0, hd, rows, :] = jnp.where(low, qh, qc_ref[hd:hd + 1, :]).astype(BF16)
        nq = N_HEADS * dh
        for grp in range(g):
            gate_ref[0, grp, rows, :] = jax.nn.sigmoid(qg[:, nq + grp * LANES: nq + (grp + 1) * LANES])


def _proj(h, g_kv, g_q, w_kv, w_qg, qconst, tm=1024):
    b, s, d = h.shape
    assert s % tm == 0 and s // L_SLC <= MAX_SLC_BLOCKS
    g, dh = N_KV_GROUPS, HEAD_DIM
    return pl.pallas_call(
        functools.partial(_proj_kernel, tm=tm, rt=tm // 4),
        grid=(b, s // tm),
        in_specs=[
            pl.BlockSpec((1, tm, d), lambda i, j: (i, j, 0)),
            _resident((1, d)), _resident((1, d)), _resident(w_kv.shape), _resident(w_qg.shape),
            _resident(qconst.shape),
        ],
        out_specs=[
            pl.BlockSpec((2, 1, g, tm // D_CMP, D_CMP * dh), lambda i, j: (0, i, 0, j, 0)),
            pl.BlockSpec((1, g, tm, 2 * LANES), lambda i, j: (i, 0, j, 0)),
            pl.BlockSpec((1, g, VT_ROWS, tm), lambda i, j: (i, 0, 0, j)),
            pl.BlockSpec((1, g, tm, LANES), lambda i, j: (i, 0, j, 0)),
            pl.BlockSpec((1, g, VT_ROWS, tm), lambda i, j: (i, 0, 0, j)),
            pl.BlockSpec((1, N_HEADS, tm, LANES), lambda i, j: (i, 0, j, 0)),
            pl.BlockSpec((1, g, tm, LANES), lambda i, j: (i, 0, j, 0)),
        ],
        out_shape=[
            jax.ShapeDtypeStruct((2, b, g, s // D_CMP, D_CMP * dh), BF16),
            jax.ShapeDtypeStruct((b, g, s, 2 * LANES), BF16),
            jax.ShapeDtypeStruct((b, g, VT_ROWS, s), BF16),
            jax.ShapeDtypeStruct((b, g, s, LANES), BF16),
            jax.ShapeDtypeStruct((b, g, VT_ROWS, s), BF16),
            jax.ShapeDtypeStruct((b, N_HEADS, s, LANES), BF16),
            jax.ShapeDtypeStruct((b, g, s, LANES), F32),
        ],
        scratch_shapes=[pltpu.VMEM((2 * (g // 2), tm, LANES), F32)],
        compiler_params=_params("arbitrary", "arbitrary"),
        name="proj",
    )(h, g_kv.reshape(1, d), g_q.reshape(1, d), w_kv, w_qg, qconst)


def _compress_kernel(r_ref, pe_ref, w1_ref, w2_ref, kca_ref, vct_ref, *, m):
    half = D_CMP * HEAD_DIM
    row = lax.broadcasted_iota(jnp.int32, (m, LANES), 0)
    lane = lax.broadcasted_iota(jnp.int32, (m, LANES), 1)

    def mlp(i):
        r = r_ref[i, 0]
        top = _dot(r, w1_ref[i, 0:half, :])
        bot = _dot(r, w1_ref[i, half:2 * half, :])
        pe_term = _dot(pe_ref[i], w1_ref[i])[0:1, :]
        hid = top + pltpu.roll(bot, m - 1, axis=0) + pe_term
        hid = hid * jax.nn.sigmoid(hid)
        out = _dot(hid.astype(BF16), w2_ref[i])
        return jnp.where(row < m - 1, out, 0.0)

    coarse = jnp.right_shift(row, 2).astype(F32)
    fine = jnp.bitwise_and(row, 3).astype(F32) * D_CMP + (L_CMP - 1) / 2.0
    kca_ref[0] = (mlp(0) + _pos_columns(coarse, fine, lane)).astype(BF16)
    vct_ref[0] = jnp.where(lane == HEAD_DIM, 1.0, mlp(1)).T[0:VT_ROWS].astype(BF16)


def _compress(kvc, pe2, w1_2, w2_2):
    _, b, g, m, _ = kvc.shape
    dh = HEAD_DIM
    r = kvc.reshape(2, b * g, m, D_CMP * dh)
    hid = w1_2.shape[-1]
    return pl.pallas_call(
        functools.partial(_compress_kernel, m=m),
        grid=(b * g,),
        in_specs=[
            pl.BlockSpec((2, 1, m, D_CMP * dh), lambda j: (0, j, 0, 0)),
            pl.BlockSpec((2, 8, L_CMP * dh), lambda j: (0, 0, 0)),
            pl.BlockSpec((2, L_CMP * dh, hid), lambda j: (0, 0, 0)),
            pl.BlockSpec((2, hid, LANES), lambda j: (0, 0, 0)),
        ],
        out_specs=[
            pl.BlockSpec((1, m, LANES), lambda j: (j, 0, 0)),
            pl.BlockSpec((1, VT_ROWS, m), lambda j: (j, 0, 0)),
        ],
        out_shape=[
            jax.ShapeDtypeStruct((b * g, m, LANES), BF16),
            jax.ShapeDtypeStruct((b * g, VT_ROWS, m), BF16),
        ],
        compiler_params=_params("arbitrary"),
        name="compress",
    )(r, pe2, w1_2, w2_2)


def _select_rows(pred, a, b):
    a32, b32 = pltpu.bitcast(a, jnp.int32), pltpu.bitcast(b, jnp.int32)
    return pltpu.bitcast(jnp.where(pred, a32, b32), BF16)


def _head_stack(ot, gt, branch, tq):
    return jnp.concatenate(
        [ot[:, h * tq:(h + 1) * tq] * gt[3 * h + branch:3 * h + branch + 1, :] for h in range(HEADS_PER_GROUP)],
        axis=0)


def _nsa_cmp_kernel(q_ref, kca_ref, vct_ref, gate_ref, oc_ref, negm_ref, used_ref, *, tq, m):
    q0 = pl.program_id(2) * tq
    hpg = HEADS_PER_GROUP
    ncol = hpg * tq
    q = q_ref[0].reshape(ncol, LANES)
    tcol = q0 + jnp.bitwise_and(lax.broadcasted_iota(jnp.int32, (1, ncol), 1), tq - 1)

    def attend(rows):
        st = _dot_nt(kca_ref[0, 0:rows, :], q)
        band = max(rows - row_tile - 8, 0)
        nrow = band + lax.broadcasted_iota(jnp.int32, (rows - band, 1), 0)
        low = jnp.where((nrow * D_CMP + (L_CMP - 1)) <= tcol, st[band:], NEG_INF)
        st = low if band == 0 else jnp.concatenate([st[0:band], low], axis=0)
        e = jnp.exp2(st - jnp.max(st, axis=0, keepdims=True)).astype(BF16)
        acc = _dot(vct_ref[0, :, 0:rows], e)
        rcp = jnp.where(tcol >= L_CMP - 1, 1.0 / acc[HEAD_DIM:HEAD_DIM + 1], 0.0)
        ot = acc[0:HEAD_DIM] * rcp
        nblk = min(MAX_SLC_BLOCKS, rows * D_CMP // L_SLC)
        jj = lax.broadcasted_iota(jnp.int32, (nblk, rows), 0) * L_SLC
        nn = lax.broadcasted_iota(jnp.int32, (nblk, rows), 1) * D_CMP
        overlap_t = ((nn < jj + L_SLC) & (nn + L_CMP > jj)).astype(BF16)
        imp_h = _dot(overlap_t, e) * rcp
        imp = imp_h[:, 0:tq]
        for h in range(1, hpg):
            imp = imp + imp_h[:, h * tq:(h + 1) * tq]

        jrow = lax.broadcasted_iota(jnp.int32, (nblk, tq), 0)
        cur = jnp.right_shift(q0 + lax.broadcasted_iota(jnp.int32, (1, tq), 1), L_SLC.bit_length() - 1)
        valid = jrow <= cur
        forced = (jrow == 0) | (jrow == cur) | (jrow == cur - 1)
        val = jnp.where(valid & jnp.logical_not(forced), imp, NEG_INF)
        jrow_f = jrow.astype(F32)
        for _ in range(N_SEL - N_FORCED):
            mx = jnp.max(val, axis=0, keepdims=True)
            first = jnp.min(jnp.where(val == mx, jrow_f, float(MAX_SLC_BLOCKS)), axis=0, keepdims=True)
            val = jnp.where(jrow_f == first, PICKED, val)
        sel = jnp.where(valid & (forced | (val == PICKED)), 1.0, 0.0)
        if nblk < MAX_SLC_BLOCKS:
            sel = jnp.concatenate([sel, jnp.zeros((MAX_SLC_BLOCKS - nblk, tq), F32)], axis=0)
        gt = gate_ref[0, 0].T
        oc_ref[0] = _head_stack(ot, gt, 0, tq).T
        sel_t = sel.T
        negm_ref[0, 0] = jnp.where(sel_t > 0.5, 0.0, NEG_INF).astype(BF16)
        for r in range(tq // SUPER_KEYS):
            used_ref[r] = jnp.max(sel_t[r * SUPER_KEYS:(r + 1) * SUPER_KEYS], axis=0, keepdims=True)

    row_tile = min(m, CMP_ROW_TILE)
    n_variants = m // row_tile
    assert n_variants == 1 or row_tile % (tq // D_CMP) == 0
    if n_variants == 1:
        attend(m)
    else:
        needed = pl.program_id(2) * (tq // D_CMP) + (tq // D_CMP - 1)
        variant = jnp.minimum(jnp.right_shift(needed - 1, row_tile.bit_length() - 1), n_variants - 1)
        for v in range(n_variants):
            pl.when(variant == v)(functools.partial(attend, (v + 1) * row_tile))


def _nsa_cmp(q, kca, vct, gates, tq=1024):
    b, nh, s, _ = q.shape
    g, hpg, dh = N_KV_GROUPS, HEADS_PER_GROUP, HEAD_DIM
    m = kca.shape[1]
    assert s % tq == 0 and tq % SUPER_KEYS == 0 and tq & (tq - 1) == 0
    nqt = s // tq
    sub = tq // SUPER_KEYS
    return pl.pallas_call(
        functools.partial(_nsa_cmp_kernel, tq=tq, m=m),
        grid=(b, g, s // tq),
        in_specs=[
            pl.BlockSpec((1, hpg, tq, LANES), lambda i, j, k: (i, j, k, 0)),
            pl.BlockSpec((1, m, LANES), lambda i, j, k: (i * N_KV_GROUPS + j, 0, 0)),
            pl.BlockSpec((1, VT_ROWS, m), lambda i, j, k: (i * N_KV_GROUPS + j, 0, 0)),
            pl.BlockSpec((1, 1, tq, LANES), lambda i, j, k: (i, j, k, 0)),
        ],
        out_specs=[
            pl.BlockSpec((1, tq, hpg * dh), lambda i, j, k: (i, k, j)),
            pl.BlockSpec((1, 1, tq, MAX_SLC_BLOCKS), lambda i, j, k: (i, j, k, 0)),
            pl.BlockSpec((sub, 1, MAX_SLC_BLOCKS), lambda i, j, k: ((i * N_KV_GROUPS + j) * nqt + k, 0, 0)),
        ],
        out_shape=[
            jax.ShapeDtypeStruct((b, s, nh * dh), F32),
            jax.ShapeDtypeStruct((b, g, s, MAX_SLC_BLOCKS), BF16),
            jax.ShapeDtypeStruct((b * g * nqt * sub, 1, MAX_SLC_BLOCKS), F32),
        ],
        compiler_params=_params("arbitrary", "arbitrary", "arbitrary"),
        name="nsa_cmp",
    )(q, kca, vct, gates)


def _super_lists_kernel(used_ref, lst_ref, cnt_ref, *, tb):
    ut = used_ref[...].T.astype(BF16)
    a_i = lax.broadcasted_iota(jnp.int32, (MAX_SLC_BLOCKS, MAX_SLC_BLOCKS), 0)
    b_i = lax.broadcasted_iota(jnp.int32, (MAX_SLC_BLOCKS, MAX_SLC_BLOCKS), 1)
    pair_t = (a_i == jnp.right_shift(b_i, SUPER_SHIFT)).astype(BF16)
    sbf = jnp.where(_dot(pair_t, ut) > 0.5, 1.0, 0.0).astype(BF16)
    incl = _dot((b_i <= a_i).astype(BF16), sbf)[0:MAX_SUPER]
    cnt_ref[...] = jnp.broadcast_to(incl[MAX_SUPER - 1:MAX_SUPER, :], (8, tb)).astype(jnp.int32)
    for i in range(MAX_SUPER):
        lst_ref[i:i + 1, :] = jnp.sum(jnp.where(incl <= float(i), 1.0, 0.0), axis=0, keepdims=True).astype(jnp.int32)


def _super_lists(used):
    t = used.shape[0]
    tb = min(t, LANES)
    assert t % tb == 0
    return pl.pallas_call(
        functools.partial(_super_lists_kernel, tb=tb),
        grid=(t // tb,),
        in_specs=[pl.BlockSpec((tb, MAX_SLC_BLOCKS), lambda i: (i, 0))],
        out_specs=[
            pl.BlockSpec((MAX_SUPER, tb), lambda i: (0, i)),
            pl.BlockSpec((8, tb), lambda i: (0, i)),
        ],
        out_shape=[
            jax.ShapeDtypeStruct((MAX_SUPER, t), jnp.int32),
            jax.ShapeDtypeStruct((8, t), jnp.int32),
        ],
        compiler_params=_params("arbitrary"),
        name="super_lists",
    )(used)


def _nsa_slc_kernel(lst_ref, cnt_ref, q_ref, kas_ref, vts_ref, kaw_ref, vtw_ref, negm_ref, gate_ref, oc_ref, o_ref,
                    m_ref, acc_ref, *, tq, sub):
    hpg = HEADS_PER_GROUP
    ncol = hpg * tq
    n_win = WINDOW // SUPER_KEYS + 1
    pad_k = (lax.broadcasted_iota(jnp.int32, (SUPER_KEYS, 2 * LANES), 1) == PAD_LANE).astype(BF16)
    pad_w = (lax.broadcasted_iota(jnp.int32, (SUPER_KEYS, LANES), 1) == PAD_LANE).astype(BF16)
    n_tiles = pl.num_programs(0) * pl.num_programs(1) * pl.num_programs(2) * sub
    tile0 = ((pl.program_id(0) * pl.num_programs(1) + pl.program_id(1)) * pl.num_programs(2) + pl.program_id(2)) * sub
    row = lax.broadcasted_iota(jnp.int32, (SUPER_KEYS, 1), 0)

    class Tile:
        def __init__(self, s, slots):
            self.s = s
            self.slots = slots
            self.rows = slice(s * tq, (s + 1) * tq)
            self.qt = pl.program_id(2) * sub + s
            self.q = q_ref[0, :, self.rows, :].reshape(ncol, LANES)
            negm = negm_ref[0, 0, self.rows, :]
            self.qa = jnp.concatenate([self.q, jnp.concatenate([negm] * hpg, axis=0)], axis=1)
            self.tcol = self.qt * tq + jnp.bitwise_and(lax.broadcasted_iota(jnp.int32, (1, ncol), 1), tq - 1)
            self.tile = tile0 + s
            self.n_other = cnt_ref[self.tile] - 1
            self.n_early = jnp.right_shift(jnp.maximum(self.n_other - slots, 0) + CHUNK_SUPERS - 1,
                                           CHUNK_SUPERS.bit_length() - 1)
            m_ref[s] = jnp.full((1, ncol), NEG_INF, F32)
            acc_ref[s] = jnp.zeros((VT_ROWS, ncol), F32)

        def super_block(self, slot, ok=None):
            sb = lst_ref[slot * n_tiles + self.tile]
            if ok is not None:
                sb = jnp.where(ok, sb, 0)
            start = pl.multiple_of(sb * SUPER_KEYS, SUPER_KEYS)
            return kas_ref[0, 0, pl.ds(start, SUPER_KEYS), :], vts_ref[0, 0, :, pl.ds(start, SUPER_KEYS)]

        def update(self, st, vt):
            m_old = m_ref[self.s]
            m_new = jnp.maximum(m_old, jnp.max(st, axis=0, keepdims=True))
            p = jnp.exp2(st - m_new)
            acc_ref[self.s] = jnp.exp2(m_old - m_new) * acc_ref[self.s] + _dot(vt, p.astype(BF16))
            m_ref[self.s] = m_new

        def early_chunks(self):
            def body(c, carry):
                parts = [self.super_block(c * CHUNK_SUPERS + j) for j in range(CHUNK_SUPERS)]
                kt = jnp.concatenate([k for k, _ in parts], axis=0)
                self.update(_dot_nt(kt, self.qa), jnp.concatenate([v for _, v in parts], axis=1))
                return carry

            lax.fori_loop(0, self.n_early, body, 0)

        def final_scores(self):
            dstart = pl.multiple_of(self.qt * SUPER_KEYS, SUPER_KEYS)
            sd = _dot_nt(kas_ref[0, 0, pl.ds(dstart, SUPER_KEYS), :], self.qa)
            sd = jnp.where(self.qt * SUPER_KEYS + row <= self.tcol, sd, NEG_INF)
            ks, vs = [], [vts_ref[0, 0, :, pl.ds(dstart, SUPER_KEYS)]]
            for j in range(self.slots):
                slot = self.n_early * CHUNK_SUPERS + j
                ok = slot < self.n_other
                k, v = self.super_block(jnp.minimum(slot, MAX_SUPER - 1), ok)
                ks.append(_select_rows(ok, k, pad_k))
                vs.append(v)
            self.st = jnp.concatenate([sd, _dot_nt(jnp.concatenate(ks, axis=0), self.qa)], axis=0)
            self.vt = jnp.concatenate(vs, axis=1)
            win_scores, win_v = [], []
            for r in range(n_win):
                sb = self.qt - (n_win - 1) + r
                start = pl.multiple_of(jnp.maximum(sb, 0) * SUPER_KEYS, SUPER_KEYS)
                kw = kaw_ref[0, 0, pl.ds(start, SUPER_KEYS), :]
                if r < n_win - 1:
                    kw = _select_rows(sb >= 0, kw, pad_w)
                sw = _dot_nt(kw, self.q)
                if r == n_win - 1:
                    sw = jnp.where(sb * SUPER_KEYS + row <= self.tcol, sw, NEG_INF)
                elif r == 0:
                    sw = jnp.where(sb * SUPER_KEYS + WINDOW + row > self.tcol, sw, NEG_INF)
                win_scores.append(sw)
                win_v.append(vtw_ref[0, 0, :, pl.ds(start, SUPER_KEYS)])
            self.stw = jnp.concatenate(win_scores, axis=0)
            self.vw = jnp.concatenate(win_v, axis=1)

        def finish(self):
            self.update(self.st, self.vt)
            acc = acc_ref[self.s]
            o_slc_t = acc[0:HEAD_DIM] * (1.0 / acc[HEAD_DIM:HEAD_DIM + 1])
            e = jnp.exp2(self.stw - jnp.max(self.stw, axis=0, keepdims=True))
            acc_w = _dot(self.vw, e.astype(BF16))
            o_win_t = acc_w[0:HEAD_DIM] * (1.0 / acc_w[HEAD_DIM:HEAD_DIM + 1])
            gt = gate_ref[0, 0, self.rows, :].T
            comb = _head_stack(o_slc_t, gt, 1, tq) + _head_stack(o_win_t, gt, 2, tq)
            o_ref[0, self.rows, :] = (oc_ref[0, self.rows, :] + comb.T).astype(BF16)

    def run(slots):
        tiles = [Tile(s, slots) for s in range(sub)]
        for t in tiles:
            t.early_chunks()
        for t in tiles:
            t.final_scores()
        for t in tiles:
            t.finish()

    longest = cnt_ref[tile0] - 1
    for s in range(1, sub):
        longest = jnp.maximum(longest, cnt_ref[tile0 + s] - 1)
    pl.when(longest <= SHORT_SLOTS)(functools.partial(run, SHORT_SLOTS))
    pl.when(longest > SHORT_SLOTS)(functools.partial(run, FINAL_SLOTS))


def _nsa_slc(lists, counts, q, kas, vts, kaw, vtw, negm, gates, oc, tq=128, sub=8):
    b, nh, s, _ = q.shape
    g, hpg, dh = N_KV_GROUPS, HEADS_PER_GROUP, HEAD_DIM
    ts = tq * sub
    assert tq == SUPER_KEYS and WINDOW % tq == 0 and s >= WINDOW + tq and s % ts == 0
    grid_spec = pltpu.PrefetchScalarGridSpec(
        num_scalar_prefetch=2,
        grid=(b, g, s // ts),
        in_specs=[
            pl.BlockSpec((1, hpg, ts, LANES), lambda i, j, k, *_: (i, j, k, 0)),
            pl.BlockSpec((1, 1, s, 2 * LANES), lambda i, j, k, *_: (i, j, 0, 0)),
            pl.BlockSpec((1, 1, VT_ROWS, s), lambda i, j, k, *_: (i, j, 0, 0)),
            pl.BlockSpec((1, 1, s, LANES), lambda i, j, k, *_: (i, j, 0, 0)),
            pl.BlockSpec((1, 1, VT_ROWS, s), lambda i, j, k, *_: (i, j, 0, 0)),
            pl.BlockSpec((1, 1, ts, MAX_SLC_BLOCKS), lambda i, j, k, *_: (i, j, k, 0)),
            pl.BlockSpec((1, 1, ts, LANES), lambda i, j, k, *_: (i, j, k, 0)),
            pl.BlockSpec((1, ts, hpg * dh), lambda i, j, k, *_: (i, k, j)),
        ],
        out_specs=pl.BlockSpec((1, ts, hpg * dh), lambda i, j, k, *_: (i, k, j)),
        scratch_shapes=[
            pltpu.VMEM((sub, 1, hpg * tq), F32),
            pltpu.VMEM((sub, VT_ROWS, hpg * tq), F32),
        ],
    )
    return pl.pallas_call(
        functools.partial(_nsa_slc_kernel, tq=tq, sub=sub),
        grid_spec=grid_spec,
        out_shape=jax.ShapeDtypeStruct((b, s, nh * dh), BF16),
        compiler_params=_params("arbitrary", "arbitrary", "arbitrary"),
        name="nsa_slc",
    )(lists, counts, q, kas, vts, kaw, vtw, negm, gates, oc)


def _split_bf16(x, n):
    pieces = []
    for _ in range(n):
        p = x.astype(BF16).astype(F32)
        pieces.append(p)
        x = x - p
    return pieces


def _q_constants():
    slopes = jnp.exp2(-8.0 * jnp.arange(1, N_HEADS + 1, dtype=F32) / N_HEADS) * LOG2E
    pieces = jnp.stack(_split_bf16(slopes, N_SPLIT), axis=1)
    qc = jnp.zeros((N_HEADS, LANES), F32)
    qc = qc.at[:, POS_LANE:POS_LANE + N_SPLIT].set(pieces * L_SLC)
    qc = qc.at[:, POS_LANE + N_SPLIT:POS_LANE + 2 * N_SPLIT].set(pieces)
    return qc.at[:, PAD_LANE].set(NEG_INF)


def _qg_weight(w_qg):
    d = w_qg.shape[0]
    nq = N_HEADS * HEAD_DIM
    wg = jnp.pad(w_qg[:, nq:].reshape(d, N_KV_GROUPS, 3 * HEADS_PER_GROUP),
                 ((0, 0), (0, 0), (0, LANES - 3 * HEADS_PER_GROUP)))
    return jnp.concatenate([w_qg[:, :nq], wg.reshape(d, N_KV_GROUPS * LANES)], axis=1).astype(BF16)


def kernel(x, a_norm, a_w_in, a_conv, a_w_out, kv_norm, w_kv, cmp_pe_k, cmp_w1_k, cmp_w2_k, cmp_pe_v, cmp_w1_v,
           cmp_w2_v, b_norm, b_w_qg, b_w_o, f_norm, f_w_gu, f_w_down, final_norm):
    b, s, d = x.shape
    nq = N_HEADS * HEAD_DIM
    h = _mixer_a(x, a_norm[0], a_w_in[0].astype(BF16), a_conv[0], a_w_out[0].astype(BF16))
    h = _ffn(h.reshape(b * s, d), f_norm[0], f_w_gu[0].astype(BF16), f_w_down[0].astype(BF16), final_norm)
    h = h.reshape(b, s, d)

    kvc, kas, vts, kaw, vtw, q, gates = _proj(h, kv_norm, b_norm[0], w_kv.astype(BF16), _qg_weight(b_w_qg[0]),
                                              _q_constants())
    pe2 = jnp.stack([cmp_pe_k, cmp_pe_v]).reshape(2, 1, L_CMP * HEAD_DIM)
    pe2 = jnp.broadcast_to(pe2, (2, 8, L_CMP * HEAD_DIM)).astype(BF16)
    w2 = jnp.pad(jnp.stack([cmp_w2_k, cmp_w2_v]), ((0, 0), (0, 0), (0, LANES - HEAD_DIM))).astype(BF16)
    kca, vct = _compress(kvc, pe2, jnp.stack([cmp_w1_k, cmp_w1_v]).astype(BF16), w2)

    oc, negm, used = _nsa_cmp(q, kca, vct, gates)
    lists, counts = _super_lists(used.reshape(used.shape[0], MAX_SLC_BLOCKS))
    o = _nsa_slc(lists.reshape(-1), counts[0], q, kas, vts, kaw, vtw, negm, gates, oc)

    out = _ffn(h.reshape(b * s, d), f_norm[1], f_w_gu[1].astype(BF16), f_w_down[1].astype(BF16), final_norm,
               o=o.reshape(b * s, nq), w_o=b_w_o[0].astype(BF16), with_final=True)
    return out.reshape(b, s, d)
```
